```python
import jax
import jax.numpy as jnp
from jax import lax
import numpy as np

D_MODEL = 2048
BATCH = 2
SEQ = 4096
DEPTH = 1

GRID_W = 64
CTX_LEN = 256
GLA_HEADS = 4
GLA_DK = D_MODEL // 2 // GLA_HEADS
GLA_DV = D_MODEL // GLA_HEADS
GLA_GATE_RANK = 16
GLA_GATE_TEMP = 16.0
GLA_CHUNK = 64
NA_HEADS = 16
NA_DH = 64
NA_KR = 8
NA_KC = 16
D_FF = 256 * ((8 * D_MODEL // 3 + 255) // 256)
ROPE_THETA = 10000.0
EPS = 1e-6
NEG_INF = -1e30
QK_W = GLA_HEADS * GLA_DK
V_W = GLA_HEADS * GLA_DV
NA_W = NA_HEADS * NA_DH
IN_SPLITS = (QK_W, QK_W, V_W, V_W, GLA_GATE_RANK, GLA_GATE_RANK, NA_W, NA_W, NA_W, D_MODEL, D_MODEL)
IN_WIDTH = sum(IN_SPLITS)
SPLIT_AT = [int(s) for s in np.cumsum(IN_SPLITS)[:-1]]
N_MOD = 9

kernel_name = 'hybrid_gla_natten_macaron_dit_block'


def rmsnorm(x, g):
    xf = x.astype(jnp.float32)
    y = xf * lax.rsqrt(jnp.mean(xf * xf, axis=-1, keepdims=True) + EPS)
    return (y * g.astype(jnp.float32)).astype(x.dtype)


def modulate(x, mod, i, g):
    return rmsnorm(x, g) * (1 + mod[3 * i + 1]) + mod[3 * i]


def residual(x, y, mod, i, g, w):
    return x + w * mod[3 * i + 2] * rmsnorm(y, g)


def swiglu(h, wg, wu, wd):
    return (jax.nn.silu(h @ wg) * (h @ wu)) @ wd


def to_heads(a, n):
    b, t, _ = a.shape
    return a.reshape(b, t, n, -1).transpose(0, 2, 1, 3)


def from_heads(a):
    b, n, t, d = a.shape
    return a.transpose(0, 2, 1, 3).reshape(b, t, n * d)


def flip_t(a):
    return jnp.flip(a, axis=2)


def rope_axis(x, pos):
    half = x.shape[-1] // 2
    freqs = ROPE_THETA ** (-jnp.arange(half, dtype=jnp.float32) / half)
    ang = pos.astype(jnp.float32)[:, None] * freqs
    cos, sin = jnp.cos(ang), jnp.sin(ang)
    x1, x2 = x[..., :half], x[..., half:]
    return jnp.concatenate([x1 * cos - x2 * sin, x2 * cos + x1 * sin], axis=-1)


def rope_2d(x, row, col):
    h = x.shape[-1] // 2
    return jnp.concatenate([rope_axis(x[..., :h], row), rope_axis(x[..., h:], col)], axis=-1)


def gla_scan(q, k, v, log_a, s0):
    b, h, t, dk = q.shape
    dv = v.shape[-1]
    n = t // GLA_CHUNK
    rs = lambda a: a.reshape(b, h, n, GLA_CHUNK, a.shape[-1])
    q, k, v, log_a = rs(q), rs(k), rs(v), rs(log_a)
    cum = jnp.cumsum(log_a, axis=3)
    cum_last = cum[:, :, :, -1:, :]
    q_dec = q * jnp.exp(cum)
    k_inv = k * jnp.exp(-cum)
    k_to_end = k * jnp.exp(cum_last - cum)
    causal_in_chunk = jnp.tril(jnp.ones((GLA_CHUNK, GLA_CHUNK), dtype=bool))
    att = jnp.einsum('bhnid,bhnjd->bhnij', q_dec, k_inv)
    att = jnp.where(causal_in_chunk, att, 0.0)
    o_intra = jnp.einsum('bhnij,bhnjv->bhniv', att, v)
    kv_chunk = jnp.einsum('bhncd,bhncv->bhndv', k_to_end, v)
    decay_chunk = jnp.exp(cum_last[:, :, :, 0, :])

    def step(state, xs):
        qc, kvc, dc = xs
        o = jnp.einsum('bhcd,bhdv->bhcv', qc, state)
        return dc[..., None] * state + kvc, o

    xs = (jnp.moveaxis(q_dec, 2, 0), jnp.moveaxis(kv_chunk, 2, 0), jnp.moveaxis(decay_chunk, 2, 0))
    s_fin, o_inter = lax.scan(step, s0, xs)
    o = o_intra + jnp.moveaxis(o_inter, 0, 2)
    return o.reshape(b, h, t, dv), s_fin


def gla_streams(q, k, v, gf, gb, gla_wg, gla_bg, pos):
    q = to_heads(q, GLA_HEADS).astype(jnp.float32) * (GLA_DK ** -0.5)
    k = to_heads(k, GLA_HEADS).astype(jnp.float32)
    if pos is not None:
        q, k = rope_2d(q, pos[0], pos[1]), rope_2d(k, pos[0], pos[1])
    v = to_heads(v, GLA_HEADS).astype(jnp.float32)

    def log_decay(g, j):
        z = (g @ gla_wg[j] + gla_bg[j]).astype(jnp.float32)
        return to_heads(jax.nn.log_sigmoid(z) / GLA_GATE_TEMP, GLA_HEADS)

    return q, k, v, log_decay(gf, 0), log_decay(gb, 1)


def gla_readout(o, r, g, w_o):
    o = from_heads(rmsnorm(o, g)).astype(r.dtype)
    return (o * jax.nn.silu(r)) @ w_o


def neighbourhood_attention(q, k, v, kc, vc, rpb):
    b, t, _ = q.shape
    rows = t // GRID_W
    kr = min(NA_KR, rows)
    grid = lambda a: to_heads(a, NA_HEADS).reshape(b, NA_HEADS, rows, GRID_W, NA_DH)
    qg, kg, vg = grid(q), grid(k), grid(v)
    r = jnp.arange(rows)
    row_idx = jnp.clip(r - kr // 2, 0, rows - kr)[:, None] + jnp.arange(kr)[None, :]
    k_band = kg[:, :, row_idx]
    v_band = vg[:, :, row_idx]
    j = jnp.arange(GRID_W)
    col_start = jnp.clip(j - NA_KC // 2, 0, GRID_W - NA_KC)
    col_ok = (j[None, :] >= col_start[:, None]) & (j[None, :] < col_start[:, None] + NA_KC)
    di = row_idx - r[:, None]
    dj = jnp.clip(j[None, :] - j[:, None], -(NA_KC - 1), NA_KC - 1)
    bias = rpb[:, di[:, None, :, None] + NA_KR - 1, dj[None, :, None, :] + NA_KC - 1]
    scale = NA_DH ** -0.5
    s_lat = jnp.einsum('bhrqd,bhrakd->bhrqak', qg, k_band).astype(jnp.float32) * scale + bias.astype(jnp.float32)
    s_lat = jnp.where(col_ok[:, None, :], s_lat, NEG_INF)
    kch, vch = to_heads(kc, NA_HEADS), to_heads(vc, NA_HEADS)
    s_ctx = jnp.einsum('bhrqd,bhcd->bhrqc', qg, kch).astype(jnp.float32) * scale
    n_lat = kr * GRID_W
    s = jnp.concatenate([s_lat.reshape(b, NA_HEADS, rows, GRID_W, n_lat), s_ctx], axis=-1)
    p = jax.nn.softmax(s, axis=-1).astype(v.dtype)
    p_lat = p[..., :n_lat].reshape(b, NA_HEADS, rows, GRID_W, kr, GRID_W)
    o = jnp.einsum('bhrqak,bhrakd->bhrqd', p_lat, v_band) + jnp.einsum('bhrqc,bhcd->bhrqd', p[..., n_lat:], vch)
    return from_heads(o.reshape(b, NA_HEADS, t, NA_DH))


def context_attention(q, k, v):
    qh, kh, vh = to_heads(q, NA_HEADS), to_heads(k, NA_HEADS), to_heads(v, NA_HEADS)
    s = jnp.einsum('bhqd,bhkd->bhqk', qh, kh).astype(jnp.float32) * (NA_DH ** -0.5)
    p = jax.nn.softmax(s, axis=-1).astype(v.dtype)
    return from_heads(jnp.einsum('bhqk,bhkd->bhqd', p, vh))


def token_mixer(hl, hc, w_in, gla_wg, gla_bg, gla_norm_g, w_gla_o, na_rpb, w_na_o, w_out, row, col, ctx_out):
    ql, kl, vl, rl, gfl, gbl, nql, nkl, nvl, m1l, m2l = jnp.split(hl @ w_in, SPLIT_AT, axis=-1)
    qc, kc, vc, rc, gfc, gbc, nqc, nkc, nvc, m1c, m2c = jnp.split(hc @ w_in, SPLIT_AT, axis=-1)
    qc4, kc4, vc4, afc, abc = gla_streams(qc, kc, vc, gfc, gbc, gla_wg, gla_bg, None)
    ql4, kl4, vl4, afl, abl = gla_streams(ql, kl, vl, gfl, gbl, gla_wg, gla_bg, (row, col))
    s0 = jnp.zeros((hl.shape[0], GLA_HEADS, GLA_DK, GLA_DV), jnp.float32)
    oc_f, sc_f = gla_scan(qc4, kc4, vc4, afc, s0)
    oc_b, sc_b = gla_scan(flip_t(qc4), flip_t(kc4), flip_t(vc4), flip_t(abc), s0)
    ol_f, _ = gla_scan(ql4, kl4, vl4, afl, sc_f)
    ol_b, _ = gla_scan(flip_t(ql4), flip_t(kl4), flip_t(vl4), flip_t(abl), sc_b)
    a_l = gla_readout(ol_f + flip_t(ol_b), rl, gla_norm_g, w_gla_o)
    b_l = neighbourhood_attention(nql, nkl, nvl, nkc, nvc, na_rpb) @ w_na_o
    yl = (jax.nn.sigmoid(m1l) * a_l + jax.nn.sigmoid(m2l) * b_l) @ w_out
    if not ctx_out:
        return yl, None
    a_c = gla_readout(oc_f + flip_t(oc_b), rc, gla_norm_g, w_gla_o)
    b_c = context_attention(nqc, nkc, nvc) @ w_na_o
    yc = (jax.nn.sigmoid(m1c) * a_c + jax.nn.sigmoid(m2c) * b_c) @ w_out
    return yl, yc


def setup_inputs(seed: int = 0) -> dict:
    key = jax.random.key(seed)
    ks = jax.random.split(key, 18)
    nrm = lambda k, shape, s: jax.random.normal(k, shape, jnp.float32) * s
    D, L = D_MODEL, DEPTH
    return {
        'x': nrm(ks[0], (BATCH, SEQ, D), 1.0),
        'c': nrm(ks[1], (BATCH, D), 1.0),
        'ctx': nrm(ks[2], (BATCH, CTX_LEN, D), 1.0),
        'c_ctx': nrm(ks[3], (D,), 1.0),
        'w_ada': nrm(ks[4], (L, D, N_MOD * D), D ** -0.5),
        'b_ada': nrm(ks[5], (L, N_MOD * D), 0.02),
        'norm_g': 1.0 + nrm(ks[6], (L, 6, D), 0.02),
        'ffn_wg': nrm(ks[7], (L, 2, D, D_FF), D ** -0.5),
        'ffn_wu': nrm(ks[8], (L, 2, D, D_FF), D ** -0.5),
        'ffn_wd': nrm(ks[9], (L, 2, D_FF, D), D_FF ** -0.5),
        'w_in': nrm(ks[10], (L, D, IN_WIDTH), D ** -0.5),
        'gla_wg': nrm(ks[11], (L, 2, GLA_GATE_RANK, QK_W), GLA_GATE_RANK ** -0.5),
        'gla_bg': nrm(ks[12], (L, 2, QK_W), 0.1),
        'gla_norm_g': 1.0 + nrm(ks[13], (L, GLA_DV), 0.02),
        'w_gla_o': nrm(ks[14], (L, V_W, D), V_W ** -0.5),
        'na_rpb': nrm(ks[15], (L, NA_HEADS, 2 * NA_KR - 1, 2 * NA_KC - 1), 0.1),
        'w_na_o': nrm(ks[16], (L, NA_W, D), NA_W ** -0.5),
        'w_out': nrm(ks[17], (L, D, D), D ** -0.5),
    }


def reference(x, c, ctx, c_ctx, w_ada, b_ada, norm_g, ffn_wg, ffn_wu, ffn_wd, w_in, gla_wg, gla_bg,
              gla_norm_g, w_gla_o, na_rpb, w_na_o, w_out):
    t = jnp.arange(x.shape[1])
    row, col = t // GRID_W, t % GRID_W
    silu_c, silu_cc = jax.nn.silu(c), jax.nn.silu(c_ctx)
    h, hc = x, ctx
    for l in range(DEPTH):
        ctx_out = l < DEPTH - 1
        ml = jnp.moveaxis((silu_c @ w_ada[l] + b_ada[l]).reshape(-1, N_MOD, D_MODEL), 1, 0)[:, :, None, :]
        mc = (silu_cc @ w_ada[l] + b_ada[l]).reshape(N_MOD, D_MODEL)[:, None, None, :]
        g = norm_g[l]
        h = residual(h, swiglu(modulate(h, ml, 0, g[0]), ffn_wg[l, 0], ffn_wu[l, 0], ffn_wd[l, 0]), ml, 0, g[1], 0.5)
        hc = residual(hc, swiglu(modulate(hc, mc, 0, g[0]), ffn_wg[l, 0], ffn_wu[l, 0], ffn_wd[l, 0]), mc, 0, g[1], 0.5)
        yl, yc = token_mixer(modulate(h, ml, 1, g[2]), modulate(hc, mc, 1, g[2]), w_in[l], gla_wg[l], gla_bg[l],
                             gla_norm_g[l], w_gla_o[l], na_rpb[l], w_na_o[l], w_out[l], row, col, ctx_out)
        h = residual(h, yl, ml, 1, g[3], 1.0)
        h = residual(h, swiglu(modulate(h, ml, 2, g[4]), ffn_wg[l, 1], ffn_wu[l, 1], ffn_wd[l, 1]), ml, 2, g[5], 0.5)
        if ctx_out:
            hc = residual(hc, yc, mc, 1, g[3], 1.0)
            hc = residual(hc, swiglu(modulate(hc, mc, 2, g[4]), ffn_wg[l, 1], ffn_wu[l, 1], ffn_wd[l, 1]), mc, 2, g[5], 0.5)
    return h
```

```python
import functools

import numpy as np
import jax
import jax.numpy as jnp
from jax import lax
from jax.experimental import pallas as pl
from jax.experimental.pallas import tpu as pltpu

F32 = jnp.float32
BF16 = jnp.bfloat16

D_MODEL = 2048
GRID_W = 64
GLA_HEADS = 4
GLA_DK = 256
GLA_DV = 512
GLA_GATE_RANK = 16
GLA_GATE_TEMP = 16.0
NA_HEADS = 16
NA_DH = 64
NA_KR = 8
NA_KC = 16
ROPE_THETA = 10000.0
EPS = 1e-6
NEG_INF = -1e30
N_MOD = 9

LANES = 128
ROW_TILE = 512
FF_TILE = 512
PROJ_TILE = 512
GLA_BLOCK = 256
GLA_CHUNK = 128
NA_QROWS = 8
NA_WIN_ROWS = 16
VMEM_LIMIT = 56 * 1024 * 1024


def _params(sem):
    return pltpu.CompilerParams(dimension_semantics=sem, vmem_limit_bytes=VMEM_LIMIT)


def _rms(x, g):
    return x * lax.rsqrt(jnp.mean(x * x, axis=-1, keepdims=True) + EPS) * g


def _silu(x):
    return x * jax.nn.sigmoid(x)


def _mod_kernel(c_ref, w_ref, b_ref, o_ref):
    o_ref[...] = jnp.dot(_silu(c_ref[...]), w_ref[...], preferred_element_type=F32) + b_ref[...]


def _modulation(cvec, w_ada, b_ada):
    n = w_ada.shape[1]
    tn = 1024
    return pl.pallas_call(
        _mod_kernel,
        grid=(n // tn,),
        in_specs=[pl.BlockSpec((8, D_MODEL), lambda j: (0, 0)),
                  pl.BlockSpec((D_MODEL, tn), lambda j: (0, j)),
                  pl.BlockSpec((1, tn), lambda j: (0, j))],
        out_specs=pl.BlockSpec((8, tn), lambda j: (0, j)),
        out_shape=jax.ShapeDtypeStruct((8, n), F32),
        compiler_params=_params(("arbitrary",)),
        name="adaln_mod",
    )(cvec, w_ada, b_ada)


def _ffn_kernel(*refs, sub, n_lat_tiles, has_ctx):
    if has_ctx:
        x_ref, c_ref, mod_ref, g_ref, wg_ref, wu_ref, wd_ref, o_ref, h_ref = refs
    else:
        x_ref, mod_ref, g_ref, wg_ref, wu_ref, wd_ref, o_ref, h_ref = refs
        c_ref = None
    i = pl.program_id(0)
    f = pl.program_id(1)

    def prologue(src_ref):
        shift = mod_ref[3 * sub:3 * sub + 1, :]
        scale = mod_ref[3 * sub + 1:3 * sub + 2, :]
        h = _rms(src_ref[...], g_ref[2 * sub:2 * sub + 1, :]) * (1.0 + scale) + shift
        h_ref[...] = h.astype(BF16)
        o_ref[...] = jnp.zeros_like(o_ref)

    def epilogue(src_ref):
        gate = mod_ref[3 * sub + 2:3 * sub + 3, :]
        y = _rms(o_ref[...], g_ref[2 * sub + 1:2 * sub + 2, :])
        o_ref[...] = src_ref[...] + 0.5 * gate * y

    def on_rows(fn):
        if has_ctx:
            pl.when(i < n_lat_tiles)(lambda: fn(x_ref))
            pl.when(i >= n_lat_tiles)(lambda: fn(c_ref))
        else:
            fn(x_ref)

    pl.when(f == 0)(lambda: on_rows(prologue))
    h = h_ref[...]
    g = jnp.dot(h, wg_ref[...], preferred_element_type=F32)
    u = jnp.dot(h, wu_ref[...], preferred_element_type=F32)
    a = (_silu(g) * u).astype(BF16)
    o_ref[...] += jnp.dot(a, wd_ref[...], preferred_element_type=F32)
    pl.when(f == pl.num_programs(1) - 1)(lambda: on_rows(epilogue))


def _ffn(x2d, ctx2d, mod, norm_g, wg, wu, wd, sub):
    n_lat_tiles = x2d.shape[0] // ROW_TILE
    has_ctx = ctx2d is not None
    n_tiles = n_lat_tiles + (1 if has_ctx else 0)
    tiles_per_batch = n_lat_tiles // 2
    d_ff = wg.shape[1]
    row_spec = lambda shape: pl.BlockSpec(shape, lambda i, f: (jnp.minimum(i, n_lat_tiles - 1), 0))
    in_specs = [row_spec((ROW_TILE, D_MODEL))]
    args = [x2d]
    if has_ctx:
        in_specs.append(pl.BlockSpec((ROW_TILE, D_MODEL), lambda i, f: (0, 0)))
        args.append(ctx2d)
    in_specs += [
        pl.BlockSpec((None, N_MOD, D_MODEL), lambda i, f: (jnp.minimum(i // tiles_per_batch, 2), 0, 0)),
        pl.BlockSpec((6, D_MODEL), lambda i, f: (0, 0)),
        pl.BlockSpec((D_MODEL, FF_TILE), lambda i, f: (0, f)),
        pl.BlockSpec((D_MODEL, FF_TILE), lambda i, f: (0, f)),
        pl.BlockSpec((FF_TILE, D_MODEL), lambda i, f: (f, 0)),
    ]
    args += [mod, norm_g, wg, wu, wd]
    return pl.pallas_call(
        functools.partial(_ffn_kernel, sub=sub, n_lat_tiles=n_lat_tiles, has_ctx=has_ctx),
        grid=(n_tiles, d_ff // FF_TILE),
        in_specs=in_specs,
        out_specs=pl.BlockSpec((ROW_TILE, D_MODEL), lambda i, f: (i, 0)),
        out_shape=jax.ShapeDtypeStruct((n_tiles * ROW_TILE, D_MODEL), F32),
        scratch_shapes=[pltpu.VMEM((ROW_TILE, D_MODEL), BF16)],
        compiler_params=_params(("arbitrary", "arbitrary")),
        name="ffn%d" % sub,
    )(*args)


N_ROPE_TILES = 2 * GLA_HEADS * GLA_DK // PROJ_TILE
N_Q_TILES = N_ROPE_TILES // 2


def _rope_tables(seq):
    half = GLA_DK // 4
    freqs = ROPE_THETA ** (-np.arange(half, dtype=np.float64) / half)
    t = np.arange(seq)
    cos_parts, sin_parts = [], []
    for pos in (t // GRID_W, t % GRID_W):
        ang = pos[:, None].astype(np.float64) * freqs
        cos_parts += [np.cos(ang), np.cos(ang)]
        sin_parts += [-np.sin(ang), np.sin(ang)]
    cos = np.concatenate(cos_parts, axis=1)
    sin = np.concatenate(sin_parts, axis=1)
    cos = np.concatenate([cos, np.ones((ROW_TILE, GLA_DK))], axis=0)
    sin = np.concatenate([sin, np.zeros((ROW_TILE, GLA_DK))], axis=0)
    return jnp.asarray(cos, F32), jnp.asarray(sin, F32)


def _inproj_kernel(x_ref, mod_ref, g_ref, w_ref, wgate_ref, cos_ref, sin_ref, p_ref, gate_ref, h_ref):
    n = pl.program_id(1)

    @pl.when(n == 0)
    def _():
        h = _rms(x_ref[...], g_ref[2:3, :]) * (1.0 + mod_ref[4:5, :]) + mod_ref[3:4, :]
        h_ref[...] = h.astype(BF16)
        gate_ref[...] = jnp.dot(h_ref[...], wgate_ref[...], preferred_element_type=F32)

    y = jnp.dot(h_ref[...], w_ref[...], preferred_element_type=F32)

    @pl.when(n < N_ROPE_TILES)
    def _():
        cos = cos_ref[...]
        sin = sin_ref[...]
        qscale = jnp.where(n < N_Q_TILES, GLA_DK ** -0.5, 1.0).astype(F32)
        for j in range(PROJ_TILE // GLA_DK):
            yj = y[:, j * GLA_DK:(j + 1) * GLA_DK]
            swapped = jnp.concatenate(
                [pltpu.roll(yj[:, g * LANES:(g + 1) * LANES], LANES // 2, 1) for g in range(GLA_DK // LANES)],
                axis=1)
            p_ref[:, j * GLA_DK:(j + 1) * GLA_DK] = ((yj * cos + swapped * sin) * qscale).astype(BF16)

    @pl.when(n >= N_ROPE_TILES)
    def _():
        p_ref[...] = y.astype(BF16)


def _inproj(h1, mod, norm_g, w_main, w_gate, cos, sin, seq):
    n_tiles = h1.shape[0] // ROW_TILE
    n_lat_tiles = n_tiles - 1
    tiles_per_batch = n_lat_tiles // 2
    width = w_main.shape[1]
    tab_spec = pl.BlockSpec(
        (ROW_TILE, GLA_DK),
        lambda i, n: (jnp.where(i < n_lat_tiles, i % tiles_per_batch, seq // ROW_TILE), 0))
    return pl.pallas_call(
        _inproj_kernel,
        grid=(n_tiles, width // PROJ_TILE),
        in_specs=[pl.BlockSpec((ROW_TILE, D_MODEL), lambda i, n: (i, 0)),
                  pl.BlockSpec((None, N_MOD, D_MODEL), lambda i, n: (jnp.minimum(i // tiles_per_batch, 2), 0, 0)),
                  pl.BlockSpec((6, D_MODEL), lambda i, n: (0, 0)),
                  pl.BlockSpec((D_MODEL, PROJ_TILE), lambda i, n: (0, n)),
                  pl.BlockSpec((D_MODEL, LANES), lambda i, n: (0, 0)),
                  tab_spec, tab_spec],
        out_specs=[pl.BlockSpec((ROW_TILE, PROJ_TILE), lambda i, n: (i, n)),
                   pl.BlockSpec((ROW_TILE, LANES), lambda i, n: (i, 0))],
        out_shape=[jax.ShapeDtypeStruct((h1.shape[0], width), BF16),
                   jax.ShapeDtypeStruct((h1.shape[0], LANES), F32)],
        scratch_shapes=[pltpu.VMEM((ROW_TILE, D_MODEL), BF16)],
        compiler_params=_params(("arbitrary", "arbitrary")),
        name="inproj",
    )(h1, mod, norm_g, w_main, w_gate, cos, sin)


def _log_sigmoid(z):
    return jnp.minimum(z, 0.0) - jnp.log1p(jnp.exp(-jnp.abs(z)))


def _gla_kernel(*refs, forward):
    if forward:
        q_ref, k_ref, v_ref, gin_ref, wg_ref, bg_ref, ob_ref, r_ref, gn_ref, o_ref, st_ref = refs
    else:
        q_ref, k_ref, v_ref, gin_ref, wg_ref, bg_ref, o_ref, st_ref = refs
    s = pl.program_id(2)

    @pl.when(s == 0)
    def _():
        st_ref[...] = jnp.zeros_like(st_ref)

    c = GLA_CHUNK
    row = lax.broadcasted_iota(jnp.int32, (c, c), 0)
    col = lax.broadcasted_iota(jnp.int32, (c, c), 1)
    mask = (col <= row) if forward else (col >= row)
    tri = jnp.where(mask, 1.0, 0.0).astype(BF16)
    n_chunks = GLA_BLOCK // c
    order = range(n_chunks) if forward else range(n_chunks - 1, -1, -1)
    for ci in order:
        rows = slice(ci * c, (ci + 1) * c)
        q = q_ref[rows, :].astype(F32)
        k = k_ref[rows, :].astype(F32)
        v = v_ref[rows, :]
        z = jnp.dot(gin_ref[rows, :], wg_ref[...], preferred_element_type=F32) + bg_ref[...]
        log_a = _log_sigmoid(z) * (1.0 / GLA_GATE_TEMP)
        hi = log_a.astype(BF16)
        lo = (log_a - hi.astype(F32)).astype(BF16)
        cum = (jnp.dot(tri, hi, preferred_element_type=F32) + jnp.dot(tri, lo, preferred_element_type=F32))
        cum_end = cum[c - 1:c, :] if forward else cum[0:1, :]
        q_dec = (q * jnp.exp(cum)).astype(BF16)
        k_inv = (k * jnp.exp(-cum)).astype(BF16)
        k_end = (k * jnp.exp(cum_end - cum)).astype(BF16)
        att = lax.dot_general(q_dec, k_inv, (((1,), (1,)), ((), ())), preferred_element_type=F32)
        att = jnp.where(mask, att, 0.0).astype(BF16)
        st = st_ref[...]
        o = (jnp.dot(att, v, preferred_element_type=F32)
             + lax.dot_general(q_dec, st.astype(BF16), (((1,), (1,)), ((), ())), preferred_element_type=F32))
        v_t = v.astype(F32).T.astype(BF16)
        st_ref[...] = jnp.exp(cum_end) * st + jnp.dot(v_t, k_end, preferred_element_type=F32)

        @pl.when(s > 0)
        def _():
            if forward:
                tot = o + ob_ref[rows, :]
                a = _rms(tot, gn_ref[...]) * _silu(r_ref[rows, :].astype(F32))
                o_ref[rows, :] = a.astype(o_ref.dtype)
            else:
                o_ref[rows, :] = o


def _gla(p, gin, wg_pad, bg, o_bwd, gn, batch, seq, forward):
    n_lat = seq // GLA_BLOCK
    ctx_block0 = batch * n_lat

    def lat_block(b, s):
        j = jnp.maximum(s - 1, 0)
        return b * n_lat + (j if forward else n_lat - 1 - j)

    def in_block(b, s):
        return jnp.where(s == 0, ctx_block0 + b, lat_block(b, s))

    d = 0 if forward else 1
    kq = GLA_HEADS
    kv = 2 * GLA_HEADS * GLA_DK // GLA_DV
    kr = kv + GLA_HEADS
    in_specs = [
        pl.BlockSpec((GLA_BLOCK, GLA_DK), lambda b, h, s: (in_block(b, s), h)),
        pl.BlockSpec((GLA_BLOCK, GLA_DK), lambda b, h, s: (in_block(b, s), kq + h)),
        pl.BlockSpec((GLA_BLOCK, GLA_DV), lambda b, h, s: (in_block(b, s), kv + h)),
        pl.BlockSpec((GLA_BLOCK, LANES), lambda b, h, s: (in_block(b, s), 0)),
        pl.BlockSpec((None, LANES, GLA_DK), lambda b, h, s: (d, 0, h)),
        pl.BlockSpec((None, 1, GLA_DK), lambda b, h, s: (d, 0, h)),
    ]
    args = [p, p, p, gin, wg_pad, bg]
    if forward:
        in_specs += [
            pl.BlockSpec((GLA_BLOCK, GLA_DV), lambda b, h, s: (lat_block(b, s), h)),
            pl.BlockSpec((GLA_BLOCK, GLA_DV), lambda b, h, s: (lat_block(b, s), kr + h)),
            pl.BlockSpec((1, GLA_DV), lambda b, h, s: (0, 0)),
        ]
        args += [o_bwd, p, gn]
    return pl.pallas_call(
        functools.partial(_gla_kernel, forward=forward),
        grid=(batch, GLA_HEADS, n_lat + 1),
        in_specs=in_specs,
        out_specs=pl.BlockSpec((GLA_BLOCK, GLA_DV), lambda b, h, s: (lat_block(b, s), h)),
        out_shape=jax.ShapeDtypeStruct((batch * seq, GLA_HEADS * GLA_DV), BF16 if forward else F32),
        scratch_shapes=[pltpu.VMEM((GLA_DV, GLA_DK), F32)],
        compiler_params=_params(("arbitrary", "arbitrary", "arbitrary")),
        name="gla_fwd" if forward else "gla_bwd",
    )(*args)


NA_QTOK = NA_QROWS * GRID_W
NA_WIN = NA_WIN_ROWS * GRID_W
NA_BIAS_PAD = NA_QROWS * GRID_W
NA_BIAS_W = 2048
NA_KINDS = ((0, 0), (NA_QROWS, NA_QROWS - NA_KR // 2), (GRID_W - NA_QROWS, GRID_W - NA_WIN_ROWS))


def _na_bias_tables(rpb):
    qc = np.arange(GRID_W)
    dj = np.clip(qc[None, :] - qc[:, None], -(NA_KC - 1), NA_KC - 1) + NA_KC - 1
    t = rpb[:, :, dj]
    t = jnp.transpose(t, (0, 2, 1, 3)).reshape(rpb.shape[0], GRID_W, (2 * NA_KR - 1) * GRID_W)
    right = NA_BIAS_W + GRID_W - NA_BIAS_PAD - t.shape[-1]
    t = jnp.pad(t, ((0, 0), (0, 0), (NA_BIAS_PAD, right)))
    return jnp.stack([t[..., :NA_BIAS_W], t[..., GRID_W:GRID_W + NA_BIAS_W]], axis=1)


def _na_kernel(q_ref, k_ref, v_ref, kc_ref, vc_ref, ce_ref, o_ref, t_ref, *, rows):
    b = pl.program_id(1)
    blk = pl.program_id(2)
    n_blk = rows // NA_QROWS

    @pl.when((b == 0) & (blk == 0))
    def _():
        kk = lax.broadcasted_iota(jnp.int32, (GRID_W, NA_WIN), 1)
        k_row = kk >> 6
        k_col = kk & (GRID_W - 1)
        q_col = lax.broadcasted_iota(jnp.int32, (GRID_W, NA_WIN), 0)
        c0 = jnp.clip(q_col - NA_KC // 2, 0, GRID_W - NA_KC)
        col_ok = (k_col >= c0) & (k_col < c0 + NA_KC)
        for kind, (r0, w0) in enumerate(NA_KINDS):
            for qr in range(NA_QROWS):
                band0 = min(max(r0 + qr - NA_KR // 2, 0), rows - NA_KR) - w0
                ok = col_ok & (k_row >= band0) & (k_row < band0 + NA_KR)
                a0 = w0 - (r0 + qr) + NA_KR - 1
                lane0 = NA_BIAS_PAD + (a0 - a0 % 2) * GRID_W
                for hh in range(2):
                    bias = ce_ref[hh, a0 % 2, :, lane0:lane0 + NA_WIN]
                    t_ref[hh, kind, qr * GRID_W:(qr + 1) * GRID_W, :] = jnp.where(ok, bias, NEG_INF)

    kind = jnp.where(blk == 0, 0, jnp.where(blk == n_blk - 1, 2, 1))
    w0 = jnp.clip(blk * NA_QROWS - NA_KR // 2, 0, rows - NA_WIN_ROWS)
    win = pl.ds(pl.multiple_of(w0 * GRID_W, 256), NA_WIN)
    q = q_ref[...]
    kw = k_ref[win, :]
    vw = v_ref[win, :]
    kc = kc_ref[...]
    vc = vc_ref[...]
    lane = lax.broadcasted_iota(jnp.int32, q.shape, 1)
    scale = NA_DH ** -0.5
    nt = (((1,), (1,)), ((), ()))
    outs = []
    for hh in range(2):
        own = (lane < NA_DH) if hh == 0 else (lane >= NA_DH)
        qh = jnp.where(own, q, jnp.zeros_like(q))
        s_lat = lax.dot_general(qh, kw, nt, preferred_element_type=F32) * scale + t_ref[hh, kind]
        s_ctx = lax.dot_general(qh, kc, nt, preferred_element_type=F32) * scale
        m = jnp.maximum(jnp.max(s_lat, axis=-1, keepdims=True), jnp.max(s_ctx, axis=-1, keepdims=True))
        p_lat = jnp.exp(s_lat - m)
        p_ctx = jnp.exp(s_ctx - m)
        denom = jnp.sum(p_lat, axis=-1, keepdims=True) + jnp.sum(p_ctx, axis=-1, keepdims=True)
        o = (jnp.dot(p_lat.astype(BF16), vw, preferred_element_type=F32)
             + jnp.dot(p_ctx.astype(BF16), vc, preferred_element_type=F32))
        outs.append(o / denom)
    o_ref[...] = jnp.where(lane < NA_DH, outs[0], outs[1]).astype(o_ref.dtype)


def _na(p, ce, batch, seq, ctx_len, col0):
    rows = seq // GRID_W
    n_blk = rows // NA_QROWS
    n_pairs = NA_HEADS // 2
    cq = col0 // LANES
    ck = cq + n_pairs
    cv = ck + n_pairs
    ctx_block0 = batch * seq // ctx_len
    return pl.pallas_call(
        functools.partial(_na_kernel, rows=rows),
        grid=(n_pairs, batch, n_blk),
        in_specs=[
            pl.BlockSpec((NA_QTOK, LANES), lambda hp, b, r: (b * n_blk + r, cq + hp)),
            pl.BlockSpec((seq, LANES), lambda hp, b, r: (b, ck + hp)),
            pl.BlockSpec((seq, LANES), lambda hp, b, r: (b, cv + hp)),
            pl.BlockSpec((ctx_len, LANES), lambda hp, b, r: (ctx_block0 + b, ck + hp)),
            pl.BlockSpec((ctx_len, LANES), lambda hp, b, r: (ctx_block0 + b, cv + hp)),
            pl.BlockSpec((2, 2, GRID_W, NA_BIAS_W), lambda hp, b, r: (hp, 0, 0, 0)),
        ],
        out_specs=pl.BlockSpec((NA_QTOK, LANES), lambda hp, b, r: (b * n_blk + r, hp)),
        out_shape=jax.ShapeDtypeStruct((batch * seq, NA_HEADS * NA_DH), BF16),
        scratch_shapes=[pltpu.VMEM((2, len(NA_KINDS), NA_QTOK, NA_WIN), F32)],
        compiler_params=_params(("arbitrary", "arbitrary", "arbitrary")),
        name="natten",
    )(p, p, p, p, p, ce)


def _merge_kernel(a_ref, b_ref, wa_ref, wb_ref, m1_ref, m2_ref, o_ref):
    a = jnp.dot(a_ref[...], wa_ref[...], preferred_element_type=F32)
    b = jnp.dot(b_ref[...], wb_ref[...], preferred_element_type=F32)
    m = jax.nn.sigmoid(m1_ref[...].astype(F32)) * a + jax.nn.sigmoid(m2_ref[...].astype(F32)) * b
    o_ref[...] = m.astype(o_ref.dtype)


def _merge(a_pre, o_na, w_gla_o, w_na_o, p, col_m1):
    n_rows = a_pre.shape[0]
    tn = PROJ_TILE
    c1 = col_m1 // tn
    c2 = c1 + D_MODEL // tn
    return pl.pallas_call(
        _merge_kernel,
        grid=(n_rows // ROW_TILE, D_MODEL // tn),
        in_specs=[pl.BlockSpec((ROW_TILE, a_pre.shape[1]), lambda i, n: (i, 0)),
                  pl.BlockSpec((ROW_TILE, o_na.shape[1]), lambda i, n: (i, 0)),
                  pl.BlockSpec((w_gla_o.shape[0], tn), lambda i, n: (0, n)),
                  pl.BlockSpec((w_na_o.shape[0], tn), lambda i, n: (0, n)),
                  pl.BlockSpec((ROW_TILE, tn), lambda i, n: (i, c1 + n)),
                  pl.BlockSpec((ROW_TILE, tn), lambda i, n: (i, c2 + n))],
        out_specs=pl.BlockSpec((ROW_TILE, tn), lambda i, n: (i, n)),
        out_shape=jax.ShapeDtypeStruct((n_rows, D_MODEL), BF16),
        compiler_params=_params(("arbitrary", "arbitrary")),
        name="merge",
    )(a_pre, o_na, w_gla_o, w_na_o, p, p)


def _outproj_kernel(m_ref, w_ref, x_ref, mod_ref, g_ref, o_ref):
    y = jnp.dot(m_ref[...], w_ref[...], preferred_element_type=F32)
    o_ref[...] = x_ref[...] + mod_ref[5:6, :] * _rms(y, g_ref[3:4, :])


def _outproj(m, w_out, h1, mod, norm_g):
    n_tiles = m.shape[0] // ROW_TILE
    tiles_per_batch = n_tiles // 2
    return pl.pallas_call(
        _outproj_kernel,
        grid=(n_tiles,),
        in_specs=[pl.BlockSpec((ROW_TILE, D_MODEL), lambda i: (i, 0)),
                  pl.BlockSpec((D_MODEL, D_MODEL), lambda i: (0, 0)),
                  pl.BlockSpec((ROW_TILE, D_MODEL), lambda i: (i, 0)),
                  pl.BlockSpec((None, N_MOD, D_MODEL), lambda i: (i // tiles_per_batch, 0, 0)),
                  pl.BlockSpec((6, D_MODEL), lambda i: (0, 0))],
        out_specs=pl.BlockSpec((ROW_TILE, D_MODEL), lambda i: (i, 0)),
        out_shape=jax.ShapeDtypeStruct((m.shape[0], D_MODEL), F32),
        compiler_params=_params(("arbitrary",)),
        name="outproj",
    )(m, w_out, h1, mod, norm_g)


def kernel(x, c, ctx, c_ctx, w_ada, b_ada, norm_g, ffn_wg, ffn_wu, ffn_wd, w_in, gla_wg, gla_bg, gla_norm_g,
           w_gla_o, na_rpb, w_na_o, w_out):
    batch, seq, d = x.shape
    ctx_len = ctx.shape[1]
    depth = w_ada.shape[0]
    assert d == D_MODEL and batch == 2 and batch * ctx_len == ROW_TILE and depth == 1
    assert seq % ROW_TILE == 0 and seq // GRID_W == GRID_W

    qk_w = GLA_HEADS * GLA_DK
    v_w = GLA_HEADS * GLA_DV
    na_w = NA_HEADS * NA_DH
    gate0 = 2 * qk_w + 2 * v_w
    gate1 = gate0 + 2 * GLA_GATE_RANK
    col_nq = gate0
    col_m1 = col_nq + 3 * na_w

    cvec = jnp.zeros((8, d), F32).at[0:batch].set(c).at[batch].set(c_ctx)
    cos, sin = _rope_tables(seq)
    h = x.reshape(batch * seq, d)
    hc = ctx.reshape(batch * ctx_len, d)
    for l in range(depth):
        mod = _modulation(cvec, w_ada[l], b_ada[l].reshape(1, -1)).reshape(8, N_MOD, d)
        g = norm_g[l]
        h1 = _ffn(h, hc, mod, g, ffn_wg[l, 0].astype(BF16), ffn_wu[l, 0].astype(BF16),
                  ffn_wd[l, 0].astype(BF16), 0)
        w_main = jnp.concatenate([w_in[l][:, :gate0], w_in[l][:, gate1:]], axis=1).astype(BF16)
        w_gate = jnp.pad(w_in[l][:, gate0:gate1], ((0, 0), (0, LANES - 2 * GLA_GATE_RANK))).astype(BF16)
        p, gin = _inproj(h1, mod, g, w_main, w_gate, cos, sin, seq)
        wg_pad = jnp.zeros((2, LANES, qk_w), F32)
        wg_pad = wg_pad.at[0, :GLA_GATE_RANK].set(gla_wg[l, 0])
        wg_pad = wg_pad.at[1, GLA_GATE_RANK:2 * GLA_GATE_RANK].set(gla_wg[l, 1])
        bg = gla_bg[l].reshape(2, 1, qk_w)
        gn = gla_norm_g[l].reshape(1, GLA_DV)
        o_bwd = _gla(p, gin, wg_pad, bg, None, None, batch, seq, forward=False)
        a_pre = _gla(p, gin, wg_pad, bg, o_bwd, gn, batch, seq, forward=True)
        o_na = _na(p, _na_bias_tables(na_rpb[l]), batch, seq, ctx_len, col_nq)
        m = _merge(a_pre, o_na, w_gla_o[l].astype(BF16), w_na_o[l].astype(BF16), p, col_m1)
        h2 = _outproj(m, w_out[l].astype(BF16), h1, mod, g)
        h = _ffn(h2, None, mod, g, ffn_wg[l, 1].astype(BF16), ffn_wu[l, 1].astype(BF16),
                 ffn_wd[l, 1].astype(BF16), 2)
    return h.reshape(batch, seq, d)
```

```python
import functools

import numpy as np
import jax
import jax.numpy as jnp
from jax import lax
from jax.experimental import pallas as pl
from jax.experimental.pallas import tpu as pltpu

F32 = jnp.float32
BF16 = jnp.bfloat16

D_MODEL = 2048
GRID_W = 64
GLA_HEADS = 4
GLA_DK = 256
GLA_DV = 512
GLA_GATE_RANK = 16
GLA_GATE_TEMP = 16.0
NA_HEADS = 16
NA_DH = 64
NA_KR = 8
NA_KC = 16
ROPE_THETA = 10000.0
EPS = 1e-6
NEG_INF = -1e30
N_MOD = 9
LOG2E = 1.4426950408889634

LANES = 128
ROW_TILE = 512
FF_TILE = 512
PROJ_TILE = 512
GLA_BLOCK = 256
GLA_CHUNK = 128
NA_QROWS = 8
NA_WIN_ROWS = 16
VMEM_LIMIT = 56 * 1024 * 1024


def _params(sem):
    return pltpu.CompilerParams(dimension_semantics=sem, vmem_limit_bytes=VMEM_LIMIT)


def _rms(x, g):
    return x * lax.rsqrt(jnp.mean(x * x, axis=-1, keepdims=True) + EPS) * g


def _silu(x):
    return x * jax.nn.sigmoid(x)


def _mod_kernel(c_ref, w_ref, b_ref, o_ref):
    o_ref[...] = jnp.dot(_silu(c_ref[...]), w_ref[...], preferred_element_type=F32) + b_ref[...]


def _modulation(cvec, w_ada, b_ada):
    n = w_ada.shape[1]
    tn = 1024
    return pl.pallas_call(
        _mod_kernel,
        grid=(n // tn,),
        in_specs=[pl.BlockSpec((8, D_MODEL), lambda j: (0, 0)),
                  pl.BlockSpec((D_MODEL, tn), lambda j: (0, j)),
                  pl.BlockSpec((1, tn), lambda j: (0, j))],
        out_specs=pl.BlockSpec((8, tn), lambda j: (0, j)),
        out_shape=jax.ShapeDtypeStruct((8, n), F32),
        compiler_params=_params(("arbitrary",)),
        name="adaln_mod",
    )(cvec, w_ada, b_ada)


def _ffn_kernel(*refs, sub, n_lat_tiles, has_ctx):
    if has_ctx:
        x_ref, c_ref, mod_ref, g_ref, wg_ref, wu_ref, wd_ref, o_ref, h_ref = refs
    else:
        x_ref, mod_ref, g_ref, wg_ref, wu_ref, wd_ref, o_ref, h_ref = refs
        c_ref = None
    i = pl.program_id(0)
    f = pl.program_id(1)

    def prologue(src_ref):
        shift = mod_ref[3 * sub:3 * sub + 1, :]
        scale = mod_ref[3 * sub + 1:3 * sub + 2, :]
        h = _rms(src_ref[...], g_ref[2 * sub:2 * sub + 1, :]) * (1.0 + scale) + shift
        h_ref[...] = h.astype(BF16)
        o_ref[...] = jnp.zeros_like(o_ref)

    def epilogue(src_ref):
        gate = mod_ref[3 * sub + 2:3 * sub + 3, :]
        y = _rms(o_ref[...], g_ref[2 * sub + 1:2 * sub + 2, :])
        o_ref[...] = src_ref[...] + 0.5 * gate * y

    def on_rows(fn):
        if has_ctx:
            pl.when(i < n_lat_tiles)(lambda: fn(x_ref))
            pl.when(i >= n_lat_tiles)(lambda: fn(c_ref))
        else:
            fn(x_ref)

    pl.when(f == 0)(lambda: on_rows(prologue))
    h = h_ref[...]
    g = jnp.dot(h, wg_ref[...], preferred_element_type=F32)
    u = jnp.dot(h, wu_ref[...], preferred_element_type=F32)
    a = (_silu(g) * u).astype(BF16)
    o_ref[...] += jnp.dot(a, wd_ref[...], preferred_element_type=F32)
    pl.when(f == pl.num_programs(1) - 1)(lambda: on_rows(epilogue))


def _ffn(x2d, ctx2d, mod, norm_g, wg, wu, wd, sub):
    n_lat_tiles = x2d.shape[0] // ROW_TILE
    has_ctx = ctx2d is not None
    n_tiles = n_lat_tiles + (1 if has_ctx else 0)
    tiles_per_batch = n_lat_tiles // 2
    d_ff = wg.shape[1]
    row_spec = lambda shape: pl.BlockSpec(shape, lambda i, f: (jnp.minimum(i, n_lat_tiles - 1), 0))
    in_specs = [row_spec((ROW_TILE, D_MODEL))]
    args = [x2d]
    if has_ctx:
        in_specs.append(pl.BlockSpec((ROW_TILE, D_MODEL), lambda i, f: (0, 0)))
        args.append(ctx2d)
    in_specs += [
        pl.BlockSpec((None, N_MOD, D_MODEL), lambda i, f: (jnp.minimum(i // tiles_per_batch, 2), 0, 0)),
        pl.BlockSpec((6, D_MODEL), lambda i, f: (0, 0)),
        pl.BlockSpec((D_MODEL, FF_TILE), lambda i, f: (0, f)),
        pl.BlockSpec((D_MODEL, FF_TILE), lambda i, f: (0, f)),
        pl.BlockSpec((FF_TILE, D_MODEL), lambda i, f: (f, 0)),
    ]
    args += [mod, norm_g, wg, wu, wd]
    return pl.pallas_call(
        functools.partial(_ffn_kernel, sub=sub, n_lat_tiles=n_lat_tiles, has_ctx=has_ctx),
        grid=(n_tiles, d_ff // FF_TILE),
        in_specs=in_specs,
        out_specs=pl.BlockSpec((ROW_TILE, D_MODEL), lambda i, f: (i, 0)),
        out_shape=jax.ShapeDtypeStruct((n_tiles * ROW_TILE, D_MODEL), F32),
        scratch_shapes=[pltpu.VMEM((ROW_TILE, D_MODEL), BF16)],
        compiler_params=_params(("arbitrary", "arbitrary")),
        name="ffn%d" % sub,
    )(*args)


N_ROPE_TILES = 2 * GLA_HEADS * GLA_DK // PROJ_TILE
N_Q_TILES = N_ROPE_TILES // 2


def _rope_tables(seq):
    half = GLA_DK // 4
    freqs = ROPE_THETA ** (-np.arange(half, dtype=np.float64) / half)
    t = np.arange(seq)
    cos_parts, sin_parts = [], []
    for pos in (t // GRID_W, t % GRID_W):
        ang = pos[:, None].astype(np.float64) * freqs
        cos_parts += [np.cos(ang), np.cos(ang)]
        sin_parts += [-np.sin(ang), np.sin(ang)]
    cos = np.concatenate(cos_parts, axis=1)
    sin = np.concatenate(sin_parts, axis=1)
    cos = np.concatenate([cos, np.ones((ROW_TILE, GLA_DK))], axis=0)
    sin = np.concatenate([sin, np.zeros((ROW_TILE, GLA_DK))], axis=0)
    return jnp.asarray(cos, F32), jnp.asarray(sin, F32)


def _inproj_kernel(x_ref, mod_ref, g_ref, w_ref, wgate_ref, cos_ref, sin_ref, p_ref, gate_ref, h_ref):
    n = pl.program_id(1)

    @pl.when(n == 0)
    def _():
        h = _rms(x_ref[...], g_ref[2:3, :]) * (1.0 + mod_ref[4:5, :]) + mod_ref[3:4, :]
        h_ref[...] = h.astype(BF16)
        gate_ref[...] = jnp.dot(h_ref[...], wgate_ref[...], preferred_element_type=F32)

    y = jnp.dot(h_ref[...], w_ref[...], preferred_element_type=F32)

    @pl.when(n < N_ROPE_TILES)
    def _():
        cos = cos_ref[...]
        sin = sin_ref[...]
        qscale = jnp.where(n < N_Q_TILES, GLA_DK ** -0.5, 1.0).astype(F32)
        for j in range(PROJ_TILE // GLA_DK):
            yj = y[:, j * GLA_DK:(j + 1) * GLA_DK]
            swapped = jnp.concatenate(
                [pltpu.roll(yj[:, g * LANES:(g + 1) * LANES], LANES // 2, 1) for g in range(GLA_DK // LANES)],
                axis=1)
            p_ref[:, j * GLA_DK:(j + 1) * GLA_DK] = ((yj * cos + swapped * sin) * qscale).astype(BF16)

    @pl.when(n >= N_ROPE_TILES)
    def _():
        p_ref[...] = y.astype(BF16)


def _inproj(h1, mod, norm_g, w_main, w_gate, cos, sin, seq):
    n_tiles = h1.shape[0] // ROW_TILE
    n_lat_tiles = n_tiles - 1
    tiles_per_batch = n_lat_tiles // 2
    width = w_main.shape[1]
    tab_spec = pl.BlockSpec(
        (ROW_TILE, GLA_DK),
        lambda i, n: (jnp.where(i < n_lat_tiles, i % tiles_per_batch, seq // ROW_TILE), 0))
    return pl.pallas_call(
        _inproj_kernel,
        grid=(n_tiles, width // PROJ_TILE),
        in_specs=[pl.BlockSpec((ROW_TILE, D_MODEL), lambda i, n: (i, 0)),
                  pl.BlockSpec((None, N_MOD, D_MODEL), lambda i, n: (jnp.minimum(i // tiles_per_batch, 2), 0, 0)),
                  pl.BlockSpec((6, D_MODEL), lambda i, n: (0, 0)),
                  pl.BlockSpec((D_MODEL, PROJ_TILE), lambda i, n: (0, n)),
                  pl.BlockSpec((D_MODEL, LANES), lambda i, n: (0, 0)),
                  tab_spec, tab_spec],
        out_specs=[pl.BlockSpec((ROW_TILE, PROJ_TILE), lambda i, n: (i, n)),
                   pl.BlockSpec((ROW_TILE, LANES), lambda i, n: (i, 0))],
        out_shape=[jax.ShapeDtypeStruct((h1.shape[0], width), BF16),
                   jax.ShapeDtypeStruct((h1.shape[0], LANES), F32)],
        scratch_shapes=[pltpu.VMEM((ROW_TILE, D_MODEL), BF16)],
        compiler_params=_params(("arbitrary", "arbitrary")),
        name="inproj",
    )(h1, mod, norm_g, w_main, w_gate, cos, sin)


def _log_sigmoid(z):
    return jnp.minimum(z, 0.0) - jnp.log1p(jnp.exp(-jnp.abs(z)))


def _gla_scan_block(q_ref, k_ref, v_ref, gin_ref, wg, bg, st, forward):
    c = GLA_CHUNK
    row = lax.broadcasted_iota(jnp.int32, (c, c), 0)
    col = lax.broadcasted_iota(jnp.int32, (c, c), 1)
    mask = (col <= row) if forward else (col >= row)
    tri = jnp.where(mask, 1.0, 0.0).astype(BF16)
    nt = (((1,), (1,)), ((), ()))
    z = jnp.dot(gin_ref[...], wg, preferred_element_type=F32) + bg
    log_a = _log_sigmoid(z) * (LOG2E / GLA_GATE_TEMP)
    n_chunks = GLA_BLOCK // c
    outs = [None] * n_chunks
    for ci in (range(n_chunks) if forward else range(n_chunks - 1, -1, -1)):
        rows = slice(ci * c, (ci + 1) * c)
        q = q_ref[rows, :].astype(F32)
        k = k_ref[rows, :].astype(F32)
        v = v_ref[rows, :]
        la = log_a[rows, :]
        hi = la.astype(BF16)
        lo = (la - hi.astype(F32)).astype(BF16)
        cum = jnp.dot(tri, hi, preferred_element_type=F32) + jnp.dot(tri, lo, preferred_element_type=F32)
        cum_end = cum[c - 1:c, :] if forward else cum[0:1, :]
        q_dec = (q * jnp.exp2(cum)).astype(BF16)
        k_inv = (k * jnp.exp2(-cum)).astype(BF16)
        k_end = (k * jnp.exp2(cum_end - cum)).astype(BF16)
        att = lax.dot_general(q_dec, k_inv, nt, preferred_element_type=F32)
        att = jnp.where(mask, att, 0.0).astype(BF16)
        outs[ci] = (jnp.dot(att, v, preferred_element_type=F32)
                    + lax.dot_general(q_dec, st.astype(BF16), nt, preferred_element_type=F32))
        kv_t = lax.dot_general(v, k_end, (((0,), (0,)), ((), ())), preferred_element_type=F32)
        st = jnp.exp2(cum_end) * st + kv_t
    return jnp.concatenate(outs, axis=0), st


def _gla_kernel(qf_ref, kf_ref, vf_ref, gf_ref, qb_ref, kb_ref, vb_ref, gb_ref, wg_ref, bg_ref,
                of_ref, ob_ref, stf_ref, stb_ref):
    @pl.when(pl.program_id(2) == 0)
    def _():
        stf_ref[...] = jnp.zeros_like(stf_ref)
        stb_ref[...] = jnp.zeros_like(stb_ref)

    of, stf = _gla_scan_block(qf_ref, kf_ref, vf_ref, gf_ref, wg_ref[0], bg_ref[0], stf_ref[...], True)
    ob, stb = _gla_scan_block(qb_ref, kb_ref, vb_ref, gb_ref, wg_ref[1], bg_ref[1], stb_ref[...], False)
    of_ref[...] = of.astype(of_ref.dtype)
    ob_ref[...] = ob.astype(ob_ref.dtype)
    stf_ref[...] = stf
    stb_ref[...] = stb


def _gla(p, gin, wg_pad, bg, batch, seq):
    n_lat = seq // GLA_BLOCK
    ctx_block0 = batch * n_lat

    def lat_block(b, s, forward):
        j = jnp.maximum(s - 1, 0)
        return b * n_lat + (j if forward else n_lat - 1 - j)

    def in_block(b, s, forward):
        return jnp.where(s == 0, ctx_block0 + b, lat_block(b, s, forward))

    kq = GLA_HEADS
    kv = 2 * GLA_HEADS * GLA_DK // GLA_DV

    def dir_specs(forward):
        return [
            pl.BlockSpec((GLA_BLOCK, GLA_DK), lambda b, h, s: (in_block(b, s, forward), h)),
            pl.BlockSpec((GLA_BLOCK, GLA_DK), lambda b, h, s: (in_block(b, s, forward), kq + h)),
            pl.BlockSpec((GLA_BLOCK, GLA_DV), lambda b, h, s: (in_block(b, s, forward), kv + h)),
            pl.BlockSpec((GLA_BLOCK, LANES), lambda b, h, s: (in_block(b, s, forward), 0)),
        ]

    out_shape = jax.ShapeDtypeStruct((batch * seq, GLA_HEADS * GLA_DV), BF16)
    return pl.pallas_call(
        _gla_kernel,
        grid=(batch, GLA_HEADS, n_lat + 1),
        in_specs=dir_specs(True) + dir_specs(False) + [
            pl.BlockSpec((2, LANES, GLA_DK), lambda b, h, s: (0, 0, h)),
            pl.BlockSpec((2, 1, GLA_DK), lambda b, h, s: (0, 0, h))],
        out_specs=[pl.BlockSpec((GLA_BLOCK, GLA_DV), lambda b, h, s: (lat_block(b, s, True), h)),
                   pl.BlockSpec((GLA_BLOCK, GLA_DV), lambda b, h, s: (lat_block(b, s, False), h))],
        out_shape=[out_shape, out_shape],
        scratch_shapes=[pltpu.VMEM((GLA_DV, GLA_DK), F32), pltpu.VMEM((GLA_DV, GLA_DK), F32)],
        compiler_params=_params(("arbitrary", "arbitrary", "arbitrary")),
        name="gla",
    )(p, p, p, gin, p, p, p, gin, wg_pad, bg)


NA_QTOK = NA_QROWS * GRID_W
NA_WIN = NA_WIN_ROWS * GRID_W
NA_BIAS_PAD = NA_QROWS * GRID_W
NA_BIAS_W = 2048
NA_KINDS = ((0, 0), (NA_QROWS, NA_QROWS - NA_KR // 2), (GRID_W - NA_QROWS, GRID_W - NA_WIN_ROWS))


def _na_bias_tables(rpb):
    qc = np.arange(GRID_W)
    dj = np.clip(qc[None, :] - qc[:, None], -(NA_KC - 1), NA_KC - 1) + NA_KC - 1
    t = rpb[:, :, dj]
    t = jnp.transpose(t, (0, 2, 1, 3)).reshape(rpb.shape[0], GRID_W, (2 * NA_KR - 1) * GRID_W)
    right = NA_BIAS_W + GRID_W - NA_BIAS_PAD - t.shape[-1]
    t = jnp.pad(t, ((0, 0), (0, 0), (NA_BIAS_PAD, right)))
    return jnp.stack([t[..., :NA_BIAS_W], t[..., GRID_W:GRID_W + NA_BIAS_W]], axis=1)


def _na_kernel(q_ref, k_ref, v_ref, kc_ref, vc_ref, ce_ref, o_ref, t_ref, *, rows):
    b = pl.program_id(1)
    blk = pl.program_id(2)
    n_blk = rows // NA_QROWS

    @pl.when((b == 0) & (blk == 0))
    def _():
        kk = lax.broadcasted_iota(jnp.int32, (GRID_W, NA_WIN), 1)
        k_row = kk >> 6
        k_col = kk & (GRID_W - 1)
        q_col = lax.broadcasted_iota(jnp.int32, (GRID_W, NA_WIN), 0)
        c0 = jnp.clip(q_col - NA_KC // 2, 0, GRID_W - NA_KC)
        col_ok = (k_col >= c0) & (k_col < c0 + NA_KC)
        for kind, (r0, w0) in enumerate(NA_KINDS):
            for qr in range(NA_QROWS):
                band0 = min(max(r0 + qr - NA_KR // 2, 0), rows - NA_KR) - w0
                ok = col_ok & (k_row >= band0) & (k_row < band0 + NA_KR)
                a0 = w0 - (r0 + qr) + NA_KR - 1
                lane0 = NA_BIAS_PAD + (a0 - a0 % 2) * GRID_W
                for hh in range(2):
                    bias = ce_ref[hh, a0 % 2, :, lane0:lane0 + NA_WIN]
                    t_ref[hh, kind, qr * GRID_W:(qr + 1) * GRID_W, :] = jnp.where(ok, bias, NEG_INF)

    kind = jnp.where(blk == 0, 0, jnp.where(blk == n_blk - 1, 2, 1))
    w0 = jnp.clip(blk * NA_QROWS - NA_KR // 2, 0, rows - NA_WIN_ROWS)
    win = pl.ds(pl.multiple_of(w0 * GRID_W, 256), NA_WIN)
    q = q_ref[...]
    kw = k_ref[win, :]
    vw = v_ref[win, :]
    kc = kc_ref[...]
    vc = vc_ref[...]
    lane = lax.broadcasted_iota(jnp.int32, q.shape, 1)
    scale = NA_DH ** -0.5
    nt = (((1,), (1,)), ((), ()))
    outs = []
    for hh in range(2):
        own = (lane < NA_DH) if hh == 0 else (lane >= NA_DH)
        qh = jnp.where(own, q, jnp.zeros_like(q))
        s_lat = lax.dot_general(qh, kw, nt, preferred_element_type=F32) * scale + t_ref[hh, kind]
        s_ctx = lax.dot_general(qh, kc, nt, preferred_element_type=F32) * scale
        m = jnp.maximum(jnp.max(s_lat, axis=-1, keepdims=True), jnp.max(s_ctx, axis=-1, keepdims=True))
        p_lat = jnp.exp(s_lat - m)
        p_ctx = jnp.exp(s_ctx - m)
        denom = jnp.sum(p_lat, axis=-1, keepdims=True) + jnp.sum(p_ctx, axis=-1, keepdims=True)
        o = (jnp.dot(p_lat.astype(BF16), vw, preferred_element_type=F32)
             + jnp.dot(p_ctx.astype(BF16), vc, preferred_element_type=F32))
        outs.append(o / denom)
    o_ref[...] = jnp.where(lane < NA_DH, outs[0], outs[1]).astype(o_ref.dtype)


def _na(p, ce, batch, seq, ctx_len, col0):
    rows = seq // GRID_W
    n_blk = rows // NA_QROWS
    n_pairs = NA_HEADS // 2
    cq = col0 // LANES
    ck = cq + n_pairs
    cv = ck + n_pairs
    ctx_block0 = batch * seq // ctx_len
    return pl.pallas_call(
        functools.partial(_na_kernel, rows=rows),
        grid=(n_pairs, batch, n_blk),
        in_specs=[
            pl.BlockSpec((NA_QTOK, LANES), lambda hp, b, r: (b * n_blk + r, cq + hp)),
            pl.BlockSpec((seq, LANES), lambda hp, b, r: (b, ck + hp)),
            pl.BlockSpec((seq, LANES), lambda hp, b, r: (b, cv + hp)),
            pl.BlockSpec((ctx_len, LANES), lambda hp, b, r: (ctx_block0 + b, ck + hp)),
            pl.BlockSpec((ctx_len, LANES), lambda hp, b, r: (ctx_block0 + b, cv + hp)),
            pl.BlockSpec((2, 2, GRID_W, NA_BIAS_W), lambda hp, b, r: (hp, 0, 0, 0)),
        ],
        out_specs=pl.BlockSpec((NA_QTOK, LANES), lambda hp, b, r: (b * n_blk + r, hp)),
        out_shape=jax.ShapeDtypeStruct((batch * seq, NA_HEADS * NA_DH), BF16),
        scratch_shapes=[pltpu.VMEM((2, len(NA_KINDS), NA_QTOK, NA_WIN), F32)],
        compiler_params=_params(("arbitrary", "arbitrary", "arbitrary")),
        name="natten",
    )(p, p, p, p, p, ce)


def _merge_kernel(of_ref, ob_ref, r_ref, gn_ref, b_ref, wa_ref, wb_ref, m1_ref, m2_ref, o_ref, a_ref):
    @pl.when(pl.program_id(1) == 0)
    def _():
        for h in range(GLA_HEADS):
            cols = slice(h * GLA_DV, (h + 1) * GLA_DV)
            tot = of_ref[:, cols].astype(F32) + ob_ref[:, cols].astype(F32)
            a_ref[:, cols] = (_rms(tot, gn_ref[...]) * _silu(r_ref[:, cols].astype(F32))).astype(BF16)

    a = jnp.dot(a_ref[...], wa_ref[...], preferred_element_type=F32)
    b = jnp.dot(b_ref[...], wb_ref[...], preferred_element_type=F32)
    m = jax.nn.sigmoid(m1_ref[...].astype(F32)) * a + jax.nn.sigmoid(m2_ref[...].astype(F32)) * b
    o_ref[...] = m.astype(o_ref.dtype)


def _merge(o_fwd, o_bwd, gn, o_na, w_gla_o, w_na_o, p, col_r, col_m1):
    n_rows = o_fwd.shape[0]
    v_w = o_fwd.shape[1]
    tn = PROJ_TILE
    c1 = col_m1 // tn
    c2 = c1 + D_MODEL // tn
    return pl.pallas_call(
        _merge_kernel,
        grid=(n_rows // ROW_TILE, D_MODEL // tn),
        in_specs=[pl.BlockSpec((ROW_TILE, v_w), lambda i, n: (i, 0)),
                  pl.BlockSpec((ROW_TILE, v_w), lambda i, n: (i, 0)),
                  pl.BlockSpec((ROW_TILE, v_w), lambda i, n: (i, col_r // v_w)),
                  pl.BlockSpec((1, GLA_DV), lambda i, n: (0, 0)),
                  pl.BlockSpec((ROW_TILE, o_na.shape[1]), lambda i, n: (i, 0)),
                  pl.BlockSpec((w_gla_o.shape[0], tn), lambda i, n: (0, n)),
                  pl.BlockSpec((w_na_o.shape[0], tn), lambda i, n: (0, n)),
                  pl.BlockSpec((ROW_TILE, tn), lambda i, n: (i, c1 + n)),
                  pl.BlockSpec((ROW_TILE, tn), lambda i, n: (i, c2 + n))],
        out_specs=pl.BlockSpec((ROW_TILE, tn), lambda i, n: (i, n)),
        out_shape=jax.ShapeDtypeStruct((n_rows, D_MODEL), BF16),
        scratch_shapes=[pltpu.VMEM((ROW_TILE, v_w), BF16)],
        compiler_params=_params(("arbitrary", "arbitrary")),
        name="merge",
    )(o_fwd, o_bwd, p, gn, o_na, w_gla_o, w_na_o, p, p)


def _outproj_kernel(m_ref, w_ref, x_ref, mod_ref, g_ref, o_ref):
    y = jnp.dot(m_ref[...], w_ref[...], preferred_element_type=F32)
    o_ref[...] = x_ref[...] + mod_ref[5:6, :] * _rms(y, g_ref[3:4, :])


def _outproj(m, w_out, h1, mod, norm_g):
    n_tiles = m.shape[0] // ROW_TILE
    tiles_per_batch = n_tiles // 2
    return pl.pallas_call(
        _outproj_kernel,
        grid=(n_tiles,),
        in_specs=[pl.BlockSpec((ROW_TILE, D_MODEL), lambda i: (i, 0)),
                  pl.BlockSpec((D_MODEL, D_MODEL), lambda i: (0, 0)),
                  pl.BlockSpec((ROW_TILE, D_MODEL), lambda i: (i, 0)),
                  pl.BlockSpec((None, N_MOD, D_MODEL), lambda i: (i // tiles_per_batch, 0, 0)),
                  pl.BlockSpec((6, D_MODEL), lambda i: (0, 0))],
        out_specs=pl.BlockSpec((ROW_TILE, D_MODEL), lambda i: (i, 0)),
        out_shape=jax.ShapeDtypeStruct((m.shape[0], D_MODEL), F32),
        compiler_params=_params(("arbitrary",)),
        name="outproj",
    )(m, w_out, h1, mod, norm_g)


def kernel(x, c, ctx, c_ctx, w_ada, b_ada, norm_g, ffn_wg, ffn_wu, ffn_wd, w_in, gla_wg, gla_bg, gla_norm_g,
           w_gla_o, na_rpb, w_na_o, w_out):
    batch, seq, d = x.shape
    ctx_len = ctx.shape[1]
    depth = w_ada.shape[0]
    assert d == D_MODEL and batch == 2 and batch * ctx_len == ROW_TILE and depth == 1
    assert seq % ROW_TILE == 0 and seq // GRID_W == GRID_W

    qk_w = GLA_HEADS * GLA_DK
    v_w = GLA_HEADS * GLA_DV
    na_w = NA_HEADS * NA_DH
    gate0 = 2 * qk_w + 2 * v_w
    gate1 = gate0 + 2 * GLA_GATE_RANK
    col_nq = gate0
    col_m1 = col_nq + 3 * na_w

    cvec = jnp.zeros((8, d), F32).at[0:batch].set(c).at[batch].set(c_ctx)
    cos, sin = _rope_tables(seq)
    h = x.reshape(batch * seq, d)
    hc = ctx.reshape(batch * ctx_len, d)
    for l in range(depth):
        mod = _modulation(cvec, w_ada[l], b_ada[l].reshape(1, -1)).reshape(8, N_MOD, d)
        g = norm_g[l]
        h1 = _ffn(h, hc, mod, g, ffn_wg[l, 0].astype(BF16), ffn_wu[l, 0].astype(BF16),
                  ffn_wd[l, 0].astype(BF16), 0)
        w_main = jnp.concatenate([w_in[l][:, :gate0], w_in[l][:, gate1:]], axis=1).astype(BF16)
        w_gate = jnp.pad(w_in[l][:, gate0:gate1], ((0, 0), (0, LANES - 2 * GLA_GATE_RANK))).astype(BF16)
        p, gin = _inproj(h1, mod, g, w_main, w_gate, cos, sin, seq)
        wg_pad = jnp.zeros((2, LANES, qk_w), F32)
        wg_pad = wg_pad.at[0, :GLA_GATE_RANK].set(gla_wg[l, 0])
        wg_pad = wg_pad.at[1, GLA_GATE_RANK:2 * GLA_GATE_RANK].set(gla_wg[l, 1])
        bg = gla_bg[l].reshape(2, 1, qk_w)
        gn = gla_norm_g[l].reshape(1, GLA_DV)
        o_fwd, o_bwd = _gla(p, gin, wg_pad, bg, batch, seq)
        o_na = _na(p, _na_bias_tables(na_rpb[l]), batch, seq, ctx_len, col_nq)
        m = _merge(o_fwd, o_bwd, gn, o_na, w_gla_o[l].astype(BF16), w_na_o[l].astype(BF16), p,
                   2 * qk_w + v_w, col_m1)
        h2 = _outproj(m, w_out[l].astype(BF16), h1, mod, g)
        h = _ffn(h2, None, mod, g, ffn_wg[l, 1].astype(BF16), ffn_wu[l, 1].astype(BF16),
                 ffn_wd[l, 1].astype(BF16), 2)
    return h.reshape(batch, seq, d)
```

```python
import functools

import numpy as np
import jax
import jax.numpy as jnp
from jax import lax
from jax.experimental import pallas as pl
from jax.experimental.pallas import tpu as pltpu

F32 = jnp.float32
BF16 = jnp.bfloat16

D_MODEL = 2048
GRID_W = 64
GLA_HEADS = 4
GLA_DK = 256
GLA_DV = 512
GLA_GATE_RANK = 16
GLA_GATE_TEMP = 16.0
NA_HEADS = 16
NA_DH = 64
NA_KR = 8
NA_KC = 16
ROPE_THETA = 10000.0
EPS = 1e-6
NEG_INF = -1e30
N_MOD = 9
LOG2E = 1.4426950408889634

LANES = 128
ROW_TILE = 512
BIG_ROW_TILE = 1024
FF_TILE = 512
FF_TILE_BIG = 256
PROJ_TILE = 512
GLA_BLOCK = 256
GLA_CHUNK = 128
NA_QROWS = 8
NA_WIN_ROWS = 16
VMEM_LIMIT = 56 * 1024 * 1024


def _params(sem):
    return pltpu.CompilerParams(dimension_semantics=sem, vmem_limit_bytes=VMEM_LIMIT)


def _rms(x, g):
    return x * lax.rsqrt(jnp.mean(x * x, axis=-1, keepdims=True) + EPS) * g


def _silu(x):
    return x * jax.nn.sigmoid(x)


def _mod_kernel(c_ref, w_ref, b_ref, o_ref):
    o_ref[...] = jnp.dot(_silu(c_ref[...]), w_ref[...], preferred_element_type=F32) + b_ref[...]


def _modulation(cvec, w_ada, b_ada):
    n = w_ada.shape[1]
    tn = 1024
    return pl.pallas_call(
        _mod_kernel,
        grid=(n // tn,),
        in_specs=[pl.BlockSpec((8, D_MODEL), lambda j: (0, 0)),
                  pl.BlockSpec((D_MODEL, tn), lambda j: (0, j)),
                  pl.BlockSpec((1, tn), lambda j: (0, j))],
        out_specs=pl.BlockSpec((8, tn), lambda j: (0, j)),
        out_shape=jax.ShapeDtypeStruct((8, n), F32),
        compiler_params=_params(("arbitrary",)),
        name="adaln_mod",
    )(cvec, w_ada, b_ada)


def _mixed_dot(a, w):
    return lax.dot_general(a, w, (((1,), (0,)), ((), ())), preferred_element_type=F32)


def _ffn_kernel(x_ref, mod_ref, g_ref, wg_ref, wu_ref, wd_ref, *rest, sub):
    o_ref, h_ref = rest[-2:]
    f = pl.program_id(1)

    @pl.when(f == 0)
    def _():
        shift = mod_ref[3 * sub:3 * sub + 1, :]
        scale = mod_ref[3 * sub + 1:3 * sub + 2, :]
        h = _rms(x_ref[...], g_ref[2 * sub:2 * sub + 1, :]) * (1.0 + scale) + shift
        h_ref[...] = h.astype(BF16)
        o_ref[...] = jnp.zeros_like(o_ref)

    h = h_ref[...]
    g = _mixed_dot(h, wg_ref[...])
    u = _mixed_dot(h, wu_ref[...])
    o_ref[...] += _mixed_dot((_silu(g) * u).astype(BF16), wd_ref[...])

    @pl.when(f == pl.num_programs(1) - 1)
    def _():
        gate = mod_ref[3 * sub + 2:3 * sub + 3, :]
        y = _rms(o_ref[...], g_ref[2 * sub + 1:2 * sub + 2, :])
        o_ref[...] = x_ref[...] + 0.5 * gate * y


def _ffn(x2d, mod, norm_g, wg, wu, wd, sub, *, tm, tf, mod_row, out_rows=None, out_tile0=0, into=None):
    n_tiles = x2d.shape[0] // tm
    out_rows = out_rows or x2d.shape[0]
    d_ff = wg.shape[1]
    in_specs = [
        pl.BlockSpec((tm, D_MODEL), lambda i, f: (i, 0), pipeline_mode=pl.Buffered(1)),
        pl.BlockSpec((None, N_MOD, D_MODEL), lambda i, f: (mod_row(i), 0, 0)),
        pl.BlockSpec((6, D_MODEL), lambda i, f: (0, 0)),
        pl.BlockSpec((D_MODEL, tf), lambda i, f: (0, f)),
        pl.BlockSpec((D_MODEL, tf), lambda i, f: (0, f)),
        pl.BlockSpec((tf, D_MODEL), lambda i, f: (f, 0)),
    ]
    args = [x2d, mod, norm_g, wg, wu, wd]
    aliases = {}
    if into is not None:
        in_specs.append(pl.BlockSpec(memory_space=pl.ANY))
        args.append(into)
        aliases = {len(args) - 1: 0}
    return pl.pallas_call(
        functools.partial(_ffn_kernel, sub=sub),
        grid=(n_tiles, d_ff // tf),
        in_specs=in_specs,
        out_specs=pl.BlockSpec((tm, D_MODEL), lambda i, f: (out_tile0 + i, 0)),
        out_shape=jax.ShapeDtypeStruct((out_rows, D_MODEL), F32),
        scratch_shapes=[pltpu.VMEM((tm, D_MODEL), BF16)],
        input_output_aliases=aliases,
        compiler_params=_params(("arbitrary", "arbitrary")),
        name="ffn%d%s" % (sub, "" if into is None else "_ctx"),
    )(*args)


N_ROPE_TILES = 2 * GLA_HEADS * GLA_DK // PROJ_TILE
N_Q_TILES = N_ROPE_TILES // 2


def _rope_tables(seq):
    half = GLA_DK // 4
    freqs = ROPE_THETA ** (-np.arange(half, dtype=np.float64) / half)
    t = np.arange(seq)
    cos_parts, sin_parts = [], []
    for pos in (t // GRID_W, t % GRID_W):
        ang = pos[:, None].astype(np.float64) * freqs
        cos_parts += [np.cos(ang), np.cos(ang)]
        sin_parts += [-np.sin(ang), np.sin(ang)]
    cos = np.concatenate(cos_parts, axis=1)
    sin = np.concatenate(sin_parts, axis=1)
    cos = np.concatenate([cos, np.ones((BIG_ROW_TILE, GLA_DK))], axis=0)
    sin = np.concatenate([sin, np.zeros((BIG_ROW_TILE, GLA_DK))], axis=0)
    return jnp.asarray(cos, F32), jnp.asarray(sin, F32)


def _inproj_kernel(x_ref, mod_ref, g_ref, wa_ref, wb_ref, wgate_ref, cos_ref, sin_ref, p_ref, gate_ref, h_ref,
                   *, n_a_tiles):
    n = pl.program_id(1)

    @pl.when(n == 0)
    def _():
        h = _rms(x_ref[...], g_ref[2:3, :]) * (1.0 + mod_ref[4:5, :]) + mod_ref[3:4, :]
        h_ref[...] = h.astype(BF16)
        gate_ref[...] = _mixed_dot(h_ref[...], wgate_ref[...])

    @pl.when(n < N_ROPE_TILES)
    def _():
        y = _mixed_dot(h_ref[...], wa_ref[...])
        cos = cos_ref[...]
        sin = sin_ref[...]
        qscale = jnp.where(n < N_Q_TILES, GLA_DK ** -0.5, 1.0).astype(F32)
        for j in range(PROJ_TILE // GLA_DK):
            yj = y[:, j * GLA_DK:(j + 1) * GLA_DK]
            swapped = jnp.concatenate(
                [pltpu.roll(yj[:, g * LANES:(g + 1) * LANES], LANES // 2, 1) for g in range(GLA_DK // LANES)],
                axis=1)
            p_ref[:, j * GLA_DK:(j + 1) * GLA_DK] = ((yj * cos + swapped * sin) * qscale).astype(BF16)

    @pl.when((n >= N_ROPE_TILES) & (n < n_a_tiles))
    def _():
        p_ref[...] = _mixed_dot(h_ref[...], wa_ref[...]).astype(BF16)

    @pl.when(n >= n_a_tiles)
    def _():
        p_ref[...] = _mixed_dot(h_ref[...], wb_ref[...]).astype(BF16)


def _inproj(h1, mod, norm_g, w_in, w_tail, w_gate, cos, sin, seq, n_a_cols):
    tm = BIG_ROW_TILE
    n_tiles = pl.cdiv(h1.shape[0], tm)
    n_lat_tiles = seq * 2 // tm
    tiles_per_batch = n_lat_tiles // 2
    n_a_tiles = n_a_cols // PROJ_TILE
    n_b_tiles = w_tail.shape[1] // PROJ_TILE
    width = n_a_cols + w_tail.shape[1]
    tab_spec = pl.BlockSpec(
        (tm, GLA_DK), lambda i, n: (jnp.where(i < n_lat_tiles, i % tiles_per_batch, seq // tm), 0))
    return pl.pallas_call(
        functools.partial(_inproj_kernel, n_a_tiles=n_a_tiles),
        grid=(n_tiles, n_a_tiles + n_b_tiles),
        in_specs=[pl.BlockSpec((tm, D_MODEL), lambda i, n: (i, 0)),
                  pl.BlockSpec((None, N_MOD, D_MODEL), lambda i, n: (jnp.minimum(i // tiles_per_batch, 2), 0, 0)),
                  pl.BlockSpec((6, D_MODEL), lambda i, n: (0, 0)),
                  pl.BlockSpec((D_MODEL, PROJ_TILE), lambda i, n: (0, jnp.minimum(n, n_a_tiles - 1))),
                  pl.BlockSpec((D_MODEL, PROJ_TILE), lambda i, n: (0, jnp.maximum(n - n_a_tiles, 0))),
                  pl.BlockSpec((D_MODEL, LANES), lambda i, n: (0, 0)),
                  tab_spec, tab_spec],
        out_specs=[pl.BlockSpec((tm, PROJ_TILE), lambda i, n: (i, n)),
                   pl.BlockSpec((tm, LANES), lambda i, n: (i, 0))],
        out_shape=[jax.ShapeDtypeStruct((h1.shape[0], width), BF16),
                   jax.ShapeDtypeStruct((h1.shape[0], LANES), F32)],
        scratch_shapes=[pltpu.VMEM((tm, D_MODEL), BF16)],
        compiler_params=_params(("arbitrary", "arbitrary")),
        name="inproj",
    )(h1, mod, norm_g, w_in, w_tail, w_gate, cos, sin)


def _log_sigmoid(z):
    return jnp.minimum(z, 0.0) - jnp.log1p(jnp.exp(-jnp.abs(z)))


def _gla_scan_block(q_ref, k_ref, v_ref, gin_ref, wg, bg, st, forward):
    c = GLA_CHUNK
    row = lax.broadcasted_iota(jnp.int32, (c, c), 0)
    col = lax.broadcasted_iota(jnp.int32, (c, c), 1)
    mask = (col <= row) if forward else (col >= row)
    tri = jnp.where(mask, 1.0, 0.0).astype(BF16)
    nt = (((1,), (1,)), ((), ()))
    z = jnp.dot(gin_ref[...], wg, preferred_element_type=F32) + bg
    log_a = _log_sigmoid(z) * (LOG2E / GLA_GATE_TEMP)
    n_chunks = GLA_BLOCK // c
    outs = [None] * n_chunks
    for ci in (range(n_chunks) if forward else range(n_chunks - 1, -1, -1)):
        rows = slice(ci * c, (ci + 1) * c)
        q = q_ref[rows, :].astype(F32)
        k = k_ref[rows, :].astype(F32)
        v = v_ref[rows, :]
        la = log_a[rows, :]
        hi = la.astype(BF16)
        lo = (la - hi.astype(F32)).astype(BF16)
        cum = jnp.dot(tri, hi, preferred_element_type=F32) + jnp.dot(tri, lo, preferred_element_type=F32)
        cum_end = cum[c - 1:c, :] if forward else cum[0:1, :]
        q_dec = (q * jnp.exp2(cum)).astype(BF16)
        k_inv = (k * jnp.exp2(-cum)).astype(BF16)
        k_end = (k * jnp.exp2(cum_end - cum)).astype(BF16)
        att = lax.dot_general(q_dec, k_inv, nt, preferred_element_type=F32)
        att = jnp.where(mask, att, 0.0).astype(BF16)
        outs[ci] = (jnp.dot(att, v, preferred_element_type=F32)
                    + lax.dot_general(q_dec, st.astype(BF16), nt, preferred_element_type=F32))
        kv_t = lax.dot_general(v, k_end, (((0,), (0,)), ((), ())), preferred_element_type=F32)
        st = jnp.exp2(cum_end) * st + kv_t
    return jnp.concatenate(outs, axis=0), st


def _gla_kernel(qf_ref, kf_ref, vf_ref, gf_ref, qb_ref, kb_ref, vb_ref, gb_ref, wg_ref, bg_ref,
                of_ref, ob_ref, stf_ref, stb_ref):
    @pl.when(pl.program_id(2) == 0)
    def _():
        stf_ref[...] = jnp.zeros_like(stf_ref)
        stb_ref[...] = jnp.zeros_like(stb_ref)

    of, stf = _gla_scan_block(qf_ref, kf_ref, vf_ref, gf_ref, wg_ref[0], bg_ref[0], stf_ref[...], True)
    ob, stb = _gla_scan_block(qb_ref, kb_ref, vb_ref, gb_ref, wg_ref[1], bg_ref[1], stb_ref[...], False)
    of_ref[...] = of.astype(of_ref.dtype)
    ob_ref[...] = ob.astype(ob_ref.dtype)
    stf_ref[...] = stf
    stb_ref[...] = stb


def _gla(p, gin, wg_pad, bg, batch, seq):
    n_lat = seq // GLA_BLOCK
    ctx_block0 = batch * n_lat

    def lat_block(b, s, forward):
        j = jnp.maximum(s - 1, 0)
        return b * n_lat + (j if forward else n_lat - 1 - j)

    def in_block(b, s, forward):
        return jnp.where(s == 0, ctx_block0 + b, lat_block(b, s, forward))

    kq = GLA_HEADS
    kv = 2 * GLA_HEADS * GLA_DK // GLA_DV

    def dir_specs(forward):
        return [
            pl.BlockSpec((GLA_BLOCK, GLA_DK), lambda b, h, s: (in_block(b, s, forward), h)),
            pl.BlockSpec((GLA_BLOCK, GLA_DK), lambda b, h, s: (in_block(b, s, forward), kq + h)),
            pl.BlockSpec((GLA_BLOCK, GLA_DV), lambda b, h, s: (in_block(b, s, forward), kv + h)),
            pl.BlockSpec((GLA_BLOCK, LANES), lambda b, h, s: (in_block(b, s, forward), 0)),
        ]

    out_shape = jax.ShapeDtypeStruct((batch * seq, GLA_HEADS * GLA_DV), BF16)
    return pl.pallas_call(
        _gla_kernel,
        grid=(batch, GLA_HEADS, n_lat + 1),
        in_specs=dir_specs(True) + dir_specs(False) + [
            pl.BlockSpec((2, LANES, GLA_DK), lambda b, h, s: (0, 0, h)),
            pl.BlockSpec((2, 1, GLA_DK), lambda b, h, s: (0, 0, h))],
        out_specs=[pl.BlockSpec((GLA_BLOCK, GLA_DV), lambda b, h, s: (lat_block(b, s, True), h)),
                   pl.BlockSpec((GLA_BLOCK, GLA_DV), lambda b, h, s: (lat_block(b, s, False), h))],
        out_shape=[out_shape, out_shape],
        scratch_shapes=[pltpu.VMEM((GLA_DV, GLA_DK), F32), pltpu.VMEM((GLA_DV, GLA_DK), F32)],
        compiler_params=_params(("arbitrary", "arbitrary", "arbitrary")),
        name="gla",
    )(p, p, p, gin, p, p, p, gin, wg_pad, bg)


NA_QTOK = NA_QROWS * GRID_W
NA_WIN = NA_WIN_ROWS * GRID_W
NA_BIAS_PAD = NA_QROWS * GRID_W
NA_BIAS_W = 2048
NA_KINDS = ((0, 0), (NA_QROWS, NA_QROWS - NA_KR // 2), (GRID_W - NA_QROWS, GRID_W - NA_WIN_ROWS))


def _na_bias_tables(rpb):
    qc = np.arange(GRID_W)
    dj = np.clip(qc[None, :] - qc[:, None], -(NA_KC - 1), NA_KC - 1) + NA_KC - 1
    t = rpb[:, :, dj]
    t = jnp.transpose(t, (0, 2, 1, 3)).reshape(rpb.shape[0], GRID_W, (2 * NA_KR - 1) * GRID_W)
    right = NA_BIAS_W + GRID_W - NA_BIAS_PAD - t.shape[-1]
    t = jnp.pad(t, ((0, 0), (0, 0), (NA_BIAS_PAD, right)))
    return jnp.stack([t[..., :NA_BIAS_W], t[..., GRID_W:GRID_W + NA_BIAS_W]], axis=1)


def _na_kernel(q_ref, k_ref, v_ref, kc_ref, vc_ref, ce_ref, o_ref, t_ref, *, rows):
    b = pl.program_id(1)
    blk = pl.program_id(2)
    n_blk = rows // NA_QROWS

    @pl.when((b == 0) & (blk == 0))
    def _():
        kk = lax.broadcasted_iota(jnp.int32, (GRID_W, NA_WIN), 1)
        k_row = kk >> 6
        k_col = kk & (GRID_W - 1)
        q_col = lax.broadcasted_iota(jnp.int32, (GRID_W, NA_WIN), 0)
        c0 = jnp.clip(q_col - NA_KC // 2, 0, GRID_W - NA_KC)
        col_ok = (k_col >= c0) & (k_col < c0 + NA_KC)
        for kind, (r0, w0) in enumerate(NA_KINDS):
            for qr in range(NA_QROWS):
                band0 = min(max(r0 + qr - NA_KR // 2, 0), rows - NA_KR) - w0
                ok = col_ok & (k_row >= band0) & (k_row < band0 + NA_KR)
                a0 = w0 - (r0 + qr) + NA_KR - 1
                lane0 = NA_BIAS_PAD + (a0 - a0 % 2) * GRID_W
                for hh in range(2):
                    bias = ce_ref[hh, a0 % 2, :, lane0:lane0 + NA_WIN]
                    t_ref[hh, kind, qr * GRID_W:(qr + 1) * GRID_W, :] = jnp.where(ok, bias, NEG_INF)

    kind = jnp.where(blk == 0, 0, jnp.where(blk == n_blk - 1, 2, 1))
    w0 = jnp.clip(blk * NA_QROWS - NA_KR // 2, 0, rows - NA_WIN_ROWS)
    win = pl.ds(pl.multiple_of(w0 * GRID_W, 256), NA_WIN)
    q = q_ref[...]
    kw = k_ref[win, :]
    vw = v_ref[win, :]
    kc = kc_ref[...]
    vc = vc_ref[...]
    lane = lax.broadcasted_iota(jnp.int32, q.shape, 1)
    scale = NA_DH ** -0.5
    nt = (((1,), (1,)), ((), ()))
    outs = []
    for hh in range(2):
        own = (lane < NA_DH) if hh == 0 else (lane >= NA_DH)
        qh = jnp.where(own, q, jnp.zeros_like(q))
        s_lat = lax.dot_general(qh, kw, nt, preferred_element_type=F32) * scale + t_ref[hh, kind]
        s_ctx = lax.dot_general(qh, kc, nt, preferred_element_type=F32) * scale
        m = jnp.maximum(jnp.max(s_lat, axis=-1, keepdims=True), jnp.max(s_ctx, axis=-1, keepdims=True))
        p_lat = jnp.exp(s_lat - m)
        p_ctx = jnp.exp(s_ctx - m)
        denom = jnp.sum(p_lat, axis=-1, keepdims=True) + jnp.sum(p_ctx, axis=-1, keepdims=True)
        o = (jnp.dot(p_lat.astype(BF16), vw, preferred_element_type=F32)
             + jnp.dot(p_ctx.astype(BF16), vc, preferred_element_type=F32))
        outs.append(o / denom)
    o_ref[...] = jnp.where(lane < NA_DH, outs[0], outs[1]).astype(o_ref.dtype)


def _na(p, ce, batch, seq, ctx_len, col0):
    rows = seq // GRID_W
    n_blk = rows // NA_QROWS
    n_pairs = NA_HEADS // 2
    cq = col0 // LANES
    ck = cq + n_pairs
    cv = ck + n_pairs
    ctx_block0 = batch * seq // ctx_len
    return pl.pallas_call(
        functools.partial(_na_kernel, rows=rows),
        grid=(n_pairs, batch, n_blk),
        in_specs=[
            pl.BlockSpec((NA_QTOK, LANES), lambda hp, b, r: (b * n_blk + r, cq + hp)),
            pl.BlockSpec((seq, LANES), lambda hp, b, r: (b, ck + hp)),
            pl.BlockSpec((seq, LANES), lambda hp, b, r: (b, cv + hp)),
            pl.BlockSpec((ctx_len, LANES), lambda hp, b, r: (ctx_block0 + b, ck + hp)),
            pl.BlockSpec((ctx_len, LANES), lambda hp, b, r: (ctx_block0 + b, cv + hp)),
            pl.BlockSpec((2, 2, GRID_W, NA_BIAS_W), lambda hp, b, r: (hp, 0, 0, 0)),
        ],
        out_specs=pl.BlockSpec((NA_QTOK, LANES), lambda hp, b, r: (b * n_blk + r, hp)),
        out_shape=jax.ShapeDtypeStruct((batch * seq, NA_HEADS * NA_DH), BF16),
        scratch_shapes=[pltpu.VMEM((2, len(NA_KINDS), NA_QTOK, NA_WIN), F32)],
        compiler_params=_params(("arbitrary", "arbitrary", "arbitrary")),
        name="natten",
    )(p, p, p, p, p, ce)


def _merge_kernel(of_ref, ob_ref, r_ref, gn_ref, b_ref, wa_ref, wb_ref, m1_ref, m2_ref, o_ref, a_ref):
    @pl.when(pl.program_id(1) == 0)
    def _():
        for h in range(GLA_HEADS):
            cols = slice(h * GLA_DV, (h + 1) * GLA_DV)
            tot = of_ref[:, cols].astype(F32) + ob_ref[:, cols].astype(F32)
            a_ref[:, cols] = (_rms(tot, gn_ref[...]) * _silu(r_ref[:, cols].astype(F32))).astype(BF16)

    a = jnp.dot(a_ref[...], wa_ref[...], preferred_element_type=F32)
    b = jnp.dot(b_ref[...], wb_ref[...], preferred_element_type=F32)
    m = jax.nn.sigmoid(m1_ref[...].astype(F32)) * a + jax.nn.sigmoid(m2_ref[...].astype(F32)) * b
    o_ref[...] = m.astype(o_ref.dtype)


def _merge(o_fwd, o_bwd, gn, o_na, w_gla_o, w_na_o, p, col_r, col_m1):
    n_rows = o_fwd.shape[0]
    v_w = o_fwd.shape[1]
    tn = PROJ_TILE
    c1 = col_m1 // tn
    c2 = c1 + D_MODEL // tn
    return pl.pallas_call(
        _merge_kernel,
        grid=(n_rows // ROW_TILE, D_MODEL // tn),
        in_specs=[pl.BlockSpec((ROW_TILE, v_w), lambda i, n: (i, 0)),
                  pl.BlockSpec((ROW_TILE, v_w), lambda i, n: (i, 0)),
                  pl.BlockSpec((ROW_TILE, v_w), lambda i, n: (i, col_r // v_w)),
                  pl.BlockSpec((1, GLA_DV), lambda i, n: (0, 0)),
                  pl.BlockSpec((ROW_TILE, o_na.shape[1]), lambda i, n: (i, 0)),
                  pl.BlockSpec((w_gla_o.shape[0], tn), lambda i, n: (0, n)),
                  pl.BlockSpec((w_na_o.shape[0], tn), lambda i, n: (0, n)),
                  pl.BlockSpec((ROW_TILE, tn), lambda i, n: (i, c1 + n)),
                  pl.BlockSpec((ROW_TILE, tn), lambda i, n: (i, c2 + n))],
        out_specs=pl.BlockSpec((ROW_TILE, tn), lambda i, n: (i, n)),
        out_shape=jax.ShapeDtypeStruct((n_rows, D_MODEL), BF16),
        scratch_shapes=[pltpu.VMEM((ROW_TILE, v_w), BF16)],
        compiler_params=_params(("arbitrary", "arbitrary")),
        name="merge",
    )(o_fwd, o_bwd, p, gn, o_na, w_gla_o, w_na_o, p, p)


def _outproj_kernel(m_ref, w_ref, x_ref, mod_ref, g_ref, o_ref):
    y = jnp.dot(m_ref[...], w_ref[...], preferred_element_type=F32)
    o_ref[...] = x_ref[...] + mod_ref[5:6, :] * _rms(y, g_ref[3:4, :])


def _outproj(m, w_out, h1, mod, norm_g):
    n_tiles = m.shape[0] // ROW_TILE
    tiles_per_batch = n_tiles // 2
    return pl.pallas_call(
        _outproj_kernel,
        grid=(n_tiles,),
        in_specs=[pl.BlockSpec((ROW_TILE, D_MODEL), lambda i: (i, 0)),
                  pl.BlockSpec((D_MODEL, D_MODEL), lambda i: (0, 0)),
                  pl.BlockSpec((ROW_TILE, D_MODEL), lambda i: (i, 0)),
                  pl.BlockSpec((None, N_MOD, D_MODEL), lambda i: (i // tiles_per_batch, 0, 0)),
                  pl.BlockSpec((6, D_MODEL), lambda i: (0, 0))],
        out_specs=pl.BlockSpec((ROW_TILE, D_MODEL), lambda i: (i, 0)),
        out_shape=jax.ShapeDtypeStruct((m.shape[0], D_MODEL), F32),
        compiler_params=_params(("arbitrary",)),
        name="outproj",
    )(m, w_out, h1, mod, norm_g)


def kernel(x, c, ctx, c_ctx, w_ada, b_ada, norm_g, ffn_wg, ffn_wu, ffn_wd, w_in, gla_wg, gla_bg, gla_norm_g,
           w_gla_o, na_rpb, w_na_o, w_out):
    batch, seq, d = x.shape
    ctx_len = ctx.shape[1]
    depth = w_ada.shape[0]
    assert d == D_MODEL and batch == 2 and batch * ctx_len == ROW_TILE and depth == 1
    assert seq % ROW_TILE == 0 and seq // GRID_W == GRID_W

    qk_w = GLA_HEADS * GLA_DK
    v_w = GLA_HEADS * GLA_DV
    na_w = NA_HEADS * NA_DH
    gate0 = 2 * qk_w + 2 * v_w
    gate1 = gate0 + 2 * GLA_GATE_RANK
    col_nq = gate0
    col_m1 = col_nq + 3 * na_w

    cvec = jnp.zeros((8, d), F32).at[0:batch].set(c).at[batch].set(c_ctx)
    cos, sin = _rope_tables(seq)
    h = x.reshape(batch * seq, d)
    hc = ctx.reshape(batch * ctx_len, d)
    n_rows = batch * (seq + ctx_len)
    lat_tiles_per_batch = seq // BIG_ROW_TILE
    for l in range(depth):
        mod = _modulation(cvec, w_ada[l], b_ada[l].reshape(1, -1)).reshape(8, N_MOD, d)
        g = norm_g[l]
        h1 = _ffn(h, mod, g, ffn_wg[l, 0], ffn_wu[l, 0], ffn_wd[l, 0], 0, tm=BIG_ROW_TILE, tf=FF_TILE_BIG,
                  mod_row=lambda i: i // lat_tiles_per_batch, out_rows=n_rows)
        h1 = _ffn(hc, mod, g, ffn_wg[l, 0], ffn_wu[l, 0], ffn_wd[l, 0], 0, tm=ROW_TILE, tf=FF_TILE,
                  mod_row=lambda i: batch, out_rows=n_rows, out_tile0=batch * seq // ROW_TILE, into=h1)
        w_gate = jnp.pad(w_in[l][:, gate0:gate1], ((0, 0), (0, LANES - 2 * GLA_GATE_RANK)))
        p, gin = _inproj(h1, mod, g, w_in[l], w_in[l][:, gate1:], w_gate, cos, sin, seq, gate0)
        wg_pad = jnp.zeros((2, LANES, qk_w), F32)
        wg_pad = wg_pad.at[0, :GLA_GATE_RANK].set(gla_wg[l, 0])
        wg_pad = wg_pad.at[1, GLA_GATE_RANK:2 * GLA_GATE_RANK].set(gla_wg[l, 1])
        bg = gla_bg[l].reshape(2, 1, qk_w)
        gn = gla_norm_g[l].reshape(1, GLA_DV)
        o_fwd, o_bwd = _gla(p, gin, wg_pad, bg, batch, seq)
        o_na = _na(p, _na_bias_tables(na_rpb[l]), batch, seq, ctx_len, col_nq)
        m = _merge(o_fwd, o_bwd, gn, o_na, w_gla_o[l].astype(BF16), w_na_o[l].astype(BF16), p,
                   2 * qk_w + v_w, col_m1)
        h2 = _outproj(m, w_out[l].astype(BF16), h1, mod, g)
        h = _ffn(h2, mod, g, ffn_wg[l, 1], ffn_wu[l, 1], ffn_wd[l, 1], 2, tm=BIG_ROW_TILE, tf=FF_TILE_BIG,
                 mod_row=lambda i: i // lat_tiles_per_batch)
    return h.reshape(batch, seq, d)
```

```python
import functools

import numpy as np
import jax
import jax.numpy as jnp
from jax import lax
from jax.experimental import pallas as pl
from jax.experimental.pallas import tpu as pltpu

F32 = jnp.float32
BF16 = jnp.bfloat16

D_MODEL = 2048
GRID_W = 64
GLA_HEADS = 4
GLA_DK = 256
GLA_DV = 512
GLA_GATE_RANK = 16
GLA_GATE_TEMP = 16.0
NA_HEADS = 16
NA_DH = 64
NA_KR = 8
NA_KC = 16
ROPE_THETA = 10000.0
EPS = 1e-6
NEG_INF = -1e30
N_MOD = 9
LOG2E = 1.4426950408889634

LANES = 128
ROW_TILE = 512
BIG_ROW_TILE = 1024
FF_TILE = 512
FF_TILE_BIG = 256
PROJ_TILE = 512
GLA_BLOCK = 256
GLA_CHUNK = 128
NA_QROWS = 8
NA_WIN_ROWS = 16
VMEM_LIMIT = 56 * 1024 * 1024


def _params(sem):
    return pltpu.CompilerParams(dimension_semantics=sem, vmem_limit_bytes=VMEM_LIMIT)


def _rms(x, g):
    return x * lax.rsqrt(jnp.mean(x * x, axis=-1, keepdims=True) + EPS) * g


def _silu(x):
    return x * jax.nn.sigmoid(x)


def _mod_kernel(c_ref, w_ref, b_ref, o_ref):
    o_ref[...] = jnp.dot(_silu(c_ref[...]), w_ref[...], preferred_element_type=F32) + b_ref[...]


def _modulation(cvec, w_ada, b_ada):
    n = w_ada.shape[1]
    tn = 1024
    return pl.pallas_call(
        _mod_kernel,
        grid=(n // tn,),
        in_specs=[pl.BlockSpec((8, D_MODEL), lambda j: (0, 0)),
                  pl.BlockSpec((D_MODEL, tn), lambda j: (0, j)),
                  pl.BlockSpec((1, tn), lambda j: (0, j))],
        out_specs=pl.BlockSpec((8, tn), lambda j: (0, j)),
        out_shape=jax.ShapeDtypeStruct((8, n), F32),
        compiler_params=_params(("arbitrary",)),
        name="adaln_mod",
    )(cvec, w_ada, b_ada)


def _mixed_dot(a, w):
    return lax.dot_general(a, w, (((1,), (0,)), ((), ())), preferred_element_type=F32)


def _ffn_kernel(x_ref, mod_ref, g_ref, wg_ref, wu_ref, wd_ref, *rest, sub):
    o_ref, h_ref = rest[-2:]
    f = pl.program_id(1)

    @pl.when(f == 0)
    def _():
        shift = mod_ref[3 * sub:3 * sub + 1, :]
        scale = mod_ref[3 * sub + 1:3 * sub + 2, :]
        h = _rms(x_ref[...], g_ref[2 * sub:2 * sub + 1, :]) * (1.0 + scale) + shift
        h_ref[...] = h.astype(BF16)
        o_ref[...] = jnp.zeros_like(o_ref)

    h = h_ref[...]
    g = _mixed_dot(h, wg_ref[...])
    u = _mixed_dot(h, wu_ref[...])
    o_ref[...] += _mixed_dot((_silu(g) * u).astype(BF16), wd_ref[...])

    @pl.when(f == pl.num_programs(1) - 1)
    def _():
        gate = mod_ref[3 * sub + 2:3 * sub + 3, :]
        y = _rms(o_ref[...], g_ref[2 * sub + 1:2 * sub + 2, :])
        o_ref[...] = x_ref[...] + 0.5 * gate * y


def _ffn(x2d, mod, norm_g, wg, wu, wd, which, sub, *, tm, tf, mod_row, out_rows=None, out_tile0=0, into=None):
    n_tiles = x2d.shape[0] // tm
    out_rows = out_rows or x2d.shape[0]
    d_ff = wg.shape[2]
    in_specs = [
        pl.BlockSpec((tm, D_MODEL), lambda i, f: (i, 0), pipeline_mode=pl.Buffered(1)),
        pl.BlockSpec((None, N_MOD, D_MODEL), lambda i, f: (mod_row(i), 0, 0)),
        pl.BlockSpec((6, D_MODEL), lambda i, f: (0, 0)),
        pl.BlockSpec((None, D_MODEL, tf), lambda i, f: (which, 0, f)),
        pl.BlockSpec((None, D_MODEL, tf), lambda i, f: (which, 0, f)),
        pl.BlockSpec((None, tf, D_MODEL), lambda i, f: (which, f, 0)),
    ]
    args = [x2d, mod, norm_g, wg, wu, wd]
    aliases = {}
    if into is not None:
        in_specs.append(pl.BlockSpec(memory_space=pl.ANY))
        args.append(into)
        aliases = {len(args) - 1: 0}
    return pl.pallas_call(
        functools.partial(_ffn_kernel, sub=sub),
        grid=(n_tiles, d_ff // tf),
        in_specs=in_specs,
        out_specs=pl.BlockSpec((tm, D_MODEL), lambda i, f: (out_tile0 + i, 0)),
        out_shape=jax.ShapeDtypeStruct((out_rows, D_MODEL), F32),
        scratch_shapes=[pltpu.VMEM((tm, D_MODEL), BF16)],
        input_output_aliases=aliases,
        compiler_params=_params(("arbitrary", "arbitrary")),
        name="ffn%d%s" % (sub, "" if into is None else "_ctx"),
    )(*args)


N_ROPE_TILES = 2 * GLA_HEADS * GLA_DK // PROJ_TILE
N_Q_TILES = N_ROPE_TILES // 2


def _rope_tables(seq):
    half = GLA_DK // 4
    freqs = ROPE_THETA ** (-np.arange(half, dtype=np.float64) / half)
    t = np.arange(seq)
    cos_parts, sin_parts = [], []
    for pos in (t // GRID_W, t % GRID_W):
        ang = pos[:, None].astype(np.float64) * freqs
        cos_parts += [np.cos(ang), np.cos(ang)]
        sin_parts += [-np.sin(ang), np.sin(ang)]
    cos = np.concatenate(cos_parts, axis=1)
    sin = np.concatenate(sin_parts, axis=1)
    cos = np.concatenate([cos, np.ones((BIG_ROW_TILE, GLA_DK))], axis=0)
    sin = np.concatenate([sin, np.zeros((BIG_ROW_TILE, GLA_DK))], axis=0)
    return jnp.asarray(cos, F32), jnp.asarray(sin, F32)


def _inproj_kernel(x_ref, mod_ref, g_ref, wa_ref, wb_ref, wgate_ref, cos_ref, sin_ref, p_ref, gate_ref, h_ref,
                   *, n_a_tiles):
    n = pl.program_id(1)

    @pl.when(n == 0)
    def _():
        h = _rms(x_ref[...], g_ref[2:3, :]) * (1.0 + mod_ref[4:5, :]) + mod_ref[3:4, :]
        h_ref[...] = h.astype(BF16)
        gate_ref[...] = _mixed_dot(h_ref[...], wgate_ref[...])

    @pl.when(n < N_ROPE_TILES)
    def _():
        y = _mixed_dot(h_ref[...], wa_ref[...])
        cos = cos_ref[...]
        sin = sin_ref[...]
        qscale = jnp.where(n < N_Q_TILES, GLA_DK ** -0.5, 1.0).astype(F32)
        for j in range(PROJ_TILE // GLA_DK):
            yj = y[:, j * GLA_DK:(j + 1) * GLA_DK]
            swapped = jnp.concatenate(
                [pltpu.roll(yj[:, g * LANES:(g + 1) * LANES], LANES // 2, 1) for g in range(GLA_DK // LANES)],
                axis=1)
            p_ref[:, j * GLA_DK:(j + 1) * GLA_DK] = ((yj * cos + swapped * sin) * qscale).astype(BF16)

    @pl.when((n >= N_ROPE_TILES) & (n < n_a_tiles))
    def _():
        p_ref[...] = _mixed_dot(h_ref[...], wa_ref[...]).astype(BF16)

    @pl.when(n >= n_a_tiles)
    def _():
        p_ref[...] = _mixed_dot(h_ref[...], wb_ref[...]).astype(BF16)


def _inproj(h1, mod, norm_g, w_in, w_tail, w_gate, cos, sin, seq, n_a_cols):
    tm = BIG_ROW_TILE
    n_tiles = pl.cdiv(h1.shape[0], tm)
    n_lat_tiles = seq * 2 // tm
    tiles_per_batch = n_lat_tiles // 2
    n_a_tiles = n_a_cols // PROJ_TILE
    n_b_tiles = w_tail.shape[1] // PROJ_TILE
    width = n_a_cols + w_tail.shape[1]
    tab_spec = pl.BlockSpec(
        (tm, GLA_DK), lambda i, n: (jnp.where(i < n_lat_tiles, i % tiles_per_batch, seq // tm), 0))
    return pl.pallas_call(
        functools.partial(_inproj_kernel, n_a_tiles=n_a_tiles),
        grid=(n_tiles, n_a_tiles + n_b_tiles),
        in_specs=[pl.BlockSpec((tm, D_MODEL), lambda i, n: (i, 0)),
                  pl.BlockSpec((None, N_MOD, D_MODEL), lambda i, n: (jnp.minimum(i // tiles_per_batch, 2), 0, 0)),
                  pl.BlockSpec((6, D_MODEL), lambda i, n: (0, 0)),
                  pl.BlockSpec((D_MODEL, PROJ_TILE), lambda i, n: (0, jnp.minimum(n, n_a_tiles - 1))),
                  pl.BlockSpec((D_MODEL, PROJ_TILE), lambda i, n: (0, jnp.maximum(n - n_a_tiles, 0))),
                  pl.BlockSpec((D_MODEL, LANES), lambda i, n: (0, 0)),
                  tab_spec, tab_spec],
        out_specs=[pl.BlockSpec((tm, PROJ_TILE), lambda i, n: (i, n)),
                   pl.BlockSpec((tm, LANES), lambda i, n: (i, 0))],
        out_shape=[jax.ShapeDtypeStruct((h1.shape[0], width), BF16),
                   jax.ShapeDtypeStruct((h1.shape[0], LANES), F32)],
        scratch_shapes=[pltpu.VMEM((tm, D_MODEL), BF16)],
        compiler_params=_params(("arbitrary", "arbitrary")),
        name="inproj",
    )(h1, mod, norm_g, w_in, w_tail, w_gate, cos, sin)


def _log_sigmoid(z):
    return jnp.minimum(z, 0.0) - jnp.log1p(jnp.exp(-jnp.abs(z)))


def _gla_scan_block(q_ref, k_ref, v_ref, gin_ref, wg, bg, st, forward):
    c = GLA_CHUNK
    row = lax.broadcasted_iota(jnp.int32, (c, c), 0)
    col = lax.broadcasted_iota(jnp.int32, (c, c), 1)
    mask = (col <= row) if forward else (col >= row)
    tri = jnp.where(mask, 1.0, 0.0).astype(BF16)
    nt = (((1,), (1,)), ((), ()))
    z = jnp.dot(gin_ref[...], wg, preferred_element_type=F32) + bg
    log_a = _log_sigmoid(z) * (LOG2E / GLA_GATE_TEMP)
    n_chunks = GLA_BLOCK // c
    outs = [None] * n_chunks
    for ci in (range(n_chunks) if forward else range(n_chunks - 1, -1, -1)):
        rows = slice(ci * c, (ci + 1) * c)
        q = q_ref[rows, :].astype(F32)
        k = k_ref[rows, :].astype(F32)
        v = v_ref[rows, :]
        la = log_a[rows, :]
        hi = la.astype(BF16)
        lo = (la - hi.astype(F32)).astype(BF16)
        cum = jnp.dot(tri, hi, preferred_element_type=F32) + jnp.dot(tri, lo, preferred_element_type=F32)
        cum_end = cum[c - 1:c, :] if forward else cum[0:1, :]
        q_dec = (q * jnp.exp2(cum)).astype(BF16)
        k_inv = (k * jnp.exp2(-cum)).astype(BF16)
        k_end = (k * jnp.exp2(cum_end - cum)).astype(BF16)
        att = lax.dot_general(q_dec, k_inv, nt, preferred_element_type=F32)
        att = jnp.where(mask, att, 0.0).astype(BF16)
        outs[ci] = (jnp.dot(att, v, preferred_element_type=F32)
                    + lax.dot_general(q_dec, st.astype(BF16), nt, preferred_element_type=F32))
        kv_t = lax.dot_general(v, k_end, (((0,), (0,)), ((), ())), preferred_element_type=F32)
        st = jnp.exp2(cum_end) * st + kv_t
    return jnp.concatenate(outs, axis=0), st


def _gla_kernel(qf_ref, kf_ref, vf_ref, gf_ref, qb_ref, kb_ref, vb_ref, gb_ref, wg_ref, bg_ref,
                of_ref, ob_ref, stf_ref, stb_ref):
    @pl.when(pl.program_id(2) == 0)
    def _():
        stf_ref[...] = jnp.zeros_like(stf_ref)
        stb_ref[...] = jnp.zeros_like(stb_ref)

    of, stf = _gla_scan_block(qf_ref, kf_ref, vf_ref, gf_ref, wg_ref[0], bg_ref[0], stf_ref[...], True)
    ob, stb = _gla_scan_block(qb_ref, kb_ref, vb_ref, gb_ref, wg_ref[1], bg_ref[1], stb_ref[...], False)
    of_ref[...] = of.astype(of_ref.dtype)
    ob_ref[...] = ob.astype(ob_ref.dtype)
    stf_ref[...] = stf
    stb_ref[...] = stb


def _gla(p, gin, wg_pad, bg, batch, seq):
    n_lat = seq // GLA_BLOCK
    ctx_block0 = batch * n_lat

    def lat_block(b, s, forward):
        j = jnp.maximum(s - 1, 0)
        return b * n_lat + (j if forward else n_lat - 1 - j)

    def in_block(b, s, forward):
        return jnp.where(s == 0, ctx_block0 + b, lat_block(b, s, forward))

    kq = GLA_HEADS
    kv = 2 * GLA_HEADS * GLA_DK // GLA_DV

    def dir_specs(forward):
        return [
            pl.BlockSpec((GLA_BLOCK, GLA_DK), lambda b, h, s: (in_block(b, s, forward), h)),
            pl.BlockSpec((GLA_BLOCK, GLA_DK), lambda b, h, s: (in_block(b, s, forward), kq + h)),
            pl.BlockSpec((GLA_BLOCK, GLA_DV), lambda b, h, s: (in_block(b, s, forward), kv + h)),
            pl.BlockSpec((GLA_BLOCK, LANES), lambda b, h, s: (in_block(b, s, forward), 0)),
        ]

    out_shape = jax.ShapeDtypeStruct((batch * seq, GLA_HEADS * GLA_DV), BF16)
    return pl.pallas_call(
        _gla_kernel,
        grid=(batch, GLA_HEADS, n_lat + 1),
        in_specs=dir_specs(True) + dir_specs(False) + [
            pl.BlockSpec((2, LANES, GLA_DK), lambda b, h, s: (0, 0, h)),
            pl.BlockSpec((2, 1, GLA_DK), lambda b, h, s: (0, 0, h))],
        out_specs=[pl.BlockSpec((GLA_BLOCK, GLA_DV), lambda b, h, s: (lat_block(b, s, True), h)),
                   pl.BlockSpec((GLA_BLOCK, GLA_DV), lambda b, h, s: (lat_block(b, s, False), h))],
        out_shape=[out_shape, out_shape],
        scratch_shapes=[pltpu.VMEM((GLA_DV, GLA_DK), F32), pltpu.VMEM((GLA_DV, GLA_DK), F32)],
        compiler_params=_params(("arbitrary", "arbitrary", "arbitrary")),
        name="gla",
    )(p, p, p, gin, p, p, p, gin, wg_pad, bg)


NA_QTOK = NA_QROWS * GRID_W
NA_WIN = NA_WIN_ROWS * GRID_W
NA_BIAS_PAD = NA_QROWS * GRID_W
NA_BIAS_W = 2048
NA_KINDS = ((0, 0), (NA_QROWS, NA_QROWS - NA_KR // 2), (GRID_W - NA_QROWS, GRID_W - NA_WIN_ROWS))


def _na_bias_tables(rpb):
    qc = np.arange(GRID_W)
    dj = np.clip(qc[None, :] - qc[:, None], -(NA_KC - 1), NA_KC - 1) + NA_KC - 1
    t = rpb[:, :, dj]
    t = jnp.transpose(t, (0, 2, 1, 3)).reshape(rpb.shape[0], GRID_W, (2 * NA_KR - 1) * GRID_W)
    right = NA_BIAS_W + GRID_W - NA_BIAS_PAD - t.shape[-1]
    t = jnp.pad(t, ((0, 0), (0, 0), (NA_BIAS_PAD, right)))
    return jnp.stack([t[..., :NA_BIAS_W], t[..., GRID_W:GRID_W + NA_BIAS_W]], axis=1)


def _na_kernel(q_ref, k_ref, v_ref, kc_ref, vc_ref, ce_ref, o_ref, t_ref, *, rows):
    b = pl.program_id(1)
    blk = pl.program_id(2)
    n_blk = rows // NA_QROWS

    @pl.when((b == 0) & (blk == 0))
    def _():
        kk = lax.broadcasted_iota(jnp.int32, (GRID_W, NA_WIN), 1)
        k_row = kk >> 6
        k_col = kk & (GRID_W - 1)
        q_col = lax.broadcasted_iota(jnp.int32, (GRID_W, NA_WIN), 0)
        c0 = jnp.clip(q_col - NA_KC // 2, 0, GRID_W - NA_KC)
        col_ok = (k_col >= c0) & (k_col < c0 + NA_KC)
        for kind, (r0, w0) in enumerate(NA_KINDS):
            for qr in range(NA_QROWS):
                band0 = min(max(r0 + qr - NA_KR // 2, 0), rows - NA_KR) - w0
                ok = col_ok & (k_row >= band0) & (k_row < band0 + NA_KR)
                a0 = w0 - (r0 + qr) + NA_KR - 1
                lane0 = NA_BIAS_PAD + (a0 - a0 % 2) * GRID_W
                for hh in range(2):
                    bias = ce_ref[hh, a0 % 2, :, lane0:lane0 + NA_WIN]
                    t_ref[hh, kind, qr * GRID_W:(qr + 1) * GRID_W, :] = jnp.where(ok, bias, NEG_INF)

    kind = jnp.where(blk == 0, 0, jnp.where(blk == n_blk - 1, 2, 1))
    w0 = jnp.clip(blk * NA_QROWS - NA_KR // 2, 0, rows - NA_WIN_ROWS)
    win = pl.ds(pl.multiple_of(w0 * GRID_W, 256), NA_WIN)
    q = q_ref[...]
    kw = k_ref[win, :]
    vw = v_ref[win, :]
    kc = kc_ref[...]
    vc = vc_ref[...]
    lane = lax.broadcasted_iota(jnp.int32, q.shape, 1)
    scale = NA_DH ** -0.5
    nt = (((1,), (1,)), ((), ()))
    outs = []
    for hh in range(2):
        own = (lane < NA_DH) if hh == 0 else (lane >= NA_DH)
        qh = jnp.where(own, q, jnp.zeros_like(q))
        s_lat = lax.dot_general(qh, kw, nt, preferred_element_type=F32) * scale + t_ref[hh, kind]
        s_ctx = lax.dot_general(qh, kc, nt, preferred_element_type=F32) * scale
        m = jnp.maximum(jnp.max(s_lat, axis=-1, keepdims=True), jnp.max(s_ctx, axis=-1, keepdims=True))
        p_lat = jnp.exp(s_lat - m)
        p_ctx = jnp.exp(s_ctx - m)
        denom = jnp.sum(p_lat, axis=-1, keepdims=True) + jnp.sum(p_ctx, axis=-1, keepdims=True)
        o = (jnp.dot(p_lat.astype(BF16), vw, preferred_element_type=F32)
             + jnp.dot(p_ctx.astype(BF16), vc, preferred_element_type=F32))
        outs.append(o / denom)
    o_ref[...] = jnp.where(lane < NA_DH, outs[0], outs[1]).astype(o_ref.dtype)


def _na(p, ce, batch, seq, ctx_len, col0):
    rows = seq // GRID_W
    n_blk = rows // NA_QROWS
    n_pairs = NA_HEADS // 2
    cq = col0 // LANES
    ck = cq + n_pairs
    cv = ck + n_pairs
    ctx_block0 = batch * seq // ctx_len
    return pl.pallas_call(
        functools.partial(_na_kernel, rows=rows),
        grid=(n_pairs, batch, n_blk),
        in_specs=[
            pl.BlockSpec((NA_QTOK, LANES), lambda hp, b, r: (b * n_blk + r, cq + hp)),
            pl.BlockSpec((seq, LANES), lambda hp, b, r: (b, ck + hp)),
            pl.BlockSpec((seq, LANES), lambda hp, b, r: (b, cv + hp)),
            pl.BlockSpec((ctx_len, LANES), lambda hp, b, r: (ctx_block0 + b, ck + hp)),
            pl.BlockSpec((ctx_len, LANES), lambda hp, b, r: (ctx_block0 + b, cv + hp)),
            pl.BlockSpec((2, 2, GRID_W, NA_BIAS_W), lambda hp, b, r: (hp, 0, 0, 0)),
        ],
        out_specs=pl.BlockSpec((NA_QTOK, LANES), lambda hp, b, r: (b * n_blk + r, hp)),
        out_shape=jax.ShapeDtypeStruct((batch * seq, NA_HEADS * NA_DH), BF16),
        scratch_shapes=[pltpu.VMEM((2, len(NA_KINDS), NA_QTOK, NA_WIN), F32)],
        compiler_params=_params(("arbitrary", "arbitrary", "arbitrary")),
        name="natten",
    )(p, p, p, p, p, ce)


def _merge_kernel(of_ref, ob_ref, r_ref, gn_ref, b_ref, wa_ref, wb_ref, m1_ref, m2_ref, o_ref, a_ref):
    @pl.when(pl.program_id(1) == 0)
    def _():
        for h in range(GLA_HEADS):
            cols = slice(h * GLA_DV, (h + 1) * GLA_DV)
            tot = of_ref[:, cols].astype(F32) + ob_ref[:, cols].astype(F32)
            a_ref[:, cols] = (_rms(tot, gn_ref[...]) * _silu(r_ref[:, cols].astype(F32))).astype(BF16)

    a = jnp.dot(a_ref[...], wa_ref[...], preferred_element_type=F32)
    b = jnp.dot(b_ref[...], wb_ref[...], preferred_element_type=F32)
    m = jax.nn.sigmoid(m1_ref[...].astype(F32)) * a + jax.nn.sigmoid(m2_ref[...].astype(F32)) * b
    o_ref[...] = m.astype(o_ref.dtype)


def _merge(o_fwd, o_bwd, gn, o_na, w_gla_o, w_na_o, p, col_r, col_m1):
    n_rows = o_fwd.shape[0]
    v_w = o_fwd.shape[1]
    tn = PROJ_TILE
    c1 = col_m1 // tn
    c2 = c1 + D_MODEL // tn
    return pl.pallas_call(
        _merge_kernel,
        grid=(n_rows // ROW_TILE, D_MODEL // tn),
        in_specs=[pl.BlockSpec((ROW_TILE, v_w), lambda i, n: (i, 0)),
                  pl.BlockSpec((ROW_TILE, v_w), lambda i, n: (i, 0)),
                  pl.BlockSpec((ROW_TILE, v_w), lambda i, n: (i, col_r // v_w)),
                  pl.BlockSpec((1, GLA_DV), lambda i, n: (0, 0)),
                  pl.BlockSpec((ROW_TILE, o_na.shape[1]), lambda i, n: (i, 0)),
                  pl.BlockSpec((w_gla_o.shape[0], tn), lambda i, n: (0, n)),
                  pl.BlockSpec((w_na_o.shape[0], tn), lambda i, n: (0, n)),
                  pl.BlockSpec((ROW_TILE, tn), lambda i, n: (i, c1 + n)),
                  pl.BlockSpec((ROW_TILE, tn), lambda i, n: (i, c2 + n))],
        out_specs=pl.BlockSpec((ROW_TILE, tn), lambda i, n: (i, n)),
        out_shape=jax.ShapeDtypeStruct((n_rows, D_MODEL), BF16),
        scratch_shapes=[pltpu.VMEM((ROW_TILE, v_w), BF16)],
        compiler_params=_params(("arbitrary", "arbitrary")),
        name="merge",
    )(o_fwd, o_bwd, p, gn, o_na, w_gla_o, w_na_o, p, p)


def _outproj_kernel(m_ref, w_ref, x_ref, mod_ref, g_ref, o_ref):
    y = jnp.dot(m_ref[...], w_ref[...], preferred_element_type=F32)
    o_ref[...] = x_ref[...] + mod_ref[5:6, :] * _rms(y, g_ref[3:4, :])


def _outproj(m, w_out, h1, mod, norm_g):
    n_tiles = m.shape[0] // ROW_TILE
    tiles_per_batch = n_tiles // 2
    return pl.pallas_call(
        _outproj_kernel,
        grid=(n_tiles,),
        in_specs=[pl.BlockSpec((ROW_TILE, D_MODEL), lambda i: (i, 0)),
                  pl.BlockSpec((D_MODEL, D_MODEL), lambda i: (0, 0)),
                  pl.BlockSpec((ROW_TILE, D_MODEL), lambda i: (i, 0)),
                  pl.BlockSpec((None, N_MOD, D_MODEL), lambda i: (i // tiles_per_batch, 0, 0)),
                  pl.BlockSpec((6, D_MODEL), lambda i: (0, 0))],
        out_specs=pl.BlockSpec((ROW_TILE, D_MODEL), lambda i: (i, 0)),
        out_shape=jax.ShapeDtypeStruct((m.shape[0], D_MODEL), F32),
        compiler_params=_params(("arbitrary",)),
        name="outproj",
    )(m, w_out, h1, mod, norm_g)


def kernel(x, c, ctx, c_ctx, w_ada, b_ada, norm_g, ffn_wg, ffn_wu, ffn_wd, w_in, gla_wg, gla_bg, gla_norm_g,
           w_gla_o, na_rpb, w_na_o, w_out):
    batch, seq, d = x.shape
    ctx_len = ctx.shape[1]
    depth = w_ada.shape[0]
    assert d == D_MODEL and batch == 2 and batch * ctx_len == ROW_TILE and depth == 1
    assert seq % ROW_TILE == 0 and seq // GRID_W == GRID_W

    qk_w = GLA_HEADS * GLA_DK
    v_w = GLA_HEADS * GLA_DV
    na_w = NA_HEADS * NA_DH
    gate0 = 2 * qk_w + 2 * v_w
    gate1 = gate0 + 2 * GLA_GATE_RANK
    col_nq = gate0
    col_m1 = col_nq + 3 * na_w

    cvec = jnp.zeros((8, d), F32).at[0:batch].set(c).at[batch].set(c_ctx)
    cos, sin = _rope_tables(seq)
    h = x.reshape(batch * seq, d)
    hc = ctx.reshape(batch * ctx_len, d)
    n_rows = batch * (seq + ctx_len)
    lat_tiles_per_batch = seq // BIG_ROW_TILE
    for l in range(depth):
        mod = _modulation(cvec, w_ada[l], b_ada[l].reshape(1, -1)).reshape(8, N_MOD, d)
        g = norm_g[l]
        h1 = _ffn(h, mod, g, ffn_wg[l], ffn_wu[l], ffn_wd[l], 0, 0, tm=BIG_ROW_TILE, tf=FF_TILE_BIG,
                  mod_row=lambda i: i // lat_tiles_per_batch, out_rows=n_rows)
        h1 = _ffn(hc, mod, g, ffn_wg[l], ffn_wu[l], ffn_wd[l], 0, 0, tm=ROW_TILE, tf=FF_TILE,
                  mod_row=lambda i: batch, out_rows=n_rows, out_tile0=batch * seq // ROW_TILE, into=h1)
        w_gate = jnp.pad(w_in[l][:, gate0:gate1], ((0, 0), (0, LANES - 2 * GLA_GATE_RANK)))
        p, gin = _inproj(h1, mod, g, w_in[l], w_in[l][:, gate1:], w_gate, cos, sin, seq, gate0)
        wg_pad = jnp.zeros((2, LANES, qk_w), F32)
        wg_pad = wg_pad.at[0, :GLA_GATE_RANK].set(gla_wg[l, 0])
        wg_pad = wg_pad.at[1, GLA_GATE_RANK:2 * GLA_GATE_RANK].set(gla_wg[l, 1])
        bg = gla_bg[l].reshape(2, 1, qk_w)
        gn = gla_norm_g[l].reshape(1, GLA_DV)
        o_fwd, o_bwd = _gla(p, gin, wg_pad, bg, batch, seq)
        o_na = _na(p, _na_bias_tables(na_rpb[l]), batch, seq, ctx_len, col_nq)
        m = _merge(o_fwd, o_bwd, gn, o_na, w_gla_o[l].astype(BF16), w_na_o[l].astype(BF16), p,
                   2 * qk_w + v_w, col_m1)
        h2 = _outproj(m, w_out[l].astype(BF16), h1, mod, g)
        h = _ffn(h2, mod, g, ffn_wg[l], ffn_wu[l], ffn_wd[l], 1, 2, tm=BIG_ROW_TILE, tf=FF_TILE_BIG,
                 mod_row=lambda i: i // lat_tiles_per_batch)
    return h.reshape(batch, seq, d)
```

```python
import functools

import numpy as np
import jax
import jax.numpy as jnp
from jax import lax
from jax.experimental import pallas as pl
from jax.experimental.pallas import tpu as pltpu

F32 = jnp.float32
BF16 = jnp.bfloat16

D_MODEL = 2048
GRID_W = 64
GLA_HEADS = 4
GLA_DK = 256
GLA_DV = 512
GLA_GATE_RANK = 16
GLA_GATE_TEMP = 16.0
NA_HEADS = 16
NA_DH = 64
NA_KR = 8
NA_KC = 16
ROPE_THETA = 10000.0
EPS = 1e-6
NEG_INF = -1e30
N_MOD = 9
LOG2E = 1.4426950408889634

LANES = 128
ROW_TILE = 512
BIG_ROW_TILE = 1024
FF_TILE = 512
FF_TILE_BIG = 256
PROJ_TILE = 512
GLA_BLOCK = 256
GLA_CHUNK = 128
NA_QROWS = 8
NA_WIN_ROWS = 16
VMEM_LIMIT = 56 * 1024 * 1024


def _params(sem):
    return pltpu.CompilerParams(dimension_semantics=sem, vmem_limit_bytes=VMEM_LIMIT)


def _rms(x, g):
    return x * lax.rsqrt(jnp.mean(x * x, axis=-1, keepdims=True) + EPS) * g


def _silu(x):
    return x * jax.nn.sigmoid(x)


def _mod_kernel(c_ref, w_ref, b_ref, o_ref):
    o_ref[...] = jnp.dot(_silu(c_ref[...]), w_ref[...], preferred_element_type=F32) + b_ref[...]


def _modulation(cvec, w_ada, b_ada):
    n = w_ada.shape[1]
    tn = 1024
    return pl.pallas_call(
        _mod_kernel,
        grid=(n // tn,),
        in_specs=[pl.BlockSpec((8, D_MODEL), lambda j: (0, 0)),
                  pl.BlockSpec((D_MODEL, tn), lambda j: (0, j)),
                  pl.BlockSpec((1, tn), lambda j: (0, j))],
        out_specs=pl.BlockSpec((8, tn), lambda j: (0, j)),
        out_shape=jax.ShapeDtypeStruct((8, n), F32),
        compiler_params=_params(("arbitrary",)),
        name="adaln_mod",
    )(cvec, w_ada, b_ada)


def _mixed_dot(a, w):
    return lax.dot_general(a, w, (((1,), (0,)), ((), ())), preferred_element_type=F32)


def _ffn_kernel(x_ref, mod_ref, g_ref, wg_ref, wu_ref, wd_ref, *rest, sub):
    o_ref, h_ref = rest[-2:]
    f = pl.program_id(1)

    @pl.when(f == 0)
    def _():
        shift = mod_ref[3 * sub:3 * sub + 1, :]
        scale = mod_ref[3 * sub + 1:3 * sub + 2, :]
        h = _rms(x_ref[...], g_ref[2 * sub:2 * sub + 1, :]) * (1.0 + scale) + shift
        h_ref[...] = h.astype(BF16)
        o_ref[...] = jnp.zeros_like(o_ref)

    h = h_ref[...]
    g = _mixed_dot(h, wg_ref[...])
    u = _mixed_dot(h, wu_ref[...])
    o_ref[...] += _mixed_dot((_silu(g) * u).astype(BF16), wd_ref[...])

    @pl.when(f == pl.num_programs(1) - 1)
    def _():
        gate = mod_ref[3 * sub + 2:3 * sub + 3, :]
        y = _rms(o_ref[...], g_ref[2 * sub + 1:2 * sub + 2, :])
        o_ref[...] = x_ref[...] + 0.5 * gate * y


def _ffn(x2d, mod, norm_g, wg, wu, wd, which, sub, *, tm, tf, mod_row, out_rows=None, out_tile0=0, into=None):
    n_tiles = x2d.shape[0] // tm
    out_rows = out_rows or x2d.shape[0]
    d_ff = wg.shape[2]
    in_specs = [
        pl.BlockSpec((tm, D_MODEL), lambda i, f: (i, 0), pipeline_mode=pl.Buffered(1)),
        pl.BlockSpec((None, N_MOD, D_MODEL), lambda i, f: (mod_row(i), 0, 0)),
        pl.BlockSpec((6, D_MODEL), lambda i, f: (0, 0)),
        pl.BlockSpec((None, D_MODEL, tf), lambda i, f: (which, 0, f)),
        pl.BlockSpec((None, D_MODEL, tf), lambda i, f: (which, 0, f)),
        pl.BlockSpec((None, tf, D_MODEL), lambda i, f: (which, f, 0)),
    ]
    args = [x2d, mod, norm_g, wg, wu, wd]
    aliases = {}
    if into is not None:
        in_specs.append(pl.BlockSpec(memory_space=pl.ANY))
        args.append(into)
        aliases = {len(args) - 1: 0}
    return pl.pallas_call(
        functools.partial(_ffn_kernel, sub=sub),
        grid=(n_tiles, d_ff // tf),
        in_specs=in_specs,
        out_specs=pl.BlockSpec((tm, D_MODEL), lambda i, f: (out_tile0 + i, 0)),
        out_shape=jax.ShapeDtypeStruct((out_rows, D_MODEL), F32),
        scratch_shapes=[pltpu.VMEM((tm, D_MODEL), BF16)],
        input_output_aliases=aliases,
        compiler_params=_params(("arbitrary", "arbitrary")),
        name="ffn%d%s" % (sub, "" if into is None else "_ctx"),
    )(*args)


N_ROPE_TILES = 2 * GLA_HEADS * GLA_DK // PROJ_TILE
N_Q_TILES = N_ROPE_TILES // 2


def _rope_tables(seq):
    half = GLA_DK // 4
    freqs = ROPE_THETA ** (-np.arange(half, dtype=np.float64) / half)
    t = np.arange(seq)
    cos_parts, sin_parts = [], []
    for pos in (t // GRID_W, t % GRID_W):
        ang = pos[:, None].astype(np.float64) * freqs
        cos_parts += [np.cos(ang), np.cos(ang)]
        sin_parts += [-np.sin(ang), np.sin(ang)]
    cos = np.concatenate(cos_parts, axis=1)
    sin = np.concatenate(sin_parts, axis=1)
    cos = np.concatenate([cos, np.ones((BIG_ROW_TILE, GLA_DK))], axis=0)
    sin = np.concatenate([sin, np.zeros((BIG_ROW_TILE, GLA_DK))], axis=0)
    return jnp.asarray(cos, F32), jnp.asarray(sin, F32)


def _dot_nt(a, w_t):
    return lax.dot_general(a, w_t, (((1,), (1,)), ((), ())), preferred_element_type=F32)


def _inproj_kernel(x_ref, mod_ref, g_ref, w_ref, wgate_ref, cos_ref, sin_ref, p_ref, gate_ref, h_ref):
    n = pl.program_id(1)

    @pl.when(n == 0)
    def _():
        h = _rms(x_ref[...], g_ref[2:3, :]) * (1.0 + mod_ref[4:5, :]) + mod_ref[3:4, :]
        h_ref[...] = h.astype(BF16)
        gate = _dot_nt(h_ref[...], wgate_ref[...])
        gate_ref[...] = jnp.concatenate(
            [gate, jnp.zeros((gate.shape[0], LANES - gate.shape[1]), F32)], axis=1)

    @pl.when(n < N_ROPE_TILES)
    def _():
        y = _dot_nt(h_ref[...], w_ref[...])
        cos = cos_ref[...]
        sin = sin_ref[...]
        qscale = jnp.where(n < N_Q_TILES, GLA_DK ** -0.5, 1.0).astype(F32)
        for j in range(PROJ_TILE // GLA_DK):
            yj = y[:, j * GLA_DK:(j + 1) * GLA_DK]
            swapped = jnp.concatenate(
                [pltpu.roll(yj[:, g * LANES:(g + 1) * LANES], LANES // 2, 1) for g in range(GLA_DK // LANES)],
                axis=1)
            p_ref[:, j * GLA_DK:(j + 1) * GLA_DK] = ((yj * cos + swapped * sin) * qscale).astype(BF16)

    @pl.when(n >= N_ROPE_TILES)
    def _():
        p_ref[...] = _dot_nt(h_ref[...], w_ref[...]).astype(BF16)


def _inproj(h1, mod, norm_g, w_in_t, cos, sin, seq, gate0, gate1):
    tm = BIG_ROW_TILE
    n_tiles = pl.cdiv(h1.shape[0], tm)
    n_lat_tiles = seq * 2 // tm
    tiles_per_batch = n_lat_tiles // 2
    n_a_tiles = gate0 // PROJ_TILE
    n_b_tiles = (w_in_t.shape[0] - gate1) // PROJ_TILE
    n_gate = gate1 - gate0
    tab_spec = pl.BlockSpec(
        (tm, GLA_DK), lambda i, n: (jnp.where(i < n_lat_tiles, i % tiles_per_batch, seq // tm), 0))
    return pl.pallas_call(
        _inproj_kernel,
        grid=(n_tiles, n_a_tiles + n_b_tiles),
        in_specs=[pl.BlockSpec((tm, D_MODEL), lambda i, n: (i, 0)),
                  pl.BlockSpec((None, N_MOD, D_MODEL), lambda i, n: (jnp.minimum(i // tiles_per_batch, 2), 0, 0)),
                  pl.BlockSpec((6, D_MODEL), lambda i, n: (0, 0)),
                  pl.BlockSpec((pl.Element(PROJ_TILE), pl.Element(D_MODEL)),
                               lambda i, n: (pl.multiple_of(
                                   n * PROJ_TILE + jnp.where(n < n_a_tiles, 0, n_gate), n_gate), 0)),
                  pl.BlockSpec((n_gate, D_MODEL), lambda i, n: (gate0 // n_gate, 0)),
                  tab_spec, tab_spec],
        out_specs=[pl.BlockSpec((tm, PROJ_TILE), lambda i, n: (i, n)),
                   pl.BlockSpec((tm, LANES), lambda i, n: (i, 0))],
        out_shape=[jax.ShapeDtypeStruct((h1.shape[0], (n_a_tiles + n_b_tiles) * PROJ_TILE), BF16),
                   jax.ShapeDtypeStruct((h1.shape[0], LANES), F32)],
        scratch_shapes=[pltpu.VMEM((tm, D_MODEL), BF16)],
        compiler_params=_params(("arbitrary", "arbitrary")),
        name="inproj",
    )(h1, mod, norm_g, w_in_t, w_in_t, cos, sin)


def _log_sigmoid(z):
    return jnp.minimum(z, 0.0) - jnp.log1p(jnp.exp(-jnp.abs(z)))


def _gla_scan_block(q_ref, k_ref, v_ref, gin_ref, wg, bg, st, forward):
    c = GLA_CHUNK
    row = lax.broadcasted_iota(jnp.int32, (c, c), 0)
    col = lax.broadcasted_iota(jnp.int32, (c, c), 1)
    mask = (col <= row) if forward else (col >= row)
    tri = jnp.where(mask, 1.0, 0.0).astype(BF16)
    nt = (((1,), (1,)), ((), ()))
    z = jnp.dot(gin_ref[...], wg, preferred_element_type=F32) + bg
    log_a = _log_sigmoid(z) * (LOG2E / GLA_GATE_TEMP)
    n_chunks = GLA_BLOCK // c
    outs = [None] * n_chunks
    for ci in (range(n_chunks) if forward else range(n_chunks - 1, -1, -1)):
        rows = slice(ci * c, (ci + 1) * c)
        q = q_ref[rows, :].astype(F32)
        k = k_ref[rows, :].astype(F32)
        v = v_ref[rows, :]
        la = log_a[rows, :]
        hi = la.astype(BF16)
        lo = (la - hi.astype(F32)).astype(BF16)
        cum = jnp.dot(tri, hi, preferred_element_type=F32) + jnp.dot(tri, lo, preferred_element_type=F32)
        cum_end = cum[c - 1:c, :] if forward else cum[0:1, :]
        q_dec = (q * jnp.exp2(cum)).astype(BF16)
        k_inv = (k * jnp.exp2(-cum)).astype(BF16)
        k_end = (k * jnp.exp2(cum_end - cum)).astype(BF16)
        att = lax.dot_general(q_dec, k_inv, nt, preferred_element_type=F32)
        att = jnp.where(mask, att, 0.0).astype(BF16)
        outs[ci] = (jnp.dot(att, v, preferred_element_type=F32)
                    + lax.dot_general(q_dec, st.astype(BF16), nt, preferred_element_type=F32))
        kv_t = lax.dot_general(v, k_end, (((0,), (0,)), ((), ())), preferred_element_type=F32)
        st = jnp.exp2(cum_end) * st + kv_t
    return jnp.concatenate(outs, axis=0), st


def _gla_kernel(qf_ref, kf_ref, vf_ref, gf_ref, qb_ref, kb_ref, vb_ref, gb_ref, wg_ref, bg_ref,
                of_ref, ob_ref, stf_ref, stb_ref):
    @pl.when(pl.program_id(2) == 0)
    def _():
        stf_ref[...] = jnp.zeros_like(stf_ref)
        stb_ref[...] = jnp.zeros_like(stb_ref)

    of, stf = _gla_scan_block(qf_ref, kf_ref, vf_ref, gf_ref, wg_ref[0], bg_ref[0], stf_ref[...], True)
    ob, stb = _gla_scan_block(qb_ref, kb_ref, vb_ref, gb_ref, wg_ref[1], bg_ref[1], stb_ref[...], False)
    of_ref[...] = of.astype(of_ref.dtype)
    ob_ref[...] = ob.astype(ob_ref.dtype)
    stf_ref[...] = stf
    stb_ref[...] = stb


def _gla(p, gin, wg_pad, bg, batch, seq):
    n_lat = seq // GLA_BLOCK
    ctx_block0 = batch * n_lat

    def lat_block(b, s, forward):
        j = jnp.maximum(s - 1, 0)
        return b * n_lat + (j if forward else n_lat - 1 - j)

    def in_block(b, s, forward):
        return jnp.where(s == 0, ctx_block0 + b, lat_block(b, s, forward))

    kq = GLA_HEADS
    kv = 2 * GLA_HEADS * GLA_DK // GLA_DV

    def dir_specs(forward):
        return [
            pl.BlockSpec((GLA_BLOCK, GLA_DK), lambda b, h, s: (in_block(b, s, forward), h)),
            pl.BlockSpec((GLA_BLOCK, GLA_DK), lambda b, h, s: (in_block(b, s, forward), kq + h)),
            pl.BlockSpec((GLA_BLOCK, GLA_DV), lambda b, h, s: (in_block(b, s, forward), kv + h)),
            pl.BlockSpec((GLA_BLOCK, LANES), lambda b, h, s: (in_block(b, s, forward), 0)),
        ]

    out_shape = jax.ShapeDtypeStruct((batch * seq, GLA_HEADS * GLA_DV), BF16)
    return pl.pallas_call(
        _gla_kernel,
        grid=(batch, GLA_HEADS, n_lat + 1),
        in_specs=dir_specs(True) + dir_specs(False) + [
            pl.BlockSpec((2, LANES, GLA_DK), lambda b, h, s: (0, 0, h)),
            pl.BlockSpec((2, 1, GLA_DK), lambda b, h, s: (0, 0, h))],
        out_specs=[pl.BlockSpec((GLA_BLOCK, GLA_DV), lambda b, h, s: (lat_block(b, s, True), h)),
                   pl.BlockSpec((GLA_BLOCK, GLA_DV), lambda b, h, s: (lat_block(b, s, False), h))],
        out_shape=[out_shape, out_shape],
        scratch_shapes=[pltpu.VMEM((GLA_DV, GLA_DK), F32), pltpu.VMEM((GLA_DV, GLA_DK), F32)],
        compiler_params=_params(("arbitrary", "arbitrary", "arbitrary")),
        name="gla",
    )(p, p, p, gin, p, p, p, gin, wg_pad, bg)


NA_QTOK = NA_QROWS * GRID_W
NA_WIN = NA_WIN_ROWS * GRID_W
NA_KINDS = ((0, 0), (NA_QROWS, NA_QROWS - NA_KR // 2), (GRID_W - NA_QROWS, GRID_W - NA_WIN_ROWS))


def _na_build_bias(rpb_ref, t_ref, rows):
    n_slots = 2 * NA_KR - 1
    lane = lax.broadcasted_iota(jnp.int32, (GRID_W, LANES), 1)
    q_col = lax.broadcasted_iota(jnp.int32, (GRID_W, LANES), 0)
    k_col = lane & (GRID_W - 1)
    upper = lane >= GRID_W
    c0 = jnp.clip(q_col - NA_KC // 2, 0, GRID_W - NA_KC)
    col_ok = (k_col >= c0) & (k_col < c0 + NA_KC)
    neg = jnp.full((GRID_W, LANES), NEG_INF, F32)

    def toeplitz(hh, slot, lane_off):
        if not 0 <= slot < n_slots:
            return jnp.zeros((GRID_W, LANES), F32)
        base = jnp.broadcast_to(rpb_ref[hh, slot:slot + 1, :], (GRID_W, LANES))
        return pltpu.roll(base, (LANES - (NA_KC - 1) + lane_off) % LANES, 1, stride=1, stride_axis=0)

    for hh in range(2):
        pairs = {}
        for kind, (r0, w0) in enumerate(NA_KINDS):
            for qr in range(NA_QROWS):
                band0 = min(max(r0 + qr - NA_KR // 2, 0), rows - NA_KR) - w0
                slot0 = w0 - (r0 + qr) + NA_KR - 1
                for g in range(NA_WIN_ROWS // 2):
                    slot = slot0 + 2 * g
                    lo_ok = band0 <= 2 * g < band0 + NA_KR
                    hi_ok = band0 <= 2 * g + 1 < band0 + NA_KR
                    if lo_ok or hi_ok:
                        if slot not in pairs:
                            pairs[slot] = jnp.where(upper, toeplitz(hh, slot + 1, GRID_W), toeplitz(hh, slot, 0))
                        ok = col_ok if (lo_ok and hi_ok) else (col_ok & upper if hi_ok else col_ok & ~upper)
                        tile = jnp.where(ok, pairs[slot], neg)
                    else:
                        tile = neg
                    t_ref[hh, kind, qr * GRID_W:(qr + 1) * GRID_W, g * LANES:(g + 1) * LANES] = tile


def _na_kernel(q_ref, k_ref, v_ref, kc_ref, vc_ref, rpb_ref, o_ref, t_ref, *, rows):
    b = pl.program_id(1)
    blk = pl.program_id(2)
    n_blk = rows // NA_QROWS

    pl.when((b == 0) & (blk == 0))(lambda: _na_build_bias(rpb_ref, t_ref, rows))

    kind = jnp.where(blk == 0, 0, jnp.where(blk == n_blk - 1, 2, 1))
    w0 = jnp.clip(blk * NA_QROWS - NA_KR // 2, 0, rows - NA_WIN_ROWS)
    win = pl.ds(pl.multiple_of(w0 * GRID_W, 256), NA_WIN)
    q = q_ref[...]
    kw = k_ref[win, :]
    vw = v_ref[win, :]
    kc = kc_ref[...]
    vc = vc_ref[...]
    lane = lax.broadcasted_iota(jnp.int32, q.shape, 1)
    scale = NA_DH ** -0.5
    nt = (((1,), (1,)), ((), ()))
    outs = []
    for hh in range(2):
        own = (lane < NA_DH) if hh == 0 else (lane >= NA_DH)
        qh = jnp.where(own, q, jnp.zeros_like(q))
        s_lat = lax.dot_general(qh, kw, nt, preferred_element_type=F32) * scale + t_ref[hh, kind]
        s_ctx = lax.dot_general(qh, kc, nt, preferred_element_type=F32) * scale
        m = jnp.maximum(jnp.max(s_lat, axis=-1, keepdims=True), jnp.max(s_ctx, axis=-1, keepdims=True))
        p_lat = jnp.exp(s_lat - m)
        p_ctx = jnp.exp(s_ctx - m)
        denom = jnp.sum(p_lat, axis=-1, keepdims=True) + jnp.sum(p_ctx, axis=-1, keepdims=True)
        o = (jnp.dot(p_lat.astype(BF16), vw, preferred_element_type=F32)
             + jnp.dot(p_ctx.astype(BF16), vc, preferred_element_type=F32))
        outs.append(o / denom)
    o_ref[...] = jnp.where(lane < NA_DH, outs[0], outs[1]).astype(o_ref.dtype)


def _na(p, rpb, batch, seq, ctx_len, col0):
    rows = seq // GRID_W
    n_blk = rows // NA_QROWS
    n_pairs = NA_HEADS // 2
    cq = col0 // LANES
    ck = cq + n_pairs
    cv = ck + n_pairs
    ctx_block0 = batch * seq // ctx_len
    return pl.pallas_call(
        functools.partial(_na_kernel, rows=rows),
        grid=(n_pairs, batch, n_blk),
        in_specs=[
            pl.BlockSpec((NA_QTOK, LANES), lambda hp, b, r: (b * n_blk + r, cq + hp)),
            pl.BlockSpec((seq, LANES), lambda hp, b, r: (b, ck + hp)),
            pl.BlockSpec((seq, LANES), lambda hp, b, r: (b, cv + hp)),
            pl.BlockSpec((ctx_len, LANES), lambda hp, b, r: (ctx_block0 + b, ck + hp)),
            pl.BlockSpec((ctx_len, LANES), lambda hp, b, r: (ctx_block0 + b, cv + hp)),
            pl.BlockSpec((2, 2 * NA_KR - 1, LANES), lambda hp, b, r: (hp, 0, 0)),
        ],
        out_specs=pl.BlockSpec((NA_QTOK, LANES), lambda hp, b, r: (b * n_blk + r, hp)),
        out_shape=jax.ShapeDtypeStruct((batch * seq, NA_HEADS * NA_DH), BF16),
        scratch_shapes=[pltpu.VMEM((2, len(NA_KINDS), NA_QTOK, NA_WIN), F32)],
        compiler_params=_params(("arbitrary", "arbitrary", "arbitrary")),
        name="natten",
    )(p, p, p, p, p, rpb)


def _merge_kernel(of_ref, ob_ref, r_ref, gn_ref, b_ref, wa_ref, wb_ref, m1_ref, m2_ref, o_ref, a_ref):
    @pl.when(pl.program_id(1) == 0)
    def _():
        for h in range(GLA_HEADS):
            cols = slice(h * GLA_DV, (h + 1) * GLA_DV)
            tot = of_ref[:, cols].astype(F32) + ob_ref[:, cols].astype(F32)
            a_ref[:, cols] = (_rms(tot, gn_ref[...]) * _silu(r_ref[:, cols].astype(F32))).astype(BF16)

    a = jnp.dot(a_ref[...], wa_ref[...], preferred_element_type=F32)
    b = jnp.dot(b_ref[...], wb_ref[...], preferred_element_type=F32)
    m = jax.nn.sigmoid(m1_ref[...].astype(F32)) * a + jax.nn.sigmoid(m2_ref[...].astype(F32)) * b
    o_ref[...] = m.astype(o_ref.dtype)


def _merge(o_fwd, o_bwd, gn, o_na, w_gla_o, w_na_o, p, col_r, col_m1):
    n_rows = o_fwd.shape[0]
    v_w = o_fwd.shape[1]
    tn = PROJ_TILE
    c1 = col_m1 // tn
    c2 = c1 + D_MODEL // tn
    return pl.pallas_call(
        _merge_kernel,
        grid=(n_rows // ROW_TILE, D_MODEL // tn),
        in_specs=[pl.BlockSpec((ROW_TILE, v_w), lambda i, n: (i, 0)),
                  pl.BlockSpec((ROW_TILE, v_w), lambda i, n: (i, 0)),
                  pl.BlockSpec((ROW_TILE, v_w), lambda i, n: (i, col_r // v_w)),
                  pl.BlockSpec((1, GLA_DV), lambda i, n: (0, 0)),
                  pl.BlockSpec((ROW_TILE, o_na.shape[1]), lambda i, n: (i, 0)),
                  pl.BlockSpec((w_gla_o.shape[0], tn), lambda i, n: (0, n)),
                  pl.BlockSpec((w_na_o.shape[0], tn), lambda i, n: (0, n)),
                  pl.BlockSpec((ROW_TILE, tn), lambda i, n: (i, c1 + n)),
                  pl.BlockSpec((ROW_TILE, tn), lambda i, n: (i, c2 + n))],
        out_specs=pl.BlockSpec((ROW_TILE, tn), lambda i, n: (i, n)),
        out_shape=jax.ShapeDtypeStruct((n_rows, D_MODEL), BF16),
        scratch_shapes=[pltpu.VMEM((ROW_TILE, v_w), BF16)],
        compiler_params=_params(("arbitrary", "arbitrary")),
        name="merge",
    )(o_fwd, o_bwd, p, gn, o_na, w_gla_o, w_na_o, p, p)


def _outproj_kernel(m_ref, w_ref, x_ref, mod_ref, g_ref, o_ref):
    y = jnp.dot(m_ref[...], w_ref[...], preferred_element_type=F32)
    o_ref[...] = x_ref[...] + mod_ref[5:6, :] * _rms(y, g_ref[3:4, :])


def _outproj(m, w_out, h1, mod, norm_g):
    n_tiles = m.shape[0] // ROW_TILE
    tiles_per_batch = n_tiles // 2
    return pl.pallas_call(
        _outproj_kernel,
        grid=(n_tiles,),
        in_specs=[pl.BlockSpec((ROW_TILE, D_MODEL), lambda i: (i, 0)),
                  pl.BlockSpec((D_MODEL, D_MODEL), lambda i: (0, 0)),
                  pl.BlockSpec((ROW_TILE, D_MODEL), lambda i: (i, 0)),
                  pl.BlockSpec((None, N_MOD, D_MODEL), lambda i: (i // tiles_per_batch, 0, 0)),
                  pl.BlockSpec((6, D_MODEL), lambda i: (0, 0))],
        out_specs=pl.BlockSpec((ROW_TILE, D_MODEL), lambda i: (i, 0)),
        out_shape=jax.ShapeDtypeStruct((m.shape[0], D_MODEL), F32),
        compiler_params=_params(("arbitrary",)),
        name="outproj",
    )(m, w_out, h1, mod, norm_g)


def kernel(x, c, ctx, c_ctx, w_ada, b_ada, norm_g, ffn_wg, ffn_wu, ffn_wd, w_in, gla_wg, gla_bg, gla_norm_g,
           w_gla_o, na_rpb, w_na_o, w_out):
    batch, seq, d = x.shape
    ctx_len = ctx.shape[1]
    depth = w_ada.shape[0]
    assert d == D_MODEL and batch == 2 and batch * ctx_len == ROW_TILE and depth == 1
    assert seq % ROW_TILE == 0 and seq // GRID_W == GRID_W

    qk_w = GLA_HEADS * GLA_DK
    v_w = GLA_HEADS * GLA_DV
    na_w = NA_HEADS * NA_DH
    gate0 = 2 * qk_w + 2 * v_w
    gate1 = gate0 + 2 * GLA_GATE_RANK
    col_nq = gate0
    col_m1 = col_nq + 3 * na_w

    cvec = jnp.zeros((8, d), F32).at[0:batch].set(c).at[batch].set(c_ctx)
    cos, sin = _rope_tables(seq)
    h = x.reshape(batch * seq, d)
    hc = ctx.reshape(batch * ctx_len, d)
    n_rows = batch * (seq + ctx_len)
    lat_tiles_per_batch = seq // BIG_ROW_TILE
    for l in range(depth):
        mod = _modulation(cvec, w_ada[l], b_ada[l].reshape(1, -1)).reshape(8, N_MOD, d)
        g = norm_g[l]
        h1 = _ffn(h, mod, g, ffn_wg[l], ffn_wu[l], ffn_wd[l], 0, 0, tm=BIG_ROW_TILE, tf=FF_TILE_BIG,
                  mod_row=lambda i: i // lat_tiles_per_batch, out_rows=n_rows)
        h1 = _ffn(hc, mod, g, ffn_wg[l], ffn_wu[l], ffn_wd[l], 0, 0, tm=ROW_TILE, tf=FF_TILE,
                  mod_row=lambda i: batch, out_rows=n_rows, out_tile0=batch * seq // ROW_TILE, into=h1)
        p, gin = _inproj(h1, mod, g, jnp.swapaxes(w_in[l], 0, 1), cos, sin, seq, gate0, gate1)
        wg_pad = jnp.zeros((2, LANES, qk_w), F32)
        wg_pad = wg_pad.at[0, :GLA_GATE_RANK].set(gla_wg[l, 0])
        wg_pad = wg_pad.at[1, GLA_GATE_RANK:2 * GLA_GATE_RANK].set(gla_wg[l, 1])
        bg = gla_bg[l].reshape(2, 1, qk_w)
        gn = gla_norm_g[l].reshape(1, GLA_DV)
        o_fwd, o_bwd = _gla(p, gin, wg_pad, bg, batch, seq)
        rpb = jnp.pad(na_rpb[l], ((0, 0), (0, 0), (0, LANES - (2 * NA_KC - 1))))
        o_na = _na(p, rpb, batch, seq, ctx_len, col_nq)
        m = _merge(o_fwd, o_bwd, gn, o_na, w_gla_o[l].astype(BF16), w_na_o[l].astype(BF16), p,
                   2 * qk_w + v_w, col_m1)
        h2 = _outproj(m, w_out[l].astype(BF16), h1, mod, g)
        h = _ffn(h2, mod, g, ffn_wg[l], ffn_wu[l], ffn_wd[l], 1, 2, tm=BIG_ROW_TILE, tf=FF_TILE_BIG,
                 mod_row=lambda i: i // lat_tiles_per_batch)
    return h.reshape(batch, seq, d)
```

```python
import functools

import numpy as np
import jax
import jax.numpy as jnp
from jax import lax
from jax.experimental import pallas as pl
from jax.experimental.pallas import tpu as pltpu

F32 = jnp.float32
BF16 = jnp.bfloat16

D_MODEL = 2048
GRID_W = 64
GLA_HEADS = 4
GLA_DK = 256
GLA_DV = 512
GLA_GATE_RANK = 16
GLA_GATE_TEMP = 16.0
NA_HEADS = 16
NA_DH = 64
NA_KR = 8
NA_KC = 16
ROPE_THETA = 10000.0
EPS = 1e-6
NEG_INF = -1e30
N_MOD = 9
LOG2E = 1.4426950408889634

LANES = 128
ROW_TILE = 512
BIG_ROW_TILE = 1024
FF_TILE = 512
PROJ_TILE = 1024
MERGE_TILE = 1024
GLA_BLOCK = 256
GLA_CHUNK = 128
NA_QROWS = 8
NA_WIN_ROWS = 16
VMEM_LIMIT = 56 * 1024 * 1024


def _params(sem):
    return pltpu.CompilerParams(dimension_semantics=sem, vmem_limit_bytes=VMEM_LIMIT)


def _rms(x, g):
    return x * lax.rsqrt(jnp.mean(x * x, axis=-1, keepdims=True) + EPS) * g


def _silu(x):
    return x * jax.nn.sigmoid(x)


def _mod_kernel(c_ref, w_ref, b_ref, o_ref):
    o_ref[...] = jnp.dot(_silu(c_ref[...]), w_ref[...], preferred_element_type=F32) + b_ref[...]


def _modulation(cvec, w_ada, b_ada):
    n = w_ada.shape[1]
    tn = 1024
    return pl.pallas_call(
        _mod_kernel,
        grid=(n // tn,),
        in_specs=[pl.BlockSpec((8, D_MODEL), lambda j: (0, 0)),
                  pl.BlockSpec((D_MODEL, tn), lambda j: (0, j)),
                  pl.BlockSpec((1, tn), lambda j: (0, j))],
        out_specs=pl.BlockSpec((8, tn), lambda j: (0, j)),
        out_shape=jax.ShapeDtypeStruct((8, n), F32),
        compiler_params=_params(("arbitrary",)),
        name="adaln_mod",
    )(cvec, w_ada, b_ada)


def _mixed_dot(a, w):
    return lax.dot_general(a, w, (((1,), (0,)), ((), ())), preferred_element_type=F32)


def _ffn_kernel(x_ref, mod_ref, g_ref, wg_ref, wu_ref, wd_ref, o_ref, h_ref, *, sub):
    f = pl.program_id(1)

    @pl.when(f == 0)
    def _():
        shift = mod_ref[3 * sub:3 * sub + 1, :]
        scale = mod_ref[3 * sub + 1:3 * sub + 2, :]
        h = _rms(x_ref[...], g_ref[2 * sub:2 * sub + 1, :]) * (1.0 + scale) + shift
        h_ref[...] = h.astype(BF16)
        o_ref[...] = jnp.zeros_like(o_ref)

    for r0 in range(0, h_ref.shape[0], ROW_TILE):
        rows = slice(r0, r0 + ROW_TILE)
        h = h_ref[rows, :]
        g = _mixed_dot(h, wg_ref[...])
        u = _mixed_dot(h, wu_ref[...])
        o_ref[rows, :] += _mixed_dot((_silu(g) * u).astype(BF16), wd_ref[...])

    @pl.when(f == pl.num_programs(1) - 1)
    def _():
        gate = mod_ref[3 * sub + 2:3 * sub + 3, :]
        y = _rms(o_ref[...], g_ref[2 * sub + 1:2 * sub + 2, :])
        o_ref[...] = x_ref[...] + 0.5 * gate * y


def _ffn(x2d, mod, norm_g, wg, wu, wd, which, sub, *, tm, tf, mod_row):
    n_tiles = x2d.shape[0] // tm
    d_ff = wg.shape[2]
    once = dict(pipeline_mode=pl.Buffered(1))
    return pl.pallas_call(
        functools.partial(_ffn_kernel, sub=sub),
        grid=(n_tiles, d_ff // tf),
        in_specs=[
            pl.BlockSpec((tm, D_MODEL), lambda i, f: (i, 0), **once),
            pl.BlockSpec((None, N_MOD, D_MODEL), lambda i, f: (mod_row(i), 0, 0)),
            pl.BlockSpec((6, D_MODEL), lambda i, f: (0, 0)),
            pl.BlockSpec((None, D_MODEL, tf), lambda i, f: (which, 0, f)),
            pl.BlockSpec((None, D_MODEL, tf), lambda i, f: (which, 0, f)),
            pl.BlockSpec((None, tf, D_MODEL), lambda i, f: (which, f, 0)),
        ],
        out_specs=pl.BlockSpec((tm, D_MODEL), lambda i, f: (i, 0), **once),
        out_shape=jax.ShapeDtypeStruct(x2d.shape, F32),
        scratch_shapes=[pltpu.VMEM((tm, D_MODEL), BF16)],
        compiler_params=_params(("arbitrary", "arbitrary")),
        name="ffn%d_%d" % (sub, tm),
    )(x2d, mod, norm_g, wg, wu, wd)


N_ROPE_TILES = 2 * GLA_HEADS * GLA_DK // PROJ_TILE
N_Q_TILES = N_ROPE_TILES // 2


def _rope_tables(seq):
    half = GLA_DK // 4
    freqs = ROPE_THETA ** (-np.arange(half, dtype=np.float64) / half)
    t = np.arange(seq)
    cos_parts, sin_parts = [], []
    for pos in (t // GRID_W, t % GRID_W):
        ang = pos[:, None].astype(np.float64) * freqs
        cos_parts += [np.cos(ang), np.cos(ang)]
        sin_parts += [-np.sin(ang), np.sin(ang)]
    cos = np.concatenate(cos_parts, axis=1)
    sin = np.concatenate(sin_parts, axis=1)
    cos = np.concatenate([cos, np.ones((BIG_ROW_TILE, GLA_DK))], axis=0)
    sin = np.concatenate([sin, np.zeros((BIG_ROW_TILE, GLA_DK))], axis=0)
    return jnp.asarray(cos, F32), jnp.asarray(sin, F32)


def _dot_nt(a, w_t):
    return lax.dot_general(a, w_t, (((1,), (1,)), ((), ())), preferred_element_type=F32)


def _inproj_kernel(x_ref, xc_ref, mod_ref, g_ref, w_ref, wgate_ref, cos_ref, sin_ref, p_ref, gate_ref, h_ref,
                   *, n_lat_tiles):
    i = pl.program_id(0)
    n = pl.program_id(1)

    def project(src_ref):
        r = slice(0, src_ref.shape[0])

        @pl.when(n == 0)
        def _():
            h = _rms(src_ref[...], g_ref[2:3, :]) * (1.0 + mod_ref[4:5, :]) + mod_ref[3:4, :]
            h_ref[r, :] = h.astype(BF16)
            gate = _dot_nt(h_ref[r, :], wgate_ref[...])
            gate_ref[r, :] = jnp.concatenate(
                [gate, jnp.zeros((gate.shape[0], LANES - gate.shape[1]), F32)], axis=1)

        @pl.when(n < N_ROPE_TILES)
        def _():
            y = _dot_nt(h_ref[r, :], w_ref[...])
            cos = cos_ref[r, :]
            sin = sin_ref[r, :]
            qscale = jnp.where(n < N_Q_TILES, GLA_DK ** -0.5, 1.0).astype(F32)
            for j in range(PROJ_TILE // GLA_DK):
                yj = y[:, j * GLA_DK:(j + 1) * GLA_DK]
                swapped = jnp.concatenate(
                    [pltpu.roll(yj[:, g * LANES:(g + 1) * LANES], LANES // 2, 1) for g in range(GLA_DK // LANES)],
                    axis=1)
                p_ref[r, j * GLA_DK:(j + 1) * GLA_DK] = ((yj * cos + swapped * sin) * qscale).astype(BF16)

        @pl.when(n >= N_ROPE_TILES)
        def _():
            p_ref[r, :] = _dot_nt(h_ref[r, :], w_ref[...]).astype(BF16)

    pl.when(i < n_lat_tiles)(lambda: project(x_ref))
    pl.when(i >= n_lat_tiles)(lambda: project(xc_ref))


def _inproj(h1, h1c, mod, norm_g, w_in_t, cos, sin, seq, gate0, gate1):
    tm = BIG_ROW_TILE
    n_lat_tiles = h1.shape[0] // tm
    n_rows = h1.shape[0] + h1c.shape[0]
    tiles_per_batch = n_lat_tiles // 2
    n_a_tiles = gate0 // PROJ_TILE
    n_b_tiles = (w_in_t.shape[0] - gate1) // PROJ_TILE
    n_gate = gate1 - gate0
    tab_spec = pl.BlockSpec(
        (tm, GLA_DK), lambda i, n: (jnp.where(i < n_lat_tiles, i % tiles_per_batch, seq // tm), 0))
    return pl.pallas_call(
        functools.partial(_inproj_kernel, n_lat_tiles=n_lat_tiles),
        grid=(n_lat_tiles + 1, n_a_tiles + n_b_tiles),
        in_specs=[pl.BlockSpec((tm, D_MODEL), lambda i, n: (jnp.minimum(i, n_lat_tiles - 1), 0)),
                  pl.BlockSpec(h1c.shape, lambda i, n: (0, 0)),
                  pl.BlockSpec((None, N_MOD, D_MODEL), lambda i, n: (jnp.minimum(i // tiles_per_batch, 2), 0, 0)),
                  pl.BlockSpec((6, D_MODEL), lambda i, n: (0, 0)),
                  pl.BlockSpec((pl.Element(PROJ_TILE), pl.Element(D_MODEL)),
                               lambda i, n: (pl.multiple_of(
                                   n * PROJ_TILE + jnp.where(n < n_a_tiles, 0, n_gate), n_gate), 0)),
                  pl.BlockSpec((n_gate, D_MODEL), lambda i, n: (gate0 // n_gate, 0)),
                  tab_spec, tab_spec],
        out_specs=[pl.BlockSpec((tm, PROJ_TILE), lambda i, n: (i, n)),
                   pl.BlockSpec((tm, LANES), lambda i, n: (i, 0))],
        out_shape=[jax.ShapeDtypeStruct((n_rows, (n_a_tiles + n_b_tiles) * PROJ_TILE), BF16),
                   jax.ShapeDtypeStruct((n_rows, LANES), F32)],
        scratch_shapes=[pltpu.VMEM((tm, D_MODEL), BF16)],
        compiler_params=_params(("arbitrary", "arbitrary")),
        name="inproj",
    )(h1, h1c, mod, norm_g, w_in_t, w_in_t, cos, sin)


def _log_sigmoid(z):
    return jnp.minimum(z, 0.0) - jnp.log1p(jnp.exp(-jnp.abs(z)))


def _gla_scan_block(q_ref, k_ref, v_ref, gin_ref, wg, bg, st, forward):
    c = GLA_CHUNK
    row = lax.broadcasted_iota(jnp.int32, (c, c), 0)
    col = lax.broadcasted_iota(jnp.int32, (c, c), 1)
    mask = (col <= row) if forward else (col >= row)
    tri = jnp.where(mask, 1.0, 0.0).astype(BF16)
    nt = (((1,), (1,)), ((), ()))
    z = jnp.dot(gin_ref[...], wg, preferred_element_type=F32) + bg
    log_a = _log_sigmoid(z) * (LOG2E / GLA_GATE_TEMP)
    n_chunks = GLA_BLOCK // c
    outs = [None] * n_chunks
    for ci in (range(n_chunks) if forward else range(n_chunks - 1, -1, -1)):
        rows = slice(ci * c, (ci + 1) * c)
        q = q_ref[rows, :].astype(F32)
        k = k_ref[rows, :].astype(F32)
        v = v_ref[rows, :]
        la = log_a[rows, :]
        hi = la.astype(BF16)
        lo = (la - hi.astype(F32)).astype(BF16)
        cum = jnp.dot(tri, hi, preferred_element_type=F32) + jnp.dot(tri, lo, preferred_element_type=F32)
        cum_end = cum[c - 1:c, :] if forward else cum[0:1, :]
        q_dec = (q * jnp.exp2(cum)).astype(BF16)
        k_inv = (k * jnp.exp2(-cum)).astype(BF16)
        k_end = (k * jnp.exp2(cum_end - cum)).astype(BF16)
        att = lax.dot_general(q_dec, k_inv, nt, preferred_element_type=F32)
        att = jnp.where(mask, att, 0.0).astype(BF16)
        outs[ci] = (jnp.dot(att, v, preferred_element_type=F32)
                    + lax.dot_general(q_dec, st.astype(BF16), nt, preferred_element_type=F32))
        kv_t = lax.dot_general(v, k_end, (((0,), (0,)), ((), ())), preferred_element_type=F32)
        st = jnp.exp2(cum_end) * st + kv_t
    return jnp.concatenate(outs, axis=0), st


def _gla_kernel(qf_ref, kf_ref, vf_ref, gf_ref, qb_ref, kb_ref, vb_ref, gb_ref, wg_ref, bg_ref,
                of_ref, ob_ref, stf_ref, stb_ref):
    @pl.when(pl.program_id(2) == 0)
    def _():
        stf_ref[...] = jnp.zeros_like(stf_ref)
        stb_ref[...] = jnp.zeros_like(stb_ref)

    of, stf = _gla_scan_block(qf_ref, kf_ref, vf_ref, gf_ref, wg_ref[0], bg_ref[0], stf_ref[...], True)
    ob, stb = _gla_scan_block(qb_ref, kb_ref, vb_ref, gb_ref, wg_ref[1], bg_ref[1], stb_ref[...], False)
    of_ref[...] = of.astype(of_ref.dtype)
    ob_ref[...] = ob.astype(ob_ref.dtype)
    stf_ref[...] = stf
    stb_ref[...] = stb


def _gla(p, gin, wg_pad, bg, batch, seq):
    n_lat = seq // GLA_BLOCK
    ctx_block0 = batch * n_lat

    def lat_block(b, s, forward):
        j = jnp.maximum(s - 1, 0)
        return b * n_lat + (j if forward else n_lat - 1 - j)

    def in_block(b, s, forward):
        return jnp.where(s == 0, ctx_block0 + b, lat_block(b, s, forward))

    kq = GLA_HEADS
    kv = 2 * GLA_HEADS * GLA_DK // GLA_DV

    def dir_specs(forward):
        return [
            pl.BlockSpec((GLA_BLOCK, GLA_DK), lambda b, h, s: (in_block(b, s, forward), h)),
            pl.BlockSpec((GLA_BLOCK, GLA_DK), lambda b, h, s: (in_block(b, s, forward), kq + h)),
            pl.BlockSpec((GLA_BLOCK, GLA_DV), lambda b, h, s: (in_block(b, s, forward), kv + h)),
            pl.BlockSpec((GLA_BLOCK, LANES), lambda b, h, s: (in_block(b, s, forward), 0)),
        ]

    out_shape = jax.ShapeDtypeStruct((batch * seq, GLA_HEADS * GLA_DV), BF16)
    return pl.pallas_call(
        _gla_kernel,
        grid=(batch, GLA_HEADS, n_lat + 1),
        in_specs=dir_specs(True) + dir_specs(False) + [
            pl.BlockSpec((2, LANES, GLA_DK), lambda b, h, s: (0, 0, h)),
            pl.BlockSpec((2, 1, GLA_DK), lambda b, h, s: (0, 0, h))],
        out_specs=[pl.BlockSpec((GLA_BLOCK, GLA_DV), lambda b, h, s: (lat_block(b, s, True), h)),
                   pl.BlockSpec((GLA_BLOCK, GLA_DV), lambda b, h, s: (lat_block(b, s, False), h))],
        out_shape=[out_shape, out_shape],
        scratch_shapes=[pltpu.VMEM((GLA_DV, GLA_DK), F32), pltpu.VMEM((GLA_DV, GLA_DK), F32)],
        compiler_params=_params(("arbitrary", "arbitrary", "arbitrary")),
        name="gla",
    )(p, p, p, gin, p, p, p, gin, wg_pad, bg)


NA_QTOK = NA_QROWS * GRID_W
NA_WIN = NA_WIN_ROWS * GRID_W
NA_KINDS = ((0, 0), (NA_QROWS, NA_QROWS - NA_KR // 2), (GRID_W - NA_QROWS, GRID_W - NA_WIN_ROWS))


def _na_build_bias(rpb_ref, t_ref, rows):
    n_slots = 2 * NA_KR - 1
    lane = lax.broadcasted_iota(jnp.int32, (GRID_W, LANES), 1)
    q_col = lax.broadcasted_iota(jnp.int32, (GRID_W, LANES), 0)
    k_col = lane & (GRID_W - 1)
    upper = lane >= GRID_W
    c0 = jnp.clip(q_col - NA_KC // 2, 0, GRID_W - NA_KC)
    col_ok = (k_col >= c0) & (k_col < c0 + NA_KC)
    neg = jnp.full((GRID_W, LANES), NEG_INF, F32)

    def toeplitz(hh, slot, lane_off):
        if not 0 <= slot < n_slots:
            return jnp.zeros((GRID_W, LANES), F32)
        base = jnp.broadcast_to(rpb_ref[hh, slot:slot + 1, :], (GRID_W, LANES))
        return pltpu.roll(base, (LANES - (NA_KC - 1) + lane_off) % LANES, 1, stride=1, stride_axis=0)

    for hh in range(2):
        pairs = {}
        for kind, (r0, w0) in enumerate(NA_KINDS):
            for qr in range(NA_QROWS):
                band0 = min(max(r0 + qr - NA_KR // 2, 0), rows - NA_KR) - w0
                slot0 = w0 - (r0 + qr) + NA_KR - 1
                for g in range(NA_WIN_ROWS // 2):
                    slot = slot0 + 2 * g
                    lo_ok = band0 <= 2 * g < band0 + NA_KR
                    hi_ok = band0 <= 2 * g + 1 < band0 + NA_KR
                    if lo_ok or hi_ok:
                        if slot not in pairs:
                            pairs[slot] = jnp.where(upper, toeplitz(hh, slot + 1, GRID_W), toeplitz(hh, slot, 0))
                        ok = col_ok if (lo_ok and hi_ok) else (col_ok & upper if hi_ok else col_ok & ~upper)
                        tile = jnp.where(ok, pairs[slot], neg)
                    else:
                        tile = neg
                    t_ref[hh, kind, qr * GRID_W:(qr + 1) * GRID_W, g * LANES:(g + 1) * LANES] = tile


def _na_kernel(q_ref, k_ref, v_ref, kc_ref, vc_ref, rpb_ref, o_ref, t_ref, *, rows):
    b = pl.program_id(1)
    blk = pl.program_id(2)
    n_blk = rows // NA_QROWS

    pl.when((b == 0) & (blk == 0))(lambda: _na_build_bias(rpb_ref, t_ref, rows))

    kind = jnp.where(blk == 0, 0, jnp.where(blk == n_blk - 1, 2, 1))
    w0 = jnp.clip(blk * NA_QROWS - NA_KR // 2, 0, rows - NA_WIN_ROWS)
    win = pl.ds(pl.multiple_of(w0 * GRID_W, 256), NA_WIN)
    q = q_ref[...]
    kw = k_ref[win, :]
    vw = v_ref[win, :]
    kc = kc_ref[...]
    vc = vc_ref[...]
    lane = lax.broadcasted_iota(jnp.int32, q.shape, 1)
    scale = NA_DH ** -0.5
    nt = (((1,), (1,)), ((), ()))
    outs = []
    for hh in range(2):
        own = (lane < NA_DH) if hh == 0 else (lane >= NA_DH)
        qh = jnp.where(own, q, jnp.zeros_like(q))
        s_lat = lax.dot_general(qh, kw, nt, preferred_element_type=F32) * scale + t_ref[hh, kind]
        s_ctx = lax.dot_general(qh, kc, nt, preferred_element_type=F32) * scale
        m = jnp.maximum(jnp.max(s_lat, axis=-1, keepdims=True), jnp.max(s_ctx, axis=-1, keepdims=True))
        p_lat = jnp.exp(s_lat - m)
        p_ctx = jnp.exp(s_ctx - m)
        denom = jnp.sum(p_lat, axis=-1, keepdims=True) + jnp.sum(p_ctx, axis=-1, keepdims=True)
        o = (jnp.dot(p_lat.astype(BF16), vw, preferred_element_type=F32)
             + jnp.dot(p_ctx.astype(BF16), vc, preferred_element_type=F32))
        outs.append(o / denom)
    o_ref[...] = jnp.where(lane < NA_DH, outs[0], outs[1]).astype(o_ref.dtype)


def _na(p, rpb, batch, seq, ctx_len, col0):
    rows = seq // GRID_W
    n_blk = rows // NA_QROWS
    n_pairs = NA_HEADS // 2
    cq = col0 // LANES
    ck = cq + n_pairs
    cv = ck + n_pairs
    ctx_block0 = batch * seq // ctx_len
    return pl.pallas_call(
        functools.partial(_na_kernel, rows=rows),
        grid=(n_pairs, batch, n_blk),
        in_specs=[
            pl.BlockSpec((NA_QTOK, LANES), lambda hp, b, r: (b * n_blk + r, cq + hp)),
            pl.BlockSpec((seq, LANES), lambda hp, b, r: (b, ck + hp)),
            pl.BlockSpec((seq, LANES), lambda hp, b, r: (b, cv + hp)),
            pl.BlockSpec((ctx_len, LANES), lambda hp, b, r: (ctx_block0 + b, ck + hp)),
            pl.BlockSpec((ctx_len, LANES), lambda hp, b, r: (ctx_block0 + b, cv + hp)),
            pl.BlockSpec((2, 2 * NA_KR - 1, LANES), lambda hp, b, r: (hp, 0, 0)),
        ],
        out_specs=pl.BlockSpec((NA_QTOK, LANES), lambda hp, b, r: (b * n_blk + r, hp)),
        out_shape=jax.ShapeDtypeStruct((batch * seq, NA_HEADS * NA_DH), BF16),
        scratch_shapes=[pltpu.VMEM((2, len(NA_KINDS), NA_QTOK, NA_WIN), F32)],
        compiler_params=_params(("arbitrary", "arbitrary", "arbitrary")),
        name="natten",
    )(p, p, p, p, p, rpb)


def _merge_kernel(of_ref, ob_ref, r_ref, gn_ref, b_ref, wa_ref, wb_ref, m1_ref, m2_ref, o_ref, a_ref):
    @pl.when(pl.program_id(1) == 0)
    def _():
        for h in range(GLA_HEADS):
            cols = slice(h * GLA_DV, (h + 1) * GLA_DV)
            tot = of_ref[:, cols].astype(F32) + ob_ref[:, cols].astype(F32)
            a_ref[:, cols] = (_rms(tot, gn_ref[...]) * _silu(r_ref[:, cols].astype(F32))).astype(BF16)

    a = jnp.dot(a_ref[...], wa_ref[...], preferred_element_type=F32)
    b = jnp.dot(b_ref[...], wb_ref[...], preferred_element_type=F32)
    m = jax.nn.sigmoid(m1_ref[...].astype(F32)) * a + jax.nn.sigmoid(m2_ref[...].astype(F32)) * b
    o_ref[...] = m.astype(o_ref.dtype)


def _merge(o_fwd, o_bwd, gn, o_na, w_gla_o, w_na_o, p, col_r, col_m1):
    n_rows = o_fwd.shape[0]
    v_w = o_fwd.shape[1]
    tn = MERGE_TILE
    c1 = col_m1 // tn
    c2 = c1 + D_MODEL // tn
    return pl.pallas_call(
        _merge_kernel,
        grid=(n_rows // ROW_TILE, D_MODEL // tn),
        in_specs=[pl.BlockSpec((ROW_TILE, v_w), lambda i, n: (i, 0)),
                  pl.BlockSpec((ROW_TILE, v_w), lambda i, n: (i, 0)),
                  pl.BlockSpec((ROW_TILE, v_w), lambda i, n: (i, col_r // v_w)),
                  pl.BlockSpec((1, GLA_DV), lambda i, n: (0, 0)),
                  pl.BlockSpec((ROW_TILE, o_na.shape[1]), lambda i, n: (i, 0)),
                  pl.BlockSpec((w_gla_o.shape[0], tn), lambda i, n: (0, n)),
                  pl.BlockSpec((w_na_o.shape[0], tn), lambda i, n: (0, n)),
                  pl.BlockSpec((ROW_TILE, tn), lambda i, n: (i, c1 + n)),
                  pl.BlockSpec((ROW_TILE, tn), lambda i, n: (i, c2 + n))],
        out_specs=pl.BlockSpec((ROW_TILE, tn), lambda i, n: (i, n)),
        out_shape=jax.ShapeDtypeStruct((n_rows, D_MODEL), BF16),
        scratch_shapes=[pltpu.VMEM((ROW_TILE, v_w), BF16)],
        compiler_params=_params(("arbitrary", "arbitrary")),
        name="merge",
    )(o_fwd, o_bwd, p, gn, o_na, w_gla_o, w_na_o, p, p)


def _outproj_kernel(m_ref, w_ref, x_ref, mod_ref, g_ref, o_ref):
    y = jnp.dot(m_ref[...], w_ref[...], preferred_element_type=F32)
    o_ref[...] = x_ref[...] + mod_ref[5:6, :] * _rms(y, g_ref[3:4, :])


def _outproj(m, w_out, h1, mod, norm_g):
    n_tiles = m.shape[0] // ROW_TILE
    tiles_per_batch = n_tiles // 2
    return pl.pallas_call(
        _outproj_kernel,
        grid=(n_tiles,),
        in_specs=[pl.BlockSpec((ROW_TILE, D_MODEL), lambda i: (i, 0)),
                  pl.BlockSpec((D_MODEL, D_MODEL), lambda i: (0, 0)),
                  pl.BlockSpec((ROW_TILE, D_MODEL), lambda i: (i, 0)),
                  pl.BlockSpec((None, N_MOD, D_MODEL), lambda i: (i // tiles_per_batch, 0, 0)),
                  pl.BlockSpec((6, D_MODEL), lambda i: (0, 0))],
        out_specs=pl.BlockSpec((ROW_TILE, D_MODEL), lambda i: (i, 0)),
        out_shape=jax.ShapeDtypeStruct((m.shape[0], D_MODEL), F32),
        compiler_params=_params(("arbitrary",)),
        name="outproj",
    )(m, w_out, h1, mod, norm_g)


def kernel(x, c, ctx, c_ctx, w_ada, b_ada, norm_g, ffn_wg, ffn_wu, ffn_wd, w_in, gla_wg, gla_bg, gla_norm_g,
           w_gla_o, na_rpb, w_na_o, w_out):
    batch, seq, d = x.shape
    ctx_len = ctx.shape[1]
    depth = w_ada.shape[0]
    assert d == D_MODEL and batch == 2 and batch * ctx_len == ROW_TILE and depth == 1
    assert seq % ROW_TILE == 0 and seq // GRID_W == GRID_W

    qk_w = GLA_HEADS * GLA_DK
    v_w = GLA_HEADS * GLA_DV
    na_w = NA_HEADS * NA_DH
    gate0 = 2 * qk_w + 2 * v_w
    gate1 = gate0 + 2 * GLA_GATE_RANK
    col_nq = gate0
    col_m1 = col_nq + 3 * na_w

    cvec = jnp.zeros((8, d), F32).at[0:batch].set(c).at[batch].set(c_ctx)
    cos, sin = _rope_tables(seq)
    h = x.reshape(batch * seq, d)
    hc = ctx.reshape(batch * ctx_len, d)
    lat_tiles_per_batch = seq // BIG_ROW_TILE
    for l in range(depth):
        mod = _modulation(cvec, w_ada[l], b_ada[l].reshape(1, -1)).reshape(8, N_MOD, d)
        g = norm_g[l]
        h1 = _ffn(h, mod, g, ffn_wg[l], ffn_wu[l], ffn_wd[l], 0, 0, tm=BIG_ROW_TILE, tf=FF_TILE,
                  mod_row=lambda i: i // lat_tiles_per_batch)
        h1c = _ffn(hc, mod, g, ffn_wg[l], ffn_wu[l], ffn_wd[l], 0, 0, tm=ROW_TILE, tf=FF_TILE,
                   mod_row=lambda i: batch)
        p, gin = _inproj(h1, h1c, mod, g, jnp.swapaxes(w_in[l], 0, 1), cos, sin, seq, gate0, gate1)
        wg_pad = jnp.zeros((2, LANES, qk_w), F32)
        wg_pad = wg_pad.at[0, :GLA_GATE_RANK].set(gla_wg[l, 0])
        wg_pad = wg_pad.at[1, GLA_GATE_RANK:2 * GLA_GATE_RANK].set(gla_wg[l, 1])
        bg = gla_bg[l].reshape(2, 1, qk_w)
        gn = gla_norm_g[l].reshape(1, GLA_DV)
        o_fwd, o_bwd = _gla(p, gin, wg_pad, bg, batch, seq)
        rpb = jnp.pad(na_rpb[l], ((0, 0), (0, 0), (0, LANES - (2 * NA_KC - 1))))
        o_na = _na(p, rpb, batch, seq, ctx_len, col_nq)
        m = _merge(o_fwd, o_bwd, gn, o_na, w_gla_o[l].astype(BF16), w_na_o[l].astype(BF16), p,
                   2 * qk_w + v_w, col_m1)
        h2 = _outproj(m, w_out[l].astype(BF16), h1, mod, g)
        h = _ffn(h2, mod, g, ffn_wg[l], ffn_wu[l], ffn_wd[l], 1, 2, tm=BIG_ROW_TILE, tf=FF_TILE,
                 mod_row=lambda i: i // lat_tiles_per_batch)
    return h.reshape(batch, seq, d)
```

```python
import functools

import numpy as np
import jax
import jax.numpy as jnp
from jax import lax
from jax.experimental import pallas as pl
from jax.experimental.pallas import tpu as pltpu

F32 = jnp.float32
BF16 = jnp.bfloat16

D_MODEL = 2048
GRID_W = 64
GLA_HEADS = 4
GLA_DK = 256
GLA_DV = 512
GLA_GATE_RANK = 16
GLA_GATE_TEMP = 16.0
NA_HEADS = 16
NA_DH = 64
NA_KR = 8
NA_KC = 16
ROPE_THETA = 10000.0
EPS = 1e-6
NEG_INF = -1e30
N_MOD = 9
LOG2E = 1.4426950408889634

LANES = 128
ROW_TILE = 512
BIG_ROW_TILE = 1024
FF_TILE = 512
PROJ_TILE = 1024
MERGE_TILE = 1024
GLA_BLOCK = 256
GLA_HEADS_PER_STEP = 4
NA_QROWS = 8
NA_WIN_ROWS = 16
VMEM_LIMIT = 56 * 1024 * 1024


def _params(sem):
    return pltpu.CompilerParams(dimension_semantics=sem, vmem_limit_bytes=VMEM_LIMIT)


def _rms(x, g):
    return x * lax.rsqrt(jnp.mean(x * x, axis=-1, keepdims=True) + EPS) * g


def _silu(x):
    return x * jax.nn.sigmoid(x)


def _mod_kernel(c_ref, w_ref, b_ref, o_ref):
    o_ref[...] = jnp.dot(_silu(c_ref[...]), w_ref[...], preferred_element_type=F32) + b_ref[...]


def _modulation(cvec, w_ada, b_ada):
    n = w_ada.shape[1]
    tn = 1024
    return pl.pallas_call(
        _mod_kernel,
        grid=(n // tn,),
        in_specs=[pl.BlockSpec((8, D_MODEL), lambda j: (0, 0)),
                  pl.BlockSpec((D_MODEL, tn), lambda j: (0, j)),
                  pl.BlockSpec((1, tn), lambda j: (0, j))],
        out_specs=pl.BlockSpec((8, tn), lambda j: (0, j)),
        out_shape=jax.ShapeDtypeStruct((8, n), F32),
        compiler_params=_params(("arbitrary",)),
        name="adaln_mod",
    )(cvec, w_ada, b_ada)


def _mixed_dot(a, w):
    return lax.dot_general(a, w, (((1,), (0,)), ((), ())), preferred_element_type=F32)


def _ffn_kernel(x_ref, mod_ref, g_ref, wg_ref, wu_ref, wd_ref, o_ref, h_ref, *, sub):
    f = pl.program_id(1)

    @pl.when(f == 0)
    def _():
        shift = mod_ref[3 * sub:3 * sub + 1, :]
        scale = mod_ref[3 * sub + 1:3 * sub + 2, :]
        h = _rms(x_ref[...], g_ref[2 * sub:2 * sub + 1, :]) * (1.0 + scale) + shift
        h_ref[...] = h.astype(BF16)
        o_ref[...] = jnp.zeros_like(o_ref)

    for r0 in range(0, h_ref.shape[0], ROW_TILE):
        rows = slice(r0, r0 + ROW_TILE)
        h = h_ref[rows, :]
        g = _mixed_dot(h, wg_ref[...])
        u = _mixed_dot(h, wu_ref[...])
        o_ref[rows, :] += _mixed_dot((_silu(g) * u).astype(BF16), wd_ref[...])

    @pl.when(f == pl.num_programs(1) - 1)
    def _():
        gate = mod_ref[3 * sub + 2:3 * sub + 3, :]
        y = _rms(o_ref[...], g_ref[2 * sub + 1:2 * sub + 2, :])
        o_ref[...] = x_ref[...] + 0.5 * gate * y


def _ffn(x2d, mod, norm_g, wg, wu, wd, which, sub, *, tm, tf, mod_row):
    n_tiles = x2d.shape[0] // tm
    d_ff = wg.shape[2]
    once = dict(pipeline_mode=pl.Buffered(1))
    return pl.pallas_call(
        functools.partial(_ffn_kernel, sub=sub),
        grid=(n_tiles, d_ff // tf),
        in_specs=[
            pl.BlockSpec((tm, D_MODEL), lambda i, f: (i, 0), **once),
            pl.BlockSpec((None, N_MOD, D_MODEL), lambda i, f: (mod_row(i), 0, 0)),
            pl.BlockSpec((6, D_MODEL), lambda i, f: (0, 0)),
            pl.BlockSpec((None, D_MODEL, tf), lambda i, f: (which, 0, f)),
            pl.BlockSpec((None, D_MODEL, tf), lambda i, f: (which, 0, f)),
            pl.BlockSpec((None, tf, D_MODEL), lambda i, f: (which, f, 0)),
        ],
        out_specs=pl.BlockSpec((tm, D_MODEL), lambda i, f: (i, 0), **once),
        out_shape=jax.ShapeDtypeStruct(x2d.shape, F32),
        scratch_shapes=[pltpu.VMEM((tm, D_MODEL), BF16)],
        compiler_params=_params(("arbitrary", "arbitrary")),
        name="ffn%d_%d" % (sub, tm),
    )(x2d, mod, norm_g, wg, wu, wd)


N_ROPE_TILES = 2 * GLA_HEADS * GLA_DK // PROJ_TILE
N_Q_TILES = N_ROPE_TILES // 2


def _rope_tables(seq):
    half = GLA_DK // 4
    freqs = ROPE_THETA ** (-np.arange(half, dtype=np.float64) / half)
    t = np.arange(seq)
    cos_parts, sin_parts = [], []
    for pos in (t // GRID_W, t % GRID_W):
        ang = pos[:, None].astype(np.float64) * freqs
        cos_parts += [np.cos(ang), np.cos(ang)]
        sin_parts += [-np.sin(ang), np.sin(ang)]
    cos = np.concatenate(cos_parts, axis=1)
    sin = np.concatenate(sin_parts, axis=1)
    cos = np.concatenate([cos, np.ones((BIG_ROW_TILE, GLA_DK))], axis=0)
    sin = np.concatenate([sin, np.zeros((BIG_ROW_TILE, GLA_DK))], axis=0)
    return jnp.asarray(cos, F32), jnp.asarray(sin, F32)


def _dot_nt(a, w_t):
    return lax.dot_general(a, w_t, (((1,), (1,)), ((), ())), preferred_element_type=F32)


def _inproj_kernel(x_ref, xc_ref, mod_ref, g_ref, w_ref, wgate_ref, cos_ref, sin_ref, p_ref, gate_ref, h_ref,
                   *, n_lat_tiles):
    i = pl.program_id(0)
    n = pl.program_id(1)

    def project(src_ref):
        r = slice(0, src_ref.shape[0])

        @pl.when(n == 0)
        def _():
            h = _rms(src_ref[...], g_ref[2:3, :]) * (1.0 + mod_ref[4:5, :]) + mod_ref[3:4, :]
            h_ref[r, :] = h.astype(BF16)
            gate = _dot_nt(h_ref[r, :], wgate_ref[...])
            gate_ref[r, :] = jnp.concatenate(
                [gate, jnp.zeros((gate.shape[0], LANES - gate.shape[1]), F32)], axis=1)

        @pl.when(n < N_ROPE_TILES)
        def _():
            y = _dot_nt(h_ref[r, :], w_ref[...])
            cos = cos_ref[r, :]
            sin = sin_ref[r, :]
            qscale = jnp.where(n < N_Q_TILES, GLA_DK ** -0.5, 1.0).astype(F32)
            for j in range(PROJ_TILE // GLA_DK):
                yj = y[:, j * GLA_DK:(j + 1) * GLA_DK]
                swapped = jnp.concatenate(
                    [pltpu.roll(yj[:, g * LANES:(g + 1) * LANES], LANES // 2, 1) for g in range(GLA_DK // LANES)],
                    axis=1)
                p_ref[r, j * GLA_DK:(j + 1) * GLA_DK] = ((yj * cos + swapped * sin) * qscale).astype(BF16)

        @pl.when(n >= N_ROPE_TILES)
        def _():
            p_ref[r, :] = _dot_nt(h_ref[r, :], w_ref[...]).astype(BF16)

    pl.when(i < n_lat_tiles)(lambda: project(x_ref))
    pl.when(i >= n_lat_tiles)(lambda: project(xc_ref))


def _inproj(h1, h1c, mod, norm_g, w_in_t, cos, sin, seq, gate0, gate1):
    tm = BIG_ROW_TILE
    n_lat_tiles = h1.shape[0] // tm
    n_rows = h1.shape[0] + h1c.shape[0]
    tiles_per_batch = n_lat_tiles // 2
    n_a_tiles = gate0 // PROJ_TILE
    n_b_tiles = (w_in_t.shape[0] - gate1) // PROJ_TILE
    n_gate = gate1 - gate0
    tab_spec = pl.BlockSpec(
        (tm, GLA_DK), lambda i, n: (jnp.where(i < n_lat_tiles, i % tiles_per_batch, seq // tm), 0))
    return pl.pallas_call(
        functools.partial(_inproj_kernel, n_lat_tiles=n_lat_tiles),
        grid=(n_lat_tiles + 1, n_a_tiles + n_b_tiles),
        in_specs=[pl.BlockSpec((tm, D_MODEL), lambda i, n: (jnp.minimum(i, n_lat_tiles - 1), 0)),
                  pl.BlockSpec(h1c.shape, lambda i, n: (0, 0)),
                  pl.BlockSpec((None, N_MOD, D_MODEL), lambda i, n: (jnp.minimum(i // tiles_per_batch, 2), 0, 0)),
                  pl.BlockSpec((6, D_MODEL), lambda i, n: (0, 0)),
                  pl.BlockSpec((pl.Element(PROJ_TILE), pl.Element(D_MODEL)),
                               lambda i, n: (pl.multiple_of(
                                   n * PROJ_TILE + jnp.where(n < n_a_tiles, 0, n_gate), n_gate), 0)),
                  pl.BlockSpec((n_gate, D_MODEL), lambda i, n: (gate0 // n_gate, 0)),
                  tab_spec, tab_spec],
        out_specs=[pl.BlockSpec((tm, PROJ_TILE), lambda i, n: (i, n)),
                   pl.BlockSpec((tm, LANES), lambda i, n: (i, 0))],
        out_shape=[jax.ShapeDtypeStruct((n_rows, (n_a_tiles + n_b_tiles) * PROJ_TILE), BF16),
                   jax.ShapeDtypeStruct((n_rows, LANES), F32)],
        scratch_shapes=[pltpu.VMEM((tm, D_MODEL), BF16)],
        compiler_params=_params(("arbitrary", "arbitrary")),
        name="inproj",
    )(h1, h1c, mod, norm_g, w_in_t, w_in_t, cos, sin)


def _log2_sigmoid(z):
    t = z * LOG2E
    return jnp.minimum(t, 0.0) - jnp.log2(1.0 + jnp.exp2(-jnp.abs(t)))


def _gla_kernel(qf_ref, kf_ref, vf_ref, gf_ref, qb_ref, kb_ref, vb_ref, gb_ref, wg_ref, bg_ref,
                of_ref, ob_ref, stf_ref, stb_ref):
    @pl.when(pl.program_id(2) == 0)
    def _():
        stf_ref[...] = jnp.zeros_like(stf_ref)
        stb_ref[...] = jnp.zeros_like(stb_ref)

    c = GLA_BLOCK
    nt = (((1,), (1,)), ((), ()))
    row = lax.broadcasted_iota(jnp.int32, (c, c), 0)
    col = lax.broadcasted_iota(jnp.int32, (c, c), 1)
    mask = {True: col <= row, False: col >= row}
    tri = {fwd: jnp.where(m, 1.0, 0.0).astype(BF16) for fwd, m in mask.items()}
    refs = {True: (qf_ref, kf_ref, vf_ref, gf_ref, stf_ref, of_ref),
            False: (qb_ref, kb_ref, vb_ref, gb_ref, stb_ref, ob_ref)}
    qcols = lambda head: slice(head * GLA_DK, (head + 1) * GLA_DK)
    vcols = lambda head: slice(head * GLA_DV, (head + 1) * GLA_DV)

    state, log_a, cum, factors, att = {}, {}, {}, {}, {}

    def gate_stage(head, fwd):
        d = 0 if fwd else 1
        state[head, fwd] = refs[fwd][4][head]
        z = jnp.dot(refs[fwd][3][...], wg_ref[d, :, qcols(head)], preferred_element_type=F32)
        log_a[head, fwd] = _log2_sigmoid(z + bg_ref[d, :, qcols(head)]) * (1.0 / GLA_GATE_TEMP)

    def cumsum_stage(head, fwd):
        la = log_a[head, fwd]
        hi = la.astype(BF16)
        lo = (la - hi.astype(F32)).astype(BF16)
        cum[head, fwd] = (jnp.dot(tri[fwd], hi, preferred_element_type=F32)
                          + jnp.dot(tri[fwd], lo, preferred_element_type=F32))

    def factor_stage(head, fwd):
        cm = cum[head, fwd]
        cum_end = cm[c - 1:c, :] if fwd else cm[0:1, :]
        cum_mid = cm[c // 2:c // 2 + 1, :]
        q = refs[fwd][0][:, qcols(head)]
        k = refs[fwd][1][:, qcols(head)]
        factors[head, fwd] = (q * jnp.exp2(cm - cum_mid).astype(BF16),
                              k * jnp.exp2(cum_mid - cm).astype(BF16),
                              q * jnp.exp2(cm).astype(BF16),
                              k * jnp.exp2(cum_end - cm).astype(BF16),
                              jnp.exp2(cum_end))

    def att_stage(head, fwd):
        q_att, k_att = factors[head, fwd][:2]
        a = lax.dot_general(q_att, k_att, nt, preferred_element_type=F32)
        att[head, fwd] = jnp.where(mask[fwd], a, 0.0).astype(BF16)

    def out_stage(head, fwd):
        _, _, q_dec, k_end, decay = factors[head, fwd]
        v = refs[fwd][2][:, vcols(head)]
        st = state[head, fwd]
        o = (jnp.dot(att[head, fwd], v, preferred_element_type=F32)
             + lax.dot_general(q_dec, st.astype(BF16), nt, preferred_element_type=F32))
        kv_t = lax.dot_general(v, k_end, (((0,), (0,)), ((), ())), preferred_element_type=F32)
        refs[fwd][5][:, vcols(head)] = o.astype(BF16)
        refs[fwd][4][head] = decay * st + kv_t

    for stage in (gate_stage, cumsum_stage, factor_stage, att_stage, out_stage):
        for head in range(GLA_HEADS_PER_STEP):
            for fwd in (True, False):
                stage(head, fwd)


def _gla(p, gin, wg_pad, bg, batch, seq):
    n_lat = seq // GLA_BLOCK
    ctx_block0 = batch * n_lat
    hps = GLA_HEADS_PER_STEP
    dk, dv = hps * GLA_DK, hps * GLA_DV

    def lat_block(b, s, forward):
        j = jnp.maximum(s - 1, 0)
        return b * n_lat + (j if forward else n_lat - 1 - j)

    def in_block(b, s, forward):
        return jnp.where(s == 0, ctx_block0 + b, lat_block(b, s, forward))

    kq = GLA_HEADS // hps
    kv = 2 * GLA_HEADS * GLA_DK // dv

    def dir_specs(forward):
        return [
            pl.BlockSpec((GLA_BLOCK, dk), lambda b, h, s: (in_block(b, s, forward), h)),
            pl.BlockSpec((GLA_BLOCK, dk), lambda b, h, s: (in_block(b, s, forward), kq + h)),
            pl.BlockSpec((GLA_BLOCK, dv), lambda b, h, s: (in_block(b, s, forward), kv + h)),
            pl.BlockSpec((GLA_BLOCK, LANES), lambda b, h, s: (in_block(b, s, forward), 0)),
        ]

    out_shape = jax.ShapeDtypeStruct((batch * seq, GLA_HEADS * GLA_DV), BF16)
    state = pltpu.VMEM((hps, GLA_DV, GLA_DK), F32)
    return pl.pallas_call(
        _gla_kernel,
        grid=(batch, GLA_HEADS // hps, n_lat + 1),
        in_specs=dir_specs(True) + dir_specs(False) + [
            pl.BlockSpec((2, LANES, dk), lambda b, h, s: (0, 0, h)),
            pl.BlockSpec((2, 1, dk), lambda b, h, s: (0, 0, h))],
        out_specs=[pl.BlockSpec((GLA_BLOCK, dv), lambda b, h, s: (lat_block(b, s, True), h)),
                   pl.BlockSpec((GLA_BLOCK, dv), lambda b, h, s: (lat_block(b, s, False), h))],
        out_shape=[out_shape, out_shape],
        scratch_shapes=[state, state],
        compiler_params=_params(("arbitrary", "arbitrary", "arbitrary")),
        name="gla",
    )(p, p, p, gin, p, p, p, gin, wg_pad, bg)


NA_QTOK = NA_QROWS * GRID_W
NA_WIN = NA_WIN_ROWS * GRID_W
NA_KINDS = ((0, 0), (NA_QROWS, NA_QROWS - NA_KR // 2), (GRID_W - NA_QROWS, GRID_W - NA_WIN_ROWS))


def _na_build_bias(rpb_ref, t_ref, rows):
    n_slots = 2 * NA_KR - 1
    lane = lax.broadcasted_iota(jnp.int32, (GRID_W, LANES), 1)
    q_col = lax.broadcasted_iota(jnp.int32, (GRID_W, LANES), 0)
    k_col = lane & (GRID_W - 1)
    upper = lane >= GRID_W
    c0 = jnp.clip(q_col - NA_KC // 2, 0, GRID_W - NA_KC)
    col_ok = (k_col >= c0) & (k_col < c0 + NA_KC)
    neg = jnp.full((GRID_W, LANES), NEG_INF, F32)

    def toeplitz(hh, slot, lane_off):
        if not 0 <= slot < n_slots:
            return jnp.zeros((GRID_W, LANES), F32)
        base = jnp.broadcast_to(rpb_ref[hh, slot:slot + 1, :], (GRID_W, LANES))
        return pltpu.roll(base, (LANES - (NA_KC - 1) + lane_off) % LANES, 1, stride=1, stride_axis=0)

    for hh in range(2):
        pairs = {}
        for kind, (r0, w0) in enumerate(NA_KINDS):
            for qr in range(NA_QROWS):
                band0 = min(max(r0 + qr - NA_KR // 2, 0), rows - NA_KR) - w0
                slot0 = w0 - (r0 + qr) + NA_KR - 1
                for g in range(NA_WIN_ROWS // 2):
                    slot = slot0 + 2 * g
                    lo_ok = band0 <= 2 * g < band0 + NA_KR
                    hi_ok = band0 <= 2 * g + 1 < band0 + NA_KR
                    if lo_ok or hi_ok:
                        if slot not in pairs:
                            pairs[slot] = jnp.where(upper, toeplitz(hh, slot + 1, GRID_W), toeplitz(hh, slot, 0))
                        ok = col_ok if (lo_ok and hi_ok) else (col_ok & upper if hi_ok else col_ok & ~upper)
                        tile = jnp.where(ok, pairs[slot], neg)
                    else:
                        tile = neg
                    t_ref[hh, kind, qr * GRID_W:(qr + 1) * GRID_W, g * LANES:(g + 1) * LANES] = tile


def _na_kernel(q_ref, k_ref, v_ref, kc_ref, vc_ref, rpb_ref, o_ref, t_ref, *, rows):
    b = pl.program_id(1)
    blk = pl.program_id(2)
    n_blk = rows // NA_QROWS

    pl.when((b == 0) & (blk == 0))(lambda: _na_build_bias(rpb_ref, t_ref, rows))

    kind = jnp.where(blk == 0, 0, jnp.where(blk == n_blk - 1, 2, 1))
    w0 = jnp.clip(blk * NA_QROWS - NA_KR // 2, 0, rows - NA_WIN_ROWS)
    win = pl.ds(pl.multiple_of(w0 * GRID_W, 256), NA_WIN)
    q = q_ref[...]
    kw = k_ref[win, :]
    vw = v_ref[win, :]
    kc = kc_ref[...]
    vc = vc_ref[...]
    lane = lax.broadcasted_iota(jnp.int32, q.shape, 1)
    scale = NA_DH ** -0.5
    nt = (((1,), (1,)), ((), ()))
    outs = []
    for hh in range(2):
        own = (lane < NA_DH) if hh == 0 else (lane >= NA_DH)
        qh = jnp.where(own, q, jnp.zeros_like(q))
        s_lat = lax.dot_general(qh, kw, nt, preferred_element_type=F32) * scale + t_ref[hh, kind]
        s_ctx = lax.dot_general(qh, kc, nt, preferred_element_type=F32) * scale
        m = jnp.maximum(jnp.max(s_lat, axis=-1, keepdims=True), jnp.max(s_ctx, axis=-1, keepdims=True))
        p_lat = jnp.exp(s_lat - m)
        p_ctx = jnp.exp(s_ctx - m)
        denom = jnp.sum(p_lat, axis=-1, keepdims=True) + jnp.sum(p_ctx, axis=-1, keepdims=True)
        o = (jnp.dot(p_lat.astype(BF16), vw, preferred_element_type=F32)
             + jnp.dot(p_ctx.astype(BF16), vc, preferred_element_type=F32))
        outs.append(o / denom)
    o_ref[...] = jnp.where(lane < NA_DH, outs[0], outs[1]).astype(o_ref.dtype)


def _na(p, rpb, batch, seq, ctx_len, col0):
    rows = seq // GRID_W
    n_blk = rows // NA_QROWS
    n_pairs = NA_HEADS // 2
    cq = col0 // LANES
    ck = cq + n_pairs
    cv = ck + n_pairs
    ctx_block0 = batch * seq // ctx_len
    return pl.pallas_call(
        functools.partial(_na_kernel, rows=rows),
        grid=(n_pairs, batch, n_blk),
        in_specs=[
            pl.BlockSpec((NA_QTOK, LANES), lambda hp, b, r: (b * n_blk + r, cq + hp)),
            pl.BlockSpec((seq, LANES), lambda hp, b, r: (b, ck + hp)),
            pl.BlockSpec((seq, LANES), lambda hp, b, r: (b, cv + hp)),
            pl.BlockSpec((ctx_len, LANES), lambda hp, b, r: (ctx_block0 + b, ck + hp)),
            pl.BlockSpec((ctx_len, LANES), lambda hp, b, r: (ctx_block0 + b, cv + hp)),
            pl.BlockSpec((2, 2 * NA_KR - 1, LANES), lambda hp, b, r: (hp, 0, 0)),
        ],
        out_specs=pl.BlockSpec((NA_QTOK, LANES), lambda hp, b, r: (b * n_blk + r, hp)),
        out_shape=jax.ShapeDtypeStruct((batch * seq, NA_HEADS * NA_DH), BF16),
        scratch_shapes=[pltpu.VMEM((2, len(NA_KINDS), NA_QTOK, NA_WIN), F32)],
        compiler_params=_params(("arbitrary", "arbitrary", "arbitrary")),
        name="natten",
    )(p, p, p, p, p, rpb)


def _merge_kernel(of_ref, ob_ref, r_ref, gn_ref, b_ref, wa_ref, wb_ref, m1_ref, m2_ref, o_ref, a_ref):
    @pl.when(pl.program_id(1) == 0)
    def _():
        for h in range(GLA_HEADS):
            cols = slice(h * GLA_DV, (h + 1) * GLA_DV)
            tot = of_ref[:, cols].astype(F32) + ob_ref[:, cols].astype(F32)
            a_ref[:, cols] = (_rms(tot, gn_ref[...]) * _silu(r_ref[:, cols].astype(F32))).astype(BF16)

    a = jnp.dot(a_ref[...], wa_ref[...], preferred_element_type=F32)
    b = jnp.dot(b_ref[...], wb_ref[...], preferred_element_type=F32)
    m = jax.nn.sigmoid(m1_ref[...].astype(F32)) * a + jax.nn.sigmoid(m2_ref[...].astype(F32)) * b
    o_ref[...] = m.astype(o_ref.dtype)


def _merge(o_fwd, o_bwd, gn, o_na, w_gla_o, w_na_o, p, col_r, col_m1):
    n_rows = o_fwd.shape[0]
    v_w = o_fwd.shape[1]
    tn = MERGE_TILE
    c1 = col_m1 // tn
    c2 = c1 + D_MODEL // tn
    return pl.pallas_call(
        _merge_kernel,
        grid=(n_rows // ROW_TILE, D_MODEL // tn),
        in_specs=[pl.BlockSpec((ROW_TILE, v_w), lambda i, n: (i, 0)),
                  pl.BlockSpec((ROW_TILE, v_w), lambda i, n: (i, 0)),
                  pl.BlockSpec((ROW_TILE, v_w), lambda i, n: (i, col_r // v_w)),
                  pl.BlockSpec((1, GLA_DV), lambda i, n: (0, 0)),
                  pl.BlockSpec((ROW_TILE, o_na.shape[1]), lambda i, n: (i, 0)),
                  pl.BlockSpec((w_gla_o.shape[0], tn), lambda i, n: (0, n)),
                  pl.BlockSpec((w_na_o.shape[0], tn), lambda i, n: (0, n)),
                  pl.BlockSpec((ROW_TILE, tn), lambda i, n: (i, c1 + n)),
                  pl.BlockSpec((ROW_TILE, tn), lambda i, n: (i, c2 + n))],
        out_specs=pl.BlockSpec((ROW_TILE, tn), lambda i, n: (i, n)),
        out_shape=jax.ShapeDtypeStruct((n_rows, D_MODEL), BF16),
        scratch_shapes=[pltpu.VMEM((ROW_TILE, v_w), BF16)],
        compiler_params=_params(("arbitrary", "arbitrary")),
        name="merge",
    )(o_fwd, o_bwd, p, gn, o_na, w_gla_o, w_na_o, p, p)


def _outproj_kernel(m_ref, w_ref, x_ref, mod_ref, g_ref, o_ref):
    y = jnp.dot(m_ref[...], w_ref[...], preferred_element_type=F32)
    o_ref[...] = x_ref[...] + mod_ref[5:6, :] * _rms(y, g_ref[3:4, :])


def _outproj(m, w_out, h1, mod, norm_g):
    n_tiles = m.shape[0] // ROW_TILE
    tiles_per_batch = n_tiles // 2
    return pl.pallas_call(
        _outproj_kernel,
        grid=(n_tiles,),
        in_specs=[pl.BlockSpec((ROW_TILE, D_MODEL), lambda i: (i, 0)),
                  pl.BlockSpec((D_MODEL, D_MODEL), lambda i: (0, 0)),
                  pl.BlockSpec((ROW_TILE, D_MODEL), lambda i: (i, 0)),
                  pl.BlockSpec((None, N_MOD, D_MODEL), lambda i: (i // tiles_per_batch, 0, 0)),
                  pl.BlockSpec((6, D_MODEL), lambda i: (0, 0))],
        out_specs=pl.BlockSpec((ROW_TILE, D_MODEL), lambda i: (i, 0)),
        out_shape=jax.ShapeDtypeStruct((m.shape[0], D_MODEL), F32),
        compiler_params=_params(("arbitrary",)),
        name="outproj",
    )(m, w_out, h1, mod, norm_g)


def kernel(x, c, ctx, c_ctx, w_ada, b_ada, norm_g, ffn_wg, ffn_wu, ffn_wd, w_in, gla_wg, gla_bg, gla_norm_g,
           w_gla_o, na_rpb, w_na_o, w_out):
    batch, seq, d = x.shape
    ctx_len = ctx.shape[1]
    depth = w_ada.shape[0]
    assert d == D_MODEL and batch == 2 and batch * ctx_len == ROW_TILE and depth == 1
    assert seq % ROW_TILE == 0 and seq // GRID_W == GRID_W

    qk_w = GLA_HEADS * GLA_DK
    v_w = GLA_HEADS * GLA_DV
    na_w = NA_HEADS * NA_DH
    gate0 = 2 * qk_w + 2 * v_w
    gate1 = gate0 + 2 * GLA_GATE_RANK
    col_nq = gate0
    col_m1 = col_nq + 3 * na_w

    cvec = jnp.zeros((8, d), F32).at[0:batch].set(c).at[batch].set(c_ctx)
    cos, sin = _rope_tables(seq)
    h = x.reshape(batch * seq, d)
    hc = ctx.reshape(batch * ctx_len, d)
    lat_tiles_per_batch = seq // BIG_ROW_TILE
    for l in range(depth):
        mod = _modulation(cvec, w_ada[l], b_ada[l].reshape(1, -1)).reshape(8, N_MOD, d)
        g = norm_g[l]
        h1 = _ffn(h, mod, g, ffn_wg[l], ffn_wu[l], ffn_wd[l], 0, 0, tm=BIG_ROW_TILE, tf=FF_TILE,
                  mod_row=lambda i: i // lat_tiles_per_batch)
        h1c = _ffn(hc, mod, g, ffn_wg[l], ffn_wu[l], ffn_wd[l], 0, 0, tm=ROW_TILE, tf=FF_TILE,
                   mod_row=lambda i: batch)
        p, gin = _inproj(h1, h1c, mod, g, jnp.swapaxes(w_in[l], 0, 1), cos, sin, seq, gate0, gate1)
        wg_pad = jnp.zeros((2, LANES, qk_w), F32)
        wg_pad = wg_pad.at[0, :GLA_GATE_RANK].set(gla_wg[l, 0])
        wg_pad = wg_pad.at[1, GLA_GATE_RANK:2 * GLA_GATE_RANK].set(gla_wg[l, 1])
        bg = gla_bg[l].reshape(2, 1, qk_w)
        gn = gla_norm_g[l].reshape(1, GLA_DV)
        o_fwd, o_bwd = _gla(p, gin, wg_pad, bg, batch, seq)
        rpb = jnp.pad(na_rpb[l], ((0, 0), (0, 0), (0, LANES - (2 * NA_KC - 1))))
        o_na = _na(p, rpb, batch, seq, ctx_len, col_nq)
        m = _merge(o_fwd, o_bwd, gn, o_na, w_gla_o[l].astype(BF16), w_na_o[l].astype(BF16), p,
                   2 * qk_w + v_w, col_m1)
        h2 = _outproj(m, w_out[l].astype(BF16), h1, mod, g)
        h = _ffn(h2, mod, g, ffn_wg[l], ffn_wu[l], ffn_wd[l], 1, 2, tm=BIG_ROW_TILE, tf=FF_TILE,
                 mod_row=lambda i: i // lat_tiles_per_batch)
    return h.reshape(batch, seq, d)
```

```python
import functools

import numpy as np
import jax
import jax.numpy as jnp
from jax import lax
from jax.experimental import pallas as pl
from jax.experimental.pallas import tpu as pltpu

F32 = jnp.float32
BF16 = jnp.bfloat16

D_MODEL = 2048
GRID_W = 64
GLA_HEADS = 4
GLA_DK = 256
GLA_DV = 512
GLA_GATE_RANK = 16
GLA_GATE_TEMP = 16.0
NA_HEADS = 16
NA_DH = 64
NA_KR = 8
NA_KC = 16
ROPE_THETA = 10000.0
EPS = 1e-6
NEG_INF = -1e30
N_MOD = 9
LOG2E = 1.4426950408889634

LANES = 128
ROW_TILE = 512
BIG_ROW_TILE = 1024
FF_TILE = 512
PROJ_TILE = 1024
MERGE_TILE = 1024
GLA_BLOCK = 256
GLA_HEADS_PER_STEP = 4
NA_QROWS = 8
NA_SUB_ROWS = 4
NA_WIN_ROWS = 12
VMEM_LIMIT = 56 * 1024 * 1024


def _params(sem):
    return pltpu.CompilerParams(dimension_semantics=sem, vmem_limit_bytes=VMEM_LIMIT)


def _rms(x, g):
    return x * lax.rsqrt(jnp.mean(x * x, axis=-1, keepdims=True) + EPS) * g


def _silu(x):
    return x * jax.nn.sigmoid(x)


def _mod_kernel(c_ref, w_ref, b_ref, o_ref):
    o_ref[...] = jnp.dot(_silu(c_ref[...]), w_ref[...], preferred_element_type=F32) + b_ref[...]


def _modulation(cvec, w_ada, b_ada):
    n = w_ada.shape[1]
    tn = 1024
    return pl.pallas_call(
        _mod_kernel,
        grid=(n // tn,),
        in_specs=[pl.BlockSpec((8, D_MODEL), lambda j: (0, 0)),
                  pl.BlockSpec((D_MODEL, tn), lambda j: (0, j)),
                  pl.BlockSpec((1, tn), lambda j: (0, j))],
        out_specs=pl.BlockSpec((8, tn), lambda j: (0, j)),
        out_shape=jax.ShapeDtypeStruct((8, n), F32),
        compiler_params=_params(("arbitrary",)),
        name="adaln_mod",
    )(cvec, w_ada, b_ada)


def _mixed_dot(a, w):
    return lax.dot_general(a, w, (((1,), (0,)), ((), ())), preferred_element_type=F32)


def _ffn_kernel(x_ref, mod_ref, g_ref, wg_ref, wu_ref, wd_ref, o_ref, h_ref, *, sub):
    f = pl.program_id(1)

    @pl.when(f == 0)
    def _():
        shift = mod_ref[3 * sub:3 * sub + 1, :]
        scale = mod_ref[3 * sub + 1:3 * sub + 2, :]
        h = _rms(x_ref[...], g_ref[2 * sub:2 * sub + 1, :]) * (1.0 + scale) + shift
        h_ref[...] = h.astype(BF16)
        o_ref[...] = jnp.zeros_like(o_ref)

    for r0 in range(0, h_ref.shape[0], ROW_TILE):
        rows = slice(r0, r0 + ROW_TILE)
        h = h_ref[rows, :]
        g = _mixed_dot(h, wg_ref[...])
        u = _mixed_dot(h, wu_ref[...])
        o_ref[rows, :] += _mixed_dot((_silu(g) * u).astype(BF16), wd_ref[...])

    @pl.when(f == pl.num_programs(1) - 1)
    def _():
        gate = mod_ref[3 * sub + 2:3 * sub + 3, :]
        y = _rms(o_ref[...], g_ref[2 * sub + 1:2 * sub + 2, :])
        o_ref[...] = x_ref[...] + 0.5 * gate * y


def _ffn(x2d, mod, norm_g, wg, wu, wd, which, sub, *, tm, tf, mod_row):
    n_tiles = x2d.shape[0] // tm
    d_ff = wg.shape[2]
    once = dict(pipeline_mode=pl.Buffered(1))
    return pl.pallas_call(
        functools.partial(_ffn_kernel, sub=sub),
        grid=(n_tiles, d_ff // tf),
        in_specs=[
            pl.BlockSpec((tm, D_MODEL), lambda i, f: (i, 0), **once),
            pl.BlockSpec((None, N_MOD, D_MODEL), lambda i, f: (mod_row(i), 0, 0)),
            pl.BlockSpec((6, D_MODEL), lambda i, f: (0, 0)),
            pl.BlockSpec((None, D_MODEL, tf), lambda i, f: (which, 0, f)),
            pl.BlockSpec((None, D_MODEL, tf), lambda i, f: (which, 0, f)),
            pl.BlockSpec((None, tf, D_MODEL), lambda i, f: (which, f, 0)),
        ],
        out_specs=pl.BlockSpec((tm, D_MODEL), lambda i, f: (i, 0), **once),
        out_shape=jax.ShapeDtypeStruct(x2d.shape, F32),
        scratch_shapes=[pltpu.VMEM((tm, D_MODEL), BF16)],
        compiler_params=_params(("arbitrary", "arbitrary")),
        name="ffn%d_%d" % (sub, tm),
    )(x2d, mod, norm_g, wg, wu, wd)


N_ROPE_TILES = 2 * GLA_HEADS * GLA_DK // PROJ_TILE
N_Q_TILES = N_ROPE_TILES // 2


def _rope_tables(seq):
    half = GLA_DK // 4
    freqs = ROPE_THETA ** (-np.arange(half, dtype=np.float64) / half)
    t = np.arange(seq)
    cos_parts, sin_parts = [], []
    for pos in (t // GRID_W, t % GRID_W):
        ang = pos[:, None].astype(np.float64) * freqs
        cos_parts += [np.cos(ang), np.cos(ang)]
        sin_parts += [-np.sin(ang), np.sin(ang)]
    cos = np.concatenate(cos_parts, axis=1)
    sin = np.concatenate(sin_parts, axis=1)
    cos = np.concatenate([cos, np.ones((BIG_ROW_TILE, GLA_DK))], axis=0)
    sin = np.concatenate([sin, np.zeros((BIG_ROW_TILE, GLA_DK))], axis=0)
    return jnp.asarray(cos, F32), jnp.asarray(sin, F32)


def _dot_nt(a, w_t):
    return lax.dot_general(a, w_t, (((1,), (1,)), ((), ())), preferred_element_type=F32)


def _inproj_kernel(x_ref, xc_ref, mod_ref, g_ref, w_ref, wgate_ref, cos_ref, sin_ref, p_ref, gate_ref, h_ref,
                   *, n_lat_tiles):
    i = pl.program_id(0)
    n = pl.program_id(1)

    def project(src_ref):
        r = slice(0, src_ref.shape[0])

        @pl.when(n == 0)
        def _():
            h = _rms(src_ref[...], g_ref[2:3, :]) * (1.0 + mod_ref[4:5, :]) + mod_ref[3:4, :]
            h_ref[r, :] = h.astype(BF16)
            gate = _dot_nt(h_ref[r, :], wgate_ref[...])
            gate_ref[r, :] = jnp.concatenate(
                [gate, jnp.zeros((gate.shape[0], LANES - gate.shape[1]), F32)], axis=1)

        @pl.when(n < N_ROPE_TILES)
        def _():
            y = _dot_nt(h_ref[r, :], w_ref[...])
            cos = cos_ref[r, :]
            sin = sin_ref[r, :]
            qscale = jnp.where(n < N_Q_TILES, GLA_DK ** -0.5, 1.0).astype(F32)
            for j in range(PROJ_TILE // GLA_DK):
                yj = y[:, j * GLA_DK:(j + 1) * GLA_DK]
                swapped = jnp.concatenate(
                    [pltpu.roll(yj[:, g * LANES:(g + 1) * LANES], LANES // 2, 1) for g in range(GLA_DK // LANES)],
                    axis=1)
                p_ref[r, j * GLA_DK:(j + 1) * GLA_DK] = ((yj * cos + swapped * sin) * qscale).astype(BF16)

        @pl.when(n >= N_ROPE_TILES)
        def _():
            p_ref[r, :] = _dot_nt(h_ref[r, :], w_ref[...]).astype(BF16)

    pl.when(i < n_lat_tiles)(lambda: project(x_ref))
    pl.when(i >= n_lat_tiles)(lambda: project(xc_ref))


def _inproj(h1, h1c, mod, norm_g, w_in_t, cos, sin, seq, gate0, gate1):
    tm = BIG_ROW_TILE
    n_lat_tiles = h1.shape[0] // tm
    n_rows = h1.shape[0] + h1c.shape[0]
    tiles_per_batch = n_lat_tiles // 2
    n_a_tiles = gate0 // PROJ_TILE
    n_b_tiles = (w_in_t.shape[0] - gate1) // PROJ_TILE
    n_gate = gate1 - gate0
    tab_spec = pl.BlockSpec(
        (tm, GLA_DK), lambda i, n: (jnp.where(i < n_lat_tiles, i % tiles_per_batch, seq // tm), 0))
    return pl.pallas_call(
        functools.partial(_inproj_kernel, n_lat_tiles=n_lat_tiles),
        grid=(n_lat_tiles + 1, n_a_tiles + n_b_tiles),
        in_specs=[pl.BlockSpec((tm, D_MODEL), lambda i, n: (jnp.minimum(i, n_lat_tiles - 1), 0)),
                  pl.BlockSpec(h1c.shape, lambda i, n: (0, 0)),
                  pl.BlockSpec((None, N_MOD, D_MODEL), lambda i, n: (jnp.minimum(i // tiles_per_batch, 2), 0, 0)),
                  pl.BlockSpec((6, D_MODEL), lambda i, n: (0, 0)),
                  pl.BlockSpec((pl.Element(PROJ_TILE), pl.Element(D_MODEL)),
                               lambda i, n: (pl.multiple_of(
                                   n * PROJ_TILE + jnp.where(n < n_a_tiles, 0, n_gate), n_gate), 0)),
                  pl.BlockSpec((n_gate, D_MODEL), lambda i, n: (gate0 // n_gate, 0)),
                  tab_spec, tab_spec],
        out_specs=[pl.BlockSpec((tm, PROJ_TILE), lambda i, n: (i, n)),
                   pl.BlockSpec((tm, LANES), lambda i, n: (i, 0))],
        out_shape=[jax.ShapeDtypeStruct((n_rows, (n_a_tiles + n_b_tiles) * PROJ_TILE), BF16),
                   jax.ShapeDtypeStruct((n_rows, LANES), F32)],
        scratch_shapes=[pltpu.VMEM((tm, D_MODEL), BF16)],
        compiler_params=_params(("arbitrary", "arbitrary")),
        name="inproj",
    )(h1, h1c, mod, norm_g, w_in_t, w_in_t, cos, sin)


def _log2_sigmoid(z):
    t = z * LOG2E
    return jnp.minimum(t, 0.0) - jnp.log2(1.0 + jnp.exp2(-jnp.abs(t)))


def _gla_kernel(qf_ref, kf_ref, vf_ref, gf_ref, qb_ref, kb_ref, vb_ref, gb_ref, wg_ref, bg_ref,
                of_ref, ob_ref, stf_ref, stb_ref):
    @pl.when(pl.program_id(2) == 0)
    def _():
        stf_ref[...] = jnp.zeros_like(stf_ref)
        stb_ref[...] = jnp.zeros_like(stb_ref)

    c = GLA_BLOCK
    nt = (((1,), (1,)), ((), ()))
    row = lax.broadcasted_iota(jnp.int32, (c, c), 0)
    col = lax.broadcasted_iota(jnp.int32, (c, c), 1)
    mask = {True: col <= row, False: col >= row}
    tri = {fwd: jnp.where(m, 1.0, 0.0).astype(BF16) for fwd, m in mask.items()}
    refs = {True: (qf_ref, kf_ref, vf_ref, gf_ref, stf_ref, of_ref),
            False: (qb_ref, kb_ref, vb_ref, gb_ref, stb_ref, ob_ref)}
    qcols = lambda head: slice(head * GLA_DK, (head + 1) * GLA_DK)
    vcols = lambda head: slice(head * GLA_DV, (head + 1) * GLA_DV)

    state, log_a, cum, factors, att = {}, {}, {}, {}, {}

    def gate_stage(head, fwd):
        d = 0 if fwd else 1
        state[head, fwd] = refs[fwd][4][head]
        z = jnp.dot(refs[fwd][3][...], wg_ref[d, :, qcols(head)], preferred_element_type=F32)
        log_a[head, fwd] = _log2_sigmoid(z + bg_ref[d, :, qcols(head)]) * (1.0 / GLA_GATE_TEMP)

    def cumsum_stage(head, fwd):
        la = log_a[head, fwd]
        hi = la.astype(BF16)
        lo = (la - hi.astype(F32)).astype(BF16)
        cum[head, fwd] = (jnp.dot(tri[fwd], hi, preferred_element_type=F32)
                          + jnp.dot(tri[fwd], lo, preferred_element_type=F32))

    def factor_stage(head, fwd):
        cm = cum[head, fwd]
        cum_end = cm[c - 1:c, :] if fwd else cm[0:1, :]
        cum_mid = cm[c // 2:c // 2 + 1, :]
        q = refs[fwd][0][:, qcols(head)]
        k = refs[fwd][1][:, qcols(head)]
        factors[head, fwd] = (q * jnp.exp2(cm - cum_mid).astype(BF16),
                              k * jnp.exp2(cum_mid - cm).astype(BF16),
                              q * jnp.exp2(cm).astype(BF16),
                              k * jnp.exp2(cum_end - cm).astype(BF16),
                              jnp.exp2(cum_end))

    def att_stage(head, fwd):
        q_att, k_att = factors[head, fwd][:2]
        a = lax.dot_general(q_att, k_att, nt, preferred_element_type=F32)
        att[head, fwd] = jnp.where(mask[fwd], a, 0.0).astype(BF16)

    def out_stage(head, fwd):
        _, _, q_dec, k_end, decay = factors[head, fwd]
        v = refs[fwd][2][:, vcols(head)]
        st = state[head, fwd]
        o = (jnp.dot(att[head, fwd], v, preferred_element_type=F32)
             + lax.dot_general(q_dec, st.astype(BF16), nt, preferred_element_type=F32))
        kv_t = lax.dot_general(v, k_end, (((0,), (0,)), ((), ())), preferred_element_type=F32)
        refs[fwd][5][:, vcols(head)] = o.astype(BF16)
        refs[fwd][4][head] = decay * st + kv_t

    for stage in (gate_stage, cumsum_stage, factor_stage, att_stage, out_stage):
        for head in range(GLA_HEADS_PER_STEP):
            for fwd in (True, False):
                stage(head, fwd)


def _gla(p, gin, wg_pad, bg, batch, seq):
    n_lat = seq // GLA_BLOCK
    ctx_block0 = batch * n_lat
    hps = GLA_HEADS_PER_STEP
    dk, dv = hps * GLA_DK, hps * GLA_DV

    def lat_block(b, s, forward):
        j = jnp.maximum(s - 1, 0)
        return b * n_lat + (j if forward else n_lat - 1 - j)

    def in_block(b, s, forward):
        return jnp.where(s == 0, ctx_block0 + b, lat_block(b, s, forward))

    kq = GLA_HEADS // hps
    kv = 2 * GLA_HEADS * GLA_DK // dv

    def dir_specs(forward):
        return [
            pl.BlockSpec((GLA_BLOCK, dk), lambda b, h, s: (in_block(b, s, forward), h)),
            pl.BlockSpec((GLA_BLOCK, dk), lambda b, h, s: (in_block(b, s, forward), kq + h)),
            pl.BlockSpec((GLA_BLOCK, dv), lambda b, h, s: (in_block(b, s, forward), kv + h)),
            pl.BlockSpec((GLA_BLOCK, LANES), lambda b, h, s: (in_block(b, s, forward), 0)),
        ]

    out_shape = jax.ShapeDtypeStruct((batch * seq, GLA_HEADS * GLA_DV), BF16)
    state = pltpu.VMEM((hps, GLA_DV, GLA_DK), F32)
    return pl.pallas_call(
        _gla_kernel,
        grid=(batch, GLA_HEADS // hps, n_lat + 1),
        in_specs=dir_specs(True) + dir_specs(False) + [
            pl.BlockSpec((2, LANES, dk), lambda b, h, s: (0, 0, h)),
            pl.BlockSpec((2, 1, dk), lambda b, h, s: (0, 0, h))],
        out_specs=[pl.BlockSpec((GLA_BLOCK, dv), lambda b, h, s: (lat_block(b, s, True), h)),
                   pl.BlockSpec((GLA_BLOCK, dv), lambda b, h, s: (lat_block(b, s, False), h))],
        out_shape=[out_shape, out_shape],
        scratch_shapes=[state, state],
        compiler_params=_params(("arbitrary", "arbitrary", "arbitrary")),
        name="gla",
    )(p, p, p, gin, p, p, p, gin, wg_pad, bg)


NA_QTOK = NA_QROWS * GRID_W
NA_SUB_TOK = NA_SUB_ROWS * GRID_W
NA_WIN = NA_WIN_ROWS * GRID_W
NA_SUBS = NA_QROWS // NA_SUB_ROWS
NA_KINDS = ((0, 0), (NA_SUB_ROWS, NA_SUB_ROWS - NA_KR // 2), (GRID_W - NA_SUB_ROWS, GRID_W - NA_WIN_ROWS))


def _na_build_bias(rpb_ref, t_ref, rows):
    n_slots = 2 * NA_KR - 1
    lane = lax.broadcasted_iota(jnp.int32, (GRID_W, LANES), 1)
    q_col = lax.broadcasted_iota(jnp.int32, (GRID_W, LANES), 0)
    k_col = lane & (GRID_W - 1)
    upper = lane >= GRID_W
    c0 = jnp.clip(q_col - NA_KC // 2, 0, GRID_W - NA_KC)
    col_ok = (k_col >= c0) & (k_col < c0 + NA_KC)
    neg = jnp.full((GRID_W, LANES), NEG_INF, F32)

    def toeplitz(hh, slot, lane_off):
        if not 0 <= slot < n_slots:
            return jnp.zeros((GRID_W, LANES), F32)
        base = jnp.broadcast_to(rpb_ref[hh, slot:slot + 1, :], (GRID_W, LANES)) * LOG2E
        return pltpu.roll(base, (LANES - (NA_KC - 1) + lane_off) % LANES, 1, stride=1, stride_axis=0)

    for hh in range(2):
        pairs = {}
        for kind, (r0, w0) in enumerate(NA_KINDS):
            for qr in range(NA_SUB_ROWS):
                band0 = min(max(r0 + qr - NA_KR // 2, 0), rows - NA_KR) - w0
                slot0 = w0 - (r0 + qr) + NA_KR - 1
                for g in range(NA_WIN_ROWS // 2):
                    slot = slot0 + 2 * g
                    lo_ok = band0 <= 2 * g < band0 + NA_KR
                    hi_ok = band0 <= 2 * g + 1 < band0 + NA_KR
                    if lo_ok or hi_ok:
                        if slot not in pairs:
                            pairs[slot] = jnp.where(upper, toeplitz(hh, slot + 1, GRID_W), toeplitz(hh, slot, 0))
                        ok = col_ok if (lo_ok and hi_ok) else (col_ok & upper if hi_ok else col_ok & ~upper)
                        tile = jnp.where(ok, pairs[slot], neg)
                    else:
                        tile = neg
                    t_ref[hh, kind, qr * GRID_W:(qr + 1) * GRID_W, g * LANES:(g + 1) * LANES] = tile


def _na_kernel(q_ref, k_ref, v_ref, kc_ref, vc_ref, rpb_ref, o_ref, t_ref, *, rows):
    b = pl.program_id(1)
    blk = pl.program_id(2)
    n_blk = rows // NA_QROWS

    pl.when((b == 0) & (blk == 0))(lambda: _na_build_bias(rpb_ref, t_ref, rows))

    nt = (((1,), (1,)), ((), ()))
    kc = kc_ref[...]
    vc = vc_ref[...]
    lane = lax.broadcasted_iota(jnp.int32, (NA_SUB_TOK, LANES), 1)
    chains = [(hh, sub) for sub in range(NA_SUBS) for hh in range(2)]
    q_head, kw, vw, kind = {}, [], [], []
    for sub in range(NA_SUBS):
        q = q_ref[sub * NA_SUB_TOK:(sub + 1) * NA_SUB_TOK, :] * (NA_DH ** -0.5 * LOG2E)
        q_head[0, sub] = jnp.where(lane < NA_DH, q, jnp.zeros_like(q))
        q_head[1, sub] = jnp.where(lane >= NA_DH, q, jnp.zeros_like(q))
        r0 = blk * NA_QROWS + sub * NA_SUB_ROWS
        w0 = jnp.clip(r0 - NA_KR // 2, 0, rows - NA_WIN_ROWS)
        win = pl.ds(pl.multiple_of(w0 * GRID_W, 256), NA_WIN)
        kw.append(k_ref[win, :])
        vw.append(v_ref[win, :])
        kind.append(jnp.where(r0 == 0, 0, jnp.where(r0 == rows - NA_SUB_ROWS, 2, 1)))

    s_lat, s_ctx, p_lat, p_ctx, denom, out = {}, {}, {}, {}, {}, {}
    for hh, sub in chains:
        qs = q_head[hh, sub]
        s_lat[hh, sub] = lax.dot_general(qs, kw[sub], nt, preferred_element_type=F32) + t_ref[hh, kind[sub]]
        s_ctx[hh, sub] = lax.dot_general(qs, kc, nt, preferred_element_type=F32)
    for ch in chains:
        m = jnp.maximum(jnp.max(s_lat[ch], axis=-1, keepdims=True), jnp.max(s_ctx[ch], axis=-1, keepdims=True))
        pl_ = jnp.exp2(s_lat[ch] - m)
        pc_ = jnp.exp2(s_ctx[ch] - m)
        denom[ch] = jnp.sum(pl_, axis=-1, keepdims=True) + jnp.sum(pc_, axis=-1, keepdims=True)
        p_lat[ch] = pl_.astype(BF16)
        p_ctx[ch] = pc_.astype(BF16)
    for hh, sub in chains:
        o = (jnp.dot(p_lat[hh, sub], vw[sub], preferred_element_type=F32)
             + jnp.dot(p_ctx[hh, sub], vc, preferred_element_type=F32))
        out[hh, sub] = o / denom[hh, sub]
    for sub in range(NA_SUBS):
        rows_ = slice(sub * NA_SUB_TOK, (sub + 1) * NA_SUB_TOK)
        o_ref[rows_, :] = jnp.where(lane < NA_DH, out[0, sub], out[1, sub]).astype(o_ref.dtype)


def _na(p, rpb, batch, seq, ctx_len, col0):
    rows = seq // GRID_W
    n_blk = rows // NA_QROWS
    n_pairs = NA_HEADS // 2
    cq = col0 // LANES
    ck = cq + n_pairs
    cv = ck + n_pairs
    ctx_block0 = batch * seq // ctx_len
    return pl.pallas_call(
        functools.partial(_na_kernel, rows=rows),
        grid=(n_pairs, batch, n_blk),
        in_specs=[
            pl.BlockSpec((NA_QTOK, LANES), lambda hp, b, r: (b * n_blk + r, cq + hp)),
            pl.BlockSpec((seq, LANES), lambda hp, b, r: (b, ck + hp)),
            pl.BlockSpec((seq, LANES), lambda hp, b, r: (b, cv + hp)),
            pl.BlockSpec((ctx_len, LANES), lambda hp, b, r: (ctx_block0 + b, ck + hp)),
            pl.BlockSpec((ctx_len, LANES), lambda hp, b, r: (ctx_block0 + b, cv + hp)),
            pl.BlockSpec((2, 2 * NA_KR - 1, LANES), lambda hp, b, r: (hp, 0, 0)),
        ],
        out_specs=pl.BlockSpec((NA_QTOK, LANES), lambda hp, b, r: (b * n_blk + r, hp)),
        out_shape=jax.ShapeDtypeStruct((batch * seq, NA_HEADS * NA_DH), BF16),
        scratch_shapes=[pltpu.VMEM((2, len(NA_KINDS), NA_SUB_TOK, NA_WIN), F32)],
        compiler_params=_params(("arbitrary", "arbitrary", "arbitrary")),
        name="natten",
    )(p, p, p, p, p, rpb)


def _merge_kernel(of_ref, ob_ref, r_ref, gn_ref, b_ref, wa_ref, wb_ref, m1_ref, m2_ref, o_ref, a_ref):
    @pl.when(pl.program_id(1) == 0)
    def _():
        for h in range(GLA_HEADS):
            cols = slice(h * GLA_DV, (h + 1) * GLA_DV)
            tot = of_ref[:, cols].astype(F32) + ob_ref[:, cols].astype(F32)
            a_ref[:, cols] = (_rms(tot, gn_ref[...]) * _silu(r_ref[:, cols].astype(F32))).astype(BF16)

    a = jnp.dot(a_ref[...], wa_ref[...], preferred_element_type=F32)
    b = jnp.dot(b_ref[...], wb_ref[...], preferred_element_type=F32)
    m = jax.nn.sigmoid(m1_ref[...].astype(F32)) * a + jax.nn.sigmoid(m2_ref[...].astype(F32)) * b
    o_ref[...] = m.astype(o_ref.dtype)


def _merge(o_fwd, o_bwd, gn, o_na, w_gla_o, w_na_o, p, col_r, col_m1):
    n_rows = o_fwd.shape[0]
    v_w = o_fwd.shape[1]
    tn = MERGE_TILE
    c1 = col_m1 // tn
    c2 = c1 + D_MODEL // tn
    return pl.pallas_call(
        _merge_kernel,
        grid=(n_rows // ROW_TILE, D_MODEL // tn),
        in_specs=[pl.BlockSpec((ROW_TILE, v_w), lambda i, n: (i, 0)),
                  pl.BlockSpec((ROW_TILE, v_w), lambda i, n: (i, 0)),
                  pl.BlockSpec((ROW_TILE, v_w), lambda i, n: (i, col_r // v_w)),
                  pl.BlockSpec((1, GLA_DV), lambda i, n: (0, 0)),
                  pl.BlockSpec((ROW_TILE, o_na.shape[1]), lambda i, n: (i, 0)),
                  pl.BlockSpec((w_gla_o.shape[0], tn), lambda i, n: (0, n)),
                  pl.BlockSpec((w_na_o.shape[0], tn), lambda i, n: (0, n)),
                  pl.BlockSpec((ROW_TILE, tn), lambda i, n: (i, c1 + n)),
                  pl.BlockSpec((ROW_TILE, tn), lambda i, n: (i, c2 + n))],
        out_specs=pl.BlockSpec((ROW_TILE, tn), lambda i, n: (i, n)),
        out_shape=jax.ShapeDtypeStruct((n_rows, D_MODEL), BF16),
        scratch_shapes=[pltpu.VMEM((ROW_TILE, v_w), BF16)],
        compiler_params=_params(("arbitrary", "arbitrary")),
        name="merge",
    )(o_fwd, o_bwd, p, gn, o_na, w_gla_o, w_na_o, p, p)


def _outproj_kernel(m_ref, w_ref, x_ref, mod_ref, g_ref, o_ref):
    y = jnp.dot(m_ref[...], w_ref[...], preferred_element_type=F32)
    o_ref[...] = x_ref[...] + mod_ref[5:6, :] * _rms(y, g_ref[3:4, :])


def _outproj(m, w_out, h1, mod, norm_g):
    n_tiles = m.shape[0] // ROW_TILE
    tiles_per_batch = n_tiles // 2
    return pl.pallas_call(
        _outproj_kernel,
        grid=(n_tiles,),
        in_specs=[pl.BlockSpec((ROW_TILE, D_MODEL), lambda i: (i, 0)),
                  pl.BlockSpec((D_MODEL, D_MODEL), lambda i: (0, 0)),
                  pl.BlockSpec((ROW_TILE, D_MODEL), lambda i: (i, 0)),
                  pl.BlockSpec((None, N_MOD, D_MODEL), lambda i: (i // tiles_per_batch, 0, 0)),
                  pl.BlockSpec((6, D_MODEL), lambda i: (0, 0))],
        out_specs=pl.BlockSpec((ROW_TILE, D_MODEL), lambda i: (i, 0)),
        out_shape=jax.ShapeDtypeStruct((m.shape[0], D_MODEL), F32),
        compiler_params=_params(("arbitrary",)),
        name="outproj",
    )(m, w_out, h1, mod, norm_g)


def kernel(x, c, ctx, c_ctx, w_ada, b_ada, norm_g, ffn_wg, ffn_wu, ffn_wd, w_in, gla_wg, gla_bg, gla_norm_g,
           w_gla_o, na_rpb, w_na_o, w_out):
    batch, seq, d = x.shape
    ctx_len = ctx.shape[1]
    depth = w_ada.shape[0]
    assert d == D_MODEL and batch == 2 and batch * ctx_len == ROW_TILE and depth == 1
    assert seq % ROW_TILE == 0 and seq // GRID_W == GRID_W

    qk_w = GLA_HEADS * GLA_DK
    v_w = GLA_HEADS * GLA_DV
    na_w = NA_HEADS * NA_DH
    gate0 = 2 * qk_w + 2 * v_w
    gate1 = gate0 + 2 * GLA_GATE_RANK
    col_nq = gate0
    col_m1 = col_nq + 3 * na_w

    cvec = jnp.zeros((8, d), F32).at[0:batch].set(c).at[batch].set(c_ctx)
    cos, sin = _rope_tables(seq)
    h = x.reshape(batch * seq, d)
    hc = ctx.reshape(batch * ctx_len, d)
    lat_tiles_per_batch = seq // BIG_ROW_TILE
    for l in range(depth):
        mod = _modulation(cvec, w_ada[l], b_ada[l].reshape(1, -1)).reshape(8, N_MOD, d)
        g = norm_g[l]
        h1 = _ffn(h, mod, g, ffn_wg[l], ffn_wu[l], ffn_wd[l], 0, 0, tm=BIG_ROW_TILE, tf=FF_TILE,
                  mod_row=lambda i: i // lat_tiles_per_batch)
        h1c = _ffn(hc, mod, g, ffn_wg[l], ffn_wu[l], ffn_wd[l], 0, 0, tm=ROW_TILE, tf=FF_TILE,
                   mod_row=lambda i: batch)
        p, gin = _inproj(h1, h1c, mod, g, jnp.swapaxes(w_in[l], 0, 1), cos, sin, seq, gate0, gate1)
        wg_pad = jnp.zeros((2, LANES, qk_w), F32)
        wg_pad = wg_pad.at[0, :GLA_GATE_RANK].set(gla_wg[l, 0])
        wg_pad = wg_pad.at[1, GLA_GATE_RANK:2 * GLA_GATE_RANK].set(gla_wg[l, 1])
        bg = gla_bg[l].reshape(2, 1, qk_w)
        gn = gla_norm_g[l].reshape(1, GLA_DV)
        o_fwd, o_bwd = _gla(p, gin, wg_pad, bg, batch, seq)
        rpb = jnp.pad(na_rpb[l], ((0, 0), (0, 0), (0, LANES - (2 * NA_KC - 1))))
        o_na = _na(p, rpb, batch, seq, ctx_len, col_nq)
        m = _merge(o_fwd, o_bwd, gn, o_na, w_gla_o[l].astype(BF16), w_na_o[l].astype(BF16), p,
                   2 * qk_w + v_w, col_m1)
        h2 = _outproj(m, w_out[l].astype(BF16), h1, mod, g)
        h = _ffn(h2, mod, g, ffn_wg[l], ffn_wu[l], ffn_wd[l], 1, 2, tm=BIG_ROW_TILE, tf=FF_TILE,
                 mod_row=lambda i: i // lat_tiles_per_batch)
    return h.reshape(batch, seq, d)
```

```python
import functools

import numpy as np
import jax
import jax.numpy as jnp
from jax import lax
from jax.experimental import pallas as pl
from jax.experimental.pallas import tpu as pltpu

F32 = jnp.float32
BF16 = jnp.bfloat16

D_MODEL = 2048
GRID_W = 64
GLA_HEADS = 4
GLA_DK = 256
GLA_DV = 512
GLA_GATE_RANK = 16
GLA_GATE_TEMP = 16.0
NA_HEADS = 16
NA_DH = 64
NA_KR = 8
NA_KC = 16
ROPE_THETA = 10000.0
EPS = 1e-6
NEG_INF = -1e30
N_MOD = 9
LOG2E = 1.4426950408889634

LANES = 128
ROW_TILE = 512
BIG_ROW_TILE = 1024
FF_TILE = 512
PROJ_TILE = 1024
MERGE_TILE = 1024
GLA_BLOCK = 256
GLA_HEADS_PER_STEP = 4
NA_QROWS = 8
NA_SUB_ROWS = 4
NA_WIN_ROWS = 12
RMS_CHUNK = 16
VMEM_LIMIT = 56 * 1024 * 1024


def _params(sem):
    return pltpu.CompilerParams(dimension_semantics=sem, vmem_limit_bytes=VMEM_LIMIT)


def _rms(x, g):
    return x * lax.rsqrt(jnp.mean(x * x, axis=-1, keepdims=True) + EPS) * g


def _silu(x):
    return x * jax.nn.sigmoid(x)


def _for_row_chunks(n_rows, body):
    def step(j, carry):
        body(pl.ds(pl.multiple_of(j * RMS_CHUNK, RMS_CHUNK), RMS_CHUNK))
        return carry
    lax.fori_loop(0, n_rows // RMS_CHUNK, step, 0, unroll=2)


def _row_inv_rms(load_rows, n_rows, d, part_ref):
    def sum_squares(rows):
        x2 = jnp.square(load_rows(rows))
        parts = [x2[:, g * LANES:(g + 1) * LANES] for g in range(d // LANES)]
        while len(parts) > 1:
            parts = [a + b for a, b in zip(parts[::2], parts[1::2])]
        part_ref[rows, :] = parts[0]

    _for_row_chunks(n_rows, sum_squares)
    r = slice(0, n_rows)
    inv = lax.rsqrt(jnp.sum(part_ref[r, :], axis=-1, keepdims=True) * (1.0 / d) + EPS)
    part_ref[r, :] = jnp.broadcast_to(inv, (n_rows, LANES))


def _spread(v, d):
    return jnp.concatenate([v] * (d // LANES), axis=1)


def _set_row(coef_ref, i, row):
    coef_ref[i] = jnp.broadcast_to(row, coef_ref.shape[1:])


def _mod_kernel(c_ref, w_ref, b_ref, o_ref):
    o_ref[...] = jnp.dot(_silu(c_ref[...]), w_ref[...], preferred_element_type=F32) + b_ref[...]


def _modulation(cvec, w_ada, b_ada):
    n = w_ada.shape[1]
    tn = 1024
    return pl.pallas_call(
        _mod_kernel,
        grid=(n // tn,),
        in_specs=[pl.BlockSpec((8, D_MODEL), lambda j: (0, 0)),
                  pl.BlockSpec((D_MODEL, tn), lambda j: (0, j)),
                  pl.BlockSpec((1, tn), lambda j: (0, j))],
        out_specs=pl.BlockSpec((8, tn), lambda j: (0, j)),
        out_shape=jax.ShapeDtypeStruct((8, n), F32),
        compiler_params=_params(("arbitrary",)),
        name="adaln_mod",
    )(cvec, w_ada, b_ada)


def _mixed_dot(a, w):
    return lax.dot_general(a, w, (((1,), (0,)), ((), ())), preferred_element_type=F32)


def _ffn_kernel(x_ref, mod_ref, g_ref, wg_ref, wu_ref, wd_ref, o_ref, h_ref, part_ref, coef_ref, *, sub):
    f = pl.program_id(1)
    tm, d = x_ref.shape

    @pl.when(f == 0)
    def _():
        _set_row(coef_ref, 0, g_ref[2 * sub:2 * sub + 1, :] * (1.0 + mod_ref[3 * sub + 1:3 * sub + 2, :]))
        _set_row(coef_ref, 1, mod_ref[3 * sub:3 * sub + 1, :])
        _row_inv_rms(lambda rows: x_ref[rows, :], tm, d, part_ref)

        def emit(rows):
            h = x_ref[rows, :] * _spread(part_ref[rows, :], d) * coef_ref[0] + coef_ref[1]
            h_ref[rows, :] = h.astype(BF16)
            o_ref[rows, :] = jnp.zeros((RMS_CHUNK, d), F32)

        _for_row_chunks(tm, emit)

    for r0 in range(0, tm, ROW_TILE):
        rows = slice(r0, r0 + ROW_TILE)
        h = h_ref[rows, :]
        g = _mixed_dot(h, wg_ref[...])
        u = _mixed_dot(h, wu_ref[...])
        o_ref[rows, :] += _mixed_dot((_silu(g) * u).astype(BF16), wd_ref[...])

    @pl.when(f == pl.num_programs(1) - 1)
    def _():
        _set_row(coef_ref, 0, 0.5 * mod_ref[3 * sub + 2:3 * sub + 3, :] * g_ref[2 * sub + 1:2 * sub + 2, :])
        _row_inv_rms(lambda rows: o_ref[rows, :], tm, d, part_ref)

        def emit(rows):
            y = o_ref[rows, :] * _spread(part_ref[rows, :], d) * coef_ref[0]
            o_ref[rows, :] = x_ref[rows, :] + y

        _for_row_chunks(tm, emit)


def _ffn(x2d, mod, norm_g, wg, wu, wd, which, sub, *, tm, tf, mod_row):
    n_tiles = x2d.shape[0] // tm
    d_ff = wg.shape[2]
    once = dict(pipeline_mode=pl.Buffered(1))
    return pl.pallas_call(
        functools.partial(_ffn_kernel, sub=sub),
        grid=(n_tiles, d_ff // tf),
        in_specs=[
            pl.BlockSpec((tm, D_MODEL), lambda i, f: (i, 0), **once),
            pl.BlockSpec((None, N_MOD, D_MODEL), lambda i, f: (mod_row(i), 0, 0)),
            pl.BlockSpec((6, D_MODEL), lambda i, f: (0, 0)),
            pl.BlockSpec((None, D_MODEL, tf), lambda i, f: (which, 0, f)),
            pl.BlockSpec((None, D_MODEL, tf), lambda i, f: (which, 0, f)),
            pl.BlockSpec((None, tf, D_MODEL), lambda i, f: (which, f, 0)),
        ],
        out_specs=pl.BlockSpec((tm, D_MODEL), lambda i, f: (i, 0), **once),
        out_shape=jax.ShapeDtypeStruct(x2d.shape, F32),
        scratch_shapes=[pltpu.VMEM((tm, D_MODEL), BF16), pltpu.VMEM((tm, LANES), F32),
                        pltpu.VMEM((2, RMS_CHUNK, D_MODEL), F32)],
        compiler_params=_params(("arbitrary", "arbitrary")),
        name="ffn%d_%d" % (sub, tm),
    )(x2d, mod, norm_g, wg, wu, wd)


N_ROPE_TILES = 2 * GLA_HEADS * GLA_DK // PROJ_TILE
N_Q_TILES = N_ROPE_TILES // 2


def _rope_tables(seq):
    half = GLA_DK // 4
    freqs = ROPE_THETA ** (-np.arange(half, dtype=np.float64) / half)
    t = np.arange(seq)
    cos_parts, sin_parts = [], []
    for pos in (t // GRID_W, t % GRID_W):
        ang = pos[:, None].astype(np.float64) * freqs
        cos_parts += [np.cos(ang), np.cos(ang)]
        sin_parts += [-np.sin(ang), np.sin(ang)]
    cos = np.concatenate(cos_parts, axis=1)
    sin = np.concatenate(sin_parts, axis=1)
    cos = np.concatenate([cos, np.ones((BIG_ROW_TILE, GLA_DK))], axis=0)
    sin = np.concatenate([sin, np.zeros((BIG_ROW_TILE, GLA_DK))], axis=0)
    return jnp.asarray(cos, F32), jnp.asarray(sin, F32)


def _dot_nt(a, w_t):
    return lax.dot_general(a, w_t, (((1,), (1,)), ((), ())), preferred_element_type=F32)


def _inproj_kernel(x_ref, xc_ref, mod_ref, g_ref, w_ref, wgate_ref, cos_ref, sin_ref, p_ref, gate_ref, h_ref,
                   part_ref, coef_ref, *, n_lat_tiles):
    i = pl.program_id(0)
    n = pl.program_id(1)

    def project(src_ref):
        n_rows, d = src_ref.shape
        r = slice(0, n_rows)

        @pl.when(n == 0)
        def _():
            _set_row(coef_ref, 0, g_ref[2:3, :] * (1.0 + mod_ref[4:5, :]))
            _set_row(coef_ref, 1, mod_ref[3:4, :])
            _row_inv_rms(lambda rows: src_ref[rows, :], n_rows, d, part_ref)

            def emit(rows):
                h = src_ref[rows, :] * _spread(part_ref[rows, :], d) * coef_ref[0] + coef_ref[1]
                h_ref[rows, :] = h.astype(BF16)

            _for_row_chunks(n_rows, emit)
            gate = _dot_nt(h_ref[r, :], wgate_ref[...])
            gate_ref[r, :] = jnp.concatenate(
                [gate, jnp.zeros((gate.shape[0], LANES - gate.shape[1]), F32)], axis=1)

        @pl.when(n < N_ROPE_TILES)
        def _():
            y = _dot_nt(h_ref[r, :], w_ref[...])
            cos = cos_ref[r, :]
            sin = sin_ref[r, :]
            qscale = jnp.where(n < N_Q_TILES, GLA_DK ** -0.5, 1.0).astype(F32)
            for j in range(PROJ_TILE // GLA_DK):
                yj = y[:, j * GLA_DK:(j + 1) * GLA_DK]
                swapped = jnp.concatenate(
                    [pltpu.roll(yj[:, g * LANES:(g + 1) * LANES], LANES // 2, 1) for g in range(GLA_DK // LANES)],
                    axis=1)
                p_ref[r, j * GLA_DK:(j + 1) * GLA_DK] = ((yj * cos + swapped * sin) * qscale).astype(BF16)

        @pl.when(n >= N_ROPE_TILES)
        def _():
            p_ref[r, :] = _dot_nt(h_ref[r, :], w_ref[...]).astype(BF16)

    pl.when(i < n_lat_tiles)(lambda: project(x_ref))
    pl.when(i >= n_lat_tiles)(lambda: project(xc_ref))


def _inproj(h1, h1c, mod, norm_g, w_in_t, cos, sin, seq, gate0, gate1):
    tm = BIG_ROW_TILE
    n_lat_tiles = h1.shape[0] // tm
    n_rows = h1.shape[0] + h1c.shape[0]
    tiles_per_batch = n_lat_tiles // 2
    n_a_tiles = gate0 // PROJ_TILE
    n_b_tiles = (w_in_t.shape[0] - gate1) // PROJ_TILE
    n_gate = gate1 - gate0
    tab_spec = pl.BlockSpec(
        (tm, GLA_DK), lambda i, n: (jnp.where(i < n_lat_tiles, i % tiles_per_batch, seq // tm), 0))
    return pl.pallas_call(
        functools.partial(_inproj_kernel, n_lat_tiles=n_lat_tiles),
        grid=(n_lat_tiles + 1, n_a_tiles + n_b_tiles),
        in_specs=[pl.BlockSpec((tm, D_MODEL), lambda i, n: (jnp.minimum(i, n_lat_tiles - 1), 0)),
                  pl.BlockSpec(h1c.shape, lambda i, n: (0, 0)),
                  pl.BlockSpec((None, N_MOD, D_MODEL), lambda i, n: (jnp.minimum(i // tiles_per_batch, 2), 0, 0)),
                  pl.BlockSpec((6, D_MODEL), lambda i, n: (0, 0)),
                  pl.BlockSpec((pl.Element(PROJ_TILE), pl.Element(D_MODEL)),
                               lambda i, n: (pl.multiple_of(
                                   n * PROJ_TILE + jnp.where(n < n_a_tiles, 0, n_gate), n_gate), 0)),
                  pl.BlockSpec((n_gate, D_MODEL), lambda i, n: (gate0 // n_gate, 0)),
                  tab_spec, tab_spec],
        out_specs=[pl.BlockSpec((tm, PROJ_TILE), lambda i, n: (i, n)),
                   pl.BlockSpec((tm, LANES), lambda i, n: (i, 0))],
        out_shape=[jax.ShapeDtypeStruct((n_rows, (n_a_tiles + n_b_tiles) * PROJ_TILE), BF16),
                   jax.ShapeDtypeStruct((n_rows, LANES), F32)],
        scratch_shapes=[pltpu.VMEM((tm, D_MODEL), BF16), pltpu.VMEM((tm, LANES), F32),
                        pltpu.VMEM((2, RMS_CHUNK, D_MODEL), F32)],
        compiler_params=_params(("arbitrary", "arbitrary")),
        name="inproj",
    )(h1, h1c, mod, norm_g, w_in_t, w_in_t, cos, sin)


def _log2_sigmoid(z):
    t = z * LOG2E
    return jnp.minimum(t, 0.0) - jnp.log2(1.0 + jnp.exp2(-jnp.abs(t)))


def _gla_kernel(qf_ref, kf_ref, vf_ref, gf_ref, qb_ref, kb_ref, vb_ref, gb_ref, wg_ref, bg_ref,
                of_ref, ob_ref, stf_ref, stb_ref):
    @pl.when(pl.program_id(2) == 0)
    def _():
        stf_ref[...] = jnp.zeros_like(stf_ref)
        stb_ref[...] = jnp.zeros_like(stb_ref)

    c = GLA_BLOCK
    nt = (((1,), (1,)), ((), ()))
    row = lax.broadcasted_iota(jnp.int32, (c, c), 0)
    col = lax.broadcasted_iota(jnp.int32, (c, c), 1)
    mask = {True: col <= row, False: col >= row}
    tri = {fwd: jnp.where(m, 1.0, 0.0).astype(BF16) for fwd, m in mask.items()}
    refs = {True: (qf_ref, kf_ref, vf_ref, gf_ref, stf_ref, of_ref),
            False: (qb_ref, kb_ref, vb_ref, gb_ref, stb_ref, ob_ref)}
    qcols = lambda head: slice(head * GLA_DK, (head + 1) * GLA_DK)
    vcols = lambda head: slice(head * GLA_DV, (head + 1) * GLA_DV)

    state, log_a, cum, factors, att = {}, {}, {}, {}, {}

    def gate_stage(head, fwd):
        d = 0 if fwd else 1
        state[head, fwd] = refs[fwd][4][head]
        z = jnp.dot(refs[fwd][3][...], wg_ref[d, :, qcols(head)], preferred_element_type=F32)
        log_a[head, fwd] = _log2_sigmoid(z + bg_ref[d, :, qcols(head)]) * (1.0 / GLA_GATE_TEMP)

    def cumsum_stage(head, fwd):
        la = log_a[head, fwd]
        hi = la.astype(BF16)
        lo = (la - hi.astype(F32)).astype(BF16)
        cum[head, fwd] = (jnp.dot(tri[fwd], hi, preferred_element_type=F32)
                          + jnp.dot(tri[fwd], lo, preferred_element_type=F32))

    def factor_stage(head, fwd):
        cm = cum[head, fwd]
        cum_end = cm[c - 1:c, :] if fwd else cm[0:1, :]
        cum_mid = cm[c // 2:c // 2 + 1, :]
        q = refs[fwd][0][:, qcols(head)]
        k = refs[fwd][1][:, qcols(head)]
        factors[head, fwd] = (q * jnp.exp2(cm - cum_mid).astype(BF16),
                              k * jnp.exp2(cum_mid - cm).astype(BF16),
                              q * jnp.exp2(cm).astype(BF16),
                              k * jnp.exp2(cum_end - cm).astype(BF16),
                              jnp.exp2(cum_end))

    def att_stage(head, fwd):
        q_att, k_att = factors[head, fwd][:2]
        a = lax.dot_general(q_att, k_att, nt, preferred_element_type=F32)
        att[head, fwd] = jnp.where(mask[fwd], a, 0.0).astype(BF16)

    def out_stage(head, fwd):
        _, _, q_dec, k_end, decay = factors[head, fwd]
        v = refs[fwd][2][:, vcols(head)]
        st = state[head, fwd]
        o = (jnp.dot(att[head, fwd], v, preferred_element_type=F32)
             + lax.dot_general(q_dec, st.astype(BF16), nt, preferred_element_type=F32))
        kv_t = lax.dot_general(v, k_end, (((0,), (0,)), ((), ())), preferred_element_type=F32)
        refs[fwd][5][:, vcols(head)] = o.astype(BF16)
        refs[fwd][4][head] = decay * st + kv_t

    for stage in (gate_stage, cumsum_stage, factor_stage, att_stage, out_stage):
        for head in range(GLA_HEADS_PER_STEP):
            for fwd in (True, False):
                stage(head, fwd)


def _gla(p, gin, wg_pad, bg, batch, seq):
    n_lat = seq // GLA_BLOCK
    ctx_block0 = batch * n_lat
    hps = GLA_HEADS_PER_STEP
    dk, dv = hps * GLA_DK, hps * GLA_DV

    def lat_block(b, s, forward):
        j = jnp.maximum(s - 1, 0)
        return b * n_lat + (j if forward else n_lat - 1 - j)

    def in_block(b, s, forward):
        return jnp.where(s == 0, ctx_block0 + b, lat_block(b, s, forward))

    kq = GLA_HEADS // hps
    kv = 2 * GLA_HEADS * GLA_DK // dv

    def dir_specs(forward):
        return [
            pl.BlockSpec((GLA_BLOCK, dk), lambda b, h, s: (in_block(b, s, forward), h)),
            pl.BlockSpec((GLA_BLOCK, dk), lambda b, h, s: (in_block(b, s, forward), kq + h)),
            pl.BlockSpec((GLA_BLOCK, dv), lambda b, h, s: (in_block(b, s, forward), kv + h)),
            pl.BlockSpec((GLA_BLOCK, LANES), lambda b, h, s: (in_block(b, s, forward), 0)),
        ]

    out_shape = jax.ShapeDtypeStruct((batch * seq, GLA_HEADS * GLA_DV), BF16)
    state = pltpu.VMEM((hps, GLA_DV, GLA_DK), F32)
    return pl.pallas_call(
        _gla_kernel,
        grid=(batch, GLA_HEADS // hps, n_lat + 1),
        in_specs=dir_specs(True) + dir_specs(False) + [
            pl.BlockSpec((2, LANES, dk), lambda b, h, s: (0, 0, h)),
            pl.BlockSpec((2, 1, dk), lambda b, h, s: (0, 0, h))],
        out_specs=[pl.BlockSpec((GLA_BLOCK, dv), lambda b, h, s: (lat_block(b, s, True), h)),
                   pl.BlockSpec((GLA_BLOCK, dv), lambda b, h, s: (lat_block(b, s, False), h))],
        out_shape=[out_shape, out_shape],
        scratch_shapes=[state, state],
        compiler_params=_params(("arbitrary", "arbitrary", "arbitrary")),
        name="gla",
    )(p, p, p, gin, p, p, p, gin, wg_pad, bg)


NA_QTOK = NA_QROWS * GRID_W
NA_SUB_TOK = NA_SUB_ROWS * GRID_W
NA_WIN = NA_WIN_ROWS * GRID_W
NA_SUBS = NA_QROWS // NA_SUB_ROWS
NA_KINDS = ((0, 0), (NA_SUB_ROWS, NA_SUB_ROWS - NA_KR // 2), (GRID_W - NA_SUB_ROWS, GRID_W - NA_WIN_ROWS))


def _na_build_bias(rpb_ref, t_ref, rows):
    n_slots = 2 * NA_KR - 1
    lane = lax.broadcasted_iota(jnp.int32, (GRID_W, LANES), 1)
    q_col = lax.broadcasted_iota(jnp.int32, (GRID_W, LANES), 0)
    k_col = lane & (GRID_W - 1)
    upper = lane >= GRID_W
    c0 = jnp.clip(q_col - NA_KC // 2, 0, GRID_W - NA_KC)
    col_ok = (k_col >= c0) & (k_col < c0 + NA_KC)
    neg = jnp.full((GRID_W, LANES), NEG_INF, F32)

    def toeplitz(hh, slot, lane_off):
        if not 0 <= slot < n_slots:
            return jnp.zeros((GRID_W, LANES), F32)
        base = jnp.broadcast_to(rpb_ref[hh, slot:slot + 1, :], (GRID_W, LANES)) * LOG2E
        return pltpu.roll(base, (LANES - (NA_KC - 1) + lane_off) % LANES, 1, stride=1, stride_axis=0)

    for hh in range(2):
        pairs = {}
        for kind, (r0, w0) in enumerate(NA_KINDS):
            for qr in range(NA_SUB_ROWS):
                band0 = min(max(r0 + qr - NA_KR // 2, 0), rows - NA_KR) - w0
                slot0 = w0 - (r0 + qr) + NA_KR - 1
                for g in range(NA_WIN_ROWS // 2):
                    slot = slot0 + 2 * g
                    lo_ok = band0 <= 2 * g < band0 + NA_KR
                    hi_ok = band0 <= 2 * g + 1 < band0 + NA_KR
                    if lo_ok or hi_ok:
                        if slot not in pairs:
                            pairs[slot] = jnp.where(upper, toeplitz(hh, slot + 1, GRID_W), toeplitz(hh, slot, 0))
                        ok = col_ok if (lo_ok and hi_ok) else (col_ok & upper if hi_ok else col_ok & ~upper)
                        tile = jnp.where(ok, pairs[slot], neg)
                    else:
                        tile = neg
                    t_ref[hh, kind, qr * GRID_W:(qr + 1) * GRID_W, g * LANES:(g + 1) * LANES] = tile


def _na_kernel(q_ref, k_ref, v_ref, kc_ref, vc_ref, rpb_ref, o_ref, t_ref, *, rows):
    b = pl.program_id(1)
    blk = pl.program_id(2)
    n_blk = rows // NA_QROWS

    pl.when((b == 0) & (blk == 0))(lambda: _na_build_bias(rpb_ref, t_ref, rows))

    nt = (((1,), (1,)), ((), ()))
    kc = kc_ref[...]
    vc = vc_ref[...]
    lane = lax.broadcasted_iota(jnp.int32, (NA_SUB_TOK, LANES), 1)
    chains = [(hh, sub) for sub in range(NA_SUBS) for hh in range(2)]
    q_head, kw, vw, kind = {}, [], [], []
    for sub in range(NA_SUBS):
        q = q_ref[sub * NA_SUB_TOK:(sub + 1) * NA_SUB_TOK, :].astype(F32) * (NA_DH ** -0.5 * LOG2E)
        q = q.astype(BF16)
        q_head[0, sub] = jnp.where(lane < NA_DH, q, jnp.zeros_like(q))
        q_head[1, sub] = jnp.where(lane >= NA_DH, q, jnp.zeros_like(q))
        r0 = blk * NA_QROWS + sub * NA_SUB_ROWS
        w0 = jnp.clip(r0 - NA_KR // 2, 0, rows - NA_WIN_ROWS)
        win = pl.ds(pl.multiple_of(w0 * GRID_W, 256), NA_WIN)
        kw.append(k_ref[win, :])
        vw.append(v_ref[win, :])
        kind.append(jnp.where(r0 == 0, 0, jnp.where(r0 == rows - NA_SUB_ROWS, 2, 1)))

    s_lat, s_ctx, p_lat, p_ctx, denom, out = {}, {}, {}, {}, {}, {}
    for hh, sub in chains:
        qs = q_head[hh, sub]
        s_lat[hh, sub] = lax.dot_general(qs, kw[sub], nt, preferred_element_type=F32) + t_ref[hh, kind[sub]]
        s_ctx[hh, sub] = lax.dot_general(qs, kc, nt, preferred_element_type=F32)
    for ch in chains:
        m = jnp.maximum(jnp.max(s_lat[ch], axis=-1, keepdims=True), jnp.max(s_ctx[ch], axis=-1, keepdims=True))
        pl_ = jnp.exp2(s_lat[ch] - m)
        pc_ = jnp.exp2(s_ctx[ch] - m)
        denom[ch] = jnp.sum(pl_, axis=-1, keepdims=True) + jnp.sum(pc_, axis=-1, keepdims=True)
        p_lat[ch] = pl_.astype(BF16)
        p_ctx[ch] = pc_.astype(BF16)
    for hh, sub in chains:
        o = (jnp.dot(p_lat[hh, sub], vw[sub], preferred_element_type=F32)
             + jnp.dot(p_ctx[hh, sub], vc, preferred_element_type=F32))
        out[hh, sub] = o / denom[hh, sub]
    for sub in range(NA_SUBS):
        rows_ = slice(sub * NA_SUB_TOK, (sub + 1) * NA_SUB_TOK)
        o_ref[rows_, :] = jnp.where(lane < NA_DH, out[0, sub], out[1, sub]).astype(o_ref.dtype)


def _na(p, rpb, batch, seq, ctx_len, col0):
    rows = seq // GRID_W
    n_blk = rows // NA_QROWS
    n_pairs = NA_HEADS // 2
    cq = col0 // LANES
    ck = cq + n_pairs
    cv = ck + n_pairs
    ctx_block0 = batch * seq // ctx_len
    return pl.pallas_call(
        functools.partial(_na_kernel, rows=rows),
        grid=(n_pairs, batch, n_blk),
        in_specs=[
            pl.BlockSpec((NA_QTOK, LANES), lambda hp, b, r: (b * n_blk + r, cq + hp)),
            pl.BlockSpec((seq, LANES), lambda hp, b, r: (b, ck + hp)),
            pl.BlockSpec((seq, LANES), lambda hp, b, r: (b, cv + hp)),
            pl.BlockSpec((ctx_len, LANES), lambda hp, b, r: (ctx_block0 + b, ck + hp)),
            pl.BlockSpec((ctx_len, LANES), lambda hp, b, r: (ctx_block0 + b, cv + hp)),
            pl.BlockSpec((2, 2 * NA_KR - 1, LANES), lambda hp, b, r: (hp, 0, 0)),
        ],
        out_specs=pl.BlockSpec((NA_QTOK, LANES), lambda hp, b, r: (b * n_blk + r, hp)),
        out_shape=jax.ShapeDtypeStruct((batch * seq, NA_HEADS * NA_DH), BF16),
        scratch_shapes=[pltpu.VMEM((2, len(NA_KINDS), NA_SUB_TOK, NA_WIN), F32)],
        compiler_params=_params(("arbitrary", "arbitrary", "arbitrary")),
        name="natten",
    )(p, p, p, p, p, rpb)


def _merge_kernel(of_ref, ob_ref, r_ref, gn_ref, b_ref, wa_ref, wb_ref, m1_ref, m2_ref, o_ref, a_ref):
    @pl.when(pl.program_id(1) == 0)
    def _():
        for h in range(GLA_HEADS):
            cols = slice(h * GLA_DV, (h + 1) * GLA_DV)
            tot = of_ref[:, cols].astype(F32) + ob_ref[:, cols].astype(F32)
            a_ref[:, cols] = (_rms(tot, gn_ref[...]) * _silu(r_ref[:, cols].astype(F32))).astype(BF16)

    a = jnp.dot(a_ref[...], wa_ref[...], preferred_element_type=F32)
    b = jnp.dot(b_ref[...], wb_ref[...], preferred_element_type=F32)
    m = jax.nn.sigmoid(m1_ref[...].astype(F32)) * a + jax.nn.sigmoid(m2_ref[...].astype(F32)) * b
    o_ref[...] = m.astype(o_ref.dtype)


def _merge(o_fwd, o_bwd, gn, o_na, w_gla_o, w_na_o, p, col_r, col_m1):
    n_rows = o_fwd.shape[0]
    v_w = o_fwd.shape[1]
    tn = MERGE_TILE
    c1 = col_m1 // tn
    c2 = c1 + D_MODEL // tn
    return pl.pallas_call(
        _merge_kernel,
        grid=(n_rows // ROW_TILE, D_MODEL // tn),
        in_specs=[pl.BlockSpec((ROW_TILE, v_w), lambda i, n: (i, 0)),
                  pl.BlockSpec((ROW_TILE, v_w), lambda i, n: (i, 0)),
                  pl.BlockSpec((ROW_TILE, v_w), lambda i, n: (i, col_r // v_w)),
                  pl.BlockSpec((1, GLA_DV), lambda i, n: (0, 0)),
                  pl.BlockSpec((ROW_TILE, o_na.shape[1]), lambda i, n: (i, 0)),
                  pl.BlockSpec((w_gla_o.shape[0], tn), lambda i, n: (0, n)),
                  pl.BlockSpec((w_na_o.shape[0], tn), lambda i, n: (0, n)),
                  pl.BlockSpec((ROW_TILE, tn), lambda i, n: (i, c1 + n)),
                  pl.BlockSpec((ROW_TILE, tn), lambda i, n: (i, c2 + n))],
        out_specs=pl.BlockSpec((ROW_TILE, tn), lambda i, n: (i, n)),
        out_shape=jax.ShapeDtypeStruct((n_rows, D_MODEL), BF16),
        scratch_shapes=[pltpu.VMEM((ROW_TILE, v_w), BF16)],
        compiler_params=_params(("arbitrary", "arbitrary")),
        name="merge",
    )(o_fwd, o_bwd, p, gn, o_na, w_gla_o, w_na_o, p, p)


def _outproj_kernel(m_ref, w_ref, x_ref, mod_ref, g_ref, o_ref, part_ref, coef_ref):
    tm, d = o_ref.shape
    o_ref[...] = jnp.dot(m_ref[...], w_ref[...], preferred_element_type=F32)
    _set_row(coef_ref, 0, mod_ref[5:6, :] * g_ref[3:4, :])
    _row_inv_rms(lambda rows: o_ref[rows, :], tm, d, part_ref)

    def emit(rows):
        y = o_ref[rows, :] * _spread(part_ref[rows, :], d) * coef_ref[0]
        o_ref[rows, :] = x_ref[rows, :] + y

    _for_row_chunks(tm, emit)


def _outproj(m, w_out, h1, mod, norm_g):
    n_tiles = m.shape[0] // ROW_TILE
    tiles_per_batch = n_tiles // 2
    return pl.pallas_call(
        _outproj_kernel,
        grid=(n_tiles,),
        in_specs=[pl.BlockSpec((ROW_TILE, D_MODEL), lambda i: (i, 0)),
                  pl.BlockSpec((D_MODEL, D_MODEL), lambda i: (0, 0)),
                  pl.BlockSpec((ROW_TILE, D_MODEL), lambda i: (i, 0)),
                  pl.BlockSpec((None, N_MOD, D_MODEL), lambda i: (i // tiles_per_batch, 0, 0)),
                  pl.BlockSpec((6, D_MODEL), lambda i: (0, 0))],
        out_specs=pl.BlockSpec((ROW_TILE, D_MODEL), lambda i: (i, 0)),
        out_shape=jax.ShapeDtypeStruct((m.shape[0], D_MODEL), F32),
        scratch_shapes=[pltpu.VMEM((ROW_TILE, LANES), F32), pltpu.VMEM((2, RMS_CHUNK, D_MODEL), F32)],
        compiler_params=_params(("arbitrary",)),
        name="outproj",
    )(m, w_out, h1, mod, norm_g)


def kernel(x, c, ctx, c_ctx, w_ada, b_ada, norm_g, ffn_wg, ffn_wu, ffn_wd, w_in, gla_wg, gla_bg, gla_norm_g,
           w_gla_o, na_rpb, w_na_o, w_out):
    batch, seq, d = x.shape
    ctx_len = ctx.shape[1]
    depth = w_ada.shape[0]
    assert d == D_MODEL and batch == 2 and batch * ctx_len == ROW_TILE and depth == 1
    assert seq % ROW_TILE == 0 and seq // GRID_W == GRID_W

    qk_w = GLA_HEADS * GLA_DK
    v_w = GLA_HEADS * GLA_DV
    na_w = NA_HEADS * NA_DH
    gate0 = 2 * qk_w + 2 * v_w
    gate1 = gate0 + 2 * GLA_GATE_RANK
    col_nq = gate0
    col_m1 = col_nq + 3 * na_w

    cvec = jnp.zeros((8, d), F32).at[0:batch].set(c).at[batch].set(c_ctx)
    cos, sin = _rope_tables(seq)
    h = x.reshape(batch * seq, d)
    hc = ctx.reshape(batch * ctx_len, d)
    lat_tiles_per_batch = seq // BIG_ROW_TILE
    for l in range(depth):
        mod = _modulation(cvec, w_ada[l], b_ada[l].reshape(1, -1)).reshape(8, N_MOD, d)
        g = norm_g[l]
        h1 = _ffn(h, mod, g, ffn_wg[l], ffn_wu[l], ffn_wd[l], 0, 0, tm=BIG_ROW_TILE, tf=FF_TILE,
                  mod_row=lambda i: i // lat_tiles_per_batch)
        h1c = _ffn(hc, mod, g, ffn_wg[l], ffn_wu[l], ffn_wd[l], 0, 0, tm=ROW_TILE, tf=FF_TILE,
                   mod_row=lambda i: batch)
        p, gin = _inproj(h1, h1c, mod, g, jnp.swapaxes(w_in[l], 0, 1), cos, sin, seq, gate0, gate1)
        wg_pad = jnp.zeros((2, LANES, qk_w), F32)
        wg_pad = wg_pad.at[0, :GLA_GATE_RANK].set(gla_wg[l, 0])
        wg_pad = wg_pad.at[1, GLA_GATE_RANK:2 * GLA_GATE_RANK].set(gla_wg[l, 1])
        bg = gla_bg[l].reshape(2, 1, qk_w)
        gn = gla_norm_g[l].reshape(1, GLA_DV)
        o_fwd, o_bwd = _gla(p, gin, wg_pad, bg, batch, seq)
        rpb = jnp.pad(na_rpb[l], ((0, 0), (0, 0), (0, LANES - (2 * NA_KC - 1))))
        o_na = _na(p, rpb, batch, seq, ctx_len, col_nq)
        m = _merge(o_fwd, o_bwd, gn, o_na, w_gla_o[l].astype(BF16), w_na_o[l].astype(BF16), p,
                   2 * qk_w + v_w, col_m1)
        h2 = _outproj(m, w_out[l].astype(BF16), h1, mod, g)
        h = _ffn(h2, mod, g, ffn_wg[l], ffn_wu[l], ffn_wd[l], 1, 2, tm=BIG_ROW_TILE, tf=FF_TILE,
                 mod_row=lambda i: i // lat_tiles_per_batch)
    return h.reshape(batch, seq, d)
```

```python
import functools

import numpy as np
import jax
import jax.numpy as jnp
from jax import lax
from jax.experimental import pallas as pl
from jax.experimental.pallas import tpu as pltpu

F32 = jnp.float32
BF16 = jnp.bfloat16

D_MODEL = 2048
GRID_W = 64
GLA_HEADS = 4
GLA_DK = 256
GLA_DV = 512
GLA_GATE_RANK = 16
GLA_GATE_TEMP = 16.0
NA_HEADS = 16
NA_DH = 64
NA_KR = 8
NA_KC = 16
ROPE_THETA = 10000.0
EPS = 1e-6
NEG_INF = -1e30
N_MOD = 9
LOG2E = 1.4426950408889634

LANES = 128
ROW_TILE = 512
BIG_ROW_TILE = 1024
FF_TILE = 512
PROJ_TILE = 1024
MERGE_STRIP = 512
GLA_BLOCK = 256
GLA_HEADS_PER_STEP = 4
NA_QROWS = 8
NA_SUB_ROWS = 4
NA_WIN_ROWS = 12
RMS_CHUNK = 16
VMEM_LIMIT = 56 * 1024 * 1024


def _params(sem):
    return pltpu.CompilerParams(dimension_semantics=sem, vmem_limit_bytes=VMEM_LIMIT)


def _rms(x, g):
    return x * lax.rsqrt(jnp.mean(x * x, axis=-1, keepdims=True) + EPS) * g


def _sigmoid(x):
    return 0.5 * jnp.tanh(0.5 * x) + 0.5


def _silu(x):
    h = 0.5 * x
    return h * jnp.tanh(h) + h


def _for_row_chunks(n_rows, body):
    def step(j, carry):
        body(pl.ds(pl.multiple_of(j * RMS_CHUNK, RMS_CHUNK), RMS_CHUNK))
        return carry
    lax.fori_loop(0, n_rows // RMS_CHUNK, step, 0, unroll=2)


def _row_inv_rms(load_rows, n_rows, d, part_ref):
    def sum_squares(rows):
        x2 = jnp.square(load_rows(rows))
        parts = [x2[:, g * LANES:(g + 1) * LANES] for g in range(d // LANES)]
        while len(parts) > 1:
            parts = [a + b for a, b in zip(parts[::2], parts[1::2])]
        part_ref[rows, :] = parts[0]

    _for_row_chunks(n_rows, sum_squares)
    r = slice(0, n_rows)
    inv = lax.rsqrt(jnp.sum(part_ref[r, :], axis=-1, keepdims=True) * (1.0 / d) + EPS)
    part_ref[r, :] = jnp.broadcast_to(inv, (n_rows, LANES))


def _spread(v, d):
    return jnp.concatenate([v] * (d // LANES), axis=1)


def _set_row(coef_ref, i, row):
    coef_ref[i] = jnp.broadcast_to(row, coef_ref.shape[1:])


def _mod_kernel(c_ref, w_ref, b_ref, o_ref):
    o_ref[...] = jnp.dot(_silu(c_ref[...]), w_ref[...], preferred_element_type=F32) + b_ref[...]


def _modulation(cvec, w_ada, b_ada):
    n = w_ada.shape[1]
    tn = 1024
    return pl.pallas_call(
        _mod_kernel,
        grid=(n // tn,),
        in_specs=[pl.BlockSpec((8, D_MODEL), lambda j: (0, 0)),
                  pl.BlockSpec((D_MODEL, tn), lambda j: (0, j)),
                  pl.BlockSpec((1, tn), lambda j: (0, j))],
        out_specs=pl.BlockSpec((8, tn), lambda j: (0, j)),
        out_shape=jax.ShapeDtypeStruct((8, n), F32),
        compiler_params=_params(("arbitrary",)),
        name="adaln_mod",
    )(cvec, w_ada, b_ada)


def _mixed_dot(a, w):
    return lax.dot_general(a, w, (((1,), (0,)), ((), ())), preferred_element_type=F32)


def _ffn_kernel(x_ref, mod_ref, g_ref, wg_ref, wu_ref, wd_ref, o_ref, h_ref, part_ref, coef_ref, *, sub):
    f = pl.program_id(1)
    tm, d = x_ref.shape

    @pl.when(f == 0)
    def _():
        _set_row(coef_ref, 0, g_ref[2 * sub:2 * sub + 1, :] * (1.0 + mod_ref[3 * sub + 1:3 * sub + 2, :]))
        _set_row(coef_ref, 1, mod_ref[3 * sub:3 * sub + 1, :])
        _row_inv_rms(lambda rows: x_ref[rows, :], tm, d, part_ref)

        def emit(rows):
            h = x_ref[rows, :] * _spread(part_ref[rows, :], d) * coef_ref[0] + coef_ref[1]
            h_ref[rows, :] = h.astype(BF16)
            o_ref[rows, :] = jnp.zeros((RMS_CHUNK, d), F32)

        _for_row_chunks(tm, emit)

    for r0 in range(0, tm, ROW_TILE):
        rows = slice(r0, r0 + ROW_TILE)
        h = h_ref[rows, :]
        g = _mixed_dot(h, wg_ref[...])
        u = _mixed_dot(h, wu_ref[...])
        o_ref[rows, :] += _mixed_dot((_silu(g) * u).astype(BF16), wd_ref[...])

    @pl.when(f == pl.num_programs(1) - 1)
    def _():
        _set_row(coef_ref, 0, 0.5 * mod_ref[3 * sub + 2:3 * sub + 3, :] * g_ref[2 * sub + 1:2 * sub + 2, :])
        _row_inv_rms(lambda rows: o_ref[rows, :], tm, d, part_ref)

        def emit(rows):
            y = o_ref[rows, :] * _spread(part_ref[rows, :], d) * coef_ref[0]
            o_ref[rows, :] = x_ref[rows, :] + y

        _for_row_chunks(tm, emit)


def _ffn(x2d, mod, norm_g, wg, wu, wd, which, sub, *, tm, tf, mod_row):
    n_tiles = x2d.shape[0] // tm
    d_ff = wg.shape[2]
    once = dict(pipeline_mode=pl.Buffered(1))
    return pl.pallas_call(
        functools.partial(_ffn_kernel, sub=sub),
        grid=(n_tiles, d_ff // tf),
        in_specs=[
            pl.BlockSpec((tm, D_MODEL), lambda i, f: (i, 0), **once),
            pl.BlockSpec((None, N_MOD, D_MODEL), lambda i, f: (mod_row(i), 0, 0)),
            pl.BlockSpec((6, D_MODEL), lambda i, f: (0, 0)),
            pl.BlockSpec((None, D_MODEL, tf), lambda i, f: (which, 0, f)),
            pl.BlockSpec((None, D_MODEL, tf), lambda i, f: (which, 0, f)),
            pl.BlockSpec((None, tf, D_MODEL), lambda i, f: (which, f, 0)),
        ],
        out_specs=pl.BlockSpec((tm, D_MODEL), lambda i, f: (i, 0), **once),
        out_shape=jax.ShapeDtypeStruct(x2d.shape, F32),
        scratch_shapes=[pltpu.VMEM((tm, D_MODEL), BF16), pltpu.VMEM((tm, LANES), F32),
                        pltpu.VMEM((2, RMS_CHUNK, D_MODEL), F32)],
        compiler_params=_params(("arbitrary", "arbitrary")),
        name="ffn%d_%d" % (sub, tm),
    )(x2d, mod, norm_g, wg, wu, wd)


N_ROPE_TILES = 2 * GLA_HEADS * GLA_DK // PROJ_TILE
N_Q_TILES = N_ROPE_TILES // 2


def _rope_tables(seq):
    half = GLA_DK // 4
    freqs = ROPE_THETA ** (-np.arange(half, dtype=np.float64) / half)
    t = np.arange(seq)
    cos_parts, sin_parts = [], []
    for pos in (t // GRID_W, t % GRID_W):
        ang = pos[:, None].astype(np.float64) * freqs
        cos_parts += [np.cos(ang), np.cos(ang)]
        sin_parts += [-np.sin(ang), np.sin(ang)]
    cos = np.concatenate(cos_parts, axis=1)
    sin = np.concatenate(sin_parts, axis=1)
    cos = np.concatenate([cos, np.ones((BIG_ROW_TILE, GLA_DK))], axis=0)
    sin = np.concatenate([sin, np.zeros((BIG_ROW_TILE, GLA_DK))], axis=0)
    return jnp.asarray(cos, F32), jnp.asarray(sin, F32)


def _dot_nt(a, w_t):
    return lax.dot_general(a, w_t, (((1,), (1,)), ((), ())), preferred_element_type=F32)


def _inproj_kernel(x_ref, xc_ref, mod_ref, g_ref, w_ref, wgate_ref, cos_ref, sin_ref, p_ref, gate_ref, h_ref,
                   *, n_lat_tiles):
    i = pl.program_id(0)
    n = pl.program_id(1)

    def project(src_ref):
        r = slice(0, src_ref.shape[0])

        @pl.when(n == 0)
        def _():
            h = _rms(src_ref[...], g_ref[2:3, :]) * (1.0 + mod_ref[4:5, :]) + mod_ref[3:4, :]
            h_ref[r, :] = h.astype(BF16)
            gate = _dot_nt(h_ref[r, :], wgate_ref[...])
            gate_ref[r, :] = jnp.concatenate(
                [gate, jnp.zeros((gate.shape[0], LANES - gate.shape[1]), F32)], axis=1)

        @pl.when(n < N_ROPE_TILES)
        def _():
            y = _dot_nt(h_ref[r, :], w_ref[...])
            cos = cos_ref[r, :]
            sin = sin_ref[r, :]
            qscale = jnp.where(n < N_Q_TILES, GLA_DK ** -0.5, 1.0).astype(F32)
            for j in range(PROJ_TILE // GLA_DK):
                yj = y[:, j * GLA_DK:(j + 1) * GLA_DK]
                swapped = jnp.concatenate(
                    [pltpu.roll(yj[:, g * LANES:(g + 1) * LANES], LANES // 2, 1) for g in range(GLA_DK // LANES)],
                    axis=1)
                p_ref[r, j * GLA_DK:(j + 1) * GLA_DK] = ((yj * cos + swapped * sin) * qscale).astype(BF16)

        @pl.when(n >= N_ROPE_TILES)
        def _():
            p_ref[r, :] = _dot_nt(h_ref[r, :], w_ref[...]).astype(BF16)

    pl.when(i < n_lat_tiles)(lambda: project(x_ref))
    pl.when(i >= n_lat_tiles)(lambda: project(xc_ref))


def _inproj(h1, h1c, mod, norm_g, w_in_t, cos, sin, seq, gate0, gate1):
    tm = BIG_ROW_TILE
    n_lat_tiles = h1.shape[0] // tm
    n_rows = h1.shape[0] + h1c.shape[0]
    tiles_per_batch = n_lat_tiles // 2
    n_a_tiles = gate0 // PROJ_TILE
    n_b_tiles = (w_in_t.shape[0] - gate1) // PROJ_TILE
    n_gate = gate1 - gate0
    tab_spec = pl.BlockSpec(
        (tm, GLA_DK), lambda i, n: (jnp.where(i < n_lat_tiles, i % tiles_per_batch, seq // tm), 0))
    return pl.pallas_call(
        functools.partial(_inproj_kernel, n_lat_tiles=n_lat_tiles),
        grid=(n_lat_tiles + 1, n_a_tiles + n_b_tiles),
        in_specs=[pl.BlockSpec((tm, D_MODEL), lambda i, n: (jnp.minimum(i, n_lat_tiles - 1), 0)),
                  pl.BlockSpec(h1c.shape, lambda i, n: (0, 0)),
                  pl.BlockSpec((None, N_MOD, D_MODEL), lambda i, n: (jnp.minimum(i // tiles_per_batch, 2), 0, 0)),
                  pl.BlockSpec((6, D_MODEL), lambda i, n: (0, 0)),
                  pl.BlockSpec((pl.Element(PROJ_TILE), pl.Element(D_MODEL)),
                               lambda i, n: (pl.multiple_of(
                                   n * PROJ_TILE + jnp.where(n < n_a_tiles, 0, n_gate), n_gate), 0)),
                  pl.BlockSpec((n_gate, D_MODEL), lambda i, n: (gate0 // n_gate, 0)),
                  tab_spec, tab_spec],
        out_specs=[pl.BlockSpec((tm, PROJ_TILE), lambda i, n: (i, n)),
                   pl.BlockSpec((tm, LANES), lambda i, n: (i, 0))],
        out_shape=[jax.ShapeDtypeStruct((n_rows, (n_a_tiles + n_b_tiles) * PROJ_TILE), BF16),
                   jax.ShapeDtypeStruct((n_rows, LANES), F32)],
        scratch_shapes=[pltpu.VMEM((tm, D_MODEL), BF16)],
        compiler_params=_params(("arbitrary", "arbitrary")),
        name="inproj",
    )(h1, h1c, mod, norm_g, w_in_t, w_in_t, cos, sin)


def _log2_sigmoid(z):
    t = z * LOG2E
    return jnp.minimum(t, 0.0) - jnp.log2(1.0 + jnp.exp2(-jnp.abs(t)))


def _gla_kernel(qf_ref, kf_ref, vf_ref, gf_ref, qb_ref, kb_ref, vb_ref, gb_ref, wg_ref, bg_ref,
                of_ref, ob_ref, stf_ref, stb_ref):
    @pl.when(pl.program_id(2) == 0)
    def _():
        stf_ref[...] = jnp.zeros_like(stf_ref)
        stb_ref[...] = jnp.zeros_like(stb_ref)

    c = GLA_BLOCK
    nt = (((1,), (1,)), ((), ()))
    row = lax.broadcasted_iota(jnp.int32, (c, c), 0)
    col = lax.broadcasted_iota(jnp.int32, (c, c), 1)
    mask = {True: col <= row, False: col >= row}
    tri = {fwd: jnp.where(m, 1.0, 0.0).astype(BF16) for fwd, m in mask.items()}
    refs = {True: (qf_ref, kf_ref, vf_ref, gf_ref, stf_ref, of_ref),
            False: (qb_ref, kb_ref, vb_ref, gb_ref, stb_ref, ob_ref)}
    qcols = lambda head: slice(head * GLA_DK, (head + 1) * GLA_DK)
    vcols = lambda head: slice(head * GLA_DV, (head + 1) * GLA_DV)

    state, log_a, cum, factors, att = {}, {}, {}, {}, {}

    def gate_stage(head, fwd):
        d = 0 if fwd else 1
        state[head, fwd] = refs[fwd][4][head]
        z = jnp.dot(refs[fwd][3][...], wg_ref[d, :, qcols(head)], preferred_element_type=F32)
        log_a[head, fwd] = _log2_sigmoid(z + bg_ref[d, :, qcols(head)]) * (1.0 / GLA_GATE_TEMP)

    def cumsum_stage(head, fwd):
        la = log_a[head, fwd]
        hi = la.astype(BF16)
        lo = (la - hi.astype(F32)).astype(BF16)
        cum[head, fwd] = (jnp.dot(tri[fwd], hi, preferred_element_type=F32)
                          + jnp.dot(tri[fwd], lo, preferred_element_type=F32))

    def factor_stage(head, fwd):
        cm = cum[head, fwd]
        cum_end = cm[c - 1:c, :] if fwd else cm[0:1, :]
        cum_mid = cm[c // 2:c // 2 + 1, :]
        q = refs[fwd][0][:, qcols(head)]
        k = refs[fwd][1][:, qcols(head)]
        factors[head, fwd] = (q * jnp.exp2(cm - cum_mid).astype(BF16),
                              k * jnp.exp2(cum_mid - cm).astype(BF16),
                              q * jnp.exp2(cm).astype(BF16),
                              k * jnp.exp2(cum_end - cm).astype(BF16),
                              jnp.exp2(cum_end))

    def att_stage(head, fwd):
        q_att, k_att = factors[head, fwd][:2]
        a = lax.dot_general(q_att, k_att, nt, preferred_element_type=F32)
        att[head, fwd] = jnp.where(mask[fwd], a, 0.0).astype(BF16)

    def out_stage(head, fwd):
        _, _, q_dec, k_end, decay = factors[head, fwd]
        v = refs[fwd][2][:, vcols(head)]
        st = state[head, fwd]
        o = (jnp.dot(att[head, fwd], v, preferred_element_type=F32)
             + lax.dot_general(q_dec, st.astype(BF16), nt, preferred_element_type=F32))
        kv_t = lax.dot_general(v, k_end, (((0,), (0,)), ((), ())), preferred_element_type=F32)
        refs[fwd][5][:, vcols(head)] = o.astype(BF16)
        refs[fwd][4][head] = decay * st + kv_t

    for stage in (gate_stage, cumsum_stage, factor_stage, att_stage, out_stage):
        for head in range(GLA_HEADS_PER_STEP):
            for fwd in (True, False):
                stage(head, fwd)


def _gla(p, gin, wg_pad, bg, batch, seq):
    n_lat = seq // GLA_BLOCK
    ctx_block0 = batch * n_lat
    hps = GLA_HEADS_PER_STEP
    dk, dv = hps * GLA_DK, hps * GLA_DV

    def lat_block(b, s, forward):
        j = jnp.maximum(s - 1, 0)
        return b * n_lat + (j if forward else n_lat - 1 - j)

    def in_block(b, s, forward):
        return jnp.where(s == 0, ctx_block0 + b, lat_block(b, s, forward))

    kq = GLA_HEADS // hps
    kv = 2 * GLA_HEADS * GLA_DK // dv

    def dir_specs(forward):
        return [
            pl.BlockSpec((GLA_BLOCK, dk), lambda b, h, s: (in_block(b, s, forward), h)),
            pl.BlockSpec((GLA_BLOCK, dk), lambda b, h, s: (in_block(b, s, forward), kq + h)),
            pl.BlockSpec((GLA_BLOCK, dv), lambda b, h, s: (in_block(b, s, forward), kv + h)),
            pl.BlockSpec((GLA_BLOCK, LANES), lambda b, h, s: (in_block(b, s, forward), 0)),
        ]

    out_shape = jax.ShapeDtypeStruct((batch * seq, GLA_HEADS * GLA_DV), BF16)
    state = pltpu.VMEM((hps, GLA_DV, GLA_DK), F32)
    return pl.pallas_call(
        _gla_kernel,
        grid=(batch, GLA_HEADS // hps, n_lat + 1),
        in_specs=dir_specs(True) + dir_specs(False) + [
            pl.BlockSpec((2, LANES, dk), lambda b, h, s: (0, 0, h)),
            pl.BlockSpec((2, 1, dk), lambda b, h, s: (0, 0, h))],
        out_specs=[pl.BlockSpec((GLA_BLOCK, dv), lambda b, h, s: (lat_block(b, s, True), h)),
                   pl.BlockSpec((GLA_BLOCK, dv), lambda b, h, s: (lat_block(b, s, False), h))],
        out_shape=[out_shape, out_shape],
        scratch_shapes=[state, state],
        compiler_params=_params(("arbitrary", "arbitrary", "arbitrary")),
        name="gla",
    )(p, p, p, gin, p, p, p, gin, wg_pad, bg)


NA_QTOK = NA_QROWS * GRID_W
NA_SUB_TOK = NA_SUB_ROWS * GRID_W
NA_WIN = NA_WIN_ROWS * GRID_W
NA_SUBS = NA_QROWS // NA_SUB_ROWS
NA_KINDS = ((0, 0), (NA_SUB_ROWS, NA_SUB_ROWS - NA_KR // 2), (GRID_W - NA_SUB_ROWS, GRID_W - NA_WIN_ROWS))


def _na_build_bias(rpb_ref, t_ref, rows):
    n_slots = 2 * NA_KR - 1
    lane = lax.broadcasted_iota(jnp.int32, (GRID_W, LANES), 1)
    q_col = lax.broadcasted_iota(jnp.int32, (GRID_W, LANES), 0)
    k_col = lane & (GRID_W - 1)
    upper = lane >= GRID_W
    c0 = jnp.clip(q_col - NA_KC // 2, 0, GRID_W - NA_KC)
    col_ok = (k_col >= c0) & (k_col < c0 + NA_KC)
    neg = jnp.full((GRID_W, LANES), NEG_INF, F32)

    def toeplitz(hh, slot, lane_off):
        if not 0 <= slot < n_slots:
            return jnp.zeros((GRID_W, LANES), F32)
        base = jnp.broadcast_to(rpb_ref[hh, slot:slot + 1, :], (GRID_W, LANES)) * LOG2E
        return pltpu.roll(base, (LANES - (NA_KC - 1) + lane_off) % LANES, 1, stride=1, stride_axis=0)

    for hh in range(2):
        pairs = {}
        for kind, (r0, w0) in enumerate(NA_KINDS):
            for qr in range(NA_SUB_ROWS):
                band0 = min(max(r0 + qr - NA_KR // 2, 0), rows - NA_KR) - w0
                slot0 = w0 - (r0 + qr) + NA_KR - 1
                for g in range(NA_WIN_ROWS // 2):
                    slot = slot0 + 2 * g
                    lo_ok = band0 <= 2 * g < band0 + NA_KR
                    hi_ok = band0 <= 2 * g + 1 < band0 + NA_KR
                    if lo_ok or hi_ok:
                        if slot not in pairs:
                            pairs[slot] = jnp.where(upper, toeplitz(hh, slot + 1, GRID_W), toeplitz(hh, slot, 0))
                        ok = col_ok if (lo_ok and hi_ok) else (col_ok & upper if hi_ok else col_ok & ~upper)
                        tile = jnp.where(ok, pairs[slot], neg)
                    else:
                        tile = neg
                    t_ref[hh, kind, qr * GRID_W:(qr + 1) * GRID_W, g * LANES:(g + 1) * LANES] = tile


def _na_kernel(q_ref, k_ref, v_ref, kc_ref, vc_ref, rpb_ref, o_ref, t_ref, *, rows):
    b = pl.program_id(1)
    blk = pl.program_id(2)
    n_blk = rows // NA_QROWS

    pl.when((b == 0) & (blk == 0))(lambda: _na_build_bias(rpb_ref, t_ref, rows))

    nt = (((1,), (1,)), ((), ()))
    kc = kc_ref[...]
    vc = vc_ref[...]
    lane = lax.broadcasted_iota(jnp.int32, (NA_SUB_TOK, LANES), 1)
    chains = [(hh, sub) for sub in range(NA_SUBS) for hh in range(2)]
    q_head, kw, vw, kind = {}, [], [], []
    for sub in range(NA_SUBS):
        q = q_ref[sub * NA_SUB_TOK:(sub + 1) * NA_SUB_TOK, :].astype(F32) * (NA_DH ** -0.5 * LOG2E)
        q = q.astype(BF16)
        q_head[0, sub] = jnp.where(lane < NA_DH, q, jnp.zeros_like(q))
        q_head[1, sub] = jnp.where(lane >= NA_DH, q, jnp.zeros_like(q))
        r0 = blk * NA_QROWS + sub * NA_SUB_ROWS
        w0 = jnp.clip(r0 - NA_KR // 2, 0, rows - NA_WIN_ROWS)
        win = pl.ds(pl.multiple_of(w0 * GRID_W, 256), NA_WIN)
        kw.append(k_ref[win, :])
        vw.append(v_ref[win, :])
        kind.append(jnp.where(r0 == 0, 0, jnp.where(r0 == rows - NA_SUB_ROWS, 2, 1)))

    s_lat, s_ctx, p_lat, p_ctx, denom, out = {}, {}, {}, {}, {}, {}
    for hh, sub in chains:
        qs = q_head[hh, sub]
        s_lat[hh, sub] = lax.dot_general(qs, kw[sub], nt, preferred_element_type=F32) + t_ref[hh, kind[sub]]
        s_ctx[hh, sub] = lax.dot_general(qs, kc, nt, preferred_element_type=F32)
    for ch in chains:
        m = jnp.maximum(jnp.max(s_lat[ch], axis=-1, keepdims=True), jnp.max(s_ctx[ch], axis=-1, keepdims=True))
        pl_ = jnp.exp2(s_lat[ch] - m)
        pc_ = jnp.exp2(s_ctx[ch] - m)
        denom[ch] = jnp.sum(pl_, axis=-1, keepdims=True) + jnp.sum(pc_, axis=-1, keepdims=True)
        p_lat[ch] = pl_.astype(BF16)
        p_ctx[ch] = pc_.astype(BF16)
    for hh, sub in chains:
        o = (jnp.dot(p_lat[hh, sub], vw[sub], preferred_element_type=F32)
             + jnp.dot(p_ctx[hh, sub], vc, preferred_element_type=F32))
        out[hh, sub] = o / denom[hh, sub]
    for sub in range(NA_SUBS):
        rows_ = slice(sub * NA_SUB_TOK, (sub + 1) * NA_SUB_TOK)
        o_ref[rows_, :] = jnp.where(lane < NA_DH, out[0, sub], out[1, sub]).astype(o_ref.dtype)


def _na(p, rpb, batch, seq, ctx_len, col0):
    rows = seq // GRID_W
    n_blk = rows // NA_QROWS
    n_pairs = NA_HEADS // 2
    cq = col0 // LANES
    ck = cq + n_pairs
    cv = ck + n_pairs
    ctx_block0 = batch * seq // ctx_len
    return pl.pallas_call(
        functools.partial(_na_kernel, rows=rows),
        grid=(n_pairs, batch, n_blk),
        in_specs=[
            pl.BlockSpec((NA_QTOK, LANES), lambda hp, b, r: (b * n_blk + r, cq + hp)),
            pl.BlockSpec((seq, LANES), lambda hp, b, r: (b, ck + hp)),
            pl.BlockSpec((seq, LANES), lambda hp, b, r: (b, cv + hp)),
            pl.BlockSpec((ctx_len, LANES), lambda hp, b, r: (ctx_block0 + b, ck + hp)),
            pl.BlockSpec((ctx_len, LANES), lambda hp, b, r: (ctx_block0 + b, cv + hp)),
            pl.BlockSpec((2, 2 * NA_KR - 1, LANES), lambda hp, b, r: (hp, 0, 0)),
        ],
        out_specs=pl.BlockSpec((NA_QTOK, LANES), lambda hp, b, r: (b * n_blk + r, hp)),
        out_shape=jax.ShapeDtypeStruct((batch * seq, NA_HEADS * NA_DH), BF16),
        scratch_shapes=[pltpu.VMEM((2, len(NA_KINDS), NA_SUB_TOK, NA_WIN), F32)],
        compiler_params=_params(("arbitrary", "arbitrary", "arbitrary")),
        name="natten",
    )(p, p, p, p, p, rpb)


def _merge_kernel(of_ref, ob_ref, r_ref, gn_ref, b_ref, wa_ref, wb_ref, m1a_ref, m1b_ref, m2a_ref, m2b_ref,
                  o_ref, a_ref):
    for h in range(GLA_HEADS):
        cols = slice(h * GLA_DV, (h + 1) * GLA_DV)
        tot = of_ref[:, cols].astype(F32) + ob_ref[:, cols].astype(F32)
        a_ref[:, cols] = (_rms(tot, gn_ref[...]) * _silu(r_ref[:, cols].astype(F32))).astype(BF16)

    gate_refs = ((m1a_ref, m2a_ref), (m1b_ref, m2b_ref))
    gate_w = m1a_ref.shape[1]
    for c0 in range(0, o_ref.shape[1], MERGE_STRIP):
        cols = slice(c0, c0 + MERGE_STRIP)
        m1_ref, m2_ref = gate_refs[c0 // gate_w]
        gcols = slice(c0 % gate_w, c0 % gate_w + MERGE_STRIP)
        a = jnp.dot(a_ref[...], wa_ref[:, cols], preferred_element_type=F32)
        b = jnp.dot(b_ref[...], wb_ref[:, cols], preferred_element_type=F32)
        m = _sigmoid(m1_ref[:, gcols].astype(F32)) * a + _sigmoid(m2_ref[:, gcols].astype(F32)) * b
        o_ref[:, cols] = m.astype(o_ref.dtype)


def _merge(o_fwd, o_bwd, gn, o_na, w_gla_o, w_na_o, p, col_r, col_m1):
    n_rows = o_fwd.shape[0]
    v_w = o_fwd.shape[1]
    gw = D_MODEL // 2
    c1 = col_m1 // gw
    once = dict(pipeline_mode=pl.Buffered(1))
    gate_spec = lambda c: pl.BlockSpec((ROW_TILE, gw), lambda i: (i, c))
    return pl.pallas_call(
        _merge_kernel,
        grid=(n_rows // ROW_TILE,),
        in_specs=[pl.BlockSpec((ROW_TILE, v_w), lambda i: (i, 0)),
                  pl.BlockSpec((ROW_TILE, v_w), lambda i: (i, 0)),
                  pl.BlockSpec((ROW_TILE, v_w), lambda i: (i, col_r // v_w)),
                  pl.BlockSpec((1, GLA_DV), lambda i: (0, 0)),
                  pl.BlockSpec((ROW_TILE, o_na.shape[1]), lambda i: (i, 0)),
                  pl.BlockSpec(w_gla_o.shape, lambda i: (0, 0), **once),
                  pl.BlockSpec(w_na_o.shape, lambda i: (0, 0), **once),
                  gate_spec(c1), gate_spec(c1 + 1), gate_spec(c1 + 2), gate_spec(c1 + 3)],
        out_specs=pl.BlockSpec((ROW_TILE, D_MODEL), lambda i: (i, 0)),
        out_shape=jax.ShapeDtypeStruct((n_rows, D_MODEL), BF16),
        scratch_shapes=[pltpu.VMEM((ROW_TILE, v_w), BF16)],
        compiler_params=_params(("arbitrary",)),
        name="merge",
    )(o_fwd, o_bwd, p, gn, o_na, w_gla_o, w_na_o, p, p, p, p)


def _outproj_kernel(m_ref, w_ref, x_ref, mod_ref, g_ref, o_ref):
    y = jnp.dot(m_ref[...], w_ref[...], preferred_element_type=F32)
    o_ref[...] = x_ref[...] + mod_ref[5:6, :] * _rms(y, g_ref[3:4, :])


def _outproj(m, w_out, h1, mod, norm_g):
    n_tiles = m.shape[0] // ROW_TILE
    tiles_per_batch = n_tiles // 2
    return pl.pallas_call(
        _outproj_kernel,
        grid=(n_tiles,),
        in_specs=[pl.BlockSpec((ROW_TILE, D_MODEL), lambda i: (i, 0)),
                  pl.BlockSpec((D_MODEL, D_MODEL), lambda i: (0, 0)),
                  pl.BlockSpec((ROW_TILE, D_MODEL), lambda i: (i, 0)),
                  pl.BlockSpec((None, N_MOD, D_MODEL), lambda i: (i // tiles_per_batch, 0, 0)),
                  pl.BlockSpec((6, D_MODEL), lambda i: (0, 0))],
        out_specs=pl.BlockSpec((ROW_TILE, D_MODEL), lambda i: (i, 0)),
        out_shape=jax.ShapeDtypeStruct((m.shape[0], D_MODEL), F32),
        compiler_params=_params(("arbitrary",)),
        name="outproj",
    )(m, w_out, h1, mod, norm_g)


def kernel(x, c, ctx, c_ctx, w_ada, b_ada, norm_g, ffn_wg, ffn_wu, ffn_wd, w_in, gla_wg, gla_bg, gla_norm_g,
           w_gla_o, na_rpb, w_na_o, w_out):
    batch, seq, d = x.shape
    ctx_len = ctx.shape[1]
    depth = w_ada.shape[0]
    assert d == D_MODEL and batch == 2 and batch * ctx_len == ROW_TILE and depth == 1
    assert seq % ROW_TILE == 0 and seq // GRID_W == GRID_W

    qk_w = GLA_HEADS * GLA_DK
    v_w = GLA_HEADS * GLA_DV
    na_w = NA_HEADS * NA_DH
    gate0 = 2 * qk_w + 2 * v_w
    gate1 = gate0 + 2 * GLA_GATE_RANK
    col_nq = gate0
    col_m1 = col_nq + 3 * na_w

    cvec = jnp.zeros((8, d), F32).at[0:batch].set(c).at[batch].set(c_ctx)
    cos, sin = _rope_tables(seq)
    h = x.reshape(batch * seq, d)
    hc = ctx.reshape(batch * ctx_len, d)
    lat_tiles_per_batch = seq // BIG_ROW_TILE
    for l in range(depth):
        mod = _modulation(cvec, w_ada[l], b_ada[l].reshape(1, -1)).reshape(8, N_MOD, d)
        g = norm_g[l]
        h1 = _ffn(h, mod, g, ffn_wg[l], ffn_wu[l], ffn_wd[l], 0, 0, tm=BIG_ROW_TILE, tf=FF_TILE,
                  mod_row=lambda i: i // lat_tiles_per_batch)
        h1c = _ffn(hc, mod, g, ffn_wg[l], ffn_wu[l], ffn_wd[l], 0, 0, tm=ROW_TILE, tf=FF_TILE,
                   mod_row=lambda i: batch)
        p, gin = _inproj(h1, h1c, mod, g, jnp.swapaxes(w_in[l], 0, 1), cos, sin, seq, gate0, gate1)
        wg_pad = jnp.zeros((2, LANES, qk_w), F32)
        wg_pad = wg_pad.at[0, :GLA_GATE_RANK].set(gla_wg[l, 0])
        wg_pad = wg_pad.at[1, GLA_GATE_RANK:2 * GLA_GATE_RANK].set(gla_wg[l, 1])
        bg = gla_bg[l].reshape(2, 1, qk_w)
        gn = gla_norm_g[l].reshape(1, GLA_DV)
        o_fwd, o_bwd = _gla(p, gin, wg_pad, bg, batch, seq)
        rpb = jnp.pad(na_rpb[l], ((0, 0), (0, 0), (0, LANES - (2 * NA_KC - 1))))
        o_na = _na(p, rpb, batch, seq, ctx_len, col_nq)
        m = _merge(o_fwd, o_bwd, gn, o_na, w_gla_o[l].astype(BF16), w_na_o[l].astype(BF16), p,
                   2 * qk_w + v_w, col_m1)
        h2 = _outproj(m, w_out[l].astype(BF16), h1, mod, g)
        h = _ffn(h2, mod, g, ffn_wg[l], ffn_wu[l], ffn_wd[l], 1, 2, tm=BIG_ROW_TILE, tf=FF_TILE,
                 mod_row=lambda i: i // lat_tiles_per_batch)
    return h.reshape(batch, seq, d)
```

```python
import functools

import numpy as np
import jax
import jax.numpy as jnp
from jax import lax
from jax.experimental import pallas as pl
from jax.experimental.pallas import tpu as pltpu

F32 = jnp.float32
BF16 = jnp.bfloat16

D_MODEL = 2048
GRID_W = 64
GLA_HEADS = 4
GLA_DK = 256
GLA_DV = 512
GLA_GATE_RANK = 16
GLA_GATE_TEMP = 16.0
NA_HEADS = 16
NA_DH = 64
NA_KR = 8
NA_KC = 16
ROPE_THETA = 10000.0
EPS = 1e-6
NEG_INF = -1e30
N_MOD = 9
LOG2E = 1.4426950408889634

LANES = 128
ROW_TILE = 512
BIG_ROW_TILE = 1024
FF_TILE = 512
PROJ_TILE = 1024
MERGE_STRIP = 512
GLA_BLOCK = 256
GLA_HEADS_PER_STEP = 4
NA_QROWS = 8
NA_SUB_ROWS = 4
NA_WIN_ROWS = 12
RMS_CHUNK = 16
VMEM_LIMIT = 56 * 1024 * 1024


def _params(sem):
    return pltpu.CompilerParams(dimension_semantics=sem, vmem_limit_bytes=VMEM_LIMIT)


def _rms(x, g):
    return x * lax.rsqrt(jnp.mean(x * x, axis=-1, keepdims=True) + EPS) * g


def _sigmoid(x):
    return 0.5 * jnp.tanh(0.5 * x) + 0.5


def _silu(x):
    h = 0.5 * x
    return h * jnp.tanh(h) + h


def _for_row_chunks(n_rows, body):
    def step(j, carry):
        body(pl.ds(pl.multiple_of(j * RMS_CHUNK, RMS_CHUNK), RMS_CHUNK))
        return carry
    lax.fori_loop(0, n_rows // RMS_CHUNK, step, 0, unroll=2)


def _row_inv_rms(load_rows, n_rows, d, part_ref):
    def sum_squares(rows):
        x2 = jnp.square(load_rows(rows))
        parts = [x2[:, g * LANES:(g + 1) * LANES] for g in range(d // LANES)]
        while len(parts) > 1:
            parts = [a + b for a, b in zip(parts[::2], parts[1::2])]
        part_ref[rows, :] = parts[0]

    _for_row_chunks(n_rows, sum_squares)
    r = slice(0, n_rows)
    inv = lax.rsqrt(jnp.sum(part_ref[r, :], axis=-1, keepdims=True) * (1.0 / d) + EPS)
    part_ref[r, :] = jnp.broadcast_to(inv, (n_rows, LANES))


def _spread(v, d):
    return jnp.concatenate([v] * (d // LANES), axis=1)


def _set_row(coef_ref, i, row):
    coef_ref[i] = jnp.broadcast_to(row, coef_ref.shape[1:])


def _mod_kernel(c_ref, w_ref, b_ref, o_ref):
    o_ref[...] = jnp.dot(_silu(c_ref[...]), w_ref[...], preferred_element_type=F32) + b_ref[...]


def _modulation(cvec, w_ada, b_ada):
    n = w_ada.shape[1]
    tn = 1024
    return pl.pallas_call(
        _mod_kernel,
        grid=(n // tn,),
        in_specs=[pl.BlockSpec((8, D_MODEL), lambda j: (0, 0)),
                  pl.BlockSpec((D_MODEL, tn), lambda j: (0, j)),
                  pl.BlockSpec((1, tn), lambda j: (0, j))],
        out_specs=pl.BlockSpec((8, tn), lambda j: (0, j)),
        out_shape=jax.ShapeDtypeStruct((8, n), F32),
        compiler_params=_params(("arbitrary",)),
        name="adaln_mod",
    )(cvec, w_ada, b_ada)


def _mixed_dot(a, w):
    return lax.dot_general(a, w, (((1,), (0,)), ((), ())), preferred_element_type=F32)


def _ffn_kernel(x_ref, mod_ref, g_ref, wg_ref, wu_ref, wd_ref, o_ref, h_ref, part_ref, coef_ref, *, sub):
    f = pl.program_id(1)
    tm, d = x_ref.shape

    @pl.when(f == 0)
    def _():
        _set_row(coef_ref, 0, g_ref[2 * sub:2 * sub + 1, :] * (1.0 + mod_ref[3 * sub + 1:3 * sub + 2, :]))
        _set_row(coef_ref, 1, mod_ref[3 * sub:3 * sub + 1, :])
        _row_inv_rms(lambda rows: x_ref[rows, :], tm, d, part_ref)

        def emit(rows):
            h = x_ref[rows, :] * _spread(part_ref[rows, :], d) * coef_ref[0] + coef_ref[1]
            h_ref[rows, :] = h.astype(BF16)
            o_ref[rows, :] = jnp.zeros((RMS_CHUNK, d), F32)

        _for_row_chunks(tm, emit)

    for r0 in range(0, tm, ROW_TILE):
        rows = slice(r0, r0 + ROW_TILE)
        h = h_ref[rows, :]
        g = _mixed_dot(h, wg_ref[...])
        u = _mixed_dot(h, wu_ref[...])
        o_ref[rows, :] += _mixed_dot((_silu(g) * u).astype(BF16), wd_ref[...])

    @pl.when(f == pl.num_programs(1) - 1)
    def _():
        _set_row(coef_ref, 0, 0.5 * mod_ref[3 * sub + 2:3 * sub + 3, :] * g_ref[2 * sub + 1:2 * sub + 2, :])
        _row_inv_rms(lambda rows: o_ref[rows, :], tm, d, part_ref)

        def emit(rows):
            y = o_ref[rows, :] * _spread(part_ref[rows, :], d) * coef_ref[0]
            o_ref[rows, :] = x_ref[rows, :] + y

        _for_row_chunks(tm, emit)


def _ffn(x2d, mod, norm_g, wg, wu, wd, which, sub, *, tm, tf, mod_row):
    n_tiles = x2d.shape[0] // tm
    d_ff = wg.shape[2]
    once = dict(pipeline_mode=pl.Buffered(1))
    return pl.pallas_call(
        functools.partial(_ffn_kernel, sub=sub),
        grid=(n_tiles, d_ff // tf),
        in_specs=[
            pl.BlockSpec((tm, D_MODEL), lambda i, f: (i, 0), **once),
            pl.BlockSpec((None, N_MOD, D_MODEL), lambda i, f: (mod_row(i), 0, 0)),
            pl.BlockSpec((6, D_MODEL), lambda i, f: (0, 0)),
            pl.BlockSpec((None, D_MODEL, tf), lambda i, f: (which, 0, f)),
            pl.BlockSpec((None, D_MODEL, tf), lambda i, f: (which, 0, f)),
            pl.BlockSpec((None, tf, D_MODEL), lambda i, f: (which, f, 0)),
        ],
        out_specs=pl.BlockSpec((tm, D_MODEL), lambda i, f: (i, 0), **once),
        out_shape=jax.ShapeDtypeStruct(x2d.shape, F32),
        scratch_shapes=[pltpu.VMEM((tm, D_MODEL), BF16), pltpu.VMEM((tm, LANES), F32),
                        pltpu.VMEM((2, RMS_CHUNK, D_MODEL), F32)],
        compiler_params=_params(("arbitrary", "arbitrary")),
        name="ffn%d_%d" % (sub, tm),
    )(x2d, mod, norm_g, wg, wu, wd)


N_ROPE_TILES = 2 * GLA_HEADS * GLA_DK // PROJ_TILE
N_Q_TILES = N_ROPE_TILES // 2


def _rope_tables(seq):
    half = GLA_DK // 4
    freqs = ROPE_THETA ** (-np.arange(half, dtype=np.float64) / half)
    t = np.arange(seq)
    cos_parts, sin_parts = [], []
    for pos in (t // GRID_W, t % GRID_W):
        ang = pos[:, None].astype(np.float64) * freqs
        cos_parts += [np.cos(ang), np.cos(ang)]
        sin_parts += [-np.sin(ang), np.sin(ang)]
    cos = np.concatenate(cos_parts, axis=1)
    sin = np.concatenate(sin_parts, axis=1)
    cos = np.concatenate([cos, np.ones((BIG_ROW_TILE, GLA_DK))], axis=0)
    sin = np.concatenate([sin, np.zeros((BIG_ROW_TILE, GLA_DK))], axis=0)
    return jnp.asarray(cos, F32), jnp.asarray(sin, F32)


def _dot_nt(a, w_t):
    return lax.dot_general(a, w_t, (((1,), (1,)), ((), ())), preferred_element_type=F32)


def _prenorm_kernel(x_ref, xc_ref, mod_ref, g_ref, wgate_ref, h_ref, gate_ref, *, n_lat_tiles):
    def emit(src_ref):
        h = (_rms(src_ref[...], g_ref[2:3, :]) * (1.0 + mod_ref[4:5, :]) + mod_ref[3:4, :]).astype(BF16)
        h_ref[...] = h
        gate = _dot_nt(h, wgate_ref[...])
        gate_ref[...] = jnp.concatenate([gate, jnp.zeros((gate.shape[0], LANES - gate.shape[1]), F32)], axis=1)

    i = pl.program_id(0)
    pl.when(i < n_lat_tiles)(lambda: emit(x_ref))
    pl.when(i >= n_lat_tiles)(lambda: emit(xc_ref))


def _prenorm(h1, h1c, mod, norm_g, w_in_t, gate0, gate1):
    n_lat_tiles = h1.shape[0] // ROW_TILE
    tiles_per_batch = n_lat_tiles // 2
    n_rows = h1.shape[0] + h1c.shape[0]
    n_gate = gate1 - gate0
    return pl.pallas_call(
        functools.partial(_prenorm_kernel, n_lat_tiles=n_lat_tiles),
        grid=(n_rows // ROW_TILE,),
        in_specs=[pl.BlockSpec((ROW_TILE, D_MODEL), lambda i: (jnp.minimum(i, n_lat_tiles - 1), 0)),
                  pl.BlockSpec((ROW_TILE, D_MODEL), lambda i: (0, 0)),
                  pl.BlockSpec((None, N_MOD, D_MODEL), lambda i: (jnp.minimum(i // tiles_per_batch, 2), 0, 0)),
                  pl.BlockSpec((6, D_MODEL), lambda i: (0, 0)),
                  pl.BlockSpec((n_gate, D_MODEL), lambda i: (gate0 // n_gate, 0))],
        out_specs=[pl.BlockSpec((ROW_TILE, D_MODEL), lambda i: (i, 0)),
                   pl.BlockSpec((ROW_TILE, LANES), lambda i: (i, 0))],
        out_shape=[jax.ShapeDtypeStruct((n_rows, D_MODEL), BF16),
                   jax.ShapeDtypeStruct((n_rows, LANES), F32)],
        compiler_params=_params(("arbitrary",)),
        name="prenorm",
    )(h1, h1c, mod, norm_g, w_in_t)


def _inproj_kernel(h_ref, w_ref, cos_ref, sin_ref, p_ref):
    n = pl.program_id(0)

    @pl.when(n < N_ROPE_TILES)
    def _():
        y = _dot_nt(h_ref[...], w_ref[...])
        cos = cos_ref[...]
        sin = sin_ref[...]
        qscale = jnp.where(n < N_Q_TILES, GLA_DK ** -0.5, 1.0).astype(F32)
        for j in range(PROJ_TILE // GLA_DK):
            yj = y[:, j * GLA_DK:(j + 1) * GLA_DK]
            swapped = jnp.concatenate(
                [pltpu.roll(yj[:, g * LANES:(g + 1) * LANES], LANES // 2, 1) for g in range(GLA_DK // LANES)],
                axis=1)
            p_ref[:, j * GLA_DK:(j + 1) * GLA_DK] = ((yj * cos + swapped * sin) * qscale).astype(BF16)

    @pl.when(n >= N_ROPE_TILES)
    def _():
        p_ref[...] = _dot_nt(h_ref[...], w_ref[...]).astype(BF16)


def _inproj(hmod, w_in_t, cos, sin, seq, gate0, gate1):
    tm = BIG_ROW_TILE
    n_rows = hmod.shape[0]
    n_lat_tiles = 2 * seq // tm
    tiles_per_batch = n_lat_tiles // 2
    n_a_tiles = gate0 // PROJ_TILE
    n_b_tiles = (w_in_t.shape[0] - gate1) // PROJ_TILE
    n_gate = gate1 - gate0

    def table_block(n, i):
        block = jnp.where(i < n_lat_tiles, i % tiles_per_batch, seq // tm)
        return jnp.where(n < N_ROPE_TILES, block, 0), 0

    tab_spec = pl.BlockSpec((tm, GLA_DK), table_block)
    return pl.pallas_call(
        _inproj_kernel,
        grid=(n_a_tiles + n_b_tiles, pl.cdiv(n_rows, tm)),
        in_specs=[pl.BlockSpec((tm, D_MODEL), lambda n, i: (i, 0)),
                  pl.BlockSpec((pl.Element(PROJ_TILE), pl.Element(D_MODEL)),
                               lambda n, i: (pl.multiple_of(
                                   n * PROJ_TILE + jnp.where(n < n_a_tiles, 0, n_gate), n_gate), 0)),
                  tab_spec, tab_spec],
        out_specs=pl.BlockSpec((tm, PROJ_TILE), lambda n, i: (i, n)),
        out_shape=jax.ShapeDtypeStruct((n_rows, (n_a_tiles + n_b_tiles) * PROJ_TILE), BF16),
        compiler_params=_params(("arbitrary", "arbitrary")),
        name="inproj",
    )(hmod, w_in_t, cos, sin)


def _log2_sigmoid(z):
    t = z * LOG2E
    return jnp.minimum(t, 0.0) - jnp.log2(1.0 + jnp.exp2(-jnp.abs(t)))


def _gla_kernel(qf_ref, kf_ref, vf_ref, gf_ref, qb_ref, kb_ref, vb_ref, gb_ref, wg_ref, bg_ref,
                of_ref, ob_ref, stf_ref, stb_ref):
    @pl.when(pl.program_id(2) == 0)
    def _():
        stf_ref[...] = jnp.zeros_like(stf_ref)
        stb_ref[...] = jnp.zeros_like(stb_ref)

    c = GLA_BLOCK
    nt = (((1,), (1,)), ((), ()))
    row = lax.broadcasted_iota(jnp.int32, (c, c), 0)
    col = lax.broadcasted_iota(jnp.int32, (c, c), 1)
    mask = {True: col <= row, False: col >= row}
    tri = {fwd: jnp.where(m, 1.0, 0.0).astype(BF16) for fwd, m in mask.items()}
    refs = {True: (qf_ref, kf_ref, vf_ref, gf_ref, stf_ref, of_ref),
            False: (qb_ref, kb_ref, vb_ref, gb_ref, stb_ref, ob_ref)}
    qcols = lambda head: slice(head * GLA_DK, (head + 1) * GLA_DK)
    vcols = lambda head: slice(head * GLA_DV, (head + 1) * GLA_DV)

    state, log_a, cum, factors, att = {}, {}, {}, {}, {}

    def gate_stage(head, fwd):
        d = 0 if fwd else 1
        state[head, fwd] = refs[fwd][4][head]
        z = jnp.dot(refs[fwd][3][...], wg_ref[d, :, qcols(head)], preferred_element_type=F32)
        log_a[head, fwd] = _log2_sigmoid(z + bg_ref[d, :, qcols(head)]) * (1.0 / GLA_GATE_TEMP)

    def cumsum_stage(head, fwd):
        la = log_a[head, fwd]
        hi = la.astype(BF16)
        lo = (la - hi.astype(F32)).astype(BF16)
        cum[head, fwd] = (jnp.dot(tri[fwd], hi, preferred_element_type=F32)
                          + jnp.dot(tri[fwd], lo, preferred_element_type=F32))

    def factor_stage(head, fwd):
        cm = cum[head, fwd]
        cum_end = cm[c - 1:c, :] if fwd else cm[0:1, :]
        cum_mid = cm[c // 2:c // 2 + 1, :]
        q = refs[fwd][0][:, qcols(head)]
        k = refs[fwd][1][:, qcols(head)]
        factors[head, fwd] = (q * jnp.exp2(cm - cum_mid).astype(BF16),
                              k * jnp.exp2(cum_mid - cm).astype(BF16),
                              q * jnp.exp2(cm).astype(BF16),
                              k * jnp.exp2(cum_end - cm).astype(BF16),
                              jnp.exp2(cum_end))

    def att_stage(head, fwd):
        q_att, k_att = factors[head, fwd][:2]
        a = lax.dot_general(q_att, k_att, nt, preferred_element_type=F32)
        att[head, fwd] = jnp.where(mask[fwd], a, 0.0).astype(BF16)

    def out_stage(head, fwd):
        _, _, q_dec, k_end, decay = factors[head, fwd]
        v = refs[fwd][2][:, vcols(head)]
        st = state[head, fwd]
        o = (jnp.dot(att[head, fwd], v, preferred_element_type=F32)
             + lax.dot_general(q_dec, st.astype(BF16), nt, preferred_element_type=F32))
        kv_t = lax.dot_general(v, k_end, (((0,), (0,)), ((), ())), preferred_element_type=F32)
        refs[fwd][5][:, vcols(head)] = o.astype(BF16)
        refs[fwd][4][head] = decay * st + kv_t

    for stage in (gate_stage, cumsum_stage, factor_stage, att_stage, out_stage):
        for head in range(GLA_HEADS_PER_STEP):
            for fwd in (True, False):
                stage(head, fwd)


def _gla(p, gin, wg_pad, bg, batch, seq):
    n_lat = seq // GLA_BLOCK
    ctx_block0 = batch * n_lat
    hps = GLA_HEADS_PER_STEP
    dk, dv = hps * GLA_DK, hps * GLA_DV

    def lat_block(b, s, forward):
        j = jnp.maximum(s - 1, 0)
        return b * n_lat + (j if forward else n_lat - 1 - j)

    def in_block(b, s, forward):
        return jnp.where(s == 0, ctx_block0 + b, lat_block(b, s, forward))

    kq = GLA_HEADS // hps
    kv = 2 * GLA_HEADS * GLA_DK // dv

    def dir_specs(forward):
        return [
            pl.BlockSpec((GLA_BLOCK, dk), lambda b, h, s: (in_block(b, s, forward), h)),
            pl.BlockSpec((GLA_BLOCK, dk), lambda b, h, s: (in_block(b, s, forward), kq + h)),
            pl.BlockSpec((GLA_BLOCK, dv), lambda b, h, s: (in_block(b, s, forward), kv + h)),
            pl.BlockSpec((GLA_BLOCK, LANES), lambda b, h, s: (in_block(b, s, forward), 0)),
        ]

    out_shape = jax.ShapeDtypeStruct((batch * seq, GLA_HEADS * GLA_DV), BF16)
    state = pltpu.VMEM((hps, GLA_DV, GLA_DK), F32)
    return pl.pallas_call(
        _gla_kernel,
        grid=(batch, GLA_HEADS // hps, n_lat + 1),
        in_specs=dir_specs(True) + dir_specs(False) + [
            pl.BlockSpec((2, LANES, dk), lambda b, h, s: (0, 0, h)),
            pl.BlockSpec((2, 1, dk), lambda b, h, s: (0, 0, h))],
        out_specs=[pl.BlockSpec((GLA_BLOCK, dv), lambda b, h, s: (lat_block(b, s, True), h)),
                   pl.BlockSpec((GLA_BLOCK, dv), lambda b, h, s: (lat_block(b, s, False), h))],
        out_shape=[out_shape, out_shape],
        scratch_shapes=[state, state],
        compiler_params=_params(("arbitrary", "arbitrary", "arbitrary")),
        name="gla",
    )(p, p, p, gin, p, p, p, gin, wg_pad, bg)


NA_QTOK = NA_QROWS * GRID_W
NA_SUB_TOK = NA_SUB_ROWS * GRID_W
NA_WIN = NA_WIN_ROWS * GRID_W
NA_SUBS = NA_QROWS // NA_SUB_ROWS
NA_KINDS = ((0, 0), (NA_SUB_ROWS, NA_SUB_ROWS - NA_KR // 2), (GRID_W - NA_SUB_ROWS, GRID_W - NA_WIN_ROWS))


def _na_build_bias(rpb_ref, t_ref, rows):
    n_slots = 2 * NA_KR - 1
    lane = lax.broadcasted_iota(jnp.int32, (GRID_W, LANES), 1)
    q_col = lax.broadcasted_iota(jnp.int32, (GRID_W, LANES), 0)
    k_col = lane & (GRID_W - 1)
    upper = lane >= GRID_W
    c0 = jnp.clip(q_col - NA_KC // 2, 0, GRID_W - NA_KC)
    col_ok = (k_col >= c0) & (k_col < c0 + NA_KC)
    neg = jnp.full((GRID_W, LANES), NEG_INF, F32)

    def toeplitz(hh, slot, lane_off):
        if not 0 <= slot < n_slots:
            return jnp.zeros((GRID_W, LANES), F32)
        base = jnp.broadcast_to(rpb_ref[hh, slot:slot + 1, :], (GRID_W, LANES)) * LOG2E
        return pltpu.roll(base, (LANES - (NA_KC - 1) + lane_off) % LANES, 1, stride=1, stride_axis=0)

    for hh in range(2):
        pairs = {}
        for kind, (r0, w0) in enumerate(NA_KINDS):
            for qr in range(NA_SUB_ROWS):
                band0 = min(max(r0 + qr - NA_KR // 2, 0), rows - NA_KR) - w0
                slot0 = w0 - (r0 + qr) + NA_KR - 1
                for g in range(NA_WIN_ROWS // 2):
                    slot = slot0 + 2 * g
                    lo_ok = band0 <= 2 * g < band0 + NA_KR
                    hi_ok = band0 <= 2 * g + 1 < band0 + NA_KR
                    if lo_ok or hi_ok:
                        if slot not in pairs:
                            pairs[slot] = jnp.where(upper, toeplitz(hh, slot + 1, GRID_W), toeplitz(hh, slot, 0))
                        ok = col_ok if (lo_ok and hi_ok) else (col_ok & upper if hi_ok else col_ok & ~upper)
                        tile = jnp.where(ok, pairs[slot], neg)
                    else:
                        tile = neg
                    t_ref[hh, kind, qr * GRID_W:(qr + 1) * GRID_W, g * LANES:(g + 1) * LANES] = tile


def _na_kernel(q_ref, k_ref, v_ref, kc_ref, vc_ref, rpb_ref, o_ref, t_ref, *, rows):
    b = pl.program_id(1)
    blk = pl.program_id(2)
    n_blk = rows // NA_QROWS

    pl.when((b == 0) & (blk == 0))(lambda: _na_build_bias(rpb_ref, t_ref, rows))

    nt = (((1,), (1,)), ((), ()))
    kc = kc_ref[...]
    vc = vc_ref[...]
    lane = lax.broadcasted_iota(jnp.int32, (NA_SUB_TOK, LANES), 1)
    chains = [(hh, sub) for sub in range(NA_SUBS) for hh in range(2)]
    q_head, kw, vw, kind = {}, [], [], []
    for sub in range(NA_SUBS):
        q = q_ref[sub * NA_SUB_TOK:(sub + 1) * NA_SUB_TOK, :].astype(F32) * (NA_DH ** -0.5 * LOG2E)
        q = q.astype(BF16)
        q_head[0, sub] = jnp.where(lane < NA_DH, q, jnp.zeros_like(q))
        q_head[1, sub] = jnp.where(lane >= NA_DH, q, jnp.zeros_like(q))
        r0 = blk * NA_QROWS + sub * NA_SUB_ROWS
        w0 = jnp.clip(r0 - NA_KR // 2, 0, rows - NA_WIN_ROWS)
        win = pl.ds(pl.multiple_of(w0 * GRID_W, 256), NA_WIN)
        kw.append(k_ref[win, :])
        vw.append(v_ref[win, :])
        kind.append(jnp.where(r0 == 0, 0, jnp.where(r0 == rows - NA_SUB_ROWS, 2, 1)))

    s_lat, s_ctx, p_lat, p_ctx, denom, out = {}, {}, {}, {}, {}, {}
    for hh, sub in chains:
        qs = q_head[hh, sub]
        s_lat[hh, sub] = lax.dot_general(qs, kw[sub], nt, preferred_element_type=F32) + t_ref[hh, kind[sub]]
        s_ctx[hh, sub] = lax.dot_general(qs, kc, nt, preferred_element_type=F32)
    for ch in chains:
        m = jnp.maximum(jnp.max(s_lat[ch], axis=-1, keepdims=True), jnp.max(s_ctx[ch], axis=-1, keepdims=True))
        pl_ = jnp.exp2(s_lat[ch] - m)
        pc_ = jnp.exp2(s_ctx[ch] - m)
        denom[ch] = jnp.sum(pl_, axis=-1, keepdims=True) + jnp.sum(pc_, axis=-1, keepdims=True)
        p_lat[ch] = pl_.astype(BF16)
        p_ctx[ch] = pc_.astype(BF16)
    for hh, sub in chains:
        o = (jnp.dot(p_lat[hh, sub], vw[sub], preferred_element_type=F32)
             + jnp.dot(p_ctx[hh, sub], vc, preferred_element_type=F32))
        out[hh, sub] = o / denom[hh, sub]
    for sub in range(NA_SUBS):
        rows_ = slice(sub * NA_SUB_TOK, (sub + 1) * NA_SUB_TOK)
        o_ref[rows_, :] = jnp.where(lane < NA_DH, out[0, sub], out[1, sub]).astype(o_ref.dtype)


def _na(p, rpb, batch, seq, ctx_len, col0):
    rows = seq // GRID_W
    n_blk = rows // NA_QROWS
    n_pairs = NA_HEADS // 2
    cq = col0 // LANES
    ck = cq + n_pairs
    cv = ck + n_pairs
    ctx_block0 = batch * seq // ctx_len
    return pl.pallas_call(
        functools.partial(_na_kernel, rows=rows),
        grid=(n_pairs, batch, n_blk),
        in_specs=[
            pl.BlockSpec((NA_QTOK, LANES), lambda hp, b, r: (b * n_blk + r, cq + hp)),
            pl.BlockSpec((seq, LANES), lambda hp, b, r: (b, ck + hp)),
            pl.BlockSpec((seq, LANES), lambda hp, b, r: (b, cv + hp)),
            pl.BlockSpec((ctx_len, LANES), lambda hp, b, r: (ctx_block0 + b, ck + hp)),
            pl.BlockSpec((ctx_len, LANES), lambda hp, b, r: (ctx_block0 + b, cv + hp)),
            pl.BlockSpec((2, 2 * NA_KR - 1, LANES), lambda hp, b, r: (hp, 0, 0)),
        ],
        out_specs=pl.BlockSpec((NA_QTOK, LANES), lambda hp, b, r: (b * n_blk + r, hp)),
        out_shape=jax.ShapeDtypeStruct((batch * seq, NA_HEADS * NA_DH), BF16),
        scratch_shapes=[pltpu.VMEM((2, len(NA_KINDS), NA_SUB_TOK, NA_WIN), F32)],
        compiler_params=_params(("arbitrary", "arbitrary", "arbitrary")),
        name="natten",
    )(p, p, p, p, p, rpb)


def _merge_kernel(of_ref, ob_ref, r_ref, gn_ref, b_ref, wa_ref, wb_ref, m1a_ref, m1b_ref, m2a_ref, m2b_ref,
                  o_ref, a_ref):
    for h in range(GLA_HEADS):
        cols = slice(h * GLA_DV, (h + 1) * GLA_DV)
        tot = of_ref[:, cols].astype(F32) + ob_ref[:, cols].astype(F32)
        a_ref[:, cols] = (_rms(tot, gn_ref[...]) * _silu(r_ref[:, cols].astype(F32))).astype(BF16)

    gate_refs = ((m1a_ref, m2a_ref), (m1b_ref, m2b_ref))
    gate_w = m1a_ref.shape[1]
    for c0 in range(0, o_ref.shape[1], MERGE_STRIP):
        cols = slice(c0, c0 + MERGE_STRIP)
        m1_ref, m2_ref = gate_refs[c0 // gate_w]
        gcols = slice(c0 % gate_w, c0 % gate_w + MERGE_STRIP)
        a = jnp.dot(a_ref[...], wa_ref[:, cols], preferred_element_type=F32)
        b = jnp.dot(b_ref[...], wb_ref[:, cols], preferred_element_type=F32)
        m = _sigmoid(m1_ref[:, gcols].astype(F32)) * a + _sigmoid(m2_ref[:, gcols].astype(F32)) * b
        o_ref[:, cols] = m.astype(o_ref.dtype)


def _merge(o_fwd, o_bwd, gn, o_na, w_gla_o, w_na_o, p, col_r, col_m1):
    n_rows = o_fwd.shape[0]
    v_w = o_fwd.shape[1]
    gw = D_MODEL // 2
    c1 = col_m1 // gw
    once = dict(pipeline_mode=pl.Buffered(1))
    gate_spec = lambda c: pl.BlockSpec((ROW_TILE, gw), lambda i: (i, c))
    return pl.pallas_call(
        _merge_kernel,
        grid=(n_rows // ROW_TILE,),
        in_specs=[pl.BlockSpec((ROW_TILE, v_w), lambda i: (i, 0)),
                  pl.BlockSpec((ROW_TILE, v_w), lambda i: (i, 0)),
                  pl.BlockSpec((ROW_TILE, v_w), lambda i: (i, col_r // v_w)),
                  pl.BlockSpec((1, GLA_DV), lambda i: (0, 0)),
                  pl.BlockSpec((ROW_TILE, o_na.shape[1]), lambda i: (i, 0)),
                  pl.BlockSpec(w_gla_o.shape, lambda i: (0, 0), **once),
                  pl.BlockSpec(w_na_o.shape, lambda i: (0, 0), **once),
                  gate_spec(c1), gate_spec(c1 + 1), gate_spec(c1 + 2), gate_spec(c1 + 3)],
        out_specs=pl.BlockSpec((ROW_TILE, D_MODEL), lambda i: (i, 0)),
        out_shape=jax.ShapeDtypeStruct((n_rows, D_MODEL), BF16),
        scratch_shapes=[pltpu.VMEM((ROW_TILE, v_w), BF16)],
        compiler_params=_params(("arbitrary",)),
        name="merge",
    )(o_fwd, o_bwd, p, gn, o_na, w_gla_o, w_na_o, p, p, p, p)


def _outproj_kernel(m_ref, w_ref, x_ref, mod_ref, g_ref, o_ref):
    y = jnp.dot(m_ref[...], w_ref[...], preferred_element_type=F32)
    o_ref[...] = x_ref[...] + mod_ref[5:6, :] * _rms(y, g_ref[3:4, :])


def _outproj(m, w_out, h1, mod, norm_g):
    n_tiles = m.shape[0] // ROW_TILE
    tiles_per_batch = n_tiles // 2
    return pl.pallas_call(
        _outproj_kernel,
        grid=(n_tiles,),
        in_specs=[pl.BlockSpec((ROW_TILE, D_MODEL), lambda i: (i, 0)),
                  pl.BlockSpec((D_MODEL, D_MODEL), lambda i: (0, 0)),
                  pl.BlockSpec((ROW_TILE, D_MODEL), lambda i: (i, 0)),
                  pl.BlockSpec((None, N_MOD, D_MODEL), lambda i: (i // tiles_per_batch, 0, 0)),
                  pl.BlockSpec((6, D_MODEL), lambda i: (0, 0))],
        out_specs=pl.BlockSpec((ROW_TILE, D_MODEL), lambda i: (i, 0)),
        out_shape=jax.ShapeDtypeStruct((m.shape[0], D_MODEL), F32),
        compiler_params=_params(("arbitrary",)),
        name="outproj",
    )(m, w_out, h1, mod, norm_g)


def kernel(x, c, ctx, c_ctx, w_ada, b_ada, norm_g, ffn_wg, ffn_wu, ffn_wd, w_in, gla_wg, gla_bg, gla_norm_g,
           w_gla_o, na_rpb, w_na_o, w_out):
    batch, seq, d = x.shape
    ctx_len = ctx.shape[1]
    depth = w_ada.shape[0]
    assert d == D_MODEL and batch == 2 and batch * ctx_len == ROW_TILE and depth == 1
    assert seq % ROW_TILE == 0 and seq // GRID_W == GRID_W

    qk_w = GLA_HEADS * GLA_DK
    v_w = GLA_HEADS * GLA_DV
    na_w = NA_HEADS * NA_DH
    gate0 = 2 * qk_w + 2 * v_w
    gate1 = gate0 + 2 * GLA_GATE_RANK
    col_nq = gate0
    col_m1 = col_nq + 3 * na_w

    cvec = jnp.zeros((8, d), F32).at[0:batch].set(c).at[batch].set(c_ctx)
    cos, sin = _rope_tables(seq)
    h = x.reshape(batch * seq, d)
    hc = ctx.reshape(batch * ctx_len, d)
    lat_tiles_per_batch = seq // BIG_ROW_TILE
    for l in range(depth):
        mod = _modulation(cvec, w_ada[l], b_ada[l].reshape(1, -1)).reshape(8, N_MOD, d)
        g = norm_g[l]
        h1 = _ffn(h, mod, g, ffn_wg[l], ffn_wu[l], ffn_wd[l], 0, 0, tm=BIG_ROW_TILE, tf=FF_TILE,
                  mod_row=lambda i: i // lat_tiles_per_batch)
        h1c = _ffn(hc, mod, g, ffn_wg[l], ffn_wu[l], ffn_wd[l], 0, 0, tm=ROW_TILE, tf=FF_TILE,
                   mod_row=lambda i: batch)
        w_in_t = jnp.swapaxes(w_in[l], 0, 1)
        hmod, gin = _prenorm(h1, h1c, mod, g, w_in_t, gate0, gate1)
        p = _inproj(hmod, w_in_t, cos, sin, seq, gate0, gate1)
        wg_pad = jnp.zeros((2, LANES, qk_w), F32)
        wg_pad = wg_pad.at[0, :GLA_GATE_RANK].set(gla_wg[l, 0])
        wg_pad = wg_pad.at[1, GLA_GATE_RANK:2 * GLA_GATE_RANK].set(gla_wg[l, 1])
        bg = gla_bg[l].reshape(2, 1, qk_w)
        gn = gla_norm_g[l].reshape(1, GLA_DV)
        o_fwd, o_bwd = _gla(p, gin, wg_pad, bg, batch, seq)
        rpb = jnp.pad(na_rpb[l], ((0, 0), (0, 0), (0, LANES - (2 * NA_KC - 1))))
        o_na = _na(p, rpb, batch, seq, ctx_len, col_nq)
        m = _merge(o_fwd, o_bwd, gn, o_na, w_gla_o[l].astype(BF16), w_na_o[l].astype(BF16), p,
                   2 * qk_w + v_w, col_m1)
        h2 = _outproj(m, w_out[l].astype(BF16), h1, mod, g)
        h = _ffn(h2, mod, g, ffn_wg[l], ffn_wu[l], ffn_wd[l], 1, 2, tm=BIG_ROW_TILE, tf=FF_TILE,
                 mod_row=lambda i: i // lat_tiles_per_batch)
    return h.reshape(batch, seq, d)
```

```python
import functools

import numpy as np
import jax
import jax.numpy as jnp
from jax import lax
from jax.experimental import pallas as pl
from jax.experimental.pallas import tpu as pltpu

F32 = jnp.float32
BF16 = jnp.bfloat16

D_MODEL = 2048
GRID_W = 64
GLA_HEADS = 4
GLA_DK = 256
GLA_DV = 512
GLA_GATE_RANK = 16
GLA_GATE_TEMP = 16.0
NA_HEADS = 16
NA_DH = 64
NA_KR = 8
NA_KC = 16
ROPE_THETA = 10000.0
EPS = 1e-6
NEG_INF = -1e30
N_MOD = 9
LOG2E = 1.4426950408889634

LANES = 128
ROW_TILE = 512
BIG_ROW_TILE = 1024
FF_TILE = 512
PROJ_TILE = 1024
MERGE_STRIP = 512
GLA_BLOCK = 256
GLA_HEADS_PER_STEP = 4
NA_QROWS = 8
NA_SUB_ROWS = 4
NA_WIN_ROWS = 12
VMEM_LIMIT = 56 * 1024 * 1024


def _params(sem):
    return pltpu.CompilerParams(dimension_semantics=sem, vmem_limit_bytes=VMEM_LIMIT)


def _rms(x, g):
    return x * lax.rsqrt(jnp.mean(x * x, axis=-1, keepdims=True) + EPS) * g


def _sigmoid(x):
    return 0.5 * jnp.tanh(0.5 * x) + 0.5


def _silu(x):
    h = 0.5 * x
    return h * jnp.tanh(h) + h


def _mod_kernel(c_ref, w_ref, b_ref, o_ref):
    o_ref[...] = jnp.dot(_silu(c_ref[...]), w_ref[...], preferred_element_type=F32) + b_ref[...]


def _modulation(cvec, w_ada, b_ada):
    n = w_ada.shape[1]
    tn = 1024
    return pl.pallas_call(
        _mod_kernel,
        grid=(n // tn,),
        in_specs=[pl.BlockSpec((8, D_MODEL), lambda j: (0, 0)),
                  pl.BlockSpec((D_MODEL, tn), lambda j: (0, j)),
                  pl.BlockSpec((1, tn), lambda j: (0, j))],
        out_specs=pl.BlockSpec((8, tn), lambda j: (0, j)),
        out_shape=jax.ShapeDtypeStruct((8, n), F32),
        compiler_params=_params(("arbitrary",)),
        name="adaln_mod",
    )(cvec, w_ada, b_ada)


def _mixed_dot(a, w):
    return lax.dot_general(a, w, (((1,), (0,)), ((), ())), preferred_element_type=F32)


def _ffn_kernel(x_ref, mod_ref, g_ref, wg_ref, wu_ref, wd_ref, o_ref, h_ref, *, sub):
    f = pl.program_id(1)
    last = pl.num_programs(1) - 1
    subtiles = [slice(r0, r0 + ROW_TILE) for r0 in range(0, x_ref.shape[0], ROW_TILE)]

    def swiglu(h):
        g = _mixed_dot(h, wg_ref[...])
        u = _mixed_dot(h, wu_ref[...])
        return _mixed_dot((_silu(g) * u).astype(BF16), wd_ref[...])

    @pl.when(f == 0)
    def _():
        gain = g_ref[2 * sub:2 * sub + 1, :] * (1.0 + mod_ref[3 * sub + 1:3 * sub + 2, :])
        shift = mod_ref[3 * sub:3 * sub + 1, :]
        for rows in subtiles:
            h = (_rms(x_ref[rows, :], gain) + shift).astype(BF16)
            h_ref[rows, :] = h
            o_ref[rows, :] = swiglu(h)

    @pl.when((f > 0) & (f < last))
    def _():
        for rows in subtiles:
            o_ref[rows, :] += swiglu(h_ref[rows, :])

    @pl.when(f == last)
    def _():
        gain = 0.5 * mod_ref[3 * sub + 2:3 * sub + 3, :] * g_ref[2 * sub + 1:2 * sub + 2, :]
        for rows in subtiles:
            y = o_ref[rows, :] + swiglu(h_ref[rows, :])
            o_ref[rows, :] = x_ref[rows, :] + _rms(y, gain)


def _ffn(x2d, mod, norm_g, wg, wu, wd, which, sub, *, tm, tf, mod_row):
    n_tiles = x2d.shape[0] // tm
    d_ff = wg.shape[2]
    once = dict(pipeline_mode=pl.Buffered(1))
    return pl.pallas_call(
        functools.partial(_ffn_kernel, sub=sub),
        grid=(n_tiles, d_ff // tf),
        in_specs=[
            pl.BlockSpec((tm, D_MODEL), lambda i, f: (i, 0), **once),
            pl.BlockSpec((None, N_MOD, D_MODEL), lambda i, f: (mod_row(i), 0, 0)),
            pl.BlockSpec((6, D_MODEL), lambda i, f: (0, 0)),
            pl.BlockSpec((None, D_MODEL, tf), lambda i, f: (which, 0, f)),
            pl.BlockSpec((None, D_MODEL, tf), lambda i, f: (which, 0, f)),
            pl.BlockSpec((None, tf, D_MODEL), lambda i, f: (which, f, 0)),
        ],
        out_specs=pl.BlockSpec((tm, D_MODEL), lambda i, f: (i, 0), **once),
        out_shape=jax.ShapeDtypeStruct(x2d.shape, F32),
        scratch_shapes=[pltpu.VMEM((tm, D_MODEL), BF16)],
        compiler_params=_params(("arbitrary", "arbitrary")),
        name="ffn%d_%d" % (sub, tm),
    )(x2d, mod, norm_g, wg, wu, wd)


N_ROPE_TILES = 2 * GLA_HEADS * GLA_DK // PROJ_TILE
N_Q_TILES = N_ROPE_TILES // 2


def _rope_tables(seq):
    half = GLA_DK // 4
    freqs = ROPE_THETA ** (-np.arange(half, dtype=np.float64) / half)
    t = np.arange(seq)
    cos_parts, sin_parts = [], []
    for pos in (t // GRID_W, t % GRID_W):
        ang = pos[:, None].astype(np.float64) * freqs
        cos_parts += [np.cos(ang), np.cos(ang)]
        sin_parts += [-np.sin(ang), np.sin(ang)]
    cos = np.concatenate(cos_parts, axis=1)
    sin = np.concatenate(sin_parts, axis=1)
    cos = np.concatenate([cos, np.ones((BIG_ROW_TILE, GLA_DK))], axis=0)
    sin = np.concatenate([sin, np.zeros((BIG_ROW_TILE, GLA_DK))], axis=0)
    return jnp.asarray(cos, F32), jnp.asarray(sin, F32)


def _dot_nt(a, w_t):
    return lax.dot_general(a, w_t, (((1,), (1,)), ((), ())), preferred_element_type=F32)


def _inproj_kernel(x_ref, xc_ref, mod_ref, g_ref, w_ref, wgate_ref, cos_ref, sin_ref, p_ref, gate_ref, h_ref,
                   *, n_lat_tiles):
    i = pl.program_id(0)
    n = pl.program_id(1)

    def project(src_ref):
        r = slice(0, src_ref.shape[0])

        @pl.when(n == 0)
        def _():
            h = _rms(src_ref[...], g_ref[2:3, :]) * (1.0 + mod_ref[4:5, :]) + mod_ref[3:4, :]
            h_ref[r, :] = h.astype(BF16)
            gate = _dot_nt(h_ref[r, :], wgate_ref[...])
            gate_ref[r, :] = jnp.concatenate(
                [gate, jnp.zeros((gate.shape[0], LANES - gate.shape[1]), F32)], axis=1)

        @pl.when(n < N_ROPE_TILES)
        def _():
            y = _dot_nt(h_ref[r, :], w_ref[...])
            cos = cos_ref[r, :]
            sin = sin_ref[r, :]
            qscale = jnp.where(n < N_Q_TILES, GLA_DK ** -0.5, 1.0).astype(F32)
            for j in range(PROJ_TILE // GLA_DK):
                yj = y[:, j * GLA_DK:(j + 1) * GLA_DK]
                swapped = jnp.concatenate(
                    [pltpu.roll(yj[:, g * LANES:(g + 1) * LANES], LANES // 2, 1) for g in range(GLA_DK // LANES)],
                    axis=1)
                p_ref[r, j * GLA_DK:(j + 1) * GLA_DK] = ((yj * cos + swapped * sin) * qscale).astype(BF16)

        @pl.when(n >= N_ROPE_TILES)
        def _():
            p_ref[r, :] = _dot_nt(h_ref[r, :], w_ref[...]).astype(BF16)

    pl.when(i < n_lat_tiles)(lambda: project(x_ref))
    pl.when(i >= n_lat_tiles)(lambda: project(xc_ref))


def _inproj(h1, h1c, mod, norm_g, w_in_t, cos, sin, seq, gate0, gate1):
    tm = BIG_ROW_TILE
    n_lat_tiles = h1.shape[0] // tm
    n_rows = h1.shape[0] + h1c.shape[0]
    tiles_per_batch = n_lat_tiles // 2
    n_a_tiles = gate0 // PROJ_TILE
    n_b_tiles = (w_in_t.shape[0] - gate1) // PROJ_TILE
    n_gate = gate1 - gate0
    tab_spec = pl.BlockSpec(
        (tm, GLA_DK), lambda i, n: (jnp.where(i < n_lat_tiles, i % tiles_per_batch, seq // tm), 0))
    return pl.pallas_call(
        functools.partial(_inproj_kernel, n_lat_tiles=n_lat_tiles),
        grid=(n_lat_tiles + 1, n_a_tiles + n_b_tiles),
        in_specs=[pl.BlockSpec((tm, D_MODEL), lambda i, n: (jnp.minimum(i, n_lat_tiles - 1), 0)),
                  pl.BlockSpec(h1c.shape, lambda i, n: (0, 0)),
                  pl.BlockSpec((None, N_MOD, D_MODEL), lambda i, n: (jnp.minimum(i // tiles_per_batch, 2), 0, 0)),
                  pl.BlockSpec((6, D_MODEL), lambda i, n: (0, 0)),
                  pl.BlockSpec((pl.Element(PROJ_TILE), pl.Element(D_MODEL)),
                               lambda i, n: (pl.multiple_of(
                                   n * PROJ_TILE + jnp.where(n < n_a_tiles, 0, n_gate), n_gate), 0)),
                  pl.BlockSpec((n_gate, D_MODEL), lambda i, n: (gate0 // n_gate, 0)),
                  tab_spec, tab_spec],
        out_specs=[pl.BlockSpec((tm, PROJ_TILE), lambda i, n: (i, n)),
                   pl.BlockSpec((tm, LANES), lambda i, n: (i, 0))],
        out_shape=[jax.ShapeDtypeStruct((n_rows, (n_a_tiles + n_b_tiles) * PROJ_TILE), BF16),
                   jax.ShapeDtypeStruct((n_rows, LANES), F32)],
        scratch_shapes=[pltpu.VMEM((tm, D_MODEL), BF16)],
        compiler_params=_params(("arbitrary", "arbitrary")),
        name="inproj",
    )(h1, h1c, mod, norm_g, w_in_t, w_in_t, cos, sin)


def _log2_sigmoid(z):
    t = z * LOG2E
    return jnp.minimum(t, 0.0) - jnp.log2(1.0 + jnp.exp2(-jnp.abs(t)))


def _gla_kernel(qf_ref, kf_ref, vf_ref, gf_ref, qb_ref, kb_ref, vb_ref, gb_ref, wg_ref, bg_ref,
                of_ref, ob_ref, stf_ref, stb_ref):
    @pl.when(pl.program_id(2) == 0)
    def _():
        stf_ref[...] = jnp.zeros_like(stf_ref)
        stb_ref[...] = jnp.zeros_like(stb_ref)

    c = GLA_BLOCK
    nt = (((1,), (1,)), ((), ()))
    row = lax.broadcasted_iota(jnp.int32, (c, c), 0)
    col = lax.broadcasted_iota(jnp.int32, (c, c), 1)
    mask = {True: col <= row, False: col >= row}
    tri = {fwd: jnp.where(m, 1.0, 0.0).astype(BF16) for fwd, m in mask.items()}
    refs = {True: (qf_ref, kf_ref, vf_ref, gf_ref, stf_ref, of_ref),
            False: (qb_ref, kb_ref, vb_ref, gb_ref, stb_ref, ob_ref)}
    qcols = lambda head: slice(head * GLA_DK, (head + 1) * GLA_DK)
    vcols = lambda head: slice(head * GLA_DV, (head + 1) * GLA_DV)

    state, log_a, cum, factors, att = {}, {}, {}, {}, {}

    def gate_stage(head, fwd):
        d = 0 if fwd else 1
        state[head, fwd] = refs[fwd][4][head]
        z = jnp.dot(refs[fwd][3][...], wg_ref[d, :, qcols(head)], preferred_element_type=F32)
        log_a[head, fwd] = _log2_sigmoid(z + bg_ref[d, :, qcols(head)]) * (1.0 / GLA_GATE_TEMP)

    def cumsum_stage(head, fwd):
        la = log_a[head, fwd]
        hi = la.astype(BF16)
        lo = (la - hi.astype(F32)).astype(BF16)
        cum[head, fwd] = (jnp.dot(tri[fwd], hi, preferred_element_type=F32)
                          + jnp.dot(tri[fwd], lo, preferred_element_type=F32))

    def factor_stage(head, fwd):
        cm = cum[head, fwd]
        cum_end = cm[c - 1:c, :] if fwd else cm[0:1, :]
        cum_mid = cm[c // 2:c // 2 + 1, :]
        q = refs[fwd][0][:, qcols(head)]
        k = refs[fwd][1][:, qcols(head)]
        factors[head, fwd] = (q * jnp.exp2(cm - cum_mid).astype(BF16),
                              k * jnp.exp2(cum_mid - cm).astype(BF16),
                              q * jnp.exp2(cm).astype(BF16),
                              k * jnp.exp2(cum_end - cm).astype(BF16),
                              jnp.exp2(cum_end))

    def att_stage(head, fwd):
        q_att, k_att = factors[head, fwd][:2]
        a = lax.dot_general(q_att, k_att, nt, preferred_element_type=F32)
        att[head, fwd] = jnp.where(mask[fwd], a, 0.0).astype(BF16)

    def out_stage(head, fwd):
        _, _, q_dec, k_end, decay = factors[head, fwd]
        v = refs[fwd][2][:, vcols(head)]
        st = state[head, fwd]
        o = (jnp.dot(att[head, fwd], v, preferred_element_type=F32)
             + lax.dot_general(q_dec, st.astype(BF16), nt, preferred_element_type=F32))
        kv_t = lax.dot_general(v, k_end, (((0,), (0,)), ((), ())), preferred_element_type=F32)
        refs[fwd][5][:, vcols(head)] = o.astype(BF16)
        refs[fwd][4][head] = decay * st + kv_t

    for stage in (gate_stage, cumsum_stage, factor_stage, att_stage, out_stage):
        for head in range(GLA_HEADS_PER_STEP):
            for fwd in (True, False):
                stage(head, fwd)


def _gla(p, gin, wg_pad, bg, batch, seq):
    n_lat = seq // GLA_BLOCK
    ctx_block0 = batch * n_lat
    hps = GLA_HEADS_PER_STEP
    dk, dv = hps * GLA_DK, hps * GLA_DV

    def lat_block(b, s, forward):
        j = jnp.maximum(s - 1, 0)
        return b * n_lat + (j if forward else n_lat - 1 - j)

    def in_block(b, s, forward):
        return jnp.where(s == 0, ctx_block0 + b, lat_block(b, s, forward))

    kq = GLA_HEADS // hps
    kv = 2 * GLA_HEADS * GLA_DK // dv

    def dir_specs(forward):
        return [
            pl.BlockSpec((GLA_BLOCK, dk), lambda b, h, s: (in_block(b, s, forward), h)),
            pl.BlockSpec((GLA_BLOCK, dk), lambda b, h, s: (in_block(b, s, forward), kq + h)),
            pl.BlockSpec((GLA_BLOCK, dv), lambda b, h, s: (in_block(b, s, forward), kv + h)),
            pl.BlockSpec((GLA_BLOCK, LANES), lambda b, h, s: (in_block(b, s, forward), 0)),
        ]

    out_shape = jax.ShapeDtypeStruct((batch * seq, GLA_HEADS * GLA_DV), BF16)
    state = pltpu.VMEM((hps, GLA_DV, GLA_DK), F32)
    return pl.pallas_call(
        _gla_kernel,
        grid=(batch, GLA_HEADS // hps, n_lat + 1),
        in_specs=dir_specs(True) + dir_specs(False) + [
            pl.BlockSpec((2, LANES, dk), lambda b, h, s: (0, 0, h)),
            pl.BlockSpec((2, 1, dk), lambda b, h, s: (0, 0, h))],
        out_specs=[pl.BlockSpec((GLA_BLOCK, dv), lambda b, h, s: (lat_block(b, s, True), h)),
                   pl.BlockSpec((GLA_BLOCK, dv), lambda b, h, s: (lat_block(b, s, False), h))],
        out_shape=[out_shape, out_shape],
        scratch_shapes=[state, state],
        compiler_params=_params(("arbitrary", "arbitrary", "arbitrary")),
        name="gla",
    )(p, p, p, gin, p, p, p, gin, wg_pad, bg)


NA_QTOK = NA_QROWS * GRID_W
NA_SUB_TOK = NA_SUB_ROWS * GRID_W
NA_WIN = NA_WIN_ROWS * GRID_W
NA_SUBS = NA_QROWS // NA_SUB_ROWS
NA_KINDS = ((0, 0), (NA_SUB_ROWS, NA_SUB_ROWS - NA_KR // 2), (GRID_W - NA_SUB_ROWS, GRID_W - NA_WIN_ROWS))


def _na_build_bias(rpb_ref, t_ref, rows):
    n_slots = 2 * NA_KR - 1
    lane = lax.broadcasted_iota(jnp.int32, (GRID_W, LANES), 1)
    q_col = lax.broadcasted_iota(jnp.int32, (GRID_W, LANES), 0)
    k_col = lane & (GRID_W - 1)
    upper = lane >= GRID_W
    c0 = jnp.clip(q_col - NA_KC // 2, 0, GRID_W - NA_KC)
    col_ok = (k_col >= c0) & (k_col < c0 + NA_KC)
    neg = jnp.full((GRID_W, LANES), NEG_INF, F32)

    def toeplitz(hh, slot, lane_off):
        if not 0 <= slot < n_slots:
            return jnp.zeros((GRID_W, LANES), F32)
        base = jnp.broadcast_to(rpb_ref[hh, slot:slot + 1, :], (GRID_W, LANES)) * LOG2E
        return pltpu.roll(base, (LANES - (NA_KC - 1) + lane_off) % LANES, 1, stride=1, stride_axis=0)

    for hh in range(2):
        pairs = {}
        for kind, (r0, w0) in enumerate(NA_KINDS):
            for qr in range(NA_SUB_ROWS):
                band0 = min(max(r0 + qr - NA_KR // 2, 0), rows - NA_KR) - w0
                slot0 = w0 - (r0 + qr) + NA_KR - 1
                for g in range(NA_WIN_ROWS // 2):
                    slot = slot0 + 2 * g
                    lo_ok = band0 <= 2 * g < band0 + NA_KR
                    hi_ok = band0 <= 2 * g + 1 < band0 + NA_KR
                    if lo_ok or hi_ok:
                        if slot not in pairs:
                            pairs[slot] = jnp.where(upper, toeplitz(hh, slot + 1, GRID_W), toeplitz(hh, slot, 0))
                        ok = col_ok if (lo_ok and hi_ok) else (col_ok & upper if hi_ok else col_ok & ~upper)
                        tile = jnp.where(ok, pairs[slot], neg)
                    else:
                        tile = neg
                    t_ref[hh, kind, qr * GRID_W:(qr + 1) * GRID_W, g * LANES:(g + 1) * LANES] = tile


def _na_kernel(q_ref, k_ref, v_ref, kc_ref, vc_ref, rpb_ref, o_ref, t_ref, *, rows):
    b = pl.program_id(1)
    blk = pl.program_id(2)
    n_blk = rows // NA_QROWS

    pl.when((b == 0) & (blk == 0))(lambda: _na_build_bias(rpb_ref, t_ref, rows))

    nt = (((1,), (1,)), ((), ()))
    kc = kc_ref[...]
    vc = vc_ref[...]
    lane = lax.broadcasted_iota(jnp.int32, (NA_SUB_TOK, LANES), 1)
    chains = [(hh, sub) for sub in range(NA_SUBS) for hh in range(2)]
    q_head, kw, vw, kind = {}, [], [], []
    for sub in range(NA_SUBS):
        q = q_ref[sub * NA_SUB_TOK:(sub + 1) * NA_SUB_TOK, :].astype(F32) * (NA_DH ** -0.5 * LOG2E)
        q = q.astype(BF16)
        q_head[0, sub] = jnp.where(lane < NA_DH, q, jnp.zeros_like(q))
        q_head[1, sub] = jnp.where(lane >= NA_DH, q, jnp.zeros_like(q))
        r0 = blk * NA_QROWS + sub * NA_SUB_ROWS
        w0 = jnp.clip(r0 - NA_KR // 2, 0, rows - NA_WIN_ROWS)
        win = pl.ds(pl.multiple_of(w0 * GRID_W, 256), NA_WIN)
        kw.append(k_ref[win, :])
        vw.append(v_ref[win, :])
        kind.append(jnp.where(r0 == 0, 0, jnp.where(r0 == rows - NA_SUB_ROWS, 2, 1)))

    s_lat, s_ctx, p_lat, p_ctx, denom, out = {}, {}, {}, {}, {}, {}
    for hh, sub in chains:
        qs = q_head[hh, sub]
        s_lat[hh, sub] = lax.dot_general(qs, kw[sub], nt, preferred_element_type=F32) + t_ref[hh, kind[sub]]
        s_ctx[hh, sub] = lax.dot_general(qs, kc, nt, preferred_element_type=F32)
    for ch in chains:
        m = jnp.maximum(jnp.max(s_lat[ch], axis=-1, keepdims=True), jnp.max(s_ctx[ch], axis=-1, keepdims=True))
        pl_ = jnp.exp2(s_lat[ch] - m)
        pc_ = jnp.exp2(s_ctx[ch] - m)
        denom[ch] = jnp.sum(pl_, axis=-1, keepdims=True) + jnp.sum(pc_, axis=-1, keepdims=True)
        p_lat[ch] = pl_.astype(BF16)
        p_ctx[ch] = pc_.astype(BF16)
    for hh, sub in chains:
        o = (jnp.dot(p_lat[hh, sub], vw[sub], preferred_element_type=F32)
             + jnp.dot(p_ctx[hh, sub], vc, preferred_element_type=F32))
        out[hh, sub] = o / denom[hh, sub]
    for sub in range(NA_SUBS):
        rows_ = slice(sub * NA_SUB_TOK, (sub + 1) * NA_SUB_TOK)
        o_ref[rows_, :] = jnp.where(lane < NA_DH, out[0, sub], out[1, sub]).astype(o_ref.dtype)


def _na(p, rpb, batch, seq, ctx_len, col0):
    rows = seq // GRID_W
    n_blk = rows // NA_QROWS
    n_pairs = NA_HEADS // 2
    cq = col0 // LANES
    ck = cq + n_pairs
    cv = ck + n_pairs
    ctx_block0 = batch * seq // ctx_len
    return pl.pallas_call(
        functools.partial(_na_kernel, rows=rows),
        grid=(n_pairs, batch, n_blk),
        in_specs=[
            pl.BlockSpec((NA_QTOK, LANES), lambda hp, b, r: (b * n_blk + r, cq + hp)),
            pl.BlockSpec((seq, LANES), lambda hp, b, r: (b, ck + hp)),
            pl.BlockSpec((seq, LANES), lambda hp, b, r: (b, cv + hp)),
            pl.BlockSpec((ctx_len, LANES), lambda hp, b, r: (ctx_block0 + b, ck + hp)),
            pl.BlockSpec((ctx_len, LANES), lambda hp, b, r: (ctx_block0 + b, cv + hp)),
            pl.BlockSpec((2, 2 * NA_KR - 1, LANES), lambda hp, b, r: (hp, 0, 0)),
        ],
        out_specs=pl.BlockSpec((NA_QTOK, LANES), lambda hp, b, r: (b * n_blk + r, hp)),
        out_shape=jax.ShapeDtypeStruct((batch * seq, NA_HEADS * NA_DH), BF16),
        scratch_shapes=[pltpu.VMEM((2, len(NA_KINDS), NA_SUB_TOK, NA_WIN), F32)],
        compiler_params=_params(("arbitrary", "arbitrary", "arbitrary")),
        name="natten",
    )(p, p, p, p, p, rpb)


def _merge_kernel(of_ref, ob_ref, r_ref, gn_ref, b_ref, wa_ref, wb_ref, m1a_ref, m1b_ref, m2a_ref, m2b_ref,
                  o_ref, a_ref):
    for h in range(GLA_HEADS):
        cols = slice(h * GLA_DV, (h + 1) * GLA_DV)
        tot = of_ref[:, cols].astype(F32) + ob_ref[:, cols].astype(F32)
        a_ref[:, cols] = (_rms(tot, gn_ref[...]) * _silu(r_ref[:, cols].astype(F32))).astype(BF16)

    gate_refs = ((m1a_ref, m2a_ref), (m1b_ref, m2b_ref))
    gate_w = m1a_ref.shape[1]
    for c0 in range(0, o_ref.shape[1], MERGE_STRIP):
        cols = slice(c0, c0 + MERGE_STRIP)
        m1_ref, m2_ref = gate_refs[c0 // gate_w]
        gcols = slice(c0 % gate_w, c0 % gate_w + MERGE_STRIP)
        a = jnp.dot(a_ref[...], wa_ref[:, cols], preferred_element_type=F32)
        b = jnp.dot(b_ref[...], wb_ref[:, cols], preferred_element_type=F32)
        m = _sigmoid(m1_ref[:, gcols].astype(F32)) * a + _sigmoid(m2_ref[:, gcols].astype(F32)) * b
        o_ref[:, cols] = m.astype(o_ref.dtype)


def _merge(o_fwd, o_bwd, gn, o_na, w_gla_o, w_na_o, p, col_r, col_m1):
    n_rows = o_fwd.shape[0]
    v_w = o_fwd.shape[1]
    gw = D_MODEL // 2
    c1 = col_m1 // gw
    once = dict(pipeline_mode=pl.Buffered(1))
    gate_spec = lambda c: pl.BlockSpec((ROW_TILE, gw), lambda i: (i, c))
    return pl.pallas_call(
        _merge_kernel,
        grid=(n_rows // ROW_TILE,),
        in_specs=[pl.BlockSpec((ROW_TILE, v_w), lambda i: (i, 0)),
                  pl.BlockSpec((ROW_TILE, v_w), lambda i: (i, 0)),
                  pl.BlockSpec((ROW_TILE, v_w), lambda i: (i, col_r // v_w)),
                  pl.BlockSpec((1, GLA_DV), lambda i: (0, 0)),
                  pl.BlockSpec((ROW_TILE, o_na.shape[1]), lambda i: (i, 0)),
                  pl.BlockSpec(w_gla_o.shape, lambda i: (0, 0), **once),
                  pl.BlockSpec(w_na_o.shape, lambda i: (0, 0), **once),
                  gate_spec(c1), gate_spec(c1 + 1), gate_spec(c1 + 2), gate_spec(c1 + 3)],
        out_specs=pl.BlockSpec((ROW_TILE, D_MODEL), lambda i: (i, 0)),
        out_shape=jax.ShapeDtypeStruct((n_rows, D_MODEL), BF16),
        scratch_shapes=[pltpu.VMEM((ROW_TILE, v_w), BF16)],
        compiler_params=_params(("arbitrary",)),
        name="merge",
    )(o_fwd, o_bwd, p, gn, o_na, w_gla_o, w_na_o, p, p, p, p)


def _outproj_kernel(m_ref, w_ref, x_ref, mod_ref, g_ref, o_ref):
    y = _mixed_dot(m_ref[...], w_ref[...])
    o_ref[...] = x_ref[...] + mod_ref[5:6, :] * _rms(y, g_ref[3:4, :])


def _outproj(m, w_out, h1, mod, norm_g):
    n_tiles = m.shape[0] // ROW_TILE
    tiles_per_batch = n_tiles // 2
    return pl.pallas_call(
        _outproj_kernel,
        grid=(n_tiles,),
        in_specs=[pl.BlockSpec((ROW_TILE, D_MODEL), lambda i: (i, 0)),
                  pl.BlockSpec((D_MODEL, D_MODEL), lambda i: (0, 0), pipeline_mode=pl.Buffered(1)),
                  pl.BlockSpec((ROW_TILE, D_MODEL), lambda i: (i, 0)),
                  pl.BlockSpec((None, N_MOD, D_MODEL), lambda i: (i // tiles_per_batch, 0, 0)),
                  pl.BlockSpec((6, D_MODEL), lambda i: (0, 0))],
        out_specs=pl.BlockSpec((ROW_TILE, D_MODEL), lambda i: (i, 0)),
        out_shape=jax.ShapeDtypeStruct((m.shape[0], D_MODEL), F32),
        compiler_params=_params(("arbitrary",)),
        name="outproj",
    )(m, w_out, h1, mod, norm_g)


def kernel(x, c, ctx, c_ctx, w_ada, b_ada, norm_g, ffn_wg, ffn_wu, ffn_wd, w_in, gla_wg, gla_bg, gla_norm_g,
           w_gla_o, na_rpb, w_na_o, w_out):
    batch, seq, d = x.shape
    ctx_len = ctx.shape[1]
    depth = w_ada.shape[0]
    assert d == D_MODEL and batch == 2 and batch * ctx_len == ROW_TILE and depth == 1
    assert seq % ROW_TILE == 0 and seq // GRID_W == GRID_W

    qk_w = GLA_HEADS * GLA_DK
    v_w = GLA_HEADS * GLA_DV
    na_w = NA_HEADS * NA_DH
    gate0 = 2 * qk_w + 2 * v_w
    gate1 = gate0 + 2 * GLA_GATE_RANK
    col_nq = gate0
    col_m1 = col_nq + 3 * na_w

    cvec = jnp.zeros((8, d), F32).at[0:batch].set(c).at[batch].set(c_ctx)
    cos, sin = _rope_tables(seq)
    h = x.reshape(batch * seq, d)
    hc = ctx.reshape(batch * ctx_len, d)
    lat_tiles_per_batch = seq // BIG_ROW_TILE
    for l in range(depth):
        mod = _modulation(cvec, w_ada[l], b_ada[l].reshape(1, -1)).reshape(8, N_MOD, d)
        g = norm_g[l]
        h1 = _ffn(h, mod, g, ffn_wg[l], ffn_wu[l], ffn_wd[l], 0, 0, tm=BIG_ROW_TILE, tf=FF_TILE,
                  mod_row=lambda i: i // lat_tiles_per_batch)
        h1c = _ffn(hc, mod, g, ffn_wg[l], ffn_wu[l], ffn_wd[l], 0, 0, tm=ROW_TILE, tf=FF_TILE,
                   mod_row=lambda i: batch)
        p, gin = _inproj(h1, h1c, mod, g, jnp.swapaxes(w_in[l], 0, 1), cos, sin, seq, gate0, gate1)
        wg_pad = jnp.zeros((2, LANES, qk_w), F32)
        wg_pad = wg_pad.at[0, :GLA_GATE_RANK].set(gla_wg[l, 0])
        wg_pad = wg_pad.at[1, GLA_GATE_RANK:2 * GLA_GATE_RANK].set(gla_wg[l, 1])
        bg = gla_bg[l].reshape(2, 1, qk_w)
        gn = gla_norm_g[l].reshape(1, GLA_DV)
        o_fwd, o_bwd = _gla(p, gin, wg_pad, bg, batch, seq)
        rpb = jnp.pad(na_rpb[l], ((0, 0), (0, 0), (0, LANES - (2 * NA_KC - 1))))
        o_na = _na(p, rpb, batch, seq, ctx_len, col_nq)
        m = _merge(o_fwd, o_bwd, gn, o_na, w_gla_o[l].astype(BF16), w_na_o[l].astype(BF16), p,
                   2 * qk_w + v_w, col_m1)
        h2 = _outproj(m, w_out[l], h1, mod, g)
        h = _ffn(h2, mod, g, ffn_wg[l], ffn_wu[l], ffn_wd[l], 1, 2, tm=BIG_ROW_TILE, tf=FF_TILE,
                 mod_row=lambda i: i // lat_tiles_per_batch)
    return h.reshape(batch, seq, d)
```

```python
import functools

import numpy as np
import jax
import jax.numpy as jnp
from jax import lax
from jax.experimental import pallas as pl
from jax.experimental.pallas import tpu as pltpu

F32 = jnp.float32
BF16 = jnp.bfloat16

D_MODEL = 2048
GRID_W = 64
GLA_HEADS = 4
GLA_DK = 256
GLA_DV = 512
GLA_GATE_RANK = 16
GLA_GATE_TEMP = 16.0
NA_HEADS = 16
NA_DH = 64
NA_KR = 8
NA_KC = 16
ROPE_THETA = 10000.0
EPS = 1e-6
NEG_INF = -1e30
N_MOD = 9
LOG2E = 1.4426950408889634

LANES = 128
ROW_TILE = 512
BIG_ROW_TILE = 1024
FF_TILE = 512
PROJ_TILE = 1024
MERGE_STRIP = 512
GLA_BLOCK = 256
GLA_HEADS_PER_STEP = 4
NA_QROWS = 8
NA_SUB_ROWS = 4
NA_WIN_ROWS = 12
VMEM_LIMIT = 56 * 1024 * 1024
VMEM_LIMIT_FFN = 60 * 1024 * 1024


def _params(sem, vmem_limit=VMEM_LIMIT):
    return pltpu.CompilerParams(dimension_semantics=sem, vmem_limit_bytes=vmem_limit)


def _rms(x, g):
    return x * lax.rsqrt(jnp.mean(x * x, axis=-1, keepdims=True) + EPS) * g


def _sigmoid(x):
    return 0.5 * jnp.tanh(0.5 * x) + 0.5


def _silu(x):
    h = 0.5 * x
    return h * jnp.tanh(h) + h


def _mod_kernel(c_ref, w_ref, b_ref, o_ref):
    o_ref[...] = jnp.dot(_silu(c_ref[...]), w_ref[...], preferred_element_type=F32) + b_ref[...]


def _modulation(cvec, w_ada, b_ada):
    n = w_ada.shape[1]
    tn = 1024
    return pl.pallas_call(
        _mod_kernel,
        grid=(n // tn,),
        in_specs=[pl.BlockSpec((8, D_MODEL), lambda j: (0, 0)),
                  pl.BlockSpec((D_MODEL, tn), lambda j: (0, j)),
                  pl.BlockSpec((1, tn), lambda j: (0, j))],
        out_specs=pl.BlockSpec((8, tn), lambda j: (0, j)),
        out_shape=jax.ShapeDtypeStruct((8, n), F32),
        compiler_params=_params(("arbitrary",)),
        name="adaln_mod",
    )(cvec, w_ada, b_ada)


def _mixed_dot(a, w):
    return lax.dot_general(a, w, (((1,), (0,)), ((), ())), preferred_element_type=F32)


def _ffn_kernel(x_ref, mod_ref, g_ref, wg_ref, wu_ref, wd_ref, o_ref, h_ref, *, sub):
    f = pl.program_id(1)
    last = pl.num_programs(1) - 1
    subtiles = [slice(r0, r0 + ROW_TILE) for r0 in range(0, x_ref.shape[0], ROW_TILE)]

    def swiglu(h):
        g = _mixed_dot(h, wg_ref[...])
        u = _mixed_dot(h, wu_ref[...])
        return _mixed_dot((_silu(g) * u).astype(BF16), wd_ref[...])

    @pl.when(f == 0)
    def _():
        gain = g_ref[2 * sub:2 * sub + 1, :] * (1.0 + mod_ref[3 * sub + 1:3 * sub + 2, :])
        shift = mod_ref[3 * sub:3 * sub + 1, :]
        for rows in subtiles:
            h = (_rms(x_ref[rows, :], gain) + shift).astype(BF16)
            h_ref[rows, :] = h
            o_ref[rows, :] = swiglu(h)

    @pl.when((f > 0) & (f < last))
    def _():
        for rows in subtiles:
            o_ref[rows, :] += swiglu(h_ref[rows, :])

    @pl.when(f == last)
    def _():
        gain = 0.5 * mod_ref[3 * sub + 2:3 * sub + 3, :] * g_ref[2 * sub + 1:2 * sub + 2, :]
        for rows in subtiles:
            y = o_ref[rows, :] + swiglu(h_ref[rows, :])
            o_ref[rows, :] = x_ref[rows, :] + _rms(y, gain)


def _ffn(x2d, mod, norm_g, wg, wu, wd, which, sub, *, tm, tf, mod_row):
    n_tiles = x2d.shape[0] // tm
    d_ff = wg.shape[2]
    once = dict(pipeline_mode=pl.Buffered(1))
    return pl.pallas_call(
        functools.partial(_ffn_kernel, sub=sub),
        grid=(n_tiles, d_ff // tf),
        in_specs=[
            pl.BlockSpec((tm, D_MODEL), lambda i, f: (i, 0)),
            pl.BlockSpec((None, N_MOD, D_MODEL), lambda i, f: (mod_row(i), 0, 0)),
            pl.BlockSpec((6, D_MODEL), lambda i, f: (0, 0)),
            pl.BlockSpec((None, D_MODEL, tf), lambda i, f: (which, 0, f)),
            pl.BlockSpec((None, D_MODEL, tf), lambda i, f: (which, 0, f)),
            pl.BlockSpec((None, tf, D_MODEL), lambda i, f: (which, f, 0)),
        ],
        out_specs=pl.BlockSpec((tm, D_MODEL), lambda i, f: (i, 0), **once),
        out_shape=jax.ShapeDtypeStruct(x2d.shape, F32),
        scratch_shapes=[pltpu.VMEM((tm, D_MODEL), BF16)],
        compiler_params=_params(("arbitrary", "arbitrary"), VMEM_LIMIT_FFN),
        name="ffn%d_%d" % (sub, tm),
    )(x2d, mod, norm_g, wg, wu, wd)


N_ROPE_TILES = 2 * GLA_HEADS * GLA_DK // PROJ_TILE
N_Q_TILES = N_ROPE_TILES // 2


def _rope_tables(seq):
    half = GLA_DK // 4
    freqs = ROPE_THETA ** (-np.arange(half, dtype=np.float64) / half)
    t = np.arange(seq)
    cos_parts, sin_parts = [], []
    for pos in (t // GRID_W, t % GRID_W):
        ang = pos[:, None].astype(np.float64) * freqs
        cos_parts += [np.cos(ang), np.cos(ang)]
        sin_parts += [-np.sin(ang), np.sin(ang)]
    cos = np.concatenate(cos_parts, axis=1)
    sin = np.concatenate(sin_parts, axis=1)
    cos = np.concatenate([cos, np.ones((BIG_ROW_TILE, GLA_DK))], axis=0)
    sin = np.concatenate([sin, np.zeros((BIG_ROW_TILE, GLA_DK))], axis=0)
    return jnp.asarray(cos, F32), jnp.asarray(sin, F32)


def _dot_nt(a, w_t):
    return lax.dot_general(a, w_t, (((1,), (1,)), ((), ())), preferred_element_type=F32)


def _inproj_kernel(x_ref, xc_ref, mod_ref, g_ref, w_ref, wgate_ref, cos_ref, sin_ref, p_ref, gate_ref, h_ref,
                   *, n_lat_tiles):
    i = pl.program_id(0)
    n = pl.program_id(1)

    def project(src_ref):
        subtiles = [slice(r0, r0 + ROW_TILE) for r0 in range(0, src_ref.shape[0], ROW_TILE)]

        def rope(y, rows):
            cos = cos_ref[rows, :]
            sin = sin_ref[rows, :]
            qscale = jnp.where(n < N_Q_TILES, GLA_DK ** -0.5, 1.0).astype(F32)
            for j in range(PROJ_TILE // GLA_DK):
                yj = y[:, j * GLA_DK:(j + 1) * GLA_DK]
                swapped = jnp.concatenate(
                    [pltpu.roll(yj[:, g * LANES:(g + 1) * LANES], LANES // 2, 1) for g in range(GLA_DK // LANES)],
                    axis=1)
                p_ref[rows, j * GLA_DK:(j + 1) * GLA_DK] = ((yj * cos + swapped * sin) * qscale).astype(BF16)

        @pl.when(n == 0)
        def _():
            gain = g_ref[2:3, :] * (1.0 + mod_ref[4:5, :])
            for rows in subtiles:
                h = (_rms(src_ref[rows, :], gain) + mod_ref[3:4, :]).astype(BF16)
                h_ref[rows, :] = h
                gate = _dot_nt(h, wgate_ref[...])
                gate_ref[rows, :] = jnp.concatenate(
                    [gate, jnp.zeros((gate.shape[0], LANES - gate.shape[1]), F32)], axis=1)
                rope(_dot_nt(h, w_ref[...]), rows)

        @pl.when((n > 0) & (n < N_ROPE_TILES))
        def _():
            for rows in subtiles:
                rope(_dot_nt(h_ref[rows, :], w_ref[...]), rows)

        @pl.when(n >= N_ROPE_TILES)
        def _():
            for rows in subtiles:
                p_ref[rows, :] = _dot_nt(h_ref[rows, :], w_ref[...]).astype(BF16)

    pl.when(i < n_lat_tiles)(lambda: project(x_ref))
    pl.when(i >= n_lat_tiles)(lambda: project(xc_ref))


def _inproj(h1, h1c, mod, norm_g, w_in_t, cos, sin, seq, gate0, gate1):
    tm = BIG_ROW_TILE
    n_lat_tiles = h1.shape[0] // tm
    n_rows = h1.shape[0] + h1c.shape[0]
    tiles_per_batch = n_lat_tiles // 2
    n_a_tiles = gate0 // PROJ_TILE
    n_b_tiles = (w_in_t.shape[0] - gate1) // PROJ_TILE
    n_gate = gate1 - gate0
    tab_spec = pl.BlockSpec(
        (tm, GLA_DK), lambda i, n: (jnp.where(i < n_lat_tiles, i % tiles_per_batch, seq // tm), 0))
    return pl.pallas_call(
        functools.partial(_inproj_kernel, n_lat_tiles=n_lat_tiles),
        grid=(n_lat_tiles + 1, n_a_tiles + n_b_tiles),
        in_specs=[pl.BlockSpec((tm, D_MODEL), lambda i, n: (jnp.minimum(i, n_lat_tiles - 1), 0)),
                  pl.BlockSpec(h1c.shape, lambda i, n: (0, 0)),
                  pl.BlockSpec((None, N_MOD, D_MODEL), lambda i, n: (jnp.minimum(i // tiles_per_batch, 2), 0, 0)),
                  pl.BlockSpec((6, D_MODEL), lambda i, n: (0, 0)),
                  pl.BlockSpec((pl.Element(PROJ_TILE), pl.Element(D_MODEL)),
                               lambda i, n: (pl.multiple_of(
                                   n * PROJ_TILE + jnp.where(n < n_a_tiles, 0, n_gate), n_gate), 0)),
                  pl.BlockSpec((n_gate, D_MODEL), lambda i, n: (gate0 // n_gate, 0)),
                  tab_spec, tab_spec],
        out_specs=[pl.BlockSpec((tm, PROJ_TILE), lambda i, n: (i, n)),
                   pl.BlockSpec((tm, LANES), lambda i, n: (i, 0))],
        out_shape=[jax.ShapeDtypeStruct((n_rows, (n_a_tiles + n_b_tiles) * PROJ_TILE), BF16),
                   jax.ShapeDtypeStruct((n_rows, LANES), F32)],
        scratch_shapes=[pltpu.VMEM((tm, D_MODEL), BF16)],
        compiler_params=_params(("arbitrary", "arbitrary")),
        name="inproj",
    )(h1, h1c, mod, norm_g, w_in_t, w_in_t, cos, sin)


def _log2_sigmoid(z):
    t = z * LOG2E
    return jnp.minimum(t, 0.0) - jnp.log2(1.0 + jnp.exp2(-jnp.abs(t)))


def _gla_kernel(qf_ref, kf_ref, vf_ref, gf_ref, qb_ref, kb_ref, vb_ref, gb_ref, wg_ref, bg_ref,
                of_ref, ob_ref, stf_ref, stb_ref):
    @pl.when(pl.program_id(2) == 0)
    def _():
        stf_ref[...] = jnp.zeros_like(stf_ref)
        stb_ref[...] = jnp.zeros_like(stb_ref)

    c = GLA_BLOCK
    nt = (((1,), (1,)), ((), ()))
    row = lax.broadcasted_iota(jnp.int32, (c, c), 0)
    col = lax.broadcasted_iota(jnp.int32, (c, c), 1)
    mask = {True: col <= row, False: col >= row}
    tri = {fwd: jnp.where(m, 1.0, 0.0).astype(BF16) for fwd, m in mask.items()}
    refs = {True: (qf_ref, kf_ref, vf_ref, gf_ref, stf_ref, of_ref),
            False: (qb_ref, kb_ref, vb_ref, gb_ref, stb_ref, ob_ref)}
    qcols = lambda head: slice(head * GLA_DK, (head + 1) * GLA_DK)
    vcols = lambda head: slice(head * GLA_DV, (head + 1) * GLA_DV)

    state, log_a, cum, factors, att = {}, {}, {}, {}, {}

    def gate_stage(head, fwd):
        d = 0 if fwd else 1
        state[head, fwd] = refs[fwd][4][head]
        z = jnp.dot(refs[fwd][3][...], wg_ref[d, :, qcols(head)], preferred_element_type=F32)
        log_a[head, fwd] = _log2_sigmoid(z + bg_ref[d, :, qcols(head)]) * (1.0 / GLA_GATE_TEMP)

    def cumsum_stage(head, fwd):
        la = log_a[head, fwd]
        hi = la.astype(BF16)
        lo = (la - hi.astype(F32)).astype(BF16)
        cum[head, fwd] = (jnp.dot(tri[fwd], hi, preferred_element_type=F32)
                          + jnp.dot(tri[fwd], lo, preferred_element_type=F32))

    def factor_stage(head, fwd):
        cm = cum[head, fwd]
        cum_end = cm[c - 1:c, :] if fwd else cm[0:1, :]
        cum_mid = cm[c // 2:c // 2 + 1, :]
        q = refs[fwd][0][:, qcols(head)]
        k = refs[fwd][1][:, qcols(head)]
        factors[head, fwd] = (q * jnp.exp2(cm - cum_mid).astype(BF16),
                              k * jnp.exp2(cum_mid - cm).astype(BF16),
                              q * jnp.exp2(cm).astype(BF16),
                              k * jnp.exp2(cum_end - cm).astype(BF16),
                              jnp.exp2(cum_end))

    def att_stage(head, fwd):
        q_att, k_att = factors[head, fwd][:2]
        a = lax.dot_general(q_att, k_att, nt, preferred_element_type=F32)
        att[head, fwd] = jnp.where(mask[fwd], a, 0.0).astype(BF16)

    def out_stage(head, fwd):
        _, _, q_dec, k_end, decay = factors[head, fwd]
        v = refs[fwd][2][:, vcols(head)]
        st = state[head, fwd]
        o = (jnp.dot(att[head, fwd], v, preferred_element_type=F32)
             + lax.dot_general(q_dec, st.astype(BF16), nt, preferred_element_type=F32))
        kv_t = lax.dot_general(v, k_end, (((0,), (0,)), ((), ())), preferred_element_type=F32)
        refs[fwd][5][:, vcols(head)] = o.astype(BF16)
        refs[fwd][4][head] = decay * st + kv_t

    for stage in (gate_stage, cumsum_stage, factor_stage, att_stage, out_stage):
        for head in range(GLA_HEADS_PER_STEP):
            for fwd in (True, False):
                stage(head, fwd)


def _gla(p, gin, wg_pad, bg, batch, seq):
    n_lat = seq // GLA_BLOCK
    ctx_block0 = batch * n_lat
    hps = GLA_HEADS_PER_STEP
    dk, dv = hps * GLA_DK, hps * GLA_DV

    def lat_block(b, s, forward):
        j = jnp.maximum(s - 1, 0)
        return b * n_lat + (j if forward else n_lat - 1 - j)

    def in_block(b, s, forward):
        return jnp.where(s == 0, ctx_block0 + b, lat_block(b, s, forward))

    kq = GLA_HEADS // hps
    kv = 2 * GLA_HEADS * GLA_DK // dv

    def dir_specs(forward):
        return [
            pl.BlockSpec((GLA_BLOCK, dk), lambda b, h, s: (in_block(b, s, forward), h)),
            pl.BlockSpec((GLA_BLOCK, dk), lambda b, h, s: (in_block(b, s, forward), kq + h)),
            pl.BlockSpec((GLA_BLOCK, dv), lambda b, h, s: (in_block(b, s, forward), kv + h)),
            pl.BlockSpec((GLA_BLOCK, LANES), lambda b, h, s: (in_block(b, s, forward), 0)),
        ]

    out_shape = jax.ShapeDtypeStruct((batch * seq, GLA_HEADS * GLA_DV), BF16)
    state = pltpu.VMEM((hps, GLA_DV, GLA_DK), F32)
    return pl.pallas_call(
        _gla_kernel,
        grid=(batch, GLA_HEADS // hps, n_lat + 1),
        in_specs=dir_specs(True) + dir_specs(False) + [
            pl.BlockSpec((2, LANES, dk), lambda b, h, s: (0, 0, h)),
            pl.BlockSpec((2, 1, dk), lambda b, h, s: (0, 0, h))],
        out_specs=[pl.BlockSpec((GLA_BLOCK, dv), lambda b, h, s: (lat_block(b, s, True), h)),
                   pl.BlockSpec((GLA_BLOCK, dv), lambda b, h, s: (lat_block(b, s, False), h))],
        out_shape=[out_shape, out_shape],
        scratch_shapes=[state, state],
        compiler_params=_params(("arbitrary", "arbitrary", "arbitrary")),
        name="gla",
    )(p, p, p, gin, p, p, p, gin, wg_pad, bg)


NA_QTOK = NA_QROWS * GRID_W
NA_SUB_TOK = NA_SUB_ROWS * GRID_W
NA_WIN = NA_WIN_ROWS * GRID_W
NA_SUBS = NA_QROWS // NA_SUB_ROWS
NA_KINDS = ((0, 0), (NA_SUB_ROWS, NA_SUB_ROWS - NA_KR // 2), (GRID_W - NA_SUB_ROWS, GRID_W - NA_WIN_ROWS))


def _na_build_bias(rpb_ref, t_ref, rows):
    n_slots = 2 * NA_KR - 1
    lane = lax.broadcasted_iota(jnp.int32, (GRID_W, LANES), 1)
    q_col = lax.broadcasted_iota(jnp.int32, (GRID_W, LANES), 0)
    k_col = lane & (GRID_W - 1)
    upper = lane >= GRID_W
    c0 = jnp.clip(q_col - NA_KC // 2, 0, GRID_W - NA_KC)
    col_ok = (k_col >= c0) & (k_col < c0 + NA_KC)
    neg = jnp.full((GRID_W, LANES), NEG_INF, F32)

    def toeplitz(hh, slot, lane_off):
        if not 0 <= slot < n_slots:
            return jnp.zeros((GRID_W, LANES), F32)
        base = jnp.broadcast_to(rpb_ref[hh, slot:slot + 1, :], (GRID_W, LANES)) * LOG2E
        return pltpu.roll(base, (LANES - (NA_KC - 1) + lane_off) % LANES, 1, stride=1, stride_axis=0)

    for hh in range(2):
        pairs = {}
        for kind, (r0, w0) in enumerate(NA_KINDS):
            for qr in range(NA_SUB_ROWS):
                band0 = min(max(r0 + qr - NA_KR // 2, 0), rows - NA_KR) - w0
                slot0 = w0 - (r0 + qr) + NA_KR - 1
                for g in range(NA_WIN_ROWS // 2):
                    slot = slot0 + 2 * g
                    lo_ok = band0 <= 2 * g < band0 + NA_KR
                    hi_ok = band0 <= 2 * g + 1 < band0 + NA_KR
                    if lo_ok or hi_ok:
                        if slot not in pairs:
                            pairs[slot] = jnp.where(upper, toeplitz(hh, slot + 1, GRID_W), toeplitz(hh, slot, 0))
                        ok = col_ok if (lo_ok and hi_ok) else (col_ok & upper if hi_ok else col_ok & ~upper)
                        tile = jnp.where(ok, pairs[slot], neg)
                    else:
                        tile = neg
                    t_ref[hh, kind, qr * GRID_W:(qr + 1) * GRID_W, g * LANES:(g + 1) * LANES] = tile


def _na_kernel(q_ref, k_ref, v_ref, kc_ref, vc_ref, rpb_ref, o_ref, t_ref, *, rows):
    b = pl.program_id(1)
    blk = pl.program_id(2)
    n_blk = rows // NA_QROWS

    pl.when((b == 0) & (blk == 0))(lambda: _na_build_bias(rpb_ref, t_ref, rows))

    nt = (((1,), (1,)), ((), ()))
    kc = kc_ref[...]
    vc = vc_ref[...]
    lane = lax.broadcasted_iota(jnp.int32, (NA_SUB_TOK, LANES), 1)
    chains = [(hh, sub) for sub in range(NA_SUBS) for hh in range(2)]
    q_head, kw, vw, kind = {}, [], [], []
    for sub in range(NA_SUBS):
        q = q_ref[sub * NA_SUB_TOK:(sub + 1) * NA_SUB_TOK, :].astype(F32) * (NA_DH ** -0.5 * LOG2E)
        q = q.astype(BF16)
        q_head[0, sub] = jnp.where(lane < NA_DH, q, jnp.zeros_like(q))
        q_head[1, sub] = jnp.where(lane >= NA_DH, q, jnp.zeros_like(q))
        r0 = blk * NA_QROWS + sub * NA_SUB_ROWS
        w0 = jnp.clip(r0 - NA_KR // 2, 0, rows - NA_WIN_ROWS)
        win = pl.ds(pl.multiple_of(w0 * GRID_W, 256), NA_WIN)
        kw.append(k_ref[win, :])
        vw.append(v_ref[win, :])
        kind.append(jnp.where(r0 == 0, 0, jnp.where(r0 == rows - NA_SUB_ROWS, 2, 1)))

    s_lat, s_ctx, p_lat, p_ctx, denom, out = {}, {}, {}, {}, {}, {}
    for hh, sub in chains:
        qs = q_head[hh, sub]
        s_lat[hh, sub] = lax.dot_general(qs, kw[sub], nt, preferred_element_type=F32) + t_ref[hh, kind[sub]]
        s_ctx[hh, sub] = lax.dot_general(qs, kc, nt, preferred_element_type=F32)
    for ch in chains:
        m = jnp.maximum(jnp.max(s_lat[ch], axis=-1, keepdims=True), jnp.max(s_ctx[ch], axis=-1, keepdims=True))
        pl_ = jnp.exp2(s_lat[ch] - m)
        pc_ = jnp.exp2(s_ctx[ch] - m)
        denom[ch] = jnp.sum(pl_, axis=-1, keepdims=True) + jnp.sum(pc_, axis=-1, keepdims=True)
        p_lat[ch] = pl_.astype(BF16)
        p_ctx[ch] = pc_.astype(BF16)
    for hh, sub in chains:
        o = (jnp.dot(p_lat[hh, sub], vw[sub], preferred_element_type=F32)
             + jnp.dot(p_ctx[hh, sub], vc, preferred_element_type=F32))
        out[hh, sub] = o / denom[hh, sub]
    for sub in range(NA_SUBS):
        rows_ = slice(sub * NA_SUB_TOK, (sub + 1) * NA_SUB_TOK)
        o_ref[rows_, :] = jnp.where(lane < NA_DH, out[0, sub], out[1, sub]).astype(o_ref.dtype)


def _na(p, rpb, batch, seq, ctx_len, col0):
    rows = seq // GRID_W
    n_blk = rows // NA_QROWS
    n_pairs = NA_HEADS // 2
    cq = col0 // LANES
    ck = cq + n_pairs
    cv = ck + n_pairs
    ctx_block0 = batch * seq // ctx_len
    return pl.pallas_call(
        functools.partial(_na_kernel, rows=rows),
        grid=(n_pairs, batch, n_blk),
        in_specs=[
            pl.BlockSpec((NA_QTOK, LANES), lambda hp, b, r: (b * n_blk + r, cq + hp)),
            pl.BlockSpec((seq, LANES), lambda hp, b, r: (b, ck + hp)),
            pl.BlockSpec((seq, LANES), lambda hp, b, r: (b, cv + hp)),
            pl.BlockSpec((ctx_len, LANES), lambda hp, b, r: (ctx_block0 + b, ck + hp)),
            pl.BlockSpec((ctx_len, LANES), lambda hp, b, r: (ctx_block0 + b, cv + hp)),
            pl.BlockSpec((2, 2 * NA_KR - 1, LANES), lambda hp, b, r: (hp, 0, 0)),
        ],
        out_specs=pl.BlockSpec((NA_QTOK, LANES), lambda hp, b, r: (b * n_blk + r, hp)),
        out_shape=jax.ShapeDtypeStruct((batch * seq, NA_HEADS * NA_DH), BF16),
        scratch_shapes=[pltpu.VMEM((2, len(NA_KINDS), NA_SUB_TOK, NA_WIN), F32)],
        compiler_params=_params(("arbitrary", "arbitrary", "arbitrary")),
        name="natten",
    )(p, p, p, p, p, rpb)


def _merge_kernel(of_ref, ob_ref, r_ref, gn_ref, b_ref, wa_ref, wb_ref, m1a_ref, m1b_ref, m2a_ref, m2b_ref,
                  o_ref, a_ref):
    for h in range(GLA_HEADS):
        cols = slice(h * GLA_DV, (h + 1) * GLA_DV)
        tot = of_ref[:, cols].astype(F32) + ob_ref[:, cols].astype(F32)
        a_ref[:, cols] = (_rms(tot, gn_ref[...]) * _silu(r_ref[:, cols].astype(F32))).astype(BF16)

    gate_refs = ((m1a_ref, m2a_ref), (m1b_ref, m2b_ref))
    gate_w = m1a_ref.shape[1]
    for c0 in range(0, o_ref.shape[1], MERGE_STRIP):
        cols = slice(c0, c0 + MERGE_STRIP)
        m1_ref, m2_ref = gate_refs[c0 // gate_w]
        gcols = slice(c0 % gate_w, c0 % gate_w + MERGE_STRIP)
        a = jnp.dot(a_ref[...], wa_ref[:, cols], preferred_element_type=F32)
        b = jnp.dot(b_ref[...], wb_ref[:, cols], preferred_element_type=F32)
        m = _sigmoid(m1_ref[:, gcols].astype(F32)) * a + _sigmoid(m2_ref[:, gcols].astype(F32)) * b
        o_ref[:, cols] = m.astype(o_ref.dtype)


def _merge(o_fwd, o_bwd, gn, o_na, w_gla_o, w_na_o, p, col_r, col_m1):
    n_rows = o_fwd.shape[0]
    v_w = o_fwd.shape[1]
    gw = D_MODEL // 2
    c1 = col_m1 // gw
    once = dict(pipeline_mode=pl.Buffered(1))
    gate_spec = lambda c: pl.BlockSpec((ROW_TILE, gw), lambda i: (i, c))
    return pl.pallas_call(
        _merge_kernel,
        grid=(n_rows // ROW_TILE,),
        in_specs=[pl.BlockSpec((ROW_TILE, v_w), lambda i: (i, 0)),
                  pl.BlockSpec((ROW_TILE, v_w), lambda i: (i, 0)),
                  pl.BlockSpec((ROW_TILE, v_w), lambda i: (i, col_r // v_w)),
                  pl.BlockSpec((1, GLA_DV), lambda i: (0, 0)),
                  pl.BlockSpec((ROW_TILE, o_na.shape[1]), lambda i: (i, 0)),
                  pl.BlockSpec(w_gla_o.shape, lambda i: (0, 0), **once),
                  pl.BlockSpec(w_na_o.shape, lambda i: (0, 0), **once),
                  gate_spec(c1), gate_spec(c1 + 1), gate_spec(c1 + 2), gate_spec(c1 + 3)],
        out_specs=pl.BlockSpec((ROW_TILE, D_MODEL), lambda i: (i, 0)),
        out_shape=jax.ShapeDtypeStruct((n_rows, D_MODEL), BF16),
        scratch_shapes=[pltpu.VMEM((ROW_TILE, v_w), BF16)],
        compiler_params=_params(("arbitrary",)),
        name="merge",
    )(o_fwd, o_bwd, p, gn, o_na, w_gla_o, w_na_o, p, p, p, p)


def _outproj_kernel(m_ref, w_ref, x_ref, mod_ref, g_ref, o_ref):
    y = _mixed_dot(m_ref[...], w_ref[...])
    o_ref[...] = x_ref[...] + mod_ref[5:6, :] * _rms(y, g_ref[3:4, :])


def _outproj(m, w_out, h1, mod, norm_g):
    n_tiles = m.shape[0] // ROW_TILE
    tiles_per_batch = n_tiles // 2
    return pl.pallas_call(
        _outproj_kernel,
        grid=(n_tiles,),
        in_specs=[pl.BlockSpec((ROW_TILE, D_MODEL), lambda i: (i, 0)),
                  pl.BlockSpec((D_MODEL, D_MODEL), lambda i: (0, 0), pipeline_mode=pl.Buffered(1)),
                  pl.BlockSpec((ROW_TILE, D_MODEL), lambda i: (i, 0)),
                  pl.BlockSpec((None, N_MOD, D_MODEL), lambda i: (i // tiles_per_batch, 0, 0)),
                  pl.BlockSpec((6, D_MODEL), lambda i: (0, 0))],
        out_specs=pl.BlockSpec((ROW_TILE, D_MODEL), lambda i: (i, 0)),
        out_shape=jax.ShapeDtypeStruct((m.shape[0], D_MODEL), F32),
        compiler_params=_params(("arbitrary",)),
        name="outproj",
    )(m, w_out, h1, mod, norm_g)


def kernel(x, c, ctx, c_ctx, w_ada, b_ada, norm_g, ffn_wg, ffn_wu, ffn_wd, w_in, gla_wg, gla_bg, gla_norm_g,
           w_gla_o, na_rpb, w_na_o, w_out):
    batch, seq, d = x.shape
    ctx_len = ctx.shape[1]
    depth = w_ada.shape[0]
    assert d == D_MODEL and batch == 2 and batch * ctx_len == ROW_TILE and depth == 1
    assert seq % ROW_TILE == 0 and seq // GRID_W == GRID_W

    qk_w = GLA_HEADS * GLA_DK
    v_w = GLA_HEADS * GLA_DV
    na_w = NA_HEADS * NA_DH
    gate0 = 2 * qk_w + 2 * v_w
    gate1 = gate0 + 2 * GLA_GATE_RANK
    col_nq = gate0
    col_m1 = col_nq + 3 * na_w

    cvec = jnp.zeros((8, d), F32).at[0:batch].set(c).at[batch].set(c_ctx)
    cos, sin = _rope_tables(seq)
    h = x.reshape(batch * seq, d)
    hc = ctx.reshape(batch * ctx_len, d)
    lat_tiles_per_batch = seq // BIG_ROW_TILE
    for l in range(depth):
        mod = _modulation(cvec, w_ada[l], b_ada[l].reshape(1, -1)).reshape(8, N_MOD, d)
        g = norm_g[l]
        h1 = _ffn(h, mod, g, ffn_wg[l], ffn_wu[l], ffn_wd[l], 0, 0, tm=BIG_ROW_TILE, tf=FF_TILE,
                  mod_row=lambda i: i // lat_tiles_per_batch)
        h1c = _ffn(hc, mod, g, ffn_wg[l], ffn_wu[l], ffn_wd[l], 0, 0, tm=ROW_TILE, tf=FF_TILE,
                   mod_row=lambda i: batch)
        p, gin = _inproj(h1, h1c, mod, g, jnp.swapaxes(w_in[l], 0, 1), cos, sin, seq, gate0, gate1)
        wg_pad = jnp.zeros((2, LANES, qk_w), F32)
        wg_pad = wg_pad.at[0, :GLA_GATE_RANK].set(gla_wg[l, 0])
        wg_pad = wg_pad.at[1, GLA_GATE_RANK:2 * GLA_GATE_RANK].set(gla_wg[l, 1])
        bg = gla_bg[l].reshape(2, 1, qk_w)
        gn = gla_norm_g[l].reshape(1, GLA_DV)
        o_fwd, o_bwd = _gla(p, gin, wg_pad, bg, batch, seq)
        rpb = jnp.pad(na_rpb[l], ((0, 0), (0, 0), (0, LANES - (2 * NA_KC - 1))))
        o_na = _na(p, rpb, batch, seq, ctx_len, col_nq)
        m = _merge(o_fwd, o_bwd, gn, o_na, w_gla_o[l].astype(BF16), w_na_o[l].astype(BF16), p,
                   2 * qk_w + v_w, col_m1)
        h2 = _outproj(m, w_out[l], h1, mod, g)
        h = _ffn(h2, mod, g, ffn_wg[l], ffn_wu[l], ffn_wd[l], 1, 2, tm=BIG_ROW_TILE, tf=FF_TILE,
                 mod_row=lambda i: i // lat_tiles_per_batch)
    return h.reshape(batch, seq, d)
```

```python
import functools

import numpy as np
import jax
import jax.numpy as jnp
from jax import lax
from jax.experimental import pallas as pl
from jax.experimental.pallas import tpu as pltpu

F32 = jnp.float32
BF16 = jnp.bfloat16

D_MODEL = 2048
GRID_W = 64
GLA_HEADS = 4
GLA_DK = 256
GLA_DV = 512
GLA_GATE_RANK = 16
GLA_GATE_TEMP = 16.0
NA_HEADS = 16
NA_DH = 64
NA_KR = 8
NA_KC = 16
ROPE_THETA = 10000.0
EPS = 1e-6
NEG_INF = -1e30
N_MOD = 9
LOG2E = 1.4426950408889634

LANES = 128
ROW_TILE = 512
BIG_ROW_TILE = 1024
FF_TILE = 512
PROJ_TILE = 1024
MERGE_STRIP = 512
GLA_BLOCK = 256
GLA_HEADS_PER_STEP = 4
NA_QROWS = 8
NA_SUB_ROWS = 4
NA_WIN_ROWS = 12
VMEM_LIMIT = 56 * 1024 * 1024
VMEM_LIMIT_FFN = 60 * 1024 * 1024


def _params(sem, vmem_limit=VMEM_LIMIT):
    return pltpu.CompilerParams(dimension_semantics=sem, vmem_limit_bytes=vmem_limit)


def _rms(x, g):
    return x * lax.rsqrt(jnp.mean(x * x, axis=-1, keepdims=True) + EPS) * g


def _sigmoid(x):
    return 0.5 * jnp.tanh(0.5 * x) + 0.5


def _silu(x):
    h = 0.5 * x
    return h * jnp.tanh(h) + h


def _mod_kernel(c_ref, w_ref, b_ref, o_ref):
    o_ref[...] = jnp.dot(_silu(c_ref[...]), w_ref[...], preferred_element_type=F32) + b_ref[...]


def _modulation(cvec, w_ada, b_ada):
    n = w_ada.shape[1]
    tn = 1024
    return pl.pallas_call(
        _mod_kernel,
        grid=(n // tn,),
        in_specs=[pl.BlockSpec((8, D_MODEL), lambda j: (0, 0)),
                  pl.BlockSpec((D_MODEL, tn), lambda j: (0, j)),
                  pl.BlockSpec((1, tn), lambda j: (0, j))],
        out_specs=pl.BlockSpec((8, tn), lambda j: (0, j)),
        out_shape=jax.ShapeDtypeStruct((8, n), F32),
        compiler_params=_params(("arbitrary",)),
        name="adaln_mod",
    )(cvec, w_ada, b_ada)


def _mixed_dot(a, w):
    return lax.dot_general(a, w, (((1,), (0,)), ((), ())), preferred_element_type=F32)


def _ffn_kernel(x_ref, mod_ref, g_ref, wg_ref, wu_ref, wd_ref, o_hbm, h_ref, acc_ref, sem, *, sub):
    i = pl.program_id(0)
    f = pl.program_id(1)
    n_tiles = pl.num_programs(0)
    last = pl.num_programs(1) - 1
    tm = x_ref.shape[0]
    subtiles = [pl.ds(r0, ROW_TILE) for r0 in range(0, tm, ROW_TILE)]

    def out_copy(k, tile):
        return pltpu.make_async_copy(
            acc_ref.at[subtiles[k], :], o_hbm.at[pl.ds(tile * tm + k * ROW_TILE, ROW_TILE), :], sem.at[k])

    def gate_up(h):
        g = _mixed_dot(h, wg_ref[...])
        u = _mixed_dot(h, wu_ref[...])
        return (_silu(g) * u).astype(BF16)

    @pl.when((i == 0) & (f == 0))
    def _():
        acc_ref[...] = jnp.zeros_like(acc_ref)
        for k in range(len(subtiles)):
            out_copy(k, 0).start()

    @pl.when(f == 0)
    def _():
        gain = g_ref[2 * sub:2 * sub + 1, :] * (1.0 + mod_ref[3 * sub + 1:3 * sub + 2, :])
        shift = mod_ref[3 * sub:3 * sub + 1, :]
        for k, rows in enumerate(subtiles):
            h = (_rms(x_ref[rows, :], gain) + shift).astype(BF16)
            h_ref[rows, :] = h
            a = gate_up(h)
            out_copy(k, jnp.maximum(i - 1, 0)).wait()
            acc_ref[rows, :] = _mixed_dot(a, wd_ref[...])

    @pl.when((f > 0) & (f < last))
    def _():
        for rows in subtiles:
            acc_ref[rows, :] += _mixed_dot(gate_up(h_ref[rows, :]), wd_ref[...])

    @pl.when(f == last)
    def _():
        gain = 0.5 * mod_ref[3 * sub + 2:3 * sub + 3, :] * g_ref[2 * sub + 1:2 * sub + 2, :]
        for k, rows in enumerate(subtiles):
            y = acc_ref[rows, :] + _mixed_dot(gate_up(h_ref[rows, :]), wd_ref[...])
            acc_ref[rows, :] = x_ref[rows, :] + _rms(y, gain)
            out_copy(k, i).start()

    @pl.when((f == last) & (i == n_tiles - 1))
    def _():
        for k in range(len(subtiles)):
            out_copy(k, i).wait()


def _ffn(x2d, mod, norm_g, wg, wu, wd, which, sub, *, tm, tf, mod_row):
    n_tiles = x2d.shape[0] // tm
    d_ff = wg.shape[2]
    return pl.pallas_call(
        functools.partial(_ffn_kernel, sub=sub),
        grid=(n_tiles, d_ff // tf),
        in_specs=[
            pl.BlockSpec((tm, D_MODEL), lambda i, f: (i, 0)),
            pl.BlockSpec((None, N_MOD, D_MODEL), lambda i, f: (mod_row(i), 0, 0)),
            pl.BlockSpec((6, D_MODEL), lambda i, f: (0, 0)),
            pl.BlockSpec((None, D_MODEL, tf), lambda i, f: (which, 0, f)),
            pl.BlockSpec((None, D_MODEL, tf), lambda i, f: (which, 0, f)),
            pl.BlockSpec((None, tf, D_MODEL), lambda i, f: (which, f, 0)),
        ],
        out_specs=pl.BlockSpec(memory_space=pl.ANY),
        out_shape=jax.ShapeDtypeStruct(x2d.shape, F32),
        scratch_shapes=[pltpu.VMEM((tm, D_MODEL), BF16), pltpu.VMEM((tm, D_MODEL), F32),
                        pltpu.SemaphoreType.DMA((tm // ROW_TILE,))],
        compiler_params=_params(("arbitrary", "arbitrary"), VMEM_LIMIT_FFN),
        name="ffn%d_%d" % (sub, tm),
    )(x2d, mod, norm_g, wg, wu, wd)


N_ROPE_TILES = 2 * GLA_HEADS * GLA_DK // PROJ_TILE
N_Q_TILES = N_ROPE_TILES // 2


def _rope_tables(seq):
    half = GLA_DK // 4
    freqs = ROPE_THETA ** (-np.arange(half, dtype=np.float64) / half)
    t = np.arange(seq)
    cos_parts, sin_parts = [], []
    for pos in (t // GRID_W, t % GRID_W):
        ang = pos[:, None].astype(np.float64) * freqs
        cos_parts += [np.cos(ang), np.cos(ang)]
        sin_parts += [-np.sin(ang), np.sin(ang)]
    cos = np.concatenate(cos_parts, axis=1)
    sin = np.concatenate(sin_parts, axis=1)
    cos = np.concatenate([cos, np.ones((BIG_ROW_TILE, GLA_DK))], axis=0)
    sin = np.concatenate([sin, np.zeros((BIG_ROW_TILE, GLA_DK))], axis=0)
    return jnp.asarray(cos, F32), jnp.asarray(sin, F32)


def _dot_nt(a, w_t):
    return lax.dot_general(a, w_t, (((1,), (1,)), ((), ())), preferred_element_type=F32)


def _inproj_kernel(x_ref, xc_ref, mod_ref, g_ref, w_ref, wgate_ref, cos_ref, sin_ref, p_ref, gate_ref, h_ref,
                   *, n_lat_tiles):
    i = pl.program_id(0)
    n = pl.program_id(1)

    def project(src_ref):
        r = slice(0, src_ref.shape[0])

        @pl.when(n == 0)
        def _():
            h = _rms(src_ref[...], g_ref[2:3, :]) * (1.0 + mod_ref[4:5, :]) + mod_ref[3:4, :]
            h_ref[r, :] = h.astype(BF16)
            gate = _dot_nt(h_ref[r, :], wgate_ref[...])
            gate_ref[r, :] = jnp.concatenate(
                [gate, jnp.zeros((gate.shape[0], LANES - gate.shape[1]), F32)], axis=1)

        @pl.when(n < N_ROPE_TILES)
        def _():
            y = _dot_nt(h_ref[r, :], w_ref[...])
            cos = cos_ref[r, :]
            sin = sin_ref[r, :]
            qscale = jnp.where(n < N_Q_TILES, GLA_DK ** -0.5, 1.0).astype(F32)
            for j in range(PROJ_TILE // GLA_DK):
                yj = y[:, j * GLA_DK:(j + 1) * GLA_DK]
                swapped = jnp.concatenate(
                    [pltpu.roll(yj[:, g * LANES:(g + 1) * LANES], LANES // 2, 1) for g in range(GLA_DK // LANES)],
                    axis=1)
                p_ref[r, j * GLA_DK:(j + 1) * GLA_DK] = ((yj * cos + swapped * sin) * qscale).astype(BF16)

        @pl.when(n >= N_ROPE_TILES)
        def _():
            p_ref[r, :] = _dot_nt(h_ref[r, :], w_ref[...]).astype(BF16)

    pl.when(i < n_lat_tiles)(lambda: project(x_ref))
    pl.when(i >= n_lat_tiles)(lambda: project(xc_ref))


def _inproj(h1, h1c, mod, norm_g, w_in_t, cos, sin, seq, gate0, gate1):
    tm = BIG_ROW_TILE
    n_lat_tiles = h1.shape[0] // tm
    n_rows = h1.shape[0] + h1c.shape[0]
    tiles_per_batch = n_lat_tiles // 2
    n_a_tiles = gate0 // PROJ_TILE
    n_b_tiles = (w_in_t.shape[0] - gate1) // PROJ_TILE
    n_gate = gate1 - gate0
    tab_spec = pl.BlockSpec(
        (tm, GLA_DK), lambda i, n: (jnp.where(i < n_lat_tiles, i % tiles_per_batch, seq // tm), 0))
    return pl.pallas_call(
        functools.partial(_inproj_kernel, n_lat_tiles=n_lat_tiles),
        grid=(n_lat_tiles + 1, n_a_tiles + n_b_tiles),
        in_specs=[pl.BlockSpec((tm, D_MODEL), lambda i, n: (jnp.minimum(i, n_lat_tiles - 1), 0)),
                  pl.BlockSpec(h1c.shape, lambda i, n: (0, 0)),
                  pl.BlockSpec((None, N_MOD, D_MODEL), lambda i, n: (jnp.minimum(i // tiles_per_batch, 2), 0, 0)),
                  pl.BlockSpec((6, D_MODEL), lambda i, n: (0, 0)),
                  pl.BlockSpec((pl.Element(PROJ_TILE), pl.Element(D_MODEL)),
                               lambda i, n: (pl.multiple_of(
                                   n * PROJ_TILE + jnp.where(n < n_a_tiles, 0, n_gate), n_gate), 0)),
                  pl.BlockSpec((n_gate, D_MODEL), lambda i, n: (gate0 // n_gate, 0)),
                  tab_spec, tab_spec],
        out_specs=[pl.BlockSpec((tm, PROJ_TILE), lambda i, n: (i, n)),
                   pl.BlockSpec((tm, LANES), lambda i, n: (i, 0))],
        out_shape=[jax.ShapeDtypeStruct((n_rows, (n_a_tiles + n_b_tiles) * PROJ_TILE), BF16),
                   jax.ShapeDtypeStruct((n_rows, LANES), F32)],
        scratch_shapes=[pltpu.VMEM((tm, D_MODEL), BF16)],
        compiler_params=_params(("arbitrary", "arbitrary")),
        name="inproj",
    )(h1, h1c, mod, norm_g, w_in_t, w_in_t, cos, sin)


def _log2_sigmoid(z):
    t = z * LOG2E
    return jnp.minimum(t, 0.0) - jnp.log2(1.0 + jnp.exp2(-jnp.abs(t)))


def _gla_kernel(qf_ref, kf_ref, vf_ref, gf_ref, qb_ref, kb_ref, vb_ref, gb_ref, wg_ref, bg_ref,
                of_ref, ob_ref, stf_ref, stb_ref):
    @pl.when(pl.program_id(2) == 0)
    def _():
        stf_ref[...] = jnp.zeros_like(stf_ref)
        stb_ref[...] = jnp.zeros_like(stb_ref)

    c = GLA_BLOCK
    nt = (((1,), (1,)), ((), ()))
    row = lax.broadcasted_iota(jnp.int32, (c, c), 0)
    col = lax.broadcasted_iota(jnp.int32, (c, c), 1)
    mask = {True: col <= row, False: col >= row}
    tri = {fwd: jnp.where(m, 1.0, 0.0).astype(BF16) for fwd, m in mask.items()}
    refs = {True: (qf_ref, kf_ref, vf_ref, gf_ref, stf_ref, of_ref),
            False: (qb_ref, kb_ref, vb_ref, gb_ref, stb_ref, ob_ref)}
    qcols = lambda head: slice(head * GLA_DK, (head + 1) * GLA_DK)
    vcols = lambda head: slice(head * GLA_DV, (head + 1) * GLA_DV)

    state, log_a, cum, factors, att = {}, {}, {}, {}, {}

    def gate_stage(head, fwd):
        d = 0 if fwd else 1
        state[head, fwd] = refs[fwd][4][head]
        z = jnp.dot(refs[fwd][3][...], wg_ref[d, :, qcols(head)], preferred_element_type=F32)
        log_a[head, fwd] = _log2_sigmoid(z + bg_ref[d, :, qcols(head)]) * (1.0 / GLA_GATE_TEMP)

    def cumsum_stage(head, fwd):
        la = log_a[head, fwd]
        hi = la.astype(BF16)
        lo = (la - hi.astype(F32)).astype(BF16)
        cum[head, fwd] = (jnp.dot(tri[fwd], hi, preferred_element_type=F32)
                          + jnp.dot(tri[fwd], lo, preferred_element_type=F32))

    def factor_stage(head, fwd):
        cm = cum[head, fwd]
        cum_end = cm[c - 1:c, :] if fwd else cm[0:1, :]
        cum_mid = cm[c // 2:c // 2 + 1, :]
        q = refs[fwd][0][:, qcols(head)]
        k = refs[fwd][1][:, qcols(head)]
        factors[head, fwd] = (q * jnp.exp2(cm - cum_mid).astype(BF16),
                              k * jnp.exp2(cum_mid - cm).astype(BF16),
                              q * jnp.exp2(cm).astype(BF16),
                              k * jnp.exp2(cum_end - cm).astype(BF16),
                              jnp.exp2(cum_end))

    def att_stage(head, fwd):
        q_att, k_att = factors[head, fwd][:2]
        a = lax.dot_general(q_att, k_att, nt, preferred_element_type=F32)
        att[head, fwd] = jnp.where(mask[fwd], a, 0.0).astype(BF16)

    def out_stage(head, fwd):
        _, _, q_dec, k_end, decay = factors[head, fwd]
        v = refs[fwd][2][:, vcols(head)]
        st = state[head, fwd]
        o = (jnp.dot(att[head, fwd], v, preferred_element_type=F32)
             + lax.dot_general(q_dec, st.astype(BF16), nt, preferred_element_type=F32))
        kv_t = lax.dot_general(v, k_end, (((0,), (0,)), ((), ())), preferred_element_type=F32)
        refs[fwd][5][:, vcols(head)] = o.astype(BF16)
        refs[fwd][4][head] = decay * st + kv_t

    for stage in (gate_stage, cumsum_stage, factor_stage, att_stage, out_stage):
        for head in range(GLA_HEADS_PER_STEP):
            for fwd in (True, False):
                stage(head, fwd)


def _gla(p, gin, wg_pad, bg, batch, seq):
    n_lat = seq // GLA_BLOCK
    ctx_block0 = batch * n_lat
    hps = GLA_HEADS_PER_STEP
    dk, dv = hps * GLA_DK, hps * GLA_DV

    def lat_block(b, s, forward):
        j = jnp.maximum(s - 1, 0)
        return b * n_lat + (j if forward else n_lat - 1 - j)

    def in_block(b, s, forward):
        return jnp.where(s == 0, ctx_block0 + b, lat_block(b, s, forward))

    kq = GLA_HEADS // hps
    kv = 2 * GLA_HEADS * GLA_DK // dv

    def dir_specs(forward):
        return [
            pl.BlockSpec((GLA_BLOCK, dk), lambda b, h, s: (in_block(b, s, forward), h)),
            pl.BlockSpec((GLA_BLOCK, dk), lambda b, h, s: (in_block(b, s, forward), kq + h)),
            pl.BlockSpec((GLA_BLOCK, dv), lambda b, h, s: (in_block(b, s, forward), kv + h)),
            pl.BlockSpec((GLA_BLOCK, LANES), lambda b, h, s: (in_block(b, s, forward), 0)),
        ]

    out_shape = jax.ShapeDtypeStruct((batch * seq, GLA_HEADS * GLA_DV), BF16)
    state = pltpu.VMEM((hps, GLA_DV, GLA_DK), F32)
    return pl.pallas_call(
        _gla_kernel,
        grid=(batch, GLA_HEADS // hps, n_lat + 1),
        in_specs=dir_specs(True) + dir_specs(False) + [
            pl.BlockSpec((2, LANES, dk), lambda b, h, s: (0, 0, h)),
            pl.BlockSpec((2, 1, dk), lambda b, h, s: (0, 0, h))],
        out_specs=[pl.BlockSpec((GLA_BLOCK, dv), lambda b, h, s: (lat_block(b, s, True), h)),
                   pl.BlockSpec((GLA_BLOCK, dv), lambda b, h, s: (lat_block(b, s, False), h))],
        out_shape=[out_shape, out_shape],
        scratch_shapes=[state, state],
        compiler_params=_params(("arbitrary", "arbitrary", "arbitrary")),
        name="gla",
    )(p, p, p, gin, p, p, p, gin, wg_pad, bg)


NA_QTOK = NA_QROWS * GRID_W
NA_SUB_TOK = NA_SUB_ROWS * GRID_W
NA_WIN = NA_WIN_ROWS * GRID_W
NA_SUBS = NA_QROWS // NA_SUB_ROWS
NA_KINDS = ((0, 0), (NA_SUB_ROWS, NA_SUB_ROWS - NA_KR // 2), (GRID_W - NA_SUB_ROWS, GRID_W - NA_WIN_ROWS))


def _na_build_bias(rpb_ref, t_ref, rows):
    n_slots = 2 * NA_KR - 1
    lane = lax.broadcasted_iota(jnp.int32, (GRID_W, LANES), 1)
    q_col = lax.broadcasted_iota(jnp.int32, (GRID_W, LANES), 0)
    k_col = lane & (GRID_W - 1)
    upper = lane >= GRID_W
    c0 = jnp.clip(q_col - NA_KC // 2, 0, GRID_W - NA_KC)
    col_ok = (k_col >= c0) & (k_col < c0 + NA_KC)
    neg = jnp.full((GRID_W, LANES), NEG_INF, F32)

    def toeplitz(hh, slot, lane_off):
        if not 0 <= slot < n_slots:
            return jnp.zeros((GRID_W, LANES), F32)
        base = jnp.broadcast_to(rpb_ref[hh, slot:slot + 1, :], (GRID_W, LANES)) * LOG2E
        return pltpu.roll(base, (LANES - (NA_KC - 1) + lane_off) % LANES, 1, stride=1, stride_axis=0)

    for hh in range(2):
        pairs = {}
        for kind, (r0, w0) in enumerate(NA_KINDS):
            for qr in range(NA_SUB_ROWS):
                band0 = min(max(r0 + qr - NA_KR // 2, 0), rows - NA_KR) - w0
                slot0 = w0 - (r0 + qr) + NA_KR - 1
                for g in range(NA_WIN_ROWS // 2):
                    slot = slot0 + 2 * g
                    lo_ok = band0 <= 2 * g < band0 + NA_KR
                    hi_ok = band0 <= 2 * g + 1 < band0 + NA_KR
                    if lo_ok or hi_ok:
                        if slot not in pairs:
                            pairs[slot] = jnp.where(upper, toeplitz(hh, slot + 1, GRID_W), toeplitz(hh, slot, 0))
                        ok = col_ok if (lo_ok and hi_ok) else (col_ok & upper if hi_ok else col_ok & ~upper)
                        tile = jnp.where(ok, pairs[slot], neg)
                    else:
                        tile = neg
                    t_ref[hh, kind, qr * GRID_W:(qr + 1) * GRID_W, g * LANES:(g + 1) * LANES] = tile


def _na_kernel(q_ref, k_ref, v_ref, kc_ref, vc_ref, rpb_ref, o_ref, t_ref, *, rows):
    b = pl.program_id(1)
    blk = pl.program_id(2)
    n_blk = rows // NA_QROWS

    pl.when((b == 0) & (blk == 0))(lambda: _na_build_bias(rpb_ref, t_ref, rows))

    nt = (((1,), (1,)), ((), ()))
    kc = kc_ref[...]
    vc = vc_ref[...]
    lane = lax.broadcasted_iota(jnp.int32, (NA_SUB_TOK, LANES), 1)
    chains = [(hh, sub) for sub in range(NA_SUBS) for hh in range(2)]
    q_head, kw, vw, kind = {}, [], [], []
    for sub in range(NA_SUBS):
        q = q_ref[sub * NA_SUB_TOK:(sub + 1) * NA_SUB_TOK, :].astype(F32) * (NA_DH ** -0.5 * LOG2E)
        q = q.astype(BF16)
        q_head[0, sub] = jnp.where(lane < NA_DH, q, jnp.zeros_like(q))
        q_head[1, sub] = jnp.where(lane >= NA_DH, q, jnp.zeros_like(q))
        r0 = blk * NA_QROWS + sub * NA_SUB_ROWS
        w0 = jnp.clip(r0 - NA_KR // 2, 0, rows - NA_WIN_ROWS)
        win = pl.ds(pl.multiple_of(w0 * GRID_W, 256), NA_WIN)
        kw.append(k_ref[win, :])
        vw.append(v_ref[win, :])
        kind.append(jnp.where(r0 == 0, 0, jnp.where(r0 == rows - NA_SUB_ROWS, 2, 1)))

    s_lat, s_ctx, p_lat, p_ctx, denom, out = {}, {}, {}, {}, {}, {}
    for hh, sub in chains:
        qs = q_head[hh, sub]
        s_lat[hh, sub] = lax.dot_general(qs, kw[sub], nt, preferred_element_type=F32) + t_ref[hh, kind[sub]]
        s_ctx[hh, sub] = lax.dot_general(qs, kc, nt, preferred_element_type=F32)
    for ch in chains:
        m = jnp.maximum(jnp.max(s_lat[ch], axis=-1, keepdims=True), jnp.max(s_ctx[ch], axis=-1, keepdims=True))
        pl_ = jnp.exp2(s_lat[ch] - m)
        pc_ = jnp.exp2(s_ctx[ch] - m)
        denom[ch] = jnp.sum(pl_, axis=-1, keepdims=True) + jnp.sum(pc_, axis=-1, keepdims=True)
        p_lat[ch] = pl_.astype(BF16)
        p_ctx[ch] = pc_.astype(BF16)
    for hh, sub in chains:
        o = (jnp.dot(p_lat[hh, sub], vw[sub], preferred_element_type=F32)
             + jnp.dot(p_ctx[hh, sub], vc, preferred_element_type=F32))
        out[hh, sub] = o / denom[hh, sub]
    for sub in range(NA_SUBS):
        rows_ = slice(sub * NA_SUB_TOK, (sub + 1) * NA_SUB_TOK)
        o_ref[rows_, :] = jnp.where(lane < NA_DH, out[0, sub], out[1, sub]).astype(o_ref.dtype)


def _na(p, rpb, batch, seq, ctx_len, col0):
    rows = seq // GRID_W
    n_blk = rows // NA_QROWS
    n_pairs = NA_HEADS // 2
    cq = col0 // LANES
    ck = cq + n_pairs
    cv = ck + n_pairs
    ctx_block0 = batch * seq // ctx_len
    return pl.pallas_call(
        functools.partial(_na_kernel, rows=rows),
        grid=(n_pairs, batch, n_blk),
        in_specs=[
            pl.BlockSpec((NA_QTOK, LANES), lambda hp, b, r: (b * n_blk + r, cq + hp)),
            pl.BlockSpec((seq, LANES), lambda hp, b, r: (b, ck + hp)),
            pl.BlockSpec((seq, LANES), lambda hp, b, r: (b, cv + hp)),
            pl.BlockSpec((ctx_len, LANES), lambda hp, b, r: (ctx_block0 + b, ck + hp)),
            pl.BlockSpec((ctx_len, LANES), lambda hp, b, r: (ctx_block0 + b, cv + hp)),
            pl.BlockSpec((2, 2 * NA_KR - 1, LANES), lambda hp, b, r: (hp, 0, 0)),
        ],
        out_specs=pl.BlockSpec((NA_QTOK, LANES), lambda hp, b, r: (b * n_blk + r, hp)),
        out_shape=jax.ShapeDtypeStruct((batch * seq, NA_HEADS * NA_DH), BF16),
        scratch_shapes=[pltpu.VMEM((2, len(NA_KINDS), NA_SUB_TOK, NA_WIN), F32)],
        compiler_params=_params(("arbitrary", "arbitrary", "arbitrary")),
        name="natten",
    )(p, p, p, p, p, rpb)


def _merge_kernel(of_ref, ob_ref, r_ref, gn_ref, b_ref, wa_ref, wb_ref, m1a_ref, m1b_ref, m2a_ref, m2b_ref,
                  o_ref, a_ref):
    for h in range(GLA_HEADS):
        cols = slice(h * GLA_DV, (h + 1) * GLA_DV)
        tot = of_ref[:, cols].astype(F32) + ob_ref[:, cols].astype(F32)
        a_ref[:, cols] = (_rms(tot, gn_ref[...]) * _silu(r_ref[:, cols].astype(F32))).astype(BF16)

    gate_refs = ((m1a_ref, m2a_ref), (m1b_ref, m2b_ref))
    gate_w = m1a_ref.shape[1]
    for c0 in range(0, o_ref.shape[1], MERGE_STRIP):
        cols = slice(c0, c0 + MERGE_STRIP)
        m1_ref, m2_ref = gate_refs[c0 // gate_w]
        gcols = slice(c0 % gate_w, c0 % gate_w + MERGE_STRIP)
        a = jnp.dot(a_ref[...], wa_ref[:, cols], preferred_element_type=F32)
        b = jnp.dot(b_ref[...], wb_ref[:, cols], preferred_element_type=F32)
        m = _sigmoid(m1_ref[:, gcols].astype(F32)) * a + _sigmoid(m2_ref[:, gcols].astype(F32)) * b
        o_ref[:, cols] = m.astype(o_ref.dtype)


def _merge(o_fwd, o_bwd, gn, o_na, w_gla_o, w_na_o, p, col_r, col_m1):
    n_rows = o_fwd.shape[0]
    v_w = o_fwd.shape[1]
    gw = D_MODEL // 2
    c1 = col_m1 // gw
    once = dict(pipeline_mode=pl.Buffered(1))
    gate_spec = lambda c: pl.BlockSpec((ROW_TILE, gw), lambda i: (i, c))
    return pl.pallas_call(
        _merge_kernel,
        grid=(n_rows // ROW_TILE,),
        in_specs=[pl.BlockSpec((ROW_TILE, v_w), lambda i: (i, 0)),
                  pl.BlockSpec((ROW_TILE, v_w), lambda i: (i, 0)),
                  pl.BlockSpec((ROW_TILE, v_w), lambda i: (i, col_r // v_w)),
                  pl.BlockSpec((1, GLA_DV), lambda i: (0, 0)),
                  pl.BlockSpec((ROW_TILE, o_na.shape[1]), lambda i: (i, 0)),
                  pl.BlockSpec(w_gla_o.shape, lambda i: (0, 0), **once),
                  pl.BlockSpec(w_na_o.shape, lambda i: (0, 0), **once),
                  gate_spec(c1), gate_spec(c1 + 1), gate_spec(c1 + 2), gate_spec(c1 + 3)],
        out_specs=pl.BlockSpec((ROW_TILE, D_MODEL), lambda i: (i, 0)),
        out_shape=jax.ShapeDtypeStruct((n_rows, D_MODEL), BF16),
        scratch_shapes=[pltpu.VMEM((ROW_TILE, v_w), BF16)],
        compiler_params=_params(("arbitrary",)),
        name="merge",
    )(o_fwd, o_bwd, p, gn, o_na, w_gla_o, w_na_o, p, p, p, p)


def _outproj_kernel(m_ref, w_ref, x_ref, mod_ref, g_ref, o_ref):
    y = _mixed_dot(m_ref[...], w_ref[...])
    o_ref[...] = x_ref[...] + mod_ref[5:6, :] * _rms(y, g_ref[3:4, :])


def _outproj(m, w_out, h1, mod, norm_g):
    n_tiles = m.shape[0] // ROW_TILE
    tiles_per_batch = n_tiles // 2
    return pl.pallas_call(
        _outproj_kernel,
        grid=(n_tiles,),
        in_specs=[pl.BlockSpec((ROW_TILE, D_MODEL), lambda i: (i, 0)),
                  pl.BlockSpec((D_MODEL, D_MODEL), lambda i: (0, 0), pipeline_mode=pl.Buffered(1)),
                  pl.BlockSpec((ROW_TILE, D_MODEL), lambda i: (i, 0)),
                  pl.BlockSpec((None, N_MOD, D_MODEL), lambda i: (i // tiles_per_batch, 0, 0)),
                  pl.BlockSpec((6, D_MODEL), lambda i: (0, 0))],
        out_specs=pl.BlockSpec((ROW_TILE, D_MODEL), lambda i: (i, 0)),
        out_shape=jax.ShapeDtypeStruct((m.shape[0], D_MODEL), F32),
        compiler_params=_params(("arbitrary",)),
        name="outproj",
    )(m, w_out, h1, mod, norm_g)


def kernel(x, c, ctx, c_ctx, w_ada, b_ada, norm_g, ffn_wg, ffn_wu, ffn_wd, w_in, gla_wg, gla_bg, gla_norm_g,
           w_gla_o, na_rpb, w_na_o, w_out):
    batch, seq, d = x.shape
    ctx_len = ctx.shape[1]
    depth = w_ada.shape[0]
    assert d == D_MODEL and batch == 2 and batch * ctx_len == ROW_TILE and depth == 1
    assert seq % ROW_TILE == 0 and seq // GRID_W == GRID_W

    qk_w = GLA_HEADS * GLA_DK
    v_w = GLA_HEADS * GLA_DV
    na_w = NA_HEADS * NA_DH
    gate0 = 2 * qk_w + 2 * v_w
    gate1 = gate0 + 2 * GLA_GATE_RANK
    col_nq = gate0
    col_m1 = col_nq + 3 * na_w

    cvec = jnp.zeros((8, d), F32).at[0:batch].set(c).at[batch].set(c_ctx)
    cos, sin = _rope_tables(seq)
    h = x.reshape(batch * seq, d)
    hc = ctx.reshape(batch * ctx_len, d)
    lat_tiles_per_batch = seq // BIG_ROW_TILE
    for l in range(depth):
        mod = _modulation(cvec, w_ada[l], b_ada[l].reshape(1, -1)).reshape(8, N_MOD, d)
        g = norm_g[l]
        h1 = _ffn(h, mod, g, ffn_wg[l], ffn_wu[l], ffn_wd[l], 0, 0, tm=BIG_ROW_TILE, tf=FF_TILE,
                  mod_row=lambda i: i // lat_tiles_per_batch)
        h1c = _ffn(hc, mod, g, ffn_wg[l], ffn_wu[l], ffn_wd[l], 0, 0, tm=ROW_TILE, tf=FF_TILE,
                   mod_row=lambda i: batch)
        p, gin = _inproj(h1, h1c, mod, g, jnp.swapaxes(w_in[l], 0, 1), cos, sin, seq, gate0, gate1)
        wg_pad = jnp.zeros((2, LANES, qk_w), F32)
        wg_pad = wg_pad.at[0, :GLA_GATE_RANK].set(gla_wg[l, 0])
        wg_pad = wg_pad.at[1, GLA_GATE_RANK:2 * GLA_GATE_RANK].set(gla_wg[l, 1])
        bg = gla_bg[l].reshape(2, 1, qk_w)
        gn = gla_norm_g[l].reshape(1, GLA_DV)
        o_fwd, o_bwd = _gla(p, gin, wg_pad, bg, batch, seq)
        rpb = jnp.pad(na_rpb[l], ((0, 0), (0, 0), (0, LANES - (2 * NA_KC - 1))))
        o_na = _na(p, rpb, batch, seq, ctx_len, col_nq)
        m = _merge(o_fwd, o_bwd, gn, o_na, w_gla_o[l].astype(BF16), w_na_o[l].astype(BF16), p,
                   2 * qk_w + v_w, col_m1)
        h2 = _outproj(m, w_out[l], h1, mod, g)
        h = _ffn(h2, mod, g, ffn_wg[l], ffn_wu[l], ffn_wd[l], 1, 2, tm=BIG_ROW_TILE, tf=FF_TILE,
                 mod_row=lambda i: i // lat_tiles_per_batch)
    return h.reshape(batch, seq, d)
```

```python
import functools

import numpy as np
import jax
import jax.numpy as jnp
from jax import lax
from jax.experimental import pallas as pl
from jax.experimental.pallas import tpu as pltpu

F32 = jnp.float32
BF16 = jnp.bfloat16

D_MODEL = 2048
GRID_W = 64
GLA_HEADS = 4
GLA_DK = 256
GLA_DV = 512
GLA_GATE_RANK = 16
GLA_GATE_TEMP = 16.0
NA_HEADS = 16
NA_DH = 64
NA_KR = 8
NA_KC = 16
ROPE_THETA = 10000.0
EPS = 1e-6
NEG_INF = -1e30
N_MOD = 9
LOG2E = 1.4426950408889634

LANES = 128
ROW_TILE = 512
BIG_ROW_TILE = 1024
FF_TILE = 512
PROJ_TILE = 1024
MERGE_STRIP = 512
GLA_BLOCK = 256
GLA_HEADS_PER_STEP = 4
NA_QROWS = 16
NA_SUB_ROWS = 4
NA_WIN_ROWS = 12
VMEM_LIMIT = 56 * 1024 * 1024
VMEM_LIMIT_FFN = 60 * 1024 * 1024


def _params(sem, vmem_limit=VMEM_LIMIT):
    return pltpu.CompilerParams(dimension_semantics=sem, vmem_limit_bytes=vmem_limit)


def _rms(x, g):
    return x * lax.rsqrt(jnp.mean(x * x, axis=-1, keepdims=True) + EPS) * g


def _sigmoid(x):
    return 0.5 * jnp.tanh(0.5 * x) + 0.5


def _silu(x):
    h = 0.5 * x
    return h * jnp.tanh(h) + h


def _mod_kernel(c_ref, w_ref, b_ref, o_ref):
    o_ref[...] = jnp.dot(_silu(c_ref[...]), w_ref[...], preferred_element_type=F32) + b_ref[...]


def _modulation(cvec, w_ada, b_ada):
    n = w_ada.shape[1]
    tn = 1024
    return pl.pallas_call(
        _mod_kernel,
        grid=(n // tn,),
        in_specs=[pl.BlockSpec((8, D_MODEL), lambda j: (0, 0)),
                  pl.BlockSpec((D_MODEL, tn), lambda j: (0, j)),
                  pl.BlockSpec((1, tn), lambda j: (0, j))],
        out_specs=pl.BlockSpec((8, tn), lambda j: (0, j)),
        out_shape=jax.ShapeDtypeStruct((8, n), F32),
        compiler_params=_params(("arbitrary",)),
        name="adaln_mod",
    )(cvec, w_ada, b_ada)


def _mixed_dot(a, w):
    return lax.dot_general(a, w, (((1,), (0,)), ((), ())), preferred_element_type=F32)


def _ffn_kernel(x_ref, mod_ref, g_ref, wg_ref, wu_ref, wd_ref, o_hbm, h_ref, acc_ref, sem, *, sub):
    i = pl.program_id(0)
    f = pl.program_id(1)
    n_tiles = pl.num_programs(0)
    last = pl.num_programs(1) - 1
    tm = x_ref.shape[0]
    subtiles = [pl.ds(r0, ROW_TILE) for r0 in range(0, tm, ROW_TILE)]

    def out_copy(k, tile):
        return pltpu.make_async_copy(
            acc_ref.at[subtiles[k], :], o_hbm.at[pl.ds(tile * tm + k * ROW_TILE, ROW_TILE), :], sem.at[k])

    def gate_up(h):
        g = _mixed_dot(h, wg_ref[...])
        u = _mixed_dot(h, wu_ref[...])
        return (_silu(g) * u).astype(BF16)

    @pl.when((i == 0) & (f == 0))
    def _():
        acc_ref[...] = jnp.zeros_like(acc_ref)
        for k in range(len(subtiles)):
            out_copy(k, 0).start()

    @pl.when(f == 0)
    def _():
        gain = g_ref[2 * sub:2 * sub + 1, :] * (1.0 + mod_ref[3 * sub + 1:3 * sub + 2, :])
        shift = mod_ref[3 * sub:3 * sub + 1, :]
        for k, rows in enumerate(subtiles):
            h = (_rms(x_ref[rows, :], gain) + shift).astype(BF16)
            h_ref[rows, :] = h
            a = gate_up(h)
            out_copy(k, jnp.maximum(i - 1, 0)).wait()
            acc_ref[rows, :] = _mixed_dot(a, wd_ref[...])

    @pl.when((f > 0) & (f < last))
    def _():
        for rows in subtiles:
            acc_ref[rows, :] += _mixed_dot(gate_up(h_ref[rows, :]), wd_ref[...])

    @pl.when(f == last)
    def _():
        gain = 0.5 * mod_ref[3 * sub + 2:3 * sub + 3, :] * g_ref[2 * sub + 1:2 * sub + 2, :]
        for k, rows in enumerate(subtiles):
            y = acc_ref[rows, :] + _mixed_dot(gate_up(h_ref[rows, :]), wd_ref[...])
            acc_ref[rows, :] = x_ref[rows, :] + _rms(y, gain)
            out_copy(k, i).start()

    @pl.when((f == last) & (i == n_tiles - 1))
    def _():
        for k in range(len(subtiles)):
            out_copy(k, i).wait()


def _ffn(x2d, mod, norm_g, wg, wu, wd, which, sub, *, tm, tf, mod_row):
    n_tiles = x2d.shape[0] // tm
    d_ff = wg.shape[2]
    return pl.pallas_call(
        functools.partial(_ffn_kernel, sub=sub),
        grid=(n_tiles, d_ff // tf),
        in_specs=[
            pl.BlockSpec((tm, D_MODEL), lambda i, f: (i, 0)),
            pl.BlockSpec((None, N_MOD, D_MODEL), lambda i, f: (mod_row(i), 0, 0)),
            pl.BlockSpec((6, D_MODEL), lambda i, f: (0, 0)),
            pl.BlockSpec((None, D_MODEL, tf), lambda i, f: (which, 0, f)),
            pl.BlockSpec((None, D_MODEL, tf), lambda i, f: (which, 0, f)),
            pl.BlockSpec((None, tf, D_MODEL), lambda i, f: (which, f, 0)),
        ],
        out_specs=pl.BlockSpec(memory_space=pl.ANY),
        out_shape=jax.ShapeDtypeStruct(x2d.shape, F32),
        scratch_shapes=[pltpu.VMEM((tm, D_MODEL), BF16), pltpu.VMEM((tm, D_MODEL), F32),
                        pltpu.SemaphoreType.DMA((tm // ROW_TILE,))],
        compiler_params=_params(("arbitrary", "arbitrary"), VMEM_LIMIT_FFN),
        name="ffn%d_%d" % (sub, tm),
    )(x2d, mod, norm_g, wg, wu, wd)


N_ROPE_TILES = 2 * GLA_HEADS * GLA_DK // PROJ_TILE
N_Q_TILES = N_ROPE_TILES // 2


def _rope_tables(seq):
    half = GLA_DK // 4
    freqs = ROPE_THETA ** (-np.arange(half, dtype=np.float64) / half)
    t = np.arange(seq)
    cos_parts, sin_parts = [], []
    for pos in (t // GRID_W, t % GRID_W):
        ang = pos[:, None].astype(np.float64) * freqs
        cos_parts += [np.cos(ang), np.cos(ang)]
        sin_parts += [-np.sin(ang), np.sin(ang)]
    cos = np.concatenate(cos_parts, axis=1)
    sin = np.concatenate(sin_parts, axis=1)
    cos = np.concatenate([cos, np.ones((BIG_ROW_TILE, GLA_DK))], axis=0)
    sin = np.concatenate([sin, np.zeros((BIG_ROW_TILE, GLA_DK))], axis=0)
    return jnp.asarray(cos, F32), jnp.asarray(sin, F32)


def _dot_nt(a, w_t):
    return lax.dot_general(a, w_t, (((1,), (1,)), ((), ())), preferred_element_type=F32)


def _inproj_kernel(x_ref, xc_ref, mod_ref, g_ref, w_ref, wgate_ref, cos_ref, sin_ref, p_ref, gate_ref, h_ref,
                   *, n_lat_tiles, ctx_tiles):
    i = pl.program_id(0)
    n = pl.program_id(1)

    def project(src_ref, live):
        r = slice(0, src_ref.shape[0])
        alive = (lambda cond: cond) if live is None else (lambda cond: cond & live)

        @pl.when(n == 0)
        def _():
            h = _rms(src_ref[...], g_ref[2:3, :]) * (1.0 + mod_ref[4:5, :]) + mod_ref[3:4, :]
            h_ref[r, :] = h.astype(BF16)
            gate = _dot_nt(h_ref[r, :], wgate_ref[...])
            gate_ref[r, :] = jnp.concatenate(
                [gate, jnp.zeros((gate.shape[0], LANES - gate.shape[1]), F32)], axis=1)

        @pl.when(alive(n < N_ROPE_TILES))
        def _():
            y = _dot_nt(h_ref[r, :], w_ref[...])
            cos = cos_ref[r, :]
            sin = sin_ref[r, :]
            qscale = jnp.where(n < N_Q_TILES, GLA_DK ** -0.5, 1.0).astype(F32)
            for j in range(PROJ_TILE // GLA_DK):
                yj = y[:, j * GLA_DK:(j + 1) * GLA_DK]
                swapped = jnp.concatenate(
                    [pltpu.roll(yj[:, g * LANES:(g + 1) * LANES], LANES // 2, 1) for g in range(GLA_DK // LANES)],
                    axis=1)
                p_ref[r, j * GLA_DK:(j + 1) * GLA_DK] = ((yj * cos + swapped * sin) * qscale).astype(BF16)

        @pl.when(alive(n >= N_ROPE_TILES))
        def _():
            p_ref[r, :] = _dot_nt(h_ref[r, :], w_ref[...]).astype(BF16)

        if live is not None:
            @pl.when(jnp.logical_not(live))
            def _():
                p_ref[r, :] = jnp.zeros((src_ref.shape[0], PROJ_TILE), BF16)

    ctx_live = functools.reduce(jnp.logical_or, [n == t for t in ctx_tiles])
    pl.when(i < n_lat_tiles)(lambda: project(x_ref, None))
    pl.when(i >= n_lat_tiles)(lambda: project(xc_ref, ctx_live))


def _inproj(h1, h1c, mod, norm_g, w_in_t, cos, sin, seq, gate0, gate1, ctx_cols):
    tm = BIG_ROW_TILE
    n_lat_tiles = h1.shape[0] // tm
    n_rows = h1.shape[0] + h1c.shape[0]
    tiles_per_batch = n_lat_tiles // 2
    n_a_tiles = gate0 // PROJ_TILE
    n_b_tiles = (w_in_t.shape[0] - gate1) // PROJ_TILE
    n_gate = gate1 - gate0
    ctx_tiles = [t for t in range(n_a_tiles + n_b_tiles)
                 if any(lo < (t + 1) * PROJ_TILE and t * PROJ_TILE < hi for lo, hi in ctx_cols)]

    def weight_row0(i, n):
        t = jnp.int32(ctx_tiles[0])
        for live in ctx_tiles[1:]:
            t = jnp.where(n >= live, live, t)
        t = jnp.where(i < n_lat_tiles, n, t)
        return pl.multiple_of(t * PROJ_TILE + jnp.where(t < n_a_tiles, 0, n_gate), n_gate), 0

    tab_spec = pl.BlockSpec(
        (tm, GLA_DK), lambda i, n: (jnp.where(i < n_lat_tiles, i % tiles_per_batch, seq // tm), 0))
    return pl.pallas_call(
        functools.partial(_inproj_kernel, n_lat_tiles=n_lat_tiles, ctx_tiles=ctx_tiles),
        grid=(n_lat_tiles + 1, n_a_tiles + n_b_tiles),
        in_specs=[pl.BlockSpec((tm, D_MODEL), lambda i, n: (jnp.minimum(i, n_lat_tiles - 1), 0)),
                  pl.BlockSpec(h1c.shape, lambda i, n: (0, 0)),
                  pl.BlockSpec((None, N_MOD, D_MODEL), lambda i, n: (jnp.minimum(i // tiles_per_batch, 2), 0, 0)),
                  pl.BlockSpec((6, D_MODEL), lambda i, n: (0, 0)),
                  pl.BlockSpec((pl.Element(PROJ_TILE), pl.Element(D_MODEL)), weight_row0),
                  pl.BlockSpec((n_gate, D_MODEL), lambda i, n: (gate0 // n_gate, 0)),
                  tab_spec, tab_spec],
        out_specs=[pl.BlockSpec((tm, PROJ_TILE), lambda i, n: (i, n)),
                   pl.BlockSpec((tm, LANES), lambda i, n: (i, 0))],
        out_shape=[jax.ShapeDtypeStruct((n_rows, (n_a_tiles + n_b_tiles) * PROJ_TILE), BF16),
                   jax.ShapeDtypeStruct((n_rows, LANES), F32)],
        scratch_shapes=[pltpu.VMEM((tm, D_MODEL), BF16)],
        compiler_params=_params(("arbitrary", "arbitrary")),
        name="inproj",
    )(h1, h1c, mod, norm_g, w_in_t, w_in_t, cos, sin)


def _log2_sigmoid(z):
    t = z * LOG2E
    return jnp.minimum(t, 0.0) - jnp.log2(1.0 + jnp.exp2(-jnp.abs(t)))


def _gla_kernel(qf_ref, kf_ref, vf_ref, gf_ref, qb_ref, kb_ref, vb_ref, gb_ref, wg_ref, bg_ref,
                of_ref, ob_ref, stf_ref, stb_ref):
    @pl.when(pl.program_id(2) == 0)
    def _():
        stf_ref[...] = jnp.zeros_like(stf_ref)
        stb_ref[...] = jnp.zeros_like(stb_ref)

    c = GLA_BLOCK
    nt = (((1,), (1,)), ((), ()))
    row = lax.broadcasted_iota(jnp.int32, (c, c), 0)
    col = lax.broadcasted_iota(jnp.int32, (c, c), 1)
    mask = {True: col <= row, False: col >= row}
    tri = {fwd: jnp.where(m, 1.0, 0.0).astype(BF16) for fwd, m in mask.items()}
    refs = {True: (qf_ref, kf_ref, vf_ref, gf_ref, stf_ref, of_ref),
            False: (qb_ref, kb_ref, vb_ref, gb_ref, stb_ref, ob_ref)}
    qcols = lambda head: slice(head * GLA_DK, (head + 1) * GLA_DK)
    vcols = lambda head: slice(head * GLA_DV, (head + 1) * GLA_DV)

    state, log_a, cum, factors, att = {}, {}, {}, {}, {}

    def gate_stage(head, fwd):
        d = 0 if fwd else 1
        state[head, fwd] = refs[fwd][4][head]
        z = jnp.dot(refs[fwd][3][...], wg_ref[d, :, qcols(head)], preferred_element_type=F32)
        log_a[head, fwd] = _log2_sigmoid(z + bg_ref[d, :, qcols(head)]) * (1.0 / GLA_GATE_TEMP)

    def cumsum_stage(head, fwd):
        la = log_a[head, fwd]
        hi = la.astype(BF16)
        lo = (la - hi.astype(F32)).astype(BF16)
        cum[head, fwd] = (jnp.dot(tri[fwd], hi, preferred_element_type=F32)
                          + jnp.dot(tri[fwd], lo, preferred_element_type=F32))

    def factor_stage(head, fwd):
        cm = cum[head, fwd]
        cum_end = cm[c - 1:c, :] if fwd else cm[0:1, :]
        cum_mid = cm[c // 2:c // 2 + 1, :]
        q = refs[fwd][0][:, qcols(head)]
        k = refs[fwd][1][:, qcols(head)]
        factors[head, fwd] = (q * jnp.exp2(cm - cum_mid).astype(BF16),
                              k * jnp.exp2(cum_mid - cm).astype(BF16),
                              q * jnp.exp2(cm).astype(BF16),
                              k * jnp.exp2(cum_end - cm).astype(BF16),
                              jnp.exp2(cum_end))

    def att_stage(head, fwd):
        q_att, k_att = factors[head, fwd][:2]
        a = lax.dot_general(q_att, k_att, nt, preferred_element_type=F32)
        att[head, fwd] = jnp.where(mask[fwd], a, 0.0).astype(BF16)

    def out_stage(head, fwd):
        _, _, q_dec, k_end, decay = factors[head, fwd]
        v = refs[fwd][2][:, vcols(head)]
        st = state[head, fwd]
        o = (jnp.dot(att[head, fwd], v, preferred_element_type=F32)
             + lax.dot_general(q_dec, st.astype(BF16), nt, preferred_element_type=F32))
        kv_t = lax.dot_general(v, k_end, (((0,), (0,)), ((), ())), preferred_element_type=F32)
        refs[fwd][5][:, vcols(head)] = o.astype(BF16)
        refs[fwd][4][head] = decay * st + kv_t

    for stage in (gate_stage, cumsum_stage, factor_stage, att_stage, out_stage):
        for head in range(GLA_HEADS_PER_STEP):
            for fwd in (True, False):
                stage(head, fwd)


def _gla(p, gin, wg_pad, bg, batch, seq):
    n_lat = seq // GLA_BLOCK
    ctx_block0 = batch * n_lat
    hps = GLA_HEADS_PER_STEP
    dk, dv = hps * GLA_DK, hps * GLA_DV

    def lat_block(b, s, forward):
        j = jnp.maximum(s - 1, 0)
        return b * n_lat + (j if forward else n_lat - 1 - j)

    def in_block(b, s, forward):
        return jnp.where(s == 0, ctx_block0 + b, lat_block(b, s, forward))

    kq = GLA_HEADS // hps
    kv = 2 * GLA_HEADS * GLA_DK // dv

    def dir_specs(forward):
        return [
            pl.BlockSpec((GLA_BLOCK, dk), lambda b, h, s: (in_block(b, s, forward), h)),
            pl.BlockSpec((GLA_BLOCK, dk), lambda b, h, s: (in_block(b, s, forward), kq + h)),
            pl.BlockSpec((GLA_BLOCK, dv), lambda b, h, s: (in_block(b, s, forward), kv + h)),
            pl.BlockSpec((GLA_BLOCK, LANES), lambda b, h, s: (in_block(b, s, forward), 0)),
        ]

    out_shape = jax.ShapeDtypeStruct((batch * seq, GLA_HEADS * GLA_DV), BF16)
    state = pltpu.VMEM((hps, GLA_DV, GLA_DK), F32)
    return pl.pallas_call(
        _gla_kernel,
        grid=(batch, GLA_HEADS // hps, n_lat + 1),
        in_specs=dir_specs(True) + dir_specs(False) + [
            pl.BlockSpec((2, LANES, dk), lambda b, h, s: (0, 0, h)),
            pl.BlockSpec((2, 1, dk), lambda b, h, s: (0, 0, h))],
        out_specs=[pl.BlockSpec((GLA_BLOCK, dv), lambda b, h, s: (lat_block(b, s, True), h)),
                   pl.BlockSpec((GLA_BLOCK, dv), lambda b, h, s: (lat_block(b, s, False), h))],
        out_shape=[out_shape, out_shape],
        scratch_shapes=[state, state],
        compiler_params=_params(("arbitrary", "arbitrary", "arbitrary")),
        name="gla",
    )(p, p, p, gin, p, p, p, gin, wg_pad, bg)


NA_QTOK = NA_QROWS * GRID_W
NA_SUB_TOK = NA_SUB_ROWS * GRID_W
NA_WIN = NA_WIN_ROWS * GRID_W
NA_SUBS = NA_QROWS // NA_SUB_ROWS
NA_KINDS = ((0, 0), (NA_SUB_ROWS, NA_SUB_ROWS - NA_KR // 2), (GRID_W - NA_SUB_ROWS, GRID_W - NA_WIN_ROWS))


def _na_build_bias(rpb_ref, t_ref, rows):
    n_slots = 2 * NA_KR - 1
    lane = lax.broadcasted_iota(jnp.int32, (GRID_W, LANES), 1)
    q_col = lax.broadcasted_iota(jnp.int32, (GRID_W, LANES), 0)
    k_col = lane & (GRID_W - 1)
    upper = lane >= GRID_W
    c0 = jnp.clip(q_col - NA_KC // 2, 0, GRID_W - NA_KC)
    col_ok = (k_col >= c0) & (k_col < c0 + NA_KC)
    neg = jnp.full((GRID_W, LANES), NEG_INF, F32)

    def toeplitz(hh, slot, lane_off):
        if not 0 <= slot < n_slots:
            return jnp.zeros((GRID_W, LANES), F32)
        base = jnp.broadcast_to(rpb_ref[hh, slot:slot + 1, :], (GRID_W, LANES)) * LOG2E
        return pltpu.roll(base, (LANES - (NA_KC - 1) + lane_off) % LANES, 1, stride=1, stride_axis=0)

    for hh in range(2):
        pairs = {}
        for kind, (r0, w0) in enumerate(NA_KINDS):
            for qr in range(NA_SUB_ROWS):
                band0 = min(max(r0 + qr - NA_KR // 2, 0), rows - NA_KR) - w0
                slot0 = w0 - (r0 + qr) + NA_KR - 1
                for g in range(NA_WIN_ROWS // 2):
                    slot = slot0 + 2 * g
                    lo_ok = band0 <= 2 * g < band0 + NA_KR
                    hi_ok = band0 <= 2 * g + 1 < band0 + NA_KR
                    if lo_ok or hi_ok:
                        if slot not in pairs:
                            pairs[slot] = jnp.where(upper, toeplitz(hh, slot + 1, GRID_W), toeplitz(hh, slot, 0))
                        ok = col_ok if (lo_ok and hi_ok) else (col_ok & upper if hi_ok else col_ok & ~upper)
                        tile = jnp.where(ok, pairs[slot], neg)
                    else:
                        tile = neg
                    t_ref[hh, kind, qr * GRID_W:(qr + 1) * GRID_W, g * LANES:(g + 1) * LANES] = tile


def _na_kernel(q_ref, k_ref, v_ref, kc_ref, vc_ref, rpb_ref, o_ref, t_ref, *, rows):
    b = pl.program_id(1)
    blk = pl.program_id(2)
    n_blk = rows // NA_QROWS

    pl.when((b == 0) & (blk == 0))(lambda: _na_build_bias(rpb_ref, t_ref, rows))

    nt = (((1,), (1,)), ((), ()))
    kc = kc_ref[...]
    vc = vc_ref[...]
    lane = lax.broadcasted_iota(jnp.int32, (NA_SUB_TOK, LANES), 1)
    chains = [(hh, sub) for sub in range(NA_SUBS) for hh in range(2)]
    q_head, kw, vw, kind = {}, [], [], []
    for sub in range(NA_SUBS):
        q = q_ref[sub * NA_SUB_TOK:(sub + 1) * NA_SUB_TOK, :].astype(F32) * (NA_DH ** -0.5 * LOG2E)
        q = q.astype(BF16)
        q_head[0, sub] = jnp.where(lane < NA_DH, q, jnp.zeros_like(q))
        q_head[1, sub] = jnp.where(lane >= NA_DH, q, jnp.zeros_like(q))
        r0 = blk * NA_QROWS + sub * NA_SUB_ROWS
        w0 = jnp.clip(r0 - NA_KR // 2, 0, rows - NA_WIN_ROWS)
        win = pl.ds(pl.multiple_of(w0 * GRID_W, 256), NA_WIN)
        kw.append(k_ref[win, :])
        vw.append(v_ref[win, :])
        kind.append(jnp.where(r0 == 0, 0, jnp.where(r0 == rows - NA_SUB_ROWS, 2, 1)))

    s_lat, s_ctx, p_lat, p_ctx, denom, out = {}, {}, {}, {}, {}, {}
    for hh, sub in chains:
        qs = q_head[hh, sub]
        s_lat[hh, sub] = lax.dot_general(qs, kw[sub], nt, preferred_element_type=F32) + t_ref[hh, kind[sub]]
        s_ctx[hh, sub] = lax.dot_general(qs, kc, nt, preferred_element_type=F32)
    for ch in chains:
        m = jnp.maximum(jnp.max(s_lat[ch], axis=-1, keepdims=True), jnp.max(s_ctx[ch], axis=-1, keepdims=True))
        pl_ = jnp.exp2(s_lat[ch] - m)
        pc_ = jnp.exp2(s_ctx[ch] - m)
        denom[ch] = jnp.sum(pl_, axis=-1, keepdims=True) + jnp.sum(pc_, axis=-1, keepdims=True)
        p_lat[ch] = pl_.astype(BF16)
        p_ctx[ch] = pc_.astype(BF16)
    for hh, sub in chains:
        o = (jnp.dot(p_lat[hh, sub], vw[sub], preferred_element_type=F32)
             + jnp.dot(p_ctx[hh, sub], vc, preferred_element_type=F32))
        out[hh, sub] = o / denom[hh, sub]
    for sub in range(NA_SUBS):
        rows_ = slice(sub * NA_SUB_TOK, (sub + 1) * NA_SUB_TOK)
        o_ref[rows_, :] = jnp.where(lane < NA_DH, out[0, sub], out[1, sub]).astype(o_ref.dtype)


def _na(p, rpb, batch, seq, ctx_len, col0):
    rows = seq // GRID_W
    n_blk = rows // NA_QROWS
    n_pairs = NA_HEADS // 2
    cq = col0 // LANES
    ck = cq + n_pairs
    cv = ck + n_pairs
    ctx_block0 = batch * seq // ctx_len
    return pl.pallas_call(
        functools.partial(_na_kernel, rows=rows),
        grid=(n_pairs, batch, n_blk),
        in_specs=[
            pl.BlockSpec((NA_QTOK, LANES), lambda hp, b, r: (b * n_blk + r, cq + hp)),
            pl.BlockSpec((seq, LANES), lambda hp, b, r: (b, ck + hp)),
            pl.BlockSpec((seq, LANES), lambda hp, b, r: (b, cv + hp)),
            pl.BlockSpec((ctx_len, LANES), lambda hp, b, r: (ctx_block0 + b, ck + hp)),
            pl.BlockSpec((ctx_len, LANES), lambda hp, b, r: (ctx_block0 + b, cv + hp)),
            pl.BlockSpec((2, 2 * NA_KR - 1, LANES), lambda hp, b, r: (hp, 0, 0)),
        ],
        out_specs=pl.BlockSpec((NA_QTOK, LANES), lambda hp, b, r: (b * n_blk + r, hp)),
        out_shape=jax.ShapeDtypeStruct((batch * seq, NA_HEADS * NA_DH), BF16),
        scratch_shapes=[pltpu.VMEM((2, len(NA_KINDS), NA_SUB_TOK, NA_WIN), F32)],
        compiler_params=_params(("arbitrary", "arbitrary", "arbitrary")),
        name="natten",
    )(p, p, p, p, p, rpb)


def _merge_kernel(of_ref, ob_ref, r_ref, gn_ref, b_ref, wa_ref, wb_ref, m1a_ref, m1b_ref, m2a_ref, m2b_ref,
                  o_ref, a_ref):
    for h in range(GLA_HEADS):
        cols = slice(h * GLA_DV, (h + 1) * GLA_DV)
        tot = of_ref[:, cols].astype(F32) + ob_ref[:, cols].astype(F32)
        a_ref[:, cols] = (_rms(tot, gn_ref[...]) * _silu(r_ref[:, cols].astype(F32))).astype(BF16)

    gate_refs = ((m1a_ref, m2a_ref), (m1b_ref, m2b_ref))
    gate_w = m1a_ref.shape[1]
    for c0 in range(0, o_ref.shape[1], MERGE_STRIP):
        cols = slice(c0, c0 + MERGE_STRIP)
        m1_ref, m2_ref = gate_refs[c0 // gate_w]
        gcols = slice(c0 % gate_w, c0 % gate_w + MERGE_STRIP)
        a = jnp.dot(a_ref[...], wa_ref[:, cols], preferred_element_type=F32)
        b = jnp.dot(b_ref[...], wb_ref[:, cols], preferred_element_type=F32)
        m = _sigmoid(m1_ref[:, gcols].astype(F32)) * a + _sigmoid(m2_ref[:, gcols].astype(F32)) * b
        o_ref[:, cols] = m.astype(o_ref.dtype)


def _merge(o_fwd, o_bwd, gn, o_na, w_gla_o, w_na_o, p, col_r, col_m1):
    n_rows = o_fwd.shape[0]
    v_w = o_fwd.shape[1]
    gw = D_MODEL // 2
    c1 = col_m1 // gw
    once = dict(pipeline_mode=pl.Buffered(1))
    gate_spec = lambda c: pl.BlockSpec((ROW_TILE, gw), lambda i: (i, c))
    return pl.pallas_call(
        _merge_kernel,
        grid=(n_rows // ROW_TILE,),
        in_specs=[pl.BlockSpec((ROW_TILE, v_w), lambda i: (i, 0)),
                  pl.BlockSpec((ROW_TILE, v_w), lambda i: (i, 0)),
                  pl.BlockSpec((ROW_TILE, v_w), lambda i: (i, col_r // v_w)),
                  pl.BlockSpec((1, GLA_DV), lambda i: (0, 0)),
                  pl.BlockSpec((ROW_TILE, o_na.shape[1]), lambda i: (i, 0)),
                  pl.BlockSpec(w_gla_o.shape, lambda i: (0, 0), **once),
                  pl.BlockSpec(w_na_o.shape, lambda i: (0, 0), **once),
                  gate_spec(c1), gate_spec(c1 + 1), gate_spec(c1 + 2), gate_spec(c1 + 3)],
        out_specs=pl.BlockSpec((ROW_TILE, D_MODEL), lambda i: (i, 0)),
        out_shape=jax.ShapeDtypeStruct((n_rows, D_MODEL), BF16),
        scratch_shapes=[pltpu.VMEM((ROW_TILE, v_w), BF16)],
        compiler_params=_params(("arbitrary",)),
        name="merge",
    )(o_fwd, o_bwd, p, gn, o_na, w_gla_o, w_na_o, p, p, p, p)


def _outproj_kernel(m_ref, w_ref, x_ref, mod_ref, g_ref, o_ref):
    y = _mixed_dot(m_ref[...], w_ref[...])
    o_ref[...] = x_ref[...] + mod_ref[5:6, :] * _rms(y, g_ref[3:4, :])


def _outproj(m, w_out, h1, mod, norm_g):
    n_tiles = m.shape[0] // ROW_TILE
    tiles_per_batch = n_tiles // 2
    return pl.pallas_call(
        _outproj_kernel,
        grid=(n_tiles,),
        in_specs=[pl.BlockSpec((ROW_TILE, D_MODEL), lambda i: (i, 0)),
                  pl.BlockSpec((D_MODEL, D_MODEL), lambda i: (0, 0), pipeline_mode=pl.Buffered(1)),
                  pl.BlockSpec((ROW_TILE, D_MODEL), lambda i: (i, 0)),
                  pl.BlockSpec((None, N_MOD, D_MODEL), lambda i: (i // tiles_per_batch, 0, 0)),
                  pl.BlockSpec((6, D_MODEL), lambda i: (0, 0))],
        out_specs=pl.BlockSpec((ROW_TILE, D_MODEL), lambda i: (i, 0)),
        out_shape=jax.ShapeDtypeStruct((m.shape[0], D_MODEL), F32),
        compiler_params=_params(("arbitrary",)),
        name="outproj",
    )(m, w_out, h1, mod, norm_g)


def kernel(x, c, ctx, c_ctx, w_ada, b_ada, norm_g, ffn_wg, ffn_wu, ffn_wd, w_in, gla_wg, gla_bg, gla_norm_g,
           w_gla_o, na_rpb, w_na_o, w_out):
    batch, seq, d = x.shape
    ctx_len = ctx.shape[1]
    depth = w_ada.shape[0]
    assert d == D_MODEL and batch == 2 and batch * ctx_len == ROW_TILE and depth == 1
    assert seq % ROW_TILE == 0 and seq // GRID_W == GRID_W

    qk_w = GLA_HEADS * GLA_DK
    v_w = GLA_HEADS * GLA_DV
    na_w = NA_HEADS * NA_DH
    gate0 = 2 * qk_w + 2 * v_w
    gate1 = gate0 + 2 * GLA_GATE_RANK
    col_nq = gate0
    col_m1 = col_nq + 3 * na_w

    cvec = jnp.zeros((8, d), F32).at[0:batch].set(c).at[batch].set(c_ctx)
    cos, sin = _rope_tables(seq)
    h = x.reshape(batch * seq, d)
    hc = ctx.reshape(batch * ctx_len, d)
    lat_tiles_per_batch = seq // BIG_ROW_TILE
    for l in range(depth):
        mod = _modulation(cvec, w_ada[l], b_ada[l].reshape(1, -1)).reshape(8, N_MOD, d)
        g = norm_g[l]
        h1 = _ffn(h, mod, g, ffn_wg[l], ffn_wu[l], ffn_wd[l], 0, 0, tm=BIG_ROW_TILE, tf=FF_TILE,
                  mod_row=lambda i: i // lat_tiles_per_batch)
        h1c = _ffn(hc, mod, g, ffn_wg[l], ffn_wu[l], ffn_wd[l], 0, 0, tm=ROW_TILE, tf=FF_TILE,
                   mod_row=lambda i: batch)
        ctx_cols = [(qk_w, 2 * qk_w + v_w), (col_nq + na_w, col_nq + 3 * na_w)]
        p, gin = _inproj(h1, h1c, mod, g, jnp.swapaxes(w_in[l], 0, 1), cos, sin, seq, gate0, gate1, ctx_cols)
        wg_pad = jnp.zeros((2, LANES, qk_w), F32)
        wg_pad = wg_pad.at[0, :GLA_GATE_RANK].set(gla_wg[l, 0])
        wg_pad = wg_pad.at[1, GLA_GATE_RANK:2 * GLA_GATE_RANK].set(gla_wg[l, 1])
        bg = gla_bg[l].reshape(2, 1, qk_w)
        gn = gla_norm_g[l].reshape(1, GLA_DV)
        o_fwd, o_bwd = _gla(p, gin, wg_pad, bg, batch, seq)
        rpb = jnp.pad(na_rpb[l], ((0, 0), (0, 0), (0, LANES - (2 * NA_KC - 1))))
        o_na = _na(p, rpb, batch, seq, ctx_len, col_nq)
        m = _merge(o_fwd, o_bwd, gn, o_na, w_gla_o[l].astype(BF16), w_na_o[l].astype(BF16), p,
                   2 * qk_w + v_w, col_m1)
        h2 = _outproj(m, w_out[l], h1, mod, g)
        h = _ffn(h2, mod, g, ffn_wg[l], ffn_wu[l], ffn_wd[l], 1, 2, tm=BIG_ROW_TILE, tf=FF_TILE,
                 mod_row=lambda i: i // lat_tiles_per_batch)
    return h.reshape(batch, seq, d)
```

```python
import functools

import numpy as np
import jax
import jax.numpy as jnp
from jax import lax
from jax.experimental import pallas as pl
from jax.experimental.pallas import tpu as pltpu

F32 = jnp.float32
BF16 = jnp.bfloat16

D_MODEL = 2048
GRID_W = 64
GLA_HEADS = 4
GLA_DK = 256
GLA_DV = 512
GLA_GATE_RANK = 16
GLA_GATE_TEMP = 16.0
NA_HEADS = 16
NA_DH = 64
NA_KR = 8
NA_KC = 16
ROPE_THETA = 10000.0
EPS = 1e-6
NEG_INF = -1e30
N_MOD = 9
LOG2E = 1.4426950408889634

LANES = 128
ROW_TILE = 512
BIG_ROW_TILE = 1024
FF_TILE = 512
PROJ_TILE = 1024
MERGE_STRIP = 512
GLA_BLOCK = 256
GLA_HEADS_PER_STEP = 4
NA_QROWS = 16
NA_SUB_ROWS = 4
NA_WIN_ROWS = 12
VMEM_LIMIT = 56 * 1024 * 1024
VMEM_LIMIT_FFN = 60 * 1024 * 1024


def _params(sem, vmem_limit=VMEM_LIMIT):
    return pltpu.CompilerParams(dimension_semantics=sem, vmem_limit_bytes=vmem_limit)


def _rms(x, g):
    return x * lax.rsqrt(jnp.mean(x * x, axis=-1, keepdims=True) + EPS) * g


def _sigmoid(x):
    return 0.5 * jnp.tanh(0.5 * x) + 0.5


def _silu(x):
    h = 0.5 * x
    return h * jnp.tanh(h) + h


def _mod_kernel(c_ref, w_ref, b_ref, o_ref):
    o_ref[...] = jnp.dot(_silu(c_ref[...]), w_ref[...], preferred_element_type=F32) + b_ref[...]


def _modulation(cvec, w_ada, b_ada):
    n = w_ada.shape[1]
    tn = 1024
    return pl.pallas_call(
        _mod_kernel,
        grid=(n // tn,),
        in_specs=[pl.BlockSpec((8, D_MODEL), lambda j: (0, 0)),
                  pl.BlockSpec((D_MODEL, tn), lambda j: (0, j)),
                  pl.BlockSpec((1, tn), lambda j: (0, j))],
        out_specs=pl.BlockSpec((8, tn), lambda j: (0, j)),
        out_shape=jax.ShapeDtypeStruct((8, n), F32),
        compiler_params=_params(("arbitrary",)),
        name="adaln_mod",
    )(cvec, w_ada, b_ada)


def _mixed_dot(a, w):
    return lax.dot_general(a, w, (((1,), (0,)), ((), ())), preferred_element_type=F32)


def _ffn_kernel(x_ref, mod_ref, g_ref, wg_ref, wu_ref, wd_ref, o_hbm, h_ref, acc_ref, sem, *, sub):
    i = pl.program_id(0)
    f = pl.program_id(1)
    n_tiles = pl.num_programs(0)
    last = pl.num_programs(1) - 1
    tm = x_ref.shape[0]
    subtiles = [pl.ds(r0, ROW_TILE) for r0 in range(0, tm, ROW_TILE)]

    def out_copy(k, tile):
        return pltpu.make_async_copy(
            acc_ref.at[subtiles[k], :], o_hbm.at[pl.ds(tile * tm + k * ROW_TILE, ROW_TILE), :], sem.at[k])

    def gate_up(h):
        g = _mixed_dot(h, wg_ref[...])
        u = _mixed_dot(h, wu_ref[...])
        return (_silu(g) * u).astype(BF16)

    @pl.when((i == 0) & (f == 0))
    def _():
        acc_ref[...] = jnp.zeros_like(acc_ref)
        for k in range(len(subtiles)):
            out_copy(k, 0).start()

    @pl.when(f == 0)
    def _():
        gain = g_ref[2 * sub:2 * sub + 1, :] * (1.0 + mod_ref[3 * sub + 1:3 * sub + 2, :])
        shift = mod_ref[3 * sub:3 * sub + 1, :]
        for k, rows in enumerate(subtiles):
            h = (_rms(x_ref[rows, :], gain) + shift).astype(BF16)
            h_ref[rows, :] = h
            a = gate_up(h)
            out_copy(k, jnp.maximum(i - 1, 0)).wait()
            acc_ref[rows, :] = _mixed_dot(a, wd_ref[...])

    @pl.when((f > 0) & (f < last))
    def _():
        for rows in subtiles:
            acc_ref[rows, :] += _mixed_dot(gate_up(h_ref[rows, :]), wd_ref[...])

    @pl.when(f == last)
    def _():
        gain = 0.5 * mod_ref[3 * sub + 2:3 * sub + 3, :] * g_ref[2 * sub + 1:2 * sub + 2, :]
        for k, rows in enumerate(subtiles):
            y = acc_ref[rows, :] + _mixed_dot(gate_up(h_ref[rows, :]), wd_ref[...])
            acc_ref[rows, :] = x_ref[rows, :] + _rms(y, gain)
            out_copy(k, i).start()

    @pl.when((f == last) & (i == n_tiles - 1))
    def _():
        for k in range(len(subtiles)):
            out_copy(k, i).wait()


def _ffn(x2d, mod, norm_g, wg, wu, wd, which, sub, *, tm, tf, mod_row):
    n_tiles = x2d.shape[0] // tm
    d_ff = wg.shape[2]
    return pl.pallas_call(
        functools.partial(_ffn_kernel, sub=sub),
        grid=(n_tiles, d_ff // tf),
        in_specs=[
            pl.BlockSpec((tm, D_MODEL), lambda i, f: (i, 0)),
            pl.BlockSpec((None, N_MOD, D_MODEL), lambda i, f: (mod_row(i), 0, 0)),
            pl.BlockSpec((6, D_MODEL), lambda i, f: (0, 0)),
            pl.BlockSpec((None, D_MODEL, tf), lambda i, f: (which, 0, f)),
            pl.BlockSpec((None, D_MODEL, tf), lambda i, f: (which, 0, f)),
            pl.BlockSpec((None, tf, D_MODEL), lambda i, f: (which, f, 0)),
        ],
        out_specs=pl.BlockSpec(memory_space=pl.ANY),
        out_shape=jax.ShapeDtypeStruct(x2d.shape, F32),
        scratch_shapes=[pltpu.VMEM((tm, D_MODEL), BF16), pltpu.VMEM((tm, D_MODEL), F32),
                        pltpu.SemaphoreType.DMA((tm // ROW_TILE,))],
        compiler_params=_params(("arbitrary", "arbitrary"), VMEM_LIMIT_FFN),
        name="ffn%d_%d" % (sub, tm),
    )(x2d, mod, norm_g, wg, wu, wd)


N_ROPE_TILES = 2 * GLA_HEADS * GLA_DK // PROJ_TILE
N_Q_TILES = N_ROPE_TILES // 2


def _rope_tables(seq):
    half = GLA_DK // 4
    freqs = ROPE_THETA ** (-np.arange(half, dtype=np.float64) / half)
    t = np.arange(seq)
    cos_parts, sin_parts = [], []
    for pos in (t // GRID_W, t % GRID_W):
        ang = pos[:, None].astype(np.float64) * freqs
        cos_parts += [np.cos(ang), np.cos(ang)]
        sin_parts += [-np.sin(ang), np.sin(ang)]
    cos = np.concatenate(cos_parts, axis=1)
    sin = np.concatenate(sin_parts, axis=1)
    cos = np.concatenate([cos, np.ones((BIG_ROW_TILE, GLA_DK))], axis=0)
    sin = np.concatenate([sin, np.zeros((BIG_ROW_TILE, GLA_DK))], axis=0)
    return jnp.asarray(cos, F32), jnp.asarray(sin, F32)


def _dot_nt(a, w_t):
    return lax.dot_general(a, w_t, (((1,), (1,)), ((), ())), preferred_element_type=F32)


def _inproj_kernel(x_ref, xc_ref, mod_ref, g_ref, w_ref, wgate_ref, cos_ref, sin_ref, p_ref, gate_ref, h_ref,
                   *, n_lat_pairs, ctx_tiles):
    pair = pl.program_id(0)
    n = pl.program_id(1)
    j = pl.program_id(2)
    tm = x_ref.shape[0]

    def project(src_ref, live):
        n_rows = src_ref.shape[0]
        r = slice(0, n_rows)
        alive = (lambda cond: cond) if live is None else (lambda cond: cond & live)

        @pl.when(n == 0)
        def _():
            h = _rms(src_ref[...], g_ref[2:3, :]) * (1.0 + mod_ref[4:5, :]) + mod_ref[3:4, :]
            h_ref[j, r, :] = h.astype(BF16)
            gate = _dot_nt(h_ref[j, r, :], wgate_ref[...])
            gate_ref[pl.ds(pl.multiple_of(j * tm, tm), n_rows), :] = jnp.concatenate(
                [gate, jnp.zeros((n_rows, LANES - gate.shape[1]), F32)], axis=1)

        @pl.when(alive(n < N_ROPE_TILES))
        def _():
            y = _dot_nt(h_ref[j, r, :], w_ref[...])
            cos = cos_ref[r, :]
            sin = sin_ref[r, :]
            qscale = jnp.where(n < N_Q_TILES, GLA_DK ** -0.5, 1.0).astype(F32)
            for c in range(PROJ_TILE // GLA_DK):
                yc = y[:, c * GLA_DK:(c + 1) * GLA_DK]
                swapped = jnp.concatenate(
                    [pltpu.roll(yc[:, g * LANES:(g + 1) * LANES], LANES // 2, 1) for g in range(GLA_DK // LANES)],
                    axis=1)
                p_ref[r, c * GLA_DK:(c + 1) * GLA_DK] = ((yc * cos + swapped * sin) * qscale).astype(BF16)

        @pl.when(alive(n >= N_ROPE_TILES))
        def _():
            p_ref[r, :] = _dot_nt(h_ref[j, r, :], w_ref[...]).astype(BF16)

        if live is not None:
            @pl.when(jnp.logical_not(live))
            def _():
                p_ref[r, :] = jnp.zeros((n_rows, PROJ_TILE), BF16)

    ctx_live = functools.reduce(jnp.logical_or, [n == t for t in ctx_tiles])
    pl.when(pair < n_lat_pairs)(lambda: project(x_ref, None))
    pl.when((pair == n_lat_pairs) & (j == 0))(lambda: project(xc_ref, ctx_live))


def _inproj(h1, h1c, mod, norm_g, w_in_t, cos, sin, seq, gate0, gate1, ctx_cols):
    tm = BIG_ROW_TILE
    n_lat_tiles = h1.shape[0] // tm
    n_lat_pairs = n_lat_tiles // 2
    n_rows = h1.shape[0] + h1c.shape[0]
    tiles_per_batch = n_lat_tiles // 2
    n_a_tiles = gate0 // PROJ_TILE
    n_b_tiles = (w_in_t.shape[0] - gate1) // PROJ_TILE
    n_gate = gate1 - gate0
    ctx_tiles = [t for t in range(n_a_tiles + n_b_tiles)
                 if any(lo < (t + 1) * PROJ_TILE and t * PROJ_TILE < hi for lo, hi in ctx_cols)]

    def row_tile(pair, j):
        return pair * 2 + j

    def weight_row0(pair, n, j):
        t = jnp.int32(ctx_tiles[0])
        for live in ctx_tiles[1:]:
            t = jnp.where(n >= live, live, t)
        t = jnp.where(pair < n_lat_pairs, n, t)
        return pl.multiple_of(t * PROJ_TILE + jnp.where(t < n_a_tiles, 0, n_gate), n_gate), 0

    tab_spec = pl.BlockSpec(
        (tm, GLA_DK),
        lambda pair, n, j: (jnp.where(pair < n_lat_pairs, row_tile(pair, j) % tiles_per_batch, seq // tm), 0))
    return pl.pallas_call(
        functools.partial(_inproj_kernel, n_lat_pairs=n_lat_pairs, ctx_tiles=ctx_tiles),
        grid=(n_lat_pairs + 1, n_a_tiles + n_b_tiles, 2),
        in_specs=[pl.BlockSpec((tm, D_MODEL), lambda pair, n, j: (jnp.minimum(row_tile(pair, j), n_lat_tiles - 1), 0)),
                  pl.BlockSpec(h1c.shape, lambda pair, n, j: (0, 0), pipeline_mode=pl.Buffered(1)),
                  pl.BlockSpec((None, N_MOD, D_MODEL),
                               lambda pair, n, j: (jnp.minimum(row_tile(pair, j) // tiles_per_batch, 2), 0, 0)),
                  pl.BlockSpec((6, D_MODEL), lambda pair, n, j: (0, 0)),
                  pl.BlockSpec((pl.Element(PROJ_TILE), pl.Element(D_MODEL)), weight_row0),
                  pl.BlockSpec((n_gate, D_MODEL), lambda pair, n, j: (gate0 // n_gate, 0)),
                  tab_spec, tab_spec],
        out_specs=[pl.BlockSpec((tm, PROJ_TILE), lambda pair, n, j: (jnp.minimum(row_tile(pair, j), n_lat_tiles), n)),
                   pl.BlockSpec((2 * tm, LANES), lambda pair, n, j: (pair, 0))],
        out_shape=[jax.ShapeDtypeStruct((n_rows, (n_a_tiles + n_b_tiles) * PROJ_TILE), BF16),
                   jax.ShapeDtypeStruct((n_rows, LANES), F32)],
        scratch_shapes=[pltpu.VMEM((2, tm, D_MODEL), BF16)],
        compiler_params=_params(("arbitrary", "arbitrary", "arbitrary"), VMEM_LIMIT_FFN),
        name="inproj",
    )(h1, h1c, mod, norm_g, w_in_t, w_in_t, cos, sin)


def _log2_sigmoid(z):
    t = z * LOG2E
    return jnp.minimum(t, 0.0) - jnp.log2(1.0 + jnp.exp2(-jnp.abs(t)))


def _gla_kernel(qf_ref, kf_ref, vf_ref, gf_ref, qb_ref, kb_ref, vb_ref, gb_ref, wg_ref, bg_ref,
                of_ref, ob_ref, stf_ref, stb_ref):
    @pl.when(pl.program_id(2) == 0)
    def _():
        stf_ref[...] = jnp.zeros_like(stf_ref)
        stb_ref[...] = jnp.zeros_like(stb_ref)

    c = GLA_BLOCK
    nt = (((1,), (1,)), ((), ()))
    row = lax.broadcasted_iota(jnp.int32, (c, c), 0)
    col = lax.broadcasted_iota(jnp.int32, (c, c), 1)
    mask = {True: col <= row, False: col >= row}
    tri = {fwd: jnp.where(m, 1.0, 0.0).astype(BF16) for fwd, m in mask.items()}
    refs = {True: (qf_ref, kf_ref, vf_ref, gf_ref, stf_ref, of_ref),
            False: (qb_ref, kb_ref, vb_ref, gb_ref, stb_ref, ob_ref)}
    qcols = lambda head: slice(head * GLA_DK, (head + 1) * GLA_DK)
    vcols = lambda head: slice(head * GLA_DV, (head + 1) * GLA_DV)

    state, log_a, cum, factors, att = {}, {}, {}, {}, {}

    def gate_stage(head, fwd):
        d = 0 if fwd else 1
        state[head, fwd] = refs[fwd][4][head]
        z = jnp.dot(refs[fwd][3][...], wg_ref[d, :, qcols(head)], preferred_element_type=F32)
        log_a[head, fwd] = _log2_sigmoid(z + bg_ref[d, :, qcols(head)]) * (1.0 / GLA_GATE_TEMP)

    def cumsum_stage(head, fwd):
        la = log_a[head, fwd]
        hi = la.astype(BF16)
        lo = (la - hi.astype(F32)).astype(BF16)
        cum[head, fwd] = (jnp.dot(tri[fwd], hi, preferred_element_type=F32)
                          + jnp.dot(tri[fwd], lo, preferred_element_type=F32))

    def factor_stage(head, fwd):
        cm = cum[head, fwd]
        cum_end = cm[c - 1:c, :] if fwd else cm[0:1, :]
        cum_mid = cm[c // 2:c // 2 + 1, :]
        q = refs[fwd][0][:, qcols(head)]
        k = refs[fwd][1][:, qcols(head)]
        factors[head, fwd] = (q * jnp.exp2(cm - cum_mid).astype(BF16),
                              k * jnp.exp2(cum_mid - cm).astype(BF16),
                              q * jnp.exp2(cm).astype(BF16),
                              k * jnp.exp2(cum_end - cm).astype(BF16),
                              jnp.exp2(cum_end))

    def att_stage(head, fwd):
        q_att, k_att = factors[head, fwd][:2]
        a = lax.dot_general(q_att, k_att, nt, preferred_element_type=F32)
        att[head, fwd] = jnp.where(mask[fwd], a, 0.0).astype(BF16)

    def out_stage(head, fwd):
        _, _, q_dec, k_end, decay = factors[head, fwd]
        v = refs[fwd][2][:, vcols(head)]
        st = state[head, fwd]
        o = (jnp.dot(att[head, fwd], v, preferred_element_type=F32)
             + lax.dot_general(q_dec, st.astype(BF16), nt, preferred_element_type=F32))
        kv_t = lax.dot_general(v, k_end, (((0,), (0,)), ((), ())), preferred_element_type=F32)
        refs[fwd][5][:, vcols(head)] = o.astype(BF16)
        refs[fwd][4][head] = decay * st + kv_t

    for stage in (gate_stage, cumsum_stage, factor_stage, att_stage, out_stage):
        for head in range(GLA_HEADS_PER_STEP):
            for fwd in (True, False):
                stage(head, fwd)


def _gla(p, gin, wg_pad, bg, batch, seq):
    n_lat = seq // GLA_BLOCK
    ctx_block0 = batch * n_lat
    hps = GLA_HEADS_PER_STEP
    dk, dv = hps * GLA_DK, hps * GLA_DV

    def lat_block(b, s, forward):
        j = jnp.maximum(s - 1, 0)
        return b * n_lat + (j if forward else n_lat - 1 - j)

    def in_block(b, s, forward):
        return jnp.where(s == 0, ctx_block0 + b, lat_block(b, s, forward))

    kq = GLA_HEADS // hps
    kv = 2 * GLA_HEADS * GLA_DK // dv

    def dir_specs(forward):
        return [
            pl.BlockSpec((GLA_BLOCK, dk), lambda b, h, s: (in_block(b, s, forward), h)),
            pl.BlockSpec((GLA_BLOCK, dk), lambda b, h, s: (in_block(b, s, forward), kq + h)),
            pl.BlockSpec((GLA_BLOCK, dv), lambda b, h, s: (in_block(b, s, forward), kv + h)),
            pl.BlockSpec((GLA_BLOCK, LANES), lambda b, h, s: (in_block(b, s, forward), 0)),
        ]

    out_shape = jax.ShapeDtypeStruct((batch * seq, GLA_HEADS * GLA_DV), BF16)
    state = pltpu.VMEM((hps, GLA_DV, GLA_DK), F32)
    return pl.pallas_call(
        _gla_kernel,
        grid=(batch, GLA_HEADS // hps, n_lat + 1),
        in_specs=dir_specs(True) + dir_specs(False) + [
            pl.BlockSpec((2, LANES, dk), lambda b, h, s: (0, 0, h)),
            pl.BlockSpec((2, 1, dk), lambda b, h, s: (0, 0, h))],
        out_specs=[pl.BlockSpec((GLA_BLOCK, dv), lambda b, h, s: (lat_block(b, s, True), h)),
                   pl.BlockSpec((GLA_BLOCK, dv), lambda b, h, s: (lat_block(b, s, False), h))],
        out_shape=[out_shape, out_shape],
        scratch_shapes=[state, state],
        compiler_params=_params(("arbitrary", "arbitrary", "arbitrary")),
        name="gla",
    )(p, p, p, gin, p, p, p, gin, wg_pad, bg)


NA_QTOK = NA_QROWS * GRID_W
NA_SUB_TOK = NA_SUB_ROWS * GRID_W
NA_WIN = NA_WIN_ROWS * GRID_W
NA_SUBS = NA_QROWS // NA_SUB_ROWS
NA_KINDS = ((0, 0), (NA_SUB_ROWS, NA_SUB_ROWS - NA_KR // 2), (GRID_W - NA_SUB_ROWS, GRID_W - NA_WIN_ROWS))


def _na_build_bias(rpb_ref, t_ref, rows):
    n_slots = 2 * NA_KR - 1
    lane = lax.broadcasted_iota(jnp.int32, (GRID_W, LANES), 1)
    q_col = lax.broadcasted_iota(jnp.int32, (GRID_W, LANES), 0)
    k_col = lane & (GRID_W - 1)
    upper = lane >= GRID_W
    c0 = jnp.clip(q_col - NA_KC // 2, 0, GRID_W - NA_KC)
    col_ok = (k_col >= c0) & (k_col < c0 + NA_KC)
    neg = jnp.full((GRID_W, LANES), NEG_INF, F32)

    def toeplitz(hh, slot, lane_off):
        if not 0 <= slot < n_slots:
            return jnp.zeros((GRID_W, LANES), F32)
        base = jnp.broadcast_to(rpb_ref[hh, slot:slot + 1, :], (GRID_W, LANES)) * LOG2E
        return pltpu.roll(base, (LANES - (NA_KC - 1) + lane_off) % LANES, 1, stride=1, stride_axis=0)

    for hh in range(2):
        pairs = {}
        for kind, (r0, w0) in enumerate(NA_KINDS):
            for qr in range(NA_SUB_ROWS):
                band0 = min(max(r0 + qr - NA_KR // 2, 0), rows - NA_KR) - w0
                slot0 = w0 - (r0 + qr) + NA_KR - 1
                for g in range(NA_WIN_ROWS // 2):
                    slot = slot0 + 2 * g
                    lo_ok = band0 <= 2 * g < band0 + NA_KR
                    hi_ok = band0 <= 2 * g + 1 < band0 + NA_KR
                    if lo_ok or hi_ok:
                        if slot not in pairs:
                            pairs[slot] = jnp.where(upper, toeplitz(hh, slot + 1, GRID_W), toeplitz(hh, slot, 0))
                        ok = col_ok if (lo_ok and hi_ok) else (col_ok & upper if hi_ok else col_ok & ~upper)
                        tile = jnp.where(ok, pairs[slot], neg)
                    else:
                        tile = neg
                    t_ref[hh, kind, qr * GRID_W:(qr + 1) * GRID_W, g * LANES:(g + 1) * LANES] = tile


def _na_kernel(q_ref, k_ref, v_ref, kc_ref, vc_ref, rpb_ref, o_ref, t_ref, *, rows):
    b = pl.program_id(1)
    blk = pl.program_id(2)
    n_blk = rows // NA_QROWS

    pl.when((b == 0) & (blk == 0))(lambda: _na_build_bias(rpb_ref, t_ref, rows))

    nt = (((1,), (1,)), ((), ()))
    kc = kc_ref[...]
    vc = vc_ref[...]
    lane = lax.broadcasted_iota(jnp.int32, (NA_SUB_TOK, LANES), 1)
    chains = [(hh, sub) for sub in range(NA_SUBS) for hh in range(2)]
    q_head, kw, vw, kind = {}, [], [], []
    for sub in range(NA_SUBS):
        q = q_ref[sub * NA_SUB_TOK:(sub + 1) * NA_SUB_TOK, :].astype(F32) * (NA_DH ** -0.5 * LOG2E)
        q = q.astype(BF16)
        q_head[0, sub] = jnp.where(lane < NA_DH, q, jnp.zeros_like(q))
        q_head[1, sub] = jnp.where(lane >= NA_DH, q, jnp.zeros_like(q))
        r0 = blk * NA_QROWS + sub * NA_SUB_ROWS
        w0 = jnp.clip(r0 - NA_KR // 2, 0, rows - NA_WIN_ROWS)
        win = pl.ds(pl.multiple_of(w0 * GRID_W, 256), NA_WIN)
        kw.append(k_ref[win, :])
        vw.append(v_ref[win, :])
        kind.append(jnp.where(r0 == 0, 0, jnp.where(r0 == rows - NA_SUB_ROWS, 2, 1)))

    s_lat, s_ctx, p_lat, p_ctx, denom, out = {}, {}, {}, {}, {}, {}
    for hh, sub in chains:
        qs = q_head[hh, sub]
        s_lat[hh, sub] = lax.dot_general(qs, kw[sub], nt, preferred_element_type=F32) + t_ref[hh, kind[sub]]
        s_ctx[hh, sub] = lax.dot_general(qs, kc, nt, preferred_element_type=F32)
    for ch in chains:
        m = jnp.maximum(jnp.max(s_lat[ch], axis=-1, keepdims=True), jnp.max(s_ctx[ch], axis=-1, keepdims=True))
        pl_ = jnp.exp2(s_lat[ch] - m)
        pc_ = jnp.exp2(s_ctx[ch] - m)
        denom[ch] = jnp.sum(pl_, axis=-1, keepdims=True) + jnp.sum(pc_, axis=-1, keepdims=True)
        p_lat[ch] = pl_.astype(BF16)
        p_ctx[ch] = pc_.astype(BF16)
    for hh, sub in chains:
        o = (jnp.dot(p_lat[hh, sub], vw[sub], preferred_element_type=F32)
             + jnp.dot(p_ctx[hh, sub], vc, preferred_element_type=F32))
        out[hh, sub] = o / denom[hh, sub]
    for sub in range(NA_SUBS):
        rows_ = slice(sub * NA_SUB_TOK, (sub + 1) * NA_SUB_TOK)
        o_ref[rows_, :] = jnp.where(lane < NA_DH, out[0, sub], out[1, sub]).astype(o_ref.dtype)


def _na(p, rpb, batch, seq, ctx_len, col0):
    rows = seq // GRID_W
    n_blk = rows // NA_QROWS
    n_pairs = NA_HEADS // 2
    cq = col0 // LANES
    ck = cq + n_pairs
    cv = ck + n_pairs
    ctx_block0 = batch * seq // ctx_len
    return pl.pallas_call(
        functools.partial(_na_kernel, rows=rows),
        grid=(n_pairs, batch, n_blk),
        in_specs=[
            pl.BlockSpec((NA_QTOK, LANES), lambda hp, b, r: (b * n_blk + r, cq + hp)),
            pl.BlockSpec((seq, LANES), lambda hp, b, r: (b, ck + hp)),
            pl.BlockSpec((seq, LANES), lambda hp, b, r: (b, cv + hp)),
            pl.BlockSpec((ctx_len, LANES), lambda hp, b, r: (ctx_block0 + b, ck + hp)),
            pl.BlockSpec((ctx_len, LANES), lambda hp, b, r: (ctx_block0 + b, cv + hp)),
            pl.BlockSpec((2, 2 * NA_KR - 1, LANES), lambda hp, b, r: (hp, 0, 0)),
        ],
        out_specs=pl.BlockSpec((NA_QTOK, LANES), lambda hp, b, r: (b * n_blk + r, hp)),
        out_shape=jax.ShapeDtypeStruct((batch * seq, NA_HEADS * NA_DH), BF16),
        scratch_shapes=[pltpu.VMEM((2, len(NA_KINDS), NA_SUB_TOK, NA_WIN), F32)],
        compiler_params=_params(("arbitrary", "arbitrary", "arbitrary")),
        name="natten",
    )(p, p, p, p, p, rpb)


def _merge_kernel(of_ref, ob_ref, r_ref, gn_ref, b_ref, wa_ref, wb_ref, m1a_ref, m1b_ref, m2a_ref, m2b_ref,
                  o_ref, a_ref):
    for h in range(GLA_HEADS):
        cols = slice(h * GLA_DV, (h + 1) * GLA_DV)
        tot = of_ref[:, cols].astype(F32) + ob_ref[:, cols].astype(F32)
        a_ref[:, cols] = (_rms(tot, gn_ref[...]) * _silu(r_ref[:, cols].astype(F32))).astype(BF16)

    gate_refs = ((m1a_ref, m2a_ref), (m1b_ref, m2b_ref))
    gate_w = m1a_ref.shape[1]
    for c0 in range(0, o_ref.shape[1], MERGE_STRIP):
        cols = slice(c0, c0 + MERGE_STRIP)
        m1_ref, m2_ref = gate_refs[c0 // gate_w]
        gcols = slice(c0 % gate_w, c0 % gate_w + MERGE_STRIP)
        a = jnp.dot(a_ref[...], wa_ref[:, cols], preferred_element_type=F32)
        b = jnp.dot(b_ref[...], wb_ref[:, cols], preferred_element_type=F32)
        m = _sigmoid(m1_ref[:, gcols].astype(F32)) * a + _sigmoid(m2_ref[:, gcols].astype(F32)) * b
        o_ref[:, cols] = m.astype(o_ref.dtype)


def _merge(o_fwd, o_bwd, gn, o_na, w_gla_o, w_na_o, p, col_r, col_m1):
    n_rows = o_fwd.shape[0]
    v_w = o_fwd.shape[1]
    gw = D_MODEL // 2
    c1 = col_m1 // gw
    once = dict(pipeline_mode=pl.Buffered(1))
    gate_spec = lambda c: pl.BlockSpec((ROW_TILE, gw), lambda i: (i, c))
    return pl.pallas_call(
        _merge_kernel,
        grid=(n_rows // ROW_TILE,),
        in_specs=[pl.BlockSpec((ROW_TILE, v_w), lambda i: (i, 0)),
                  pl.BlockSpec((ROW_TILE, v_w), lambda i: (i, 0)),
                  pl.BlockSpec((ROW_TILE, v_w), lambda i: (i, col_r // v_w)),
                  pl.BlockSpec((1, GLA_DV), lambda i: (0, 0)),
                  pl.BlockSpec((ROW_TILE, o_na.shape[1]), lambda i: (i, 0)),
                  pl.BlockSpec(w_gla_o.shape, lambda i: (0, 0), **once),
                  pl.BlockSpec(w_na_o.shape, lambda i: (0, 0), **once),
                  gate_spec(c1), gate_spec(c1 + 1), gate_spec(c1 + 2), gate_spec(c1 + 3)],
        out_specs=pl.BlockSpec((ROW_TILE, D_MODEL), lambda i: (i, 0)),
        out_shape=jax.ShapeDtypeStruct((n_rows, D_MODEL), BF16),
        scratch_shapes=[pltpu.VMEM((ROW_TILE, v_w), BF16)],
        compiler_params=_params(("arbitrary",)),
        name="merge",
    )(o_fwd, o_bwd, p, gn, o_na, w_gla_o, w_na_o, p, p, p, p)


def _outproj_kernel(m_ref, w_ref, x_ref, mod_ref, g_ref, o_ref):
    y = _mixed_dot(m_ref[...], w_ref[...])
    o_ref[...] = x_ref[...] + mod_ref[5:6, :] * _rms(y, g_ref[3:4, :])


def _outproj(m, w_out, h1, mod, norm_g):
    n_tiles = m.shape[0] // ROW_TILE
    tiles_per_batch = n_tiles // 2
    return pl.pallas_call(
        _outproj_kernel,
        grid=(n_tiles,),
        in_specs=[pl.BlockSpec((ROW_TILE, D_MODEL), lambda i: (i, 0)),
                  pl.BlockSpec((D_MODEL, D_MODEL), lambda i: (0, 0), pipeline_mode=pl.Buffered(1)),
                  pl.BlockSpec((ROW_TILE, D_MODEL), lambda i: (i, 0)),
                  pl.BlockSpec((None, N_MOD, D_MODEL), lambda i: (i // tiles_per_batch, 0, 0)),
                  pl.BlockSpec((6, D_MODEL), lambda i: (0, 0))],
        out_specs=pl.BlockSpec((ROW_TILE, D_MODEL), lambda i: (i, 0)),
        out_shape=jax.ShapeDtypeStruct((m.shape[0], D_MODEL), F32),
        compiler_params=_params(("arbitrary",)),
        name="outproj",
    )(m, w_out, h1, mod, norm_g)


def kernel(x, c, ctx, c_ctx, w_ada, b_ada, norm_g, ffn_wg, ffn_wu, ffn_wd, w_in, gla_wg, gla_bg, gla_norm_g,
           w_gla_o, na_rpb, w_na_o, w_out):
    batch, seq, d = x.shape
    ctx_len = ctx.shape[1]
    depth = w_ada.shape[0]
    assert d == D_MODEL and batch == 2 and batch * ctx_len == ROW_TILE and depth == 1
    assert seq % ROW_TILE == 0 and seq // GRID_W == GRID_W

    qk_w = GLA_HEADS * GLA_DK
    v_w = GLA_HEADS * GLA_DV
    na_w = NA_HEADS * NA_DH
    gate0 = 2 * qk_w + 2 * v_w
    gate1 = gate0 + 2 * GLA_GATE_RANK
    col_nq = gate0
    col_m1 = col_nq + 3 * na_w

    cvec = jnp.zeros((8, d), F32).at[0:batch].set(c).at[batch].set(c_ctx)
    cos, sin = _rope_tables(seq)
    h = x.reshape(batch * seq, d)
    hc = ctx.reshape(batch * ctx_len, d)
    lat_tiles_per_batch = seq // BIG_ROW_TILE
    for l in range(depth):
        mod = _modulation(cvec, w_ada[l], b_ada[l].reshape(1, -1)).reshape(8, N_MOD, d)
        g = norm_g[l]
        h1 = _ffn(h, mod, g, ffn_wg[l], ffn_wu[l], ffn_wd[l], 0, 0, tm=BIG_ROW_TILE, tf=FF_TILE,
                  mod_row=lambda i: i // lat_tiles_per_batch)
        h1c = _ffn(hc, mod, g, ffn_wg[l], ffn_wu[l], ffn_wd[l], 0, 0, tm=ROW_TILE, tf=FF_TILE,
                   mod_row=lambda i: batch)
        ctx_cols = [(qk_w, 2 * qk_w + v_w), (col_nq + na_w, col_nq + 3 * na_w)]
        p, gin = _inproj(h1, h1c, mod, g, jnp.swapaxes(w_in[l], 0, 1), cos, sin, seq, gate0, gate1, ctx_cols)
        wg_pad = jnp.zeros((2, LANES, qk_w), F32)
        wg_pad = wg_pad.at[0, :GLA_GATE_RANK].set(gla_wg[l, 0])
        wg_pad = wg_pad.at[1, GLA_GATE_RANK:2 * GLA_GATE_RANK].set(gla_wg[l, 1])
        bg = gla_bg[l].reshape(2, 1, qk_w)
        gn = gla_norm_g[l].reshape(1, GLA_DV)
        o_fwd, o_bwd = _gla(p, gin, wg_pad, bg, batch, seq)
        rpb = jnp.pad(na_rpb[l], ((0, 0), (0, 0), (0, LANES - (2 * NA_KC - 1))))
        o_na = _na(p, rpb, batch, seq, ctx_len, col_nq)
        m = _merge(o_fwd, o_bwd, gn, o_na, w_gla_o[l].astype(BF16), w_na_o[l].astype(BF16), p,
                   2 * qk_w + v_w, col_m1)
        h2 = _outproj(m, w_out[l], h1, mod, g)
        h = _ffn(h2, mod, g, ffn_wg[l], ffn_wu[l], ffn_wd[l], 1, 2, tm=BIG_ROW_TILE, tf=FF_TILE,
                 mod_row=lambda i: i // lat_tiles_per_batch)
    return h.reshape(batch, seq, d)
```

```python
import functools

import numpy as np
import jax
import jax.numpy as jnp
from jax import lax
from jax.experimental import pallas as pl
from jax.experimental.pallas import tpu as pltpu

F32 = jnp.float32
BF16 = jnp.bfloat16

D_MODEL = 2048
GRID_W = 64
GLA_HEADS = 4
GLA_DK = 256
GLA_DV = 512
GLA_GATE_RANK = 16
GLA_GATE_TEMP = 16.0
NA_HEADS = 16
NA_DH = 64
NA_KR = 8
NA_KC = 16
ROPE_THETA = 10000.0
EPS = 1e-6
NEG_INF = -1e30
N_MOD = 9
LOG2E = 1.4426950408889634

LANES = 128
ROW_TILE = 512
BIG_ROW_TILE = 1024
FF_TILE = 512
PROJ_TILE = 1024
MERGE_STRIP = 512
GLA_BLOCK = 256
GLA_HEADS_PER_STEP = 4
NA_QROWS = 16
NA_SUB_ROWS = 4
NA_WIN_ROWS = 12
VMEM_LIMIT = 56 * 1024 * 1024
VMEM_LIMIT_FFN = 60 * 1024 * 1024


def _params(sem, vmem_limit=VMEM_LIMIT):
    return pltpu.CompilerParams(dimension_semantics=sem, vmem_limit_bytes=vmem_limit)


def _rms(x, g):
    return x * lax.rsqrt(jnp.mean(x * x, axis=-1, keepdims=True) + EPS) * g


def _sigmoid(x):
    return 0.5 * jnp.tanh(0.5 * x) + 0.5


def _silu(x):
    h = 0.5 * x
    return h * jnp.tanh(h) + h


def _mod_kernel(c_ref, w_ref, b_ref, o_ref):
    o_ref[...] = jnp.dot(_silu(c_ref[...]), w_ref[...], preferred_element_type=F32) + b_ref[...]


def _modulation(cvec, w_ada, b_ada):
    n = w_ada.shape[1]
    tn = 1024
    return pl.pallas_call(
        _mod_kernel,
        grid=(n // tn,),
        in_specs=[pl.BlockSpec((8, D_MODEL), lambda j: (0, 0)),
                  pl.BlockSpec((D_MODEL, tn), lambda j: (0, j)),
                  pl.BlockSpec((1, tn), lambda j: (0, j))],
        out_specs=pl.BlockSpec((8, tn), lambda j: (0, j)),
        out_shape=jax.ShapeDtypeStruct((8, n), F32),
        compiler_params=_params(("arbitrary",)),
        name="adaln_mod",
    )(cvec, w_ada, b_ada)


def _mixed_dot(a, w):
    return lax.dot_general(a, w, (((1,), (0,)), ((), ())), preferred_element_type=F32)


def _ffn_kernel(x_ref, mod_ref, g_ref, wg_ref, wu_ref, wd_ref, o_hbm, h_ref, acc_ref, sem, *, sub):
    i = pl.program_id(0)
    f = pl.program_id(1)
    n_tiles = pl.num_programs(0)
    last = pl.num_programs(1) - 1
    tm = x_ref.shape[0]
    subtiles = [pl.ds(r0, ROW_TILE) for r0 in range(0, tm, ROW_TILE)]

    def out_copy(k, tile):
        return pltpu.make_async_copy(
            acc_ref.at[subtiles[k], :], o_hbm.at[pl.ds(tile * tm + k * ROW_TILE, ROW_TILE), :], sem.at[k])

    def gate_up(h):
        g = _mixed_dot(h, wg_ref[...])
        u = _mixed_dot(h, wu_ref[...])
        return (_silu(g) * u).astype(BF16)

    @pl.when((i == 0) & (f == 0))
    def _():
        acc_ref[...] = jnp.zeros_like(acc_ref)
        for k in range(len(subtiles)):
            out_copy(k, 0).start()

    @pl.when(f == 0)
    def _():
        gain = g_ref[2 * sub:2 * sub + 1, :] * (1.0 + mod_ref[3 * sub + 1:3 * sub + 2, :])
        shift = mod_ref[3 * sub:3 * sub + 1, :]
        for k, rows in enumerate(subtiles):
            h = (_rms(x_ref[rows, :], gain) + shift).astype(BF16)
            h_ref[rows, :] = h
            a = gate_up(h)
            out_copy(k, jnp.maximum(i - 1, 0)).wait()
            acc_ref[rows, :] = _mixed_dot(a, wd_ref[...])

    @pl.when((f > 0) & (f < last))
    def _():
        for rows in subtiles:
            acc_ref[rows, :] += _mixed_dot(gate_up(h_ref[rows, :]), wd_ref[...])

    @pl.when(f == last)
    def _():
        gain = 0.5 * mod_ref[3 * sub + 2:3 * sub + 3, :] * g_ref[2 * sub + 1:2 * sub + 2, :]
        for k, rows in enumerate(subtiles):
            y = acc_ref[rows, :] + _mixed_dot(gate_up(h_ref[rows, :]), wd_ref[...])
            acc_ref[rows, :] = x_ref[rows, :] + _rms(y, gain)
            out_copy(k, i).start()

    @pl.when((f == last) & (i == n_tiles - 1))
    def _():
        for k in range(len(subtiles)):
            out_copy(k, i).wait()


def _ffn(x2d, mod, norm_g, wg, wu, wd, which, sub, *, tm, tf, mod_row):
    n_tiles = x2d.shape[0] // tm
    d_ff = wg.shape[2]
    return pl.pallas_call(
        functools.partial(_ffn_kernel, sub=sub),
        grid=(n_tiles, d_ff // tf),
        in_specs=[
            pl.BlockSpec((tm, D_MODEL), lambda i, f: (i, 0)),
            pl.BlockSpec((None, N_MOD, D_MODEL), lambda i, f: (mod_row(i), 0, 0)),
            pl.BlockSpec((6, D_MODEL), lambda i, f: (0, 0)),
            pl.BlockSpec((None, D_MODEL, tf), lambda i, f: (which, 0, f)),
            pl.BlockSpec((None, D_MODEL, tf), lambda i, f: (which, 0, f)),
            pl.BlockSpec((None, tf, D_MODEL), lambda i, f: (which, f, 0)),
        ],
        out_specs=pl.BlockSpec(memory_space=pl.ANY),
        out_shape=jax.ShapeDtypeStruct(x2d.shape, F32),
        scratch_shapes=[pltpu.VMEM((tm, D_MODEL), BF16), pltpu.VMEM((tm, D_MODEL), F32),
                        pltpu.SemaphoreType.DMA((tm // ROW_TILE,))],
        compiler_params=_params(("arbitrary", "arbitrary"), VMEM_LIMIT_FFN),
        name="ffn%d_%d" % (sub, tm),
    )(x2d, mod, norm_g, wg, wu, wd)


N_ROPE_TILES = 2 * GLA_HEADS * GLA_DK // PROJ_TILE
N_Q_TILES = N_ROPE_TILES // 2


def _rope_tables(seq):
    half = GLA_DK // 4
    freqs = ROPE_THETA ** (-np.arange(half, dtype=np.float64) / half)
    t = np.arange(seq)
    cos_parts, sin_parts = [], []
    for pos in (t // GRID_W, t % GRID_W):
        ang = pos[:, None].astype(np.float64) * freqs
        cos_parts += [np.cos(ang), np.cos(ang)]
        sin_parts += [-np.sin(ang), np.sin(ang)]
    cos = np.concatenate(cos_parts, axis=1)
    sin = np.concatenate(sin_parts, axis=1)
    cos = np.concatenate([cos, np.ones((BIG_ROW_TILE, GLA_DK))], axis=0)
    sin = np.concatenate([sin, np.zeros((BIG_ROW_TILE, GLA_DK))], axis=0)
    return jnp.asarray(cos, F32), jnp.asarray(sin, F32)


def _dot_nt(a, w_t):
    return lax.dot_general(a, w_t, (((1,), (1,)), ((), ())), preferred_element_type=F32)


def _inproj_kernel(x_ref, xc_ref, mod_ref, g_ref, w_ref, wgate_ref, cos_ref, sin_ref, p_ref, gate_ref, h_ref,
                   *, n_lat_pairs, ctx_tiles):
    pair = pl.program_id(0)
    n = pl.program_id(1)
    j = pl.program_id(2)
    tm = x_ref.shape[0]

    def project(src_ref, live):
        n_rows = src_ref.shape[0]
        r = slice(0, n_rows)
        alive = (lambda cond: cond) if live is None else (lambda cond: cond & live)

        @pl.when(n == 0)
        def _():
            h = _rms(src_ref[...], g_ref[2:3, :]) * (1.0 + mod_ref[4:5, :]) + mod_ref[3:4, :]
            h_ref[j, r, :] = h.astype(BF16)
            gate = _dot_nt(h_ref[j, r, :], wgate_ref[...])
            gate_ref[pl.ds(pl.multiple_of(j * tm, tm), n_rows), :] = jnp.concatenate(
                [gate, jnp.zeros((n_rows, LANES - gate.shape[1]), F32)], axis=1)

        @pl.when(alive(n < N_ROPE_TILES))
        def _():
            y = _dot_nt(h_ref[j, r, :], w_ref[...])
            cos = cos_ref[r, :]
            sin = sin_ref[r, :]
            qscale = jnp.where(n < N_Q_TILES, GLA_DK ** -0.5, 1.0).astype(F32)
            for c in range(PROJ_TILE // GLA_DK):
                yc = y[:, c * GLA_DK:(c + 1) * GLA_DK]
                swapped = jnp.concatenate(
                    [pltpu.roll(yc[:, g * LANES:(g + 1) * LANES], LANES // 2, 1) for g in range(GLA_DK // LANES)],
                    axis=1)
                p_ref[r, c * GLA_DK:(c + 1) * GLA_DK] = ((yc * cos + swapped * sin) * qscale).astype(BF16)

        @pl.when(alive(n >= N_ROPE_TILES))
        def _():
            p_ref[r, :] = _dot_nt(h_ref[j, r, :], w_ref[...]).astype(BF16)

        if live is not None:
            @pl.when(jnp.logical_not(live))
            def _():
                p_ref[r, :] = jnp.zeros((n_rows, PROJ_TILE), BF16)

    ctx_live = functools.reduce(jnp.logical_or, [n == t for t in ctx_tiles])
    pl.when(pair < n_lat_pairs)(lambda: project(x_ref, None))
    pl.when((pair == n_lat_pairs) & (j == 0))(lambda: project(xc_ref, ctx_live))


def _inproj(h1, h1c, mod, norm_g, w_in_t, cos, sin, seq, gate0, gate1, ctx_cols):
    tm = BIG_ROW_TILE
    n_lat_tiles = h1.shape[0] // tm
    n_lat_pairs = n_lat_tiles // 2
    n_rows = h1.shape[0] + h1c.shape[0]
    tiles_per_batch = n_lat_tiles // 2
    n_a_tiles = gate0 // PROJ_TILE
    n_b_tiles = (w_in_t.shape[0] - gate1) // PROJ_TILE
    n_gate = gate1 - gate0
    ctx_tiles = [t for t in range(n_a_tiles + n_b_tiles)
                 if any(lo < (t + 1) * PROJ_TILE and t * PROJ_TILE < hi for lo, hi in ctx_cols)]

    def weight_row0(pair, n, j):
        t = jnp.int32(ctx_tiles[0])
        for live in ctx_tiles[1:]:
            t = jnp.where(n >= live, live, t)
        t = jnp.where(pair < n_lat_pairs, n, t)
        return pl.multiple_of(t * PROJ_TILE + jnp.where(t < n_a_tiles, 0, n_gate), n_gate), 0

    def row_tile(pair, n, j, n_used):
        return pair * 2 + jnp.where(n < n_used, j, 1)

    tab_spec = pl.BlockSpec(
        (tm, GLA_DK),
        lambda pair, n, j: (jnp.where(pair < n_lat_pairs,
                                      row_tile(pair, n, j, N_ROPE_TILES) % tiles_per_batch, seq // tm), 0))
    return pl.pallas_call(
        functools.partial(_inproj_kernel, n_lat_pairs=n_lat_pairs, ctx_tiles=ctx_tiles),
        grid=(n_lat_pairs + 1, n_a_tiles + n_b_tiles, 2),
        in_specs=[pl.BlockSpec((tm, D_MODEL),
                               lambda pair, n, j: (jnp.minimum(row_tile(pair, n, j, 1), n_lat_tiles - 1), 0)),
                  pl.BlockSpec(h1c.shape, lambda pair, n, j: (0, 0), pipeline_mode=pl.Buffered(1)),
                  pl.BlockSpec((None, N_MOD, D_MODEL),
                               lambda pair, n, j: (jnp.minimum((pair * 2 + j) // tiles_per_batch, 2), 0, 0)),
                  pl.BlockSpec((6, D_MODEL), lambda pair, n, j: (0, 0)),
                  pl.BlockSpec((pl.Element(PROJ_TILE), pl.Element(D_MODEL)), weight_row0),
                  pl.BlockSpec((n_gate, D_MODEL), lambda pair, n, j: (gate0 // n_gate, 0)),
                  tab_spec, tab_spec],
        out_specs=[pl.BlockSpec((tm, PROJ_TILE), lambda pair, n, j: (jnp.minimum(pair * 2 + j, n_lat_tiles), n)),
                   pl.BlockSpec((2 * tm, LANES), lambda pair, n, j: (pair, 0))],
        out_shape=[jax.ShapeDtypeStruct((n_rows, (n_a_tiles + n_b_tiles) * PROJ_TILE), BF16),
                   jax.ShapeDtypeStruct((n_rows, LANES), F32)],
        scratch_shapes=[pltpu.VMEM((2, tm, D_MODEL), BF16)],
        compiler_params=_params(("arbitrary", "arbitrary", "arbitrary"), VMEM_LIMIT_FFN),
        name="inproj",
    )(h1, h1c, mod, norm_g, w_in_t, w_in_t, cos, sin)


def _log2_sigmoid(z):
    t = z * LOG2E
    return jnp.minimum(t, 0.0) - jnp.log2(1.0 + jnp.exp2(-jnp.abs(t)))


def _gla_kernel(qf_ref, kf_ref, vf_ref, gf_ref, qb_ref, kb_ref, vb_ref, gb_ref, wg_ref, bg_ref,
                of_ref, ob_ref, stf_ref, stb_ref):
    @pl.when(pl.program_id(2) == 0)
    def _():
        stf_ref[...] = jnp.zeros_like(stf_ref)
        stb_ref[...] = jnp.zeros_like(stb_ref)

    c = GLA_BLOCK
    nt = (((1,), (1,)), ((), ()))
    row = lax.broadcasted_iota(jnp.int32, (c, c), 0)
    col = lax.broadcasted_iota(jnp.int32, (c, c), 1)
    mask = {True: col <= row, False: col >= row}
    tri = {fwd: jnp.where(m, 1.0, 0.0).astype(BF16) for fwd, m in mask.items()}
    refs = {True: (qf_ref, kf_ref, vf_ref, gf_ref, stf_ref, of_ref),
            False: (qb_ref, kb_ref, vb_ref, gb_ref, stb_ref, ob_ref)}
    qcols = lambda head: slice(head * GLA_DK, (head + 1) * GLA_DK)
    vcols = lambda head: slice(head * GLA_DV, (head + 1) * GLA_DV)

    state, log_a, cum, factors, att = {}, {}, {}, {}, {}

    def gate_stage(head, fwd):
        d = 0 if fwd else 1
        state[head, fwd] = refs[fwd][4][head]
        z = jnp.dot(refs[fwd][3][...], wg_ref[d, :, qcols(head)], preferred_element_type=F32)
        log_a[head, fwd] = _log2_sigmoid(z + bg_ref[d, :, qcols(head)]) * (1.0 / GLA_GATE_TEMP)

    def cumsum_stage(head, fwd):
        la = log_a[head, fwd]
        hi = la.astype(BF16)
        lo = (la - hi.astype(F32)).astype(BF16)
        cum[head, fwd] = (jnp.dot(tri[fwd], hi, preferred_element_type=F32)
                          + jnp.dot(tri[fwd], lo, preferred_element_type=F32))

    def factor_stage(head, fwd):
        cm = cum[head, fwd]
        cum_end = cm[c - 1:c, :] if fwd else cm[0:1, :]
        cum_mid = cm[c // 2:c // 2 + 1, :]
        q = refs[fwd][0][:, qcols(head)]
        k = refs[fwd][1][:, qcols(head)]
        factors[head, fwd] = (q * jnp.exp2(cm - cum_mid).astype(BF16),
                              k * jnp.exp2(cum_mid - cm).astype(BF16),
                              q * jnp.exp2(cm).astype(BF16),
                              k * jnp.exp2(cum_end - cm).astype(BF16),
                              jnp.exp2(cum_end))

    def att_stage(head, fwd):
        q_att, k_att = factors[head, fwd][:2]
        a = lax.dot_general(q_att, k_att, nt, preferred_element_type=F32)
        att[head, fwd] = jnp.where(mask[fwd], a, 0.0).astype(BF16)

    def out_stage(head, fwd):
        _, _, q_dec, k_end, decay = factors[head, fwd]
        v = refs[fwd][2][:, vcols(head)]
        st = state[head, fwd]
        o = (jnp.dot(att[head, fwd], v, preferred_element_type=F32)
             + lax.dot_general(q_dec, st.astype(BF16), nt, preferred_element_type=F32))
        kv_t = lax.dot_general(v, k_end, (((0,), (0,)), ((), ())), preferred_element_type=F32)
        refs[fwd][5][:, vcols(head)] = o.astype(BF16)
        refs[fwd][4][head] = decay * st + kv_t

    for stage in (gate_stage, cumsum_stage, factor_stage, att_stage, out_stage):
        for head in range(GLA_HEADS_PER_STEP):
            for fwd in (True, False):
                stage(head, fwd)


def _gla(p, gin, wg_pad, bg, batch, seq):
    n_lat = seq // GLA_BLOCK
    ctx_block0 = batch * n_lat
    hps = GLA_HEADS_PER_STEP
    dk, dv = hps * GLA_DK, hps * GLA_DV

    def lat_block(b, s, forward):
        j = jnp.maximum(s - 1, 0)
        return b * n_lat + (j if forward else n_lat - 1 - j)

    def in_block(b, s, forward):
        return jnp.where(s == 0, ctx_block0 + b, lat_block(b, s, forward))

    kq = GLA_HEADS // hps
    kv = 2 * GLA_HEADS * GLA_DK // dv

    def dir_specs(forward):
        return [
            pl.BlockSpec((GLA_BLOCK, dk), lambda b, h, s: (in_block(b, s, forward), h)),
            pl.BlockSpec((GLA_BLOCK, dk), lambda b, h, s: (in_block(b, s, forward), kq + h)),
            pl.BlockSpec((GLA_BLOCK, dv), lambda b, h, s: (in_block(b, s, forward), kv + h)),
            pl.BlockSpec((GLA_BLOCK, LANES), lambda b, h, s: (in_block(b, s, forward), 0)),
        ]

    out_shape = jax.ShapeDtypeStruct((batch * seq, GLA_HEADS * GLA_DV), BF16)
    state = pltpu.VMEM((hps, GLA_DV, GLA_DK), F32)
    return pl.pallas_call(
        _gla_kernel,
        grid=(batch, GLA_HEADS // hps, n_lat + 1),
        in_specs=dir_specs(True) + dir_specs(False) + [
            pl.BlockSpec((2, LANES, dk), lambda b, h, s: (0, 0, h)),
            pl.BlockSpec((2, 1, dk), lambda b, h, s: (0, 0, h))],
        out_specs=[pl.BlockSpec((GLA_BLOCK, dv), lambda b, h, s: (lat_block(b, s, True), h)),
                   pl.BlockSpec((GLA_BLOCK, dv), lambda b, h, s: (lat_block(b, s, False), h))],
        out_shape=[out_shape, out_shape],
        scratch_shapes=[state, state],
        compiler_params=_params(("arbitrary", "arbitrary", "arbitrary")),
        name="gla",
    )(p, p, p, gin, p, p, p, gin, wg_pad, bg)


NA_QTOK = NA_QROWS * GRID_W
NA_SUB_TOK = NA_SUB_ROWS * GRID_W
NA_WIN = NA_WIN_ROWS * GRID_W
NA_SUBS = NA_QROWS // NA_SUB_ROWS
NA_KINDS = ((0, 0), (NA_SUB_ROWS, NA_SUB_ROWS - NA_KR // 2), (GRID_W - NA_SUB_ROWS, GRID_W - NA_WIN_ROWS))


def _na_build_bias(rpb_ref, t_ref, rows):
    n_slots = 2 * NA_KR - 1
    lane = lax.broadcasted_iota(jnp.int32, (GRID_W, LANES), 1)
    q_col = lax.broadcasted_iota(jnp.int32, (GRID_W, LANES), 0)
    k_col = lane & (GRID_W - 1)
    upper = lane >= GRID_W
    c0 = jnp.clip(q_col - NA_KC // 2, 0, GRID_W - NA_KC)
    col_ok = (k_col >= c0) & (k_col < c0 + NA_KC)
    neg = jnp.full((GRID_W, LANES), NEG_INF, F32)

    def toeplitz(hh, slot, lane_off):
        if not 0 <= slot < n_slots:
            return jnp.zeros((GRID_W, LANES), F32)
        base = jnp.broadcast_to(rpb_ref[hh, slot:slot + 1, :], (GRID_W, LANES)) * LOG2E
        return pltpu.roll(base, (LANES - (NA_KC - 1) + lane_off) % LANES, 1, stride=1, stride_axis=0)

    for hh in range(2):
        pairs = {}
        for kind, (r0, w0) in enumerate(NA_KINDS):
            for qr in range(NA_SUB_ROWS):
                band0 = min(max(r0 + qr - NA_KR // 2, 0), rows - NA_KR) - w0
                slot0 = w0 - (r0 + qr) + NA_KR - 1
                for g in range(NA_WIN_ROWS // 2):
                    slot = slot0 + 2 * g
                    lo_ok = band0 <= 2 * g < band0 + NA_KR
                    hi_ok = band0 <= 2 * g + 1 < band0 + NA_KR
                    if lo_ok or hi_ok:
                        if slot not in pairs:
                            pairs[slot] = jnp.where(upper, toeplitz(hh, slot + 1, GRID_W), toeplitz(hh, slot, 0))
                        ok = col_ok if (lo_ok and hi_ok) else (col_ok & upper if hi_ok else col_ok & ~upper)
                        tile = jnp.where(ok, pairs[slot], neg)
                    else:
                        tile = neg
                    t_ref[hh, kind, qr * GRID_W:(qr + 1) * GRID_W, g * LANES:(g + 1) * LANES] = tile


def _na_kernel(q_ref, k_ref, v_ref, kc_ref, vc_ref, rpb_ref, o_ref, t_ref, *, rows):
    b = pl.program_id(1)
    blk = pl.program_id(2)
    n_blk = rows // NA_QROWS

    pl.when((b == 0) & (blk == 0))(lambda: _na_build_bias(rpb_ref, t_ref, rows))

    nt = (((1,), (1,)), ((), ()))
    kc = kc_ref[...]
    vc = vc_ref[...]
    lane = lax.broadcasted_iota(jnp.int32, (NA_SUB_TOK, LANES), 1)
    chains = [(hh, sub) for sub in range(NA_SUBS) for hh in range(2)]
    q_head, kw, vw, kind = {}, [], [], []
    for sub in range(NA_SUBS):
        q = q_ref[sub * NA_SUB_TOK:(sub + 1) * NA_SUB_TOK, :].astype(F32) * (NA_DH ** -0.5 * LOG2E)
        q = q.astype(BF16)
        q_head[0, sub] = jnp.where(lane < NA_DH, q, jnp.zeros_like(q))
        q_head[1, sub] = jnp.where(lane >= NA_DH, q, jnp.zeros_like(q))
        r0 = blk * NA_QROWS + sub * NA_SUB_ROWS
        w0 = jnp.clip(r0 - NA_KR // 2, 0, rows - NA_WIN_ROWS)
        win = pl.ds(pl.multiple_of(w0 * GRID_W, 256), NA_WIN)
        kw.append(k_ref[win, :])
        vw.append(v_ref[win, :])
        kind.append(jnp.where(r0 == 0, 0, jnp.where(r0 == rows - NA_SUB_ROWS, 2, 1)))

    s_lat, s_ctx, p_lat, p_ctx, denom, out = {}, {}, {}, {}, {}, {}
    for hh, sub in chains:
        qs = q_head[hh, sub]
        s_lat[hh, sub] = lax.dot_general(qs, kw[sub], nt, preferred_element_type=F32) + t_ref[hh, kind[sub]]
        s_ctx[hh, sub] = lax.dot_general(qs, kc, nt, preferred_element_type=F32)
    for ch in chains:
        m = jnp.maximum(jnp.max(s_lat[ch], axis=-1, keepdims=True), jnp.max(s_ctx[ch], axis=-1, keepdims=True))
        pl_ = jnp.exp2(s_lat[ch] - m)
        pc_ = jnp.exp2(s_ctx[ch] - m)
        denom[ch] = jnp.sum(pl_, axis=-1, keepdims=True) + jnp.sum(pc_, axis=-1, keepdims=True)
        p_lat[ch] = pl_.astype(BF16)
        p_ctx[ch] = pc_.astype(BF16)
    for hh, sub in chains:
        o = (jnp.dot(p_lat[hh, sub], vw[sub], preferred_element_type=F32)
             + jnp.dot(p_ctx[hh, sub], vc, preferred_element_type=F32))
        out[hh, sub] = o / denom[hh, sub]
    for sub in range(NA_SUBS):
        rows_ = slice(sub * NA_SUB_TOK, (sub + 1) * NA_SUB_TOK)
        o_ref[rows_, :] = jnp.where(lane < NA_DH, out[0, sub], out[1, sub]).astype(o_ref.dtype)


def _na(p, rpb, batch, seq, ctx_len, col0):
    rows = seq // GRID_W
    n_blk = rows // NA_QROWS
    n_pairs = NA_HEADS // 2
    cq = col0 // LANES
    ck = cq + n_pairs
    cv = ck + n_pairs
    ctx_block0 = batch * seq // ctx_len
    return pl.pallas_call(
        functools.partial(_na_kernel, rows=rows),
        grid=(n_pairs, batch, n_blk),
        in_specs=[
            pl.BlockSpec((NA_QTOK, LANES), lambda hp, b, r: (b * n_blk + r, cq + hp)),
            pl.BlockSpec((seq, LANES), lambda hp, b, r: (b, ck + hp)),
            pl.BlockSpec((seq, LANES), lambda hp, b, r: (b, cv + hp)),
            pl.BlockSpec((ctx_len, LANES), lambda hp, b, r: (ctx_block0 + b, ck + hp)),
            pl.BlockSpec((ctx_len, LANES), lambda hp, b, r: (ctx_block0 + b, cv + hp)),
            pl.BlockSpec((2, 2 * NA_KR - 1, LANES), lambda hp, b, r: (hp, 0, 0)),
        ],
        out_specs=pl.BlockSpec((NA_QTOK, LANES), lambda hp, b, r: (b * n_blk + r, hp)),
        out_shape=jax.ShapeDtypeStruct((batch * seq, NA_HEADS * NA_DH), BF16),
        scratch_shapes=[pltpu.VMEM((2, len(NA_KINDS), NA_SUB_TOK, NA_WIN), F32)],
        compiler_params=_params(("arbitrary", "arbitrary", "arbitrary")),
        name="natten",
    )(p, p, p, p, p, rpb)


def _merge_kernel(of_ref, ob_ref, r_ref, gn_ref, b_ref, wa_ref, wb_ref, m1a_ref, m1b_ref, m2a_ref, m2b_ref,
                  o_ref, a_ref):
    for h in range(GLA_HEADS):
        cols = slice(h * GLA_DV, (h + 1) * GLA_DV)
        tot = of_ref[:, cols].astype(F32) + ob_ref[:, cols].astype(F32)
        a_ref[:, cols] = (_rms(tot, gn_ref[...]) * _silu(r_ref[:, cols].astype(F32))).astype(BF16)

    gate_refs = ((m1a_ref, m2a_ref), (m1b_ref, m2b_ref))
    gate_w = m1a_ref.shape[1]
    for c0 in range(0, o_ref.shape[1], MERGE_STRIP):
        cols = slice(c0, c0 + MERGE_STRIP)
        m1_ref, m2_ref = gate_refs[c0 // gate_w]
        gcols = slice(c0 % gate_w, c0 % gate_w + MERGE_STRIP)
        a = jnp.dot(a_ref[...], wa_ref[:, cols], preferred_element_type=F32)
        b = jnp.dot(b_ref[...], wb_ref[:, cols], preferred_element_type=F32)
        m = _sigmoid(m1_ref[:, gcols].astype(F32)) * a + _sigmoid(m2_ref[:, gcols].astype(F32)) * b
        o_ref[:, cols] = m.astype(o_ref.dtype)


def _merge(o_fwd, o_bwd, gn, o_na, w_gla_o, w_na_o, p, col_r, col_m1):
    n_rows = o_fwd.shape[0]
    v_w = o_fwd.shape[1]
    gw = D_MODEL // 2
    c1 = col_m1 // gw
    once = dict(pipeline_mode=pl.Buffered(1))
    gate_spec = lambda c: pl.BlockSpec((ROW_TILE, gw), lambda i: (i, c))
    return pl.pallas_call(
        _merge_kernel,
        grid=(n_rows // ROW_TILE,),
        in_specs=[pl.BlockSpec((ROW_TILE, v_w), lambda i: (i, 0)),
                  pl.BlockSpec((ROW_TILE, v_w), lambda i: (i, 0)),
                  pl.BlockSpec((ROW_TILE, v_w), lambda i: (i, col_r // v_w)),
                  pl.BlockSpec((1, GLA_DV), lambda i: (0, 0)),
                  pl.BlockSpec((ROW_TILE, o_na.shape[1]), lambda i: (i, 0)),
                  pl.BlockSpec(w_gla_o.shape, lambda i: (0, 0), **once),
                  pl.BlockSpec(w_na_o.shape, lambda i: (0, 0), **once),
                  gate_spec(c1), gate_spec(c1 + 1), gate_spec(c1 + 2), gate_spec(c1 + 3)],
        out_specs=pl.BlockSpec((ROW_TILE, D_MODEL), lambda i: (i, 0)),
        out_shape=jax.ShapeDtypeStruct((n_rows, D_MODEL), BF16),
        scratch_shapes=[pltpu.VMEM((ROW_TILE, v_w), BF16)],
        compiler_params=_params(("arbitrary",)),
        name="merge",
    )(o_fwd, o_bwd, p, gn, o_na, w_gla_o, w_na_o, p, p, p, p)


def _outproj_kernel(m_ref, w_ref, x_ref, mod_ref, g_ref, o_ref):
    y = _mixed_dot(m_ref[...], w_ref[...])
    o_ref[...] = x_ref[...] + mod_ref[5:6, :] * _rms(y, g_ref[3:4, :])


def _outproj(m, w_out, h1, mod, norm_g):
    n_tiles = m.shape[0] // ROW_TILE
    tiles_per_batch = n_tiles // 2
    return pl.pallas_call(
        _outproj_kernel,
        grid=(n_tiles,),
        in_specs=[pl.BlockSpec((ROW_TILE, D_MODEL), lambda i: (i, 0)),
                  pl.BlockSpec((D_MODEL, D_MODEL), lambda i: (0, 0), pipeline_mode=pl.Buffered(1)),
                  pl.BlockSpec((ROW_TILE, D_MODEL), lambda i: (i, 0)),
                  pl.BlockSpec((None, N_MOD, D_MODEL), lambda i: (i // tiles_per_batch, 0, 0)),
                  pl.BlockSpec((6, D_MODEL), lambda i: (0, 0))],
        out_specs=pl.BlockSpec((ROW_TILE, D_MODEL), lambda i: (i, 0)),
        out_shape=jax.ShapeDtypeStruct((m.shape[0], D_MODEL), F32),
        compiler_params=_params(("arbitrary",)),
        name="outproj",
    )(m, w_out, h1, mod, norm_g)


def kernel(x, c, ctx, c_ctx, w_ada, b_ada, norm_g, ffn_wg, ffn_wu, ffn_wd, w_in, gla_wg, gla_bg, gla_norm_g,
           w_gla_o, na_rpb, w_na_o, w_out):
    batch, seq, d = x.shape
    ctx_len = ctx.shape[1]
    depth = w_ada.shape[0]
    assert d == D_MODEL and batch == 2 and batch * ctx_len == ROW_TILE and depth == 1
    assert seq % ROW_TILE == 0 and seq // GRID_W == GRID_W

    qk_w = GLA_HEADS * GLA_DK
    v_w = GLA_HEADS * GLA_DV
    na_w = NA_HEADS * NA_DH
    gate0 = 2 * qk_w + 2 * v_w
    gate1 = gate0 + 2 * GLA_GATE_RANK
    col_nq = gate0
    col_m1 = col_nq + 3 * na_w

    cvec = jnp.zeros((8, d), F32).at[0:batch].set(c).at[batch].set(c_ctx)
    cos, sin = _rope_tables(seq)
    h = x.reshape(batch * seq, d)
    hc = ctx.reshape(batch * ctx_len, d)
    lat_tiles_per_batch = seq // BIG_ROW_TILE
    for l in range(depth):
        mod = _modulation(cvec, w_ada[l], b_ada[l].reshape(1, -1)).reshape(8, N_MOD, d)
        g = norm_g[l]
        h1 = _ffn(h, mod, g, ffn_wg[l], ffn_wu[l], ffn_wd[l], 0, 0, tm=BIG_ROW_TILE, tf=FF_TILE,
                  mod_row=lambda i: i // lat_tiles_per_batch)
        h1c = _ffn(hc, mod, g, ffn_wg[l], ffn_wu[l], ffn_wd[l], 0, 0, tm=ROW_TILE, tf=FF_TILE,
                   mod_row=lambda i: batch)
        ctx_cols = [(qk_w, 2 * qk_w + v_w), (col_nq + na_w, col_nq + 3 * na_w)]
        p, gin = _inproj(h1, h1c, mod, g, jnp.swapaxes(w_in[l], 0, 1), cos, sin, seq, gate0, gate1, ctx_cols)
        wg_pad = jnp.zeros((2, LANES, qk_w), F32)
        wg_pad = wg_pad.at[0, :GLA_GATE_RANK].set(gla_wg[l, 0])
        wg_pad = wg_pad.at[1, GLA_GATE_RANK:2 * GLA_GATE_RANK].set(gla_wg[l, 1])
        bg = gla_bg[l].reshape(2, 1, qk_w)
        gn = gla_norm_g[l].reshape(1, GLA_DV)
        o_fwd, o_bwd = _gla(p, gin, wg_pad, bg, batch, seq)
        rpb = jnp.pad(na_rpb[l], ((0, 0), (0, 0), (0, LANES - (2 * NA_KC - 1))))
        o_na = _na(p, rpb, batch, seq, ctx_len, col_nq)
        m = _merge(o_fwd, o_bwd, gn, o_na, w_gla_o[l].astype(BF16), w_na_o[l].astype(BF16), p,
                   2 * qk_w + v_w, col_m1)
        h2 = _outproj(m, w_out[l], h1, mod, g)
        h = _ffn(h2, mod, g, ffn_wg[l], ffn_wu[l], ffn_wd[l], 1, 2, tm=BIG_ROW_TILE, tf=FF_TILE,
                 mod_row=lambda i: i // lat_tiles_per_batch)
    return h.reshape(batch, seq, d)
```

```python
import functools

import numpy as np
import jax
import jax.numpy as jnp
from jax import lax
from jax.experimental import pallas as pl
from jax.experimental.pallas import tpu as pltpu

F32 = jnp.float32
BF16 = jnp.bfloat16

D_MODEL = 2048
GRID_W = 64
GLA_HEADS = 4
GLA_DK = 256
GLA_DV = 512
GLA_GATE_RANK = 16
GLA_GATE_TEMP = 16.0
NA_HEADS = 16
NA_DH = 64
NA_KR = 8
NA_KC = 16
ROPE_THETA = 10000.0
EPS = 1e-6
NEG_INF = -1e30
N_MOD = 9
LOG2E = 1.4426950408889634

LANES = 128
ROW_TILE = 512
BIG_ROW_TILE = 1024
FF_TILE = 512
PROJ_TILE = 1024
MERGE_STRIP = 512
GLA_BLOCK = 256
GLA_HEADS_PER_STEP = 4
NA_QROWS = 16
NA_SUB_ROWS = 4
NA_WIN_ROWS = 12
VMEM_LIMIT = 56 * 1024 * 1024
VMEM_LIMIT_FFN = 60 * 1024 * 1024


def _params(sem, vmem_limit=VMEM_LIMIT):
    return pltpu.CompilerParams(dimension_semantics=sem, vmem_limit_bytes=vmem_limit)


def _rms(x, g):
    return x * lax.rsqrt(jnp.mean(x * x, axis=-1, keepdims=True) + EPS) * g


def _sigmoid(x):
    return 0.5 * jnp.tanh(0.5 * x) + 0.5


def _silu(x):
    h = 0.5 * x
    return h * jnp.tanh(h) + h


def _mod_kernel(c_ref, w_ref, b_ref, o_ref):
    o_ref[...] = jnp.dot(_silu(c_ref[...]), w_ref[...], preferred_element_type=F32) + b_ref[...]


def _modulation(cvec, w_ada, b_ada):
    n = w_ada.shape[1]
    tn = 1024
    return pl.pallas_call(
        _mod_kernel,
        grid=(n // tn,),
        in_specs=[pl.BlockSpec((8, D_MODEL), lambda j: (0, 0)),
                  pl.BlockSpec((D_MODEL, tn), lambda j: (0, j)),
                  pl.BlockSpec((1, tn), lambda j: (0, j))],
        out_specs=pl.BlockSpec((8, tn), lambda j: (0, j)),
        out_shape=jax.ShapeDtypeStruct((8, n), F32),
        compiler_params=_params(("arbitrary",)),
        name="adaln_mod",
    )(cvec, w_ada, b_ada)


def _mixed_dot(a, w):
    return lax.dot_general(a, w, (((1,), (0,)), ((), ())), preferred_element_type=F32)


def _ffn_kernel(x_ref, mod_ref, g_ref, wg_ref, wu_ref, wd_ref, o_hbm, h_ref, acc_ref, sem, *, sub):
    i = pl.program_id(0)
    f = pl.program_id(1)
    n_tiles = pl.num_programs(0)
    last = pl.num_programs(1) - 1
    tm = x_ref.shape[0]
    subtiles = [pl.ds(r0, ROW_TILE) for r0 in range(0, tm, ROW_TILE)]

    def out_copy(k, tile):
        return pltpu.make_async_copy(
            acc_ref.at[subtiles[k], :], o_hbm.at[pl.ds(tile * tm + k * ROW_TILE, ROW_TILE), :], sem.at[k])

    def gate_up(h):
        g = _mixed_dot(h, wg_ref[...])
        u = _mixed_dot(h, wu_ref[...])
        return (_silu(g) * u).astype(BF16)

    @pl.when((i == 0) & (f == 0))
    def _():
        acc_ref[...] = jnp.zeros_like(acc_ref)
        for k in range(len(subtiles)):
            out_copy(k, 0).start()

    @pl.when(f == 0)
    def _():
        gain = g_ref[2 * sub:2 * sub + 1, :] * (1.0 + mod_ref[3 * sub + 1:3 * sub + 2, :])
        shift = mod_ref[3 * sub:3 * sub + 1, :]
        for k, rows in enumerate(subtiles):
            h = (_rms(x_ref[rows, :], gain) + shift).astype(BF16)
            h_ref[rows, :] = h
            a = gate_up(h)
            out_copy(k, jnp.maximum(i - 1, 0)).wait()
            acc_ref[rows, :] = _mixed_dot(a, wd_ref[...])

    @pl.when((f > 0) & (f < last))
    def _():
        for rows in subtiles:
            acc_ref[rows, :] += _mixed_dot(gate_up(h_ref[rows, :]), wd_ref[...])

    @pl.when(f == last)
    def _():
        gain = 0.5 * mod_ref[3 * sub + 2:3 * sub + 3, :] * g_ref[2 * sub + 1:2 * sub + 2, :]
        for k, rows in enumerate(subtiles):
            y = acc_ref[rows, :] + _mixed_dot(gate_up(h_ref[rows, :]), wd_ref[...])
            acc_ref[rows, :] = x_ref[rows, :] + _rms(y, gain)
            out_copy(k, i).start()

    @pl.when((f == last) & (i == n_tiles - 1))
    def _():
        for k in range(len(subtiles)):
            out_copy(k, i).wait()


def _ffn(x2d, mod, norm_g, wg, wu, wd, which, sub, *, tm, tf, mod_row):
    n_tiles = x2d.shape[0] // tm
    d_ff = wg.shape[2]
    return pl.pallas_call(
        functools.partial(_ffn_kernel, sub=sub),
        grid=(n_tiles, d_ff // tf),
        in_specs=[
            pl.BlockSpec((tm, D_MODEL), lambda i, f: (i, 0)),
            pl.BlockSpec((None, N_MOD, D_MODEL), lambda i, f: (mod_row(i), 0, 0)),
            pl.BlockSpec((6, D_MODEL), lambda i, f: (0, 0)),
            pl.BlockSpec((None, D_MODEL, tf), lambda i, f: (which, 0, f)),
            pl.BlockSpec((None, D_MODEL, tf), lambda i, f: (which, 0, f)),
            pl.BlockSpec((None, tf, D_MODEL), lambda i, f: (which, f, 0)),
        ],
        out_specs=pl.BlockSpec(memory_space=pl.ANY),
        out_shape=jax.ShapeDtypeStruct(x2d.shape, F32),
        scratch_shapes=[pltpu.VMEM((tm, D_MODEL), BF16), pltpu.VMEM((tm, D_MODEL), F32),
                        pltpu.SemaphoreType.DMA((tm // ROW_TILE,))],
        compiler_params=_params(("arbitrary", "arbitrary"), VMEM_LIMIT_FFN),
        name="ffn%d_%d" % (sub, tm),
    )(x2d, mod, norm_g, wg, wu, wd)


N_ROPE_TILES = 2 * GLA_HEADS * GLA_DK // PROJ_TILE
N_Q_TILES = N_ROPE_TILES // 2


def _rope_tables(seq):
    half = GLA_DK // 4
    freqs = ROPE_THETA ** (-np.arange(half, dtype=np.float64) / half)
    t = np.arange(seq)
    cos_parts, sin_parts = [], []
    for pos in (t // GRID_W, t % GRID_W):
        ang = pos[:, None].astype(np.float64) * freqs
        cos_parts += [np.cos(ang), np.cos(ang)]
        sin_parts += [-np.sin(ang), np.sin(ang)]
    cos = np.concatenate(cos_parts, axis=1)
    sin = np.concatenate(sin_parts, axis=1)
    cos = np.concatenate([cos, np.ones((BIG_ROW_TILE, GLA_DK))], axis=0)
    sin = np.concatenate([sin, np.zeros((BIG_ROW_TILE, GLA_DK))], axis=0)
    return jnp.asarray(cos, F32), jnp.asarray(sin, F32)


def _dot_nt(a, w_t):
    return lax.dot_general(a, w_t, (((1,), (1,)), ((), ())), preferred_element_type=F32)


def _inproj_kernel(x_ref, xc_ref, mod_ref, g_ref, w_ref, wgate_ref, cos_ref, sin_ref, p_ref, gate_ref, h_ref,
                   *, n_lat_tiles, ctx_tiles):
    i = pl.program_id(0)
    n = pl.program_id(1)

    def project(src_ref, live):
        r = slice(0, src_ref.shape[0])
        alive = (lambda cond: cond) if live is None else (lambda cond: cond & live)

        @pl.when(n == 0)
        def _():
            h = _rms(src_ref[...], g_ref[2:3, :]) * (1.0 + mod_ref[4:5, :]) + mod_ref[3:4, :]
            h_ref[r, :] = h.astype(BF16)
            gate = _dot_nt(h_ref[r, :], wgate_ref[...])
            gate_ref[r, :] = jnp.concatenate(
                [gate, jnp.zeros((gate.shape[0], LANES - gate.shape[1]), F32)], axis=1)

        @pl.when(alive(n < N_ROPE_TILES))
        def _():
            y = _dot_nt(h_ref[r, :], w_ref[...])
            cos = cos_ref[r, :]
            sin = sin_ref[r, :]
            qscale = jnp.where(n < N_Q_TILES, GLA_DK ** -0.5, 1.0).astype(F32)
            for j in range(PROJ_TILE // GLA_DK):
                yj = y[:, j * GLA_DK:(j + 1) * GLA_DK]
                swapped = jnp.concatenate(
                    [pltpu.roll(yj[:, g * LANES:(g + 1) * LANES], LANES // 2, 1) for g in range(GLA_DK // LANES)],
                    axis=1)
                p_ref[r, j * GLA_DK:(j + 1) * GLA_DK] = ((yj * cos + swapped * sin) * qscale).astype(BF16)

        @pl.when(alive(n >= N_ROPE_TILES))
        def _():
            p_ref[r, :] = _dot_nt(h_ref[r, :], w_ref[...]).astype(BF16)

        if live is not None:
            @pl.when(jnp.logical_not(live))
            def _():
                p_ref[r, :] = jnp.zeros((src_ref.shape[0], PROJ_TILE), BF16)

    ctx_live = functools.reduce(jnp.logical_or, [n == t for t in ctx_tiles])
    pl.when(i < n_lat_tiles)(lambda: project(x_ref, None))
    pl.when(i >= n_lat_tiles)(lambda: project(xc_ref, ctx_live))


def _inproj(h1, h1c, mod, norm_g, w_in_t, cos, sin, seq, gate0, gate1, ctx_cols):
    tm = BIG_ROW_TILE
    n_lat_tiles = h1.shape[0] // tm
    n_rows = h1.shape[0] + h1c.shape[0]
    tiles_per_batch = n_lat_tiles // 2
    n_a_tiles = gate0 // PROJ_TILE
    n_b_tiles = (w_in_t.shape[0] - gate1) // PROJ_TILE
    n_gate = gate1 - gate0
    ctx_tiles = [t for t in range(n_a_tiles + n_b_tiles)
                 if any(lo < (t + 1) * PROJ_TILE and t * PROJ_TILE < hi for lo, hi in ctx_cols)]

    def weight_row0(i, n):
        t = jnp.int32(ctx_tiles[0])
        for live in ctx_tiles[1:]:
            t = jnp.where(n >= live, live, t)
        t = jnp.where(i < n_lat_tiles, n, t)
        return pl.multiple_of(t * PROJ_TILE + jnp.where(t < n_a_tiles, 0, n_gate), n_gate), 0

    tab_spec = pl.BlockSpec(
        (tm, GLA_DK), lambda i, n: (jnp.where(i < n_lat_tiles, i % tiles_per_batch, seq // tm), 0))
    return pl.pallas_call(
        functools.partial(_inproj_kernel, n_lat_tiles=n_lat_tiles, ctx_tiles=ctx_tiles),
        grid=(n_lat_tiles + 1, n_a_tiles + n_b_tiles),
        in_specs=[pl.BlockSpec((tm, D_MODEL), lambda i, n: (jnp.minimum(i, n_lat_tiles - 1), 0)),
                  pl.BlockSpec(h1c.shape, lambda i, n: (0, 0)),
                  pl.BlockSpec((None, N_MOD, D_MODEL), lambda i, n: (jnp.minimum(i // tiles_per_batch, 2), 0, 0)),
                  pl.BlockSpec((6, D_MODEL), lambda i, n: (0, 0)),
                  pl.BlockSpec((pl.Element(PROJ_TILE), pl.Element(D_MODEL)), weight_row0),
                  pl.BlockSpec((n_gate, D_MODEL), lambda i, n: (gate0 // n_gate, 0)),
                  tab_spec, tab_spec],
        out_specs=[pl.BlockSpec((tm, PROJ_TILE), lambda i, n: (i, n)),
                   pl.BlockSpec((tm, LANES), lambda i, n: (i, 0))],
        out_shape=[jax.ShapeDtypeStruct((n_rows, (n_a_tiles + n_b_tiles) * PROJ_TILE), BF16),
                   jax.ShapeDtypeStruct((n_rows, LANES), F32)],
        scratch_shapes=[pltpu.VMEM((tm, D_MODEL), BF16)],
        compiler_params=_params(("arbitrary", "arbitrary")),
        name="inproj",
    )(h1, h1c, mod, norm_g, w_in_t, w_in_t, cos, sin)


def _log2_sigmoid(z):
    t = z * LOG2E
    return jnp.minimum(t, 0.0) - jnp.log2(1.0 + jnp.exp2(-jnp.abs(t)))


def _gla_kernel(qf_ref, kf_ref, vf_ref, gf_ref, qb_ref, kb_ref, vb_ref, gb_ref, wg_ref, bg_ref,
                of_ref, ob_ref, stf_ref, stb_ref):
    @pl.when(pl.program_id(2) == 0)
    def _():
        stf_ref[...] = jnp.zeros_like(stf_ref)
        stb_ref[...] = jnp.zeros_like(stb_ref)

    c = GLA_BLOCK
    nt = (((1,), (1,)), ((), ()))
    row = lax.broadcasted_iota(jnp.int32, (c, c), 0)
    col = lax.broadcasted_iota(jnp.int32, (c, c), 1)
    mask = {True: col <= row, False: col >= row}
    tri = {fwd: jnp.where(m, 1.0, 0.0).astype(BF16) for fwd, m in mask.items()}
    refs = {True: (qf_ref, kf_ref, vf_ref, gf_ref, stf_ref, of_ref),
            False: (qb_ref, kb_ref, vb_ref, gb_ref, stb_ref, ob_ref)}
    qcols = lambda head: slice(head * GLA_DK, (head + 1) * GLA_DK)
    vcols = lambda head: slice(head * GLA_DV, (head + 1) * GLA_DV)

    state, log_a, cum, factors, att = {}, {}, {}, {}, {}

    def gate_stage(head, fwd):
        d = 0 if fwd else 1
        state[head, fwd] = refs[fwd][4][head]
        z = jnp.dot(refs[fwd][3][...], wg_ref[d, :, qcols(head)], preferred_element_type=F32)
        log_a[head, fwd] = _log2_sigmoid(z + bg_ref[d, :, qcols(head)]) * (1.0 / GLA_GATE_TEMP)

    def cumsum_stage(head, fwd):
        la = log_a[head, fwd]
        hi = la.astype(BF16)
        lo = (la - hi.astype(F32)).astype(BF16)
        cum[head, fwd] = (jnp.dot(tri[fwd], hi, preferred_element_type=F32)
                          + jnp.dot(tri[fwd], lo, preferred_element_type=F32))

    def factor_stage(head, fwd):
        cm = cum[head, fwd]
        cum_end = cm[c - 1:c, :] if fwd else cm[0:1, :]
        cum_mid = cm[c // 2:c // 2 + 1, :]
        q = refs[fwd][0][:, qcols(head)]
        k = refs[fwd][1][:, qcols(head)]
        factors[head, fwd] = (q * jnp.exp2(cm - cum_mid).astype(BF16),
                              k * jnp.exp2(cum_mid - cm).astype(BF16),
                              q * jnp.exp2(cm).astype(BF16),
                              k * jnp.exp2(cum_end - cm).astype(BF16),
                              jnp.exp2(cum_end))

    def att_stage(head, fwd):
        q_att, k_att = factors[head, fwd][:2]
        a = lax.dot_general(q_att, k_att, nt, preferred_element_type=F32)
        att[head, fwd] = jnp.where(mask[fwd], a, 0.0).astype(BF16)

    def out_stage(head, fwd):
        _, _, q_dec, k_end, decay = factors[head, fwd]
        v = refs[fwd][2][:, vcols(head)]
        st = state[head, fwd]
        o = (jnp.dot(att[head, fwd], v, preferred_element_type=F32)
             + lax.dot_general(q_dec, st.astype(BF16), nt, preferred_element_type=F32))
        kv_t = lax.dot_general(v, k_end, (((0,), (0,)), ((), ())), preferred_element_type=F32)
        refs[fwd][5][:, vcols(head)] = o.astype(BF16)
        refs[fwd][4][head] = decay * st + kv_t

    for stage in (gate_stage, cumsum_stage, factor_stage, att_stage, out_stage):
        for head in range(GLA_HEADS_PER_STEP):
            for fwd in (True, False):
                stage(head, fwd)


def _gla(p, gin, wg_pad, bg, batch, seq):
    n_lat = seq // GLA_BLOCK
    ctx_block0 = batch * n_lat
    hps = GLA_HEADS_PER_STEP
    dk, dv = hps * GLA_DK, hps * GLA_DV

    def lat_block(b, s, forward):
        j = jnp.maximum(s - 1, 0)
        return b * n_lat + (j if forward else n_lat - 1 - j)

    def in_block(b, s, forward):
        return jnp.where(s == 0, ctx_block0 + b, lat_block(b, s, forward))

    kq = GLA_HEADS // hps
    kv = 2 * GLA_HEADS * GLA_DK // dv

    def dir_specs(forward):
        return [
            pl.BlockSpec((GLA_BLOCK, dk), lambda b, h, s: (in_block(b, s, forward), h)),
            pl.BlockSpec((GLA_BLOCK, dk), lambda b, h, s: (in_block(b, s, forward), kq + h)),
            pl.BlockSpec((GLA_BLOCK, dv), lambda b, h, s: (in_block(b, s, forward), kv + h)),
            pl.BlockSpec((GLA_BLOCK, LANES), lambda b, h, s: (in_block(b, s, forward), 0)),
        ]

    out_shape = jax.ShapeDtypeStruct((batch * seq, GLA_HEADS * GLA_DV), BF16)
    state = pltpu.VMEM((hps, GLA_DV, GLA_DK), F32)
    return pl.pallas_call(
        _gla_kernel,
        grid=(batch, GLA_HEADS // hps, n_lat + 1),
        in_specs=dir_specs(True) + dir_specs(False) + [
            pl.BlockSpec((2, LANES, dk), lambda b, h, s: (0, 0, h)),
            pl.BlockSpec((2, 1, dk), lambda b, h, s: (0, 0, h))],
        out_specs=[pl.BlockSpec((GLA_BLOCK, dv), lambda b, h, s: (lat_block(b, s, True), h)),
                   pl.BlockSpec((GLA_BLOCK, dv), lambda b, h, s: (lat_block(b, s, False), h))],
        out_shape=[out_shape, out_shape],
        scratch_shapes=[state, state],
        compiler_params=_params(("arbitrary", "arbitrary", "arbitrary")),
        name="gla",
    )(p, p, p, gin, p, p, p, gin, wg_pad, bg)


NA_QTOK = NA_QROWS * GRID_W
NA_SUB_TOK = NA_SUB_ROWS * GRID_W
NA_WIN = NA_WIN_ROWS * GRID_W
NA_SUBS = NA_QROWS // NA_SUB_ROWS
NA_KINDS = ((0, 0), (NA_SUB_ROWS, NA_SUB_ROWS - NA_KR // 2), (GRID_W - NA_SUB_ROWS, GRID_W - NA_WIN_ROWS))


def _na_build_bias(rpb_ref, t_ref, rows):
    n_slots = 2 * NA_KR - 1
    lane = lax.broadcasted_iota(jnp.int32, (GRID_W, LANES), 1)
    q_col = lax.broadcasted_iota(jnp.int32, (GRID_W, LANES), 0)
    k_col = lane & (GRID_W - 1)
    upper = lane >= GRID_W
    c0 = jnp.clip(q_col - NA_KC // 2, 0, GRID_W - NA_KC)
    col_ok = (k_col >= c0) & (k_col < c0 + NA_KC)
    neg = jnp.full((GRID_W, LANES), NEG_INF, F32)

    def toeplitz(hh, slot, lane_off):
        if not 0 <= slot < n_slots:
            return jnp.zeros((GRID_W, LANES), F32)
        base = jnp.broadcast_to(rpb_ref[hh, slot:slot + 1, :], (GRID_W, LANES)) * LOG2E
        return pltpu.roll(base, (LANES - (NA_KC - 1) + lane_off) % LANES, 1, stride=1, stride_axis=0)

    for hh in range(2):
        pairs = {}
        for kind, (r0, w0) in enumerate(NA_KINDS):
            for qr in range(NA_SUB_ROWS):
                band0 = min(max(r0 + qr - NA_KR // 2, 0), rows - NA_KR) - w0
                slot0 = w0 - (r0 + qr) + NA_KR - 1
                for g in range(NA_WIN_ROWS // 2):
                    slot = slot0 + 2 * g
                    lo_ok = band0 <= 2 * g < band0 + NA_KR
                    hi_ok = band0 <= 2 * g + 1 < band0 + NA_KR
                    if lo_ok or hi_ok:
                        if slot not in pairs:
                            pairs[slot] = jnp.where(upper, toeplitz(hh, slot + 1, GRID_W), toeplitz(hh, slot, 0))
                        ok = col_ok if (lo_ok and hi_ok) else (col_ok & upper if hi_ok else col_ok & ~upper)
                        tile = jnp.where(ok, pairs[slot], neg)
                    else:
                        tile = neg
                    t_ref[hh, kind, qr * GRID_W:(qr + 1) * GRID_W, g * LANES:(g + 1) * LANES] = tile


def _na_kernel(q_ref, k_ref, v_ref, kc_ref, vc_ref, rpb_ref, o_ref, t_ref, *, rows):
    b = pl.program_id(1)
    blk = pl.program_id(2)
    n_blk = rows // NA_QROWS

    pl.when((b == 0) & (blk == 0))(lambda: _na_build_bias(rpb_ref, t_ref, rows))

    nt = (((1,), (1,)), ((), ()))
    kc = kc_ref[...]
    vc = vc_ref[...]
    lane = lax.broadcasted_iota(jnp.int32, (NA_SUB_TOK, LANES), 1)
    chains = [(hh, sub) for sub in range(NA_SUBS) for hh in range(2)]
    q_head, kw, vw, kind = {}, [], [], []
    for sub in range(NA_SUBS):
        q = q_ref[sub * NA_SUB_TOK:(sub + 1) * NA_SUB_TOK, :].astype(F32) * (NA_DH ** -0.5 * LOG2E)
        q = q.astype(BF16)
        q_head[0, sub] = jnp.where(lane < NA_DH, q, jnp.zeros_like(q))
        q_head[1, sub] = jnp.where(lane >= NA_DH, q, jnp.zeros_like(q))
        r0 = blk * NA_QROWS + sub * NA_SUB_ROWS
        w0 = jnp.clip(r0 - NA_KR // 2, 0, rows - NA_WIN_ROWS)
        win = pl.ds(pl.multiple_of(w0 * GRID_W, 256), NA_WIN)
        kw.append(k_ref[win, :])
        vw.append(v_ref[win, :])
        kind.append(jnp.where(r0 == 0, 0, jnp.where(r0 == rows - NA_SUB_ROWS, 2, 1)))

    s_lat, s_ctx, p_lat, p_ctx, denom, out = {}, {}, {}, {}, {}, {}
    for hh, sub in chains:
        qs = q_head[hh, sub]
        s_lat[hh, sub] = lax.dot_general(qs, kw[sub], nt, preferred_element_type=F32) + t_ref[hh, kind[sub]]
        s_ctx[hh, sub] = lax.dot_general(qs, kc, nt, preferred_element_type=F32)
    for ch in chains:
        m = jnp.maximum(jnp.max(s_lat[ch], axis=-1, keepdims=True), jnp.max(s_ctx[ch], axis=-1, keepdims=True))
        pl_ = jnp.exp2(s_lat[ch] - m)
        pc_ = jnp.exp2(s_ctx[ch] - m)
        denom[ch] = jnp.sum(pl_, axis=-1, keepdims=True) + jnp.sum(pc_, axis=-1, keepdims=True)
        p_lat[ch] = pl_.astype(BF16)
        p_ctx[ch] = pc_.astype(BF16)
    for hh, sub in chains:
        o = (jnp.dot(p_lat[hh, sub], vw[sub], preferred_element_type=F32)
             + jnp.dot(p_ctx[hh, sub], vc, preferred_element_type=F32))
        out[hh, sub] = o / denom[hh, sub]
    for sub in range(NA_SUBS):
        rows_ = slice(sub * NA_SUB_TOK, (sub + 1) * NA_SUB_TOK)
        o_ref[rows_, :] = jnp.where(lane < NA_DH, out[0, sub], out[1, sub]).astype(o_ref.dtype)


def _na(p, rpb, batch, seq, ctx_len, col0):
    rows = seq // GRID_W
    n_blk = rows // NA_QROWS
    n_pairs = NA_HEADS // 2
    cq = col0 // LANES
    ck = cq + n_pairs
    cv = ck + n_pairs
    ctx_block0 = batch * seq // ctx_len
    return pl.pallas_call(
        functools.partial(_na_kernel, rows=rows),
        grid=(n_pairs, batch, n_blk),
        in_specs=[
            pl.BlockSpec((NA_QTOK, LANES), lambda hp, b, r: (b * n_blk + r, cq + hp)),
            pl.BlockSpec((seq, LANES), lambda hp, b, r: (b, ck + hp)),
            pl.BlockSpec((seq, LANES), lambda hp, b, r: (b, cv + hp)),
            pl.BlockSpec((ctx_len, LANES), lambda hp, b, r: (ctx_block0 + b, ck + hp)),
            pl.BlockSpec((ctx_len, LANES), lambda hp, b, r: (ctx_block0 + b, cv + hp)),
            pl.BlockSpec((2, 2 * NA_KR - 1, LANES), lambda hp, b, r: (hp, 0, 0)),
        ],
        out_specs=pl.BlockSpec((NA_QTOK, LANES), lambda hp, b, r: (b * n_blk + r, hp)),
        out_shape=jax.ShapeDtypeStruct((batch * seq, NA_HEADS * NA_DH), BF16),
        scratch_shapes=[pltpu.VMEM((2, len(NA_KINDS), NA_SUB_TOK, NA_WIN), F32)],
        compiler_params=_params(("arbitrary", "arbitrary", "arbitrary")),
        name="natten",
    )(p, p, p, p, p, rpb)


def _merge_kernel(of_ref, ob_ref, r_ref, gn_ref, b_ref, wa_ref, wb_ref, m1a_ref, m1b_ref, m2a_ref, m2b_ref,
                  o_ref, a_ref):
    for h in range(GLA_HEADS):
        cols = slice(h * GLA_DV, (h + 1) * GLA_DV)
        tot = of_ref[:, cols].astype(F32) + ob_ref[:, cols].astype(F32)
        a_ref[:, cols] = (_rms(tot, gn_ref[...]) * _silu(r_ref[:, cols].astype(F32))).astype(BF16)

    gate_refs = ((m1a_ref, m2a_ref), (m1b_ref, m2b_ref))
    gate_w = m1a_ref.shape[1]
    for c0 in range(0, o_ref.shape[1], MERGE_STRIP):
        cols = slice(c0, c0 + MERGE_STRIP)
        m1_ref, m2_ref = gate_refs[c0 // gate_w]
        gcols = slice(c0 % gate_w, c0 % gate_w + MERGE_STRIP)
        a = _mixed_dot(a_ref[...], wa_ref[:, cols])
        b = _mixed_dot(b_ref[...], wb_ref[:, cols])
        m = _sigmoid(m1_ref[:, gcols].astype(F32)) * a + _sigmoid(m2_ref[:, gcols].astype(F32)) * b
        o_ref[:, cols] = m.astype(o_ref.dtype)


def _merge(o_fwd, o_bwd, gn, o_na, w_gla_o, w_na_o, p, col_r, col_m1):
    n_rows = o_fwd.shape[0]
    v_w = o_fwd.shape[1]
    gw = D_MODEL // 2
    c1 = col_m1 // gw
    once = dict(pipeline_mode=pl.Buffered(1))
    gate_spec = lambda c: pl.BlockSpec((ROW_TILE, gw), lambda i: (i, c))
    return pl.pallas_call(
        _merge_kernel,
        grid=(n_rows // ROW_TILE,),
        in_specs=[pl.BlockSpec((ROW_TILE, v_w), lambda i: (i, 0)),
                  pl.BlockSpec((ROW_TILE, v_w), lambda i: (i, 0)),
                  pl.BlockSpec((ROW_TILE, v_w), lambda i: (i, col_r // v_w)),
                  pl.BlockSpec((1, GLA_DV), lambda i: (0, 0)),
                  pl.BlockSpec((ROW_TILE, o_na.shape[1]), lambda i: (i, 0)),
                  pl.BlockSpec(w_gla_o.shape, lambda i: (0, 0), **once),
                  pl.BlockSpec(w_na_o.shape, lambda i: (0, 0), **once),
                  gate_spec(c1), gate_spec(c1 + 1), gate_spec(c1 + 2), gate_spec(c1 + 3)],
        out_specs=pl.BlockSpec((ROW_TILE, D_MODEL), lambda i: (i, 0)),
        out_shape=jax.ShapeDtypeStruct((n_rows, D_MODEL), BF16),
        scratch_shapes=[pltpu.VMEM((ROW_TILE, v_w), BF16)],
        compiler_params=_params(("arbitrary",)),
        name="merge",
    )(o_fwd, o_bwd, p, gn, o_na, w_gla_o, w_na_o, p, p, p, p)


def _outproj_kernel(m_ref, w_ref, x_ref, mod_ref, g_ref, o_ref):
    y = _mixed_dot(m_ref[...], w_ref[...])
    o_ref[...] = x_ref[...] + mod_ref[5:6, :] * _rms(y, g_ref[3:4, :])


def _outproj(m, w_out, h1, mod, norm_g):
    n_tiles = m.shape[0] // ROW_TILE
    tiles_per_batch = n_tiles // 2
    return pl.pallas_call(
        _outproj_kernel,
        grid=(n_tiles,),
        in_specs=[pl.BlockSpec((ROW_TILE, D_MODEL), lambda i: (i, 0)),
                  pl.BlockSpec((D_MODEL, D_MODEL), lambda i: (0, 0), pipeline_mode=pl.Buffered(1)),
                  pl.BlockSpec((ROW_TILE, D_MODEL), lambda i: (i, 0)),
                  pl.BlockSpec((None, N_MOD, D_MODEL), lambda i: (i // tiles_per_batch, 0, 0)),
                  pl.BlockSpec((6, D_MODEL), lambda i: (0, 0))],
        out_specs=pl.BlockSpec((ROW_TILE, D_MODEL), lambda i: (i, 0)),
        out_shape=jax.ShapeDtypeStruct((m.shape[0], D_MODEL), F32),
        compiler_params=_params(("arbitrary",)),
        name="outproj",
    )(m, w_out, h1, mod, norm_g)


def kernel(x, c, ctx, c_ctx, w_ada, b_ada, norm_g, ffn_wg, ffn_wu, ffn_wd, w_in, gla_wg, gla_bg, gla_norm_g,
           w_gla_o, na_rpb, w_na_o, w_out):
    batch, seq, d = x.shape
    ctx_len = ctx.shape[1]
    depth = w_ada.shape[0]
    assert d == D_MODEL and batch == 2 and batch * ctx_len == ROW_TILE and depth == 1
    assert seq % ROW_TILE == 0 and seq // GRID_W == GRID_W

    qk_w = GLA_HEADS * GLA_DK
    v_w = GLA_HEADS * GLA_DV
    na_w = NA_HEADS * NA_DH
    gate0 = 2 * qk_w + 2 * v_w
    gate1 = gate0 + 2 * GLA_GATE_RANK
    col_nq = gate0
    col_m1 = col_nq + 3 * na_w

    cvec = jnp.zeros((8, d), F32).at[0:batch].set(c).at[batch].set(c_ctx)
    cos, sin = _rope_tables(seq)
    h = x.reshape(batch * seq, d)
    hc = ctx.reshape(batch * ctx_len, d)
    lat_tiles_per_batch = seq // BIG_ROW_TILE
    for l in range(depth):
        mod = _modulation(cvec, w_ada[l], b_ada[l].reshape(1, -1)).reshape(8, N_MOD, d)
        g = norm_g[l]
        h1 = _ffn(h, mod, g, ffn_wg[l], ffn_wu[l], ffn_wd[l], 0, 0, tm=BIG_ROW_TILE, tf=FF_TILE,
                  mod_row=lambda i: i // lat_tiles_per_batch)
        h1c = _ffn(hc, mod, g, ffn_wg[l], ffn_wu[l], ffn_wd[l], 0, 0, tm=ROW_TILE, tf=FF_TILE,
                   mod_row=lambda i: batch)
        ctx_cols = [(qk_w, 2 * qk_w + v_w), (col_nq + na_w, col_nq + 3 * na_w)]
        p, gin = _inproj(h1, h1c, mod, g, jnp.swapaxes(w_in[l], 0, 1), cos, sin, seq, gate0, gate1, ctx_cols)
        wg_pad = jnp.zeros((2, LANES, qk_w), F32)
        wg_pad = wg_pad.at[0, :GLA_GATE_RANK].set(gla_wg[l, 0])
        wg_pad = wg_pad.at[1, GLA_GATE_RANK:2 * GLA_GATE_RANK].set(gla_wg[l, 1])
        bg = gla_bg[l].reshape(2, 1, qk_w)
        gn = gla_norm_g[l].reshape(1, GLA_DV)
        o_fwd, o_bwd = _gla(p, gin, wg_pad, bg, batch, seq)
        rpb = jnp.pad(na_rpb[l], ((0, 0), (0, 0), (0, LANES - (2 * NA_KC - 1))))
        o_na = _na(p, rpb, batch, seq, ctx_len, col_nq)
        m = _merge(o_fwd, o_bwd, gn, o_na, w_gla_o[l], w_na_o[l], p,
                   2 * qk_w + v_w, col_m1)
        h2 = _outproj(m, w_out[l], h1, mod, g)
        h = _ffn(h2, mod, g, ffn_wg[l], ffn_wu[l], ffn_wd[l], 1, 2, tm=BIG_ROW_TILE, tf=FF_TILE,
                 mod_row=lambda i: i // lat_tiles_per_batch)
    return h.reshape(batch, seq, d)
```

```python
import functools

import numpy as np
import jax
import jax.numpy as jnp
from jax import lax
from jax.experimental import pallas as pl
from jax.experimental.pallas import tpu as pltpu

F32 = jnp.float32
BF16 = jnp.bfloat16

D_MODEL = 2048
GRID_W = 64
GLA_HEADS = 4
GLA_DK = 256
GLA_DV = 512
GLA_GATE_RANK = 16
GLA_GATE_TEMP = 16.0
NA_HEADS = 16
NA_DH = 64
NA_KR = 8
NA_KC = 16
ROPE_THETA = 10000.0
EPS = 1e-6
NEG_INF = -1e30
N_MOD = 9
LOG2E = 1.4426950408889634

LANES = 128
ROW_TILE = 512
BIG_ROW_TILE = 1024
FF_TILE = 512
PROJ_TILE = 1024
MERGE_STRIP = 512
GLA_BLOCK = 256
GLA_HEADS_PER_STEP = 4
NA_QROWS = 16
NA_SUB_ROWS = 4
NA_WIN_ROWS = 12
VMEM_LIMIT = 56 * 1024 * 1024
VMEM_LIMIT_FFN = 60 * 1024 * 1024


def _params(sem, vmem_limit=VMEM_LIMIT):
    return pltpu.CompilerParams(dimension_semantics=sem, vmem_limit_bytes=vmem_limit)


def _rms(x, g):
    return x * lax.rsqrt(jnp.mean(x * x, axis=-1, keepdims=True) + EPS) * g


def _sigmoid(x):
    return 0.5 * jnp.tanh(0.5 * x) + 0.5


def _silu(x):
    h = 0.5 * x
    return h * jnp.tanh(h) + h


def _mod_kernel(c_ref, w_ref, b_ref, o_ref):
    o_ref[...] = jnp.dot(_silu(c_ref[...]), w_ref[...], preferred_element_type=F32) + b_ref[...]


def _modulation(cvec, w_ada, b_ada):
    n = w_ada.shape[1]
    tn = 2048
    return pl.pallas_call(
        _mod_kernel,
        grid=(n // tn,),
        in_specs=[pl.BlockSpec((8, D_MODEL), lambda j: (0, 0)),
                  pl.BlockSpec((D_MODEL, tn), lambda j: (0, j)),
                  pl.BlockSpec((1, tn), lambda j: (0, j))],
        out_specs=pl.BlockSpec((8, tn), lambda j: (0, j)),
        out_shape=jax.ShapeDtypeStruct((8, n), F32),
        compiler_params=_params(("arbitrary",)),
        name="adaln_mod",
    )(cvec, w_ada, b_ada)


def _mixed_dot(a, w):
    return lax.dot_general(a, w, (((1,), (0,)), ((), ())), preferred_element_type=F32)


def _ffn_kernel(x_ref, mod_ref, g_ref, wg_ref, wu_ref, wd_ref, o_hbm, h_ref, acc_ref, sem, *, sub):
    i = pl.program_id(0)
    f = pl.program_id(1)
    n_tiles = pl.num_programs(0)
    last = pl.num_programs(1) - 1
    tm = x_ref.shape[0]
    subtiles = [pl.ds(r0, ROW_TILE) for r0 in range(0, tm, ROW_TILE)]

    def out_copy(k, tile):
        return pltpu.make_async_copy(
            acc_ref.at[subtiles[k], :], o_hbm.at[pl.ds(tile * tm + k * ROW_TILE, ROW_TILE), :], sem.at[k])

    def gate_up(h):
        g = _mixed_dot(h, wg_ref[...])
        u = _mixed_dot(h, wu_ref[...])
        return (_silu(g) * u).astype(BF16)

    @pl.when((i == 0) & (f == 0))
    def _():
        acc_ref[...] = jnp.zeros_like(acc_ref)
        for k in range(len(subtiles)):
            out_copy(k, 0).start()

    @pl.when(f == 0)
    def _():
        gain = g_ref[2 * sub:2 * sub + 1, :] * (1.0 + mod_ref[3 * sub + 1:3 * sub + 2, :])
        shift = mod_ref[3 * sub:3 * sub + 1, :]
        for k, rows in enumerate(subtiles):
            h = (_rms(x_ref[rows, :], gain) + shift).astype(BF16)
            h_ref[rows, :] = h
            a = gate_up(h)
            out_copy(k, jnp.maximum(i - 1, 0)).wait()
            acc_ref[rows, :] = _mixed_dot(a, wd_ref[...])

    @pl.when((f > 0) & (f < last))
    def _():
        for rows in subtiles:
            acc_ref[rows, :] += _mixed_dot(gate_up(h_ref[rows, :]), wd_ref[...])

    @pl.when(f == last)
    def _():
        gain = 0.5 * mod_ref[3 * sub + 2:3 * sub + 3, :] * g_ref[2 * sub + 1:2 * sub + 2, :]
        for k, rows in enumerate(subtiles):
            y = acc_ref[rows, :] + _mixed_dot(gate_up(h_ref[rows, :]), wd_ref[...])
            acc_ref[rows, :] = x_ref[rows, :] + _rms(y, gain)
            out_copy(k, i).start()

    @pl.when((f == last) & (i == n_tiles - 1))
    def _():
        for k in range(len(subtiles)):
            out_copy(k, i).wait()


def _ffn(x2d, mod, norm_g, wg, wu, wd, which, sub, *, tm, tf, mod_row):
    n_tiles = x2d.shape[0] // tm
    d_ff = wg.shape[2]
    return pl.pallas_call(
        functools.partial(_ffn_kernel, sub=sub),
        grid=(n_tiles, d_ff // tf),
        in_specs=[
            pl.BlockSpec((tm, D_MODEL), lambda i, f: (i, 0)),
            pl.BlockSpec((None, N_MOD, D_MODEL), lambda i, f: (mod_row(i), 0, 0)),
            pl.BlockSpec((6, D_MODEL), lambda i, f: (0, 0)),
            pl.BlockSpec((None, D_MODEL, tf), lambda i, f: (which, 0, f)),
            pl.BlockSpec((None, D_MODEL, tf), lambda i, f: (which, 0, f)),
            pl.BlockSpec((None, tf, D_MODEL), lambda i, f: (which, f, 0)),
        ],
        out_specs=pl.BlockSpec(memory_space=pl.ANY),
        out_shape=jax.ShapeDtypeStruct(x2d.shape, F32),
        scratch_shapes=[pltpu.VMEM((tm, D_MODEL), BF16), pltpu.VMEM((tm, D_MODEL), F32),
                        pltpu.SemaphoreType.DMA((tm // ROW_TILE,))],
        compiler_params=_params(("arbitrary", "arbitrary"), VMEM_LIMIT_FFN),
        name="ffn%d_%d" % (sub, tm),
    )(x2d, mod, norm_g, wg, wu, wd)


N_ROPE_TILES = 2 * GLA_HEADS * GLA_DK // PROJ_TILE
N_Q_TILES = N_ROPE_TILES // 2


def _rope_tables(seq):
    half = GLA_DK // 4
    freqs = ROPE_THETA ** (-np.arange(half, dtype=np.float64) / half)
    t = np.arange(seq)
    cos_parts, sin_parts = [], []
    for pos in (t // GRID_W, t % GRID_W):
        ang = pos[:, None].astype(np.float64) * freqs
        cos_parts += [np.cos(ang), np.cos(ang)]
        sin_parts += [-np.sin(ang), np.sin(ang)]
    cos = np.concatenate(cos_parts, axis=1)
    sin = np.concatenate(sin_parts, axis=1)
    cos = np.concatenate([cos, np.ones((BIG_ROW_TILE, GLA_DK))], axis=0)
    sin = np.concatenate([sin, np.zeros((BIG_ROW_TILE, GLA_DK))], axis=0)
    return jnp.asarray(cos, F32), jnp.asarray(sin, F32)


def _dot_nt(a, w_t):
    return lax.dot_general(a, w_t, (((1,), (1,)), ((), ())), preferred_element_type=F32)


def _inproj_kernel(x_ref, xc_ref, mod_ref, g_ref, w_ref, wgate_ref, cos_ref, sin_ref, p_ref, gate_ref, h_ref,
                   *, n_lat_tiles, ctx_tiles):
    i = pl.program_id(0)
    n = pl.program_id(1)

    def project(src_ref, live):
        r = slice(0, src_ref.shape[0])
        alive = (lambda cond: cond) if live is None else (lambda cond: cond & live)

        @pl.when(n == 0)
        def _():
            h = _rms(src_ref[...], g_ref[2:3, :]) * (1.0 + mod_ref[4:5, :]) + mod_ref[3:4, :]
            h_ref[r, :] = h.astype(BF16)
            gate = _dot_nt(h_ref[r, :], wgate_ref[...])
            gate_ref[r, :] = jnp.concatenate(
                [gate, jnp.zeros((gate.shape[0], LANES - gate.shape[1]), F32)], axis=1)

        @pl.when(alive(n < N_ROPE_TILES))
        def _():
            y = _dot_nt(h_ref[r, :], w_ref[...])
            cos = cos_ref[r, :]
            sin = sin_ref[r, :]
            qscale = jnp.where(n < N_Q_TILES, GLA_DK ** -0.5, 1.0).astype(F32)
            for j in range(PROJ_TILE // GLA_DK):
                yj = y[:, j * GLA_DK:(j + 1) * GLA_DK]
                swapped = jnp.concatenate(
                    [pltpu.roll(yj[:, g * LANES:(g + 1) * LANES], LANES // 2, 1) for g in range(GLA_DK // LANES)],
                    axis=1)
                p_ref[r, j * GLA_DK:(j + 1) * GLA_DK] = ((yj * cos + swapped * sin) * qscale).astype(BF16)

        @pl.when(alive(n >= N_ROPE_TILES))
        def _():
            p_ref[r, :] = _dot_nt(h_ref[r, :], w_ref[...]).astype(BF16)

        if live is not None:
            @pl.when(jnp.logical_not(live))
            def _():
                p_ref[r, :] = jnp.zeros((src_ref.shape[0], PROJ_TILE), BF16)

    ctx_live = functools.reduce(jnp.logical_or, [n == t for t in ctx_tiles])
    pl.when(i < n_lat_tiles)(lambda: project(x_ref, None))
    pl.when(i >= n_lat_tiles)(lambda: project(xc_ref, ctx_live))


def _inproj(h1, h1c, mod, norm_g, w_in_t, cos, sin, seq, gate0, gate1, ctx_cols):
    tm = BIG_ROW_TILE
    n_lat_tiles = h1.shape[0] // tm
    n_rows = h1.shape[0] + h1c.shape[0]
    tiles_per_batch = n_lat_tiles // 2
    n_a_tiles = gate0 // PROJ_TILE
    n_b_tiles = (w_in_t.shape[0] - gate1) // PROJ_TILE
    n_gate = gate1 - gate0
    ctx_tiles = [t for t in range(n_a_tiles + n_b_tiles)
                 if any(lo < (t + 1) * PROJ_TILE and t * PROJ_TILE < hi for lo, hi in ctx_cols)]

    def weight_row0(i, n):
        t = jnp.int32(ctx_tiles[0])
        for live in ctx_tiles[1:]:
            t = jnp.where(n >= live, live, t)
        t = jnp.where(i < n_lat_tiles, n, t)
        return pl.multiple_of(t * PROJ_TILE + jnp.where(t < n_a_tiles, 0, n_gate), n_gate), 0

    def ahead(i, n, n_used):
        return i + (n >= (n_used + n_a_tiles + n_b_tiles) // 2).astype(jnp.int32)

    def table_block(i, n):
        r = ahead(i, n, N_ROPE_TILES)
        return jnp.where(r < n_lat_tiles, r % tiles_per_batch, seq // tm), 0

    tab_spec = pl.BlockSpec((tm, GLA_DK), table_block)
    return pl.pallas_call(
        functools.partial(_inproj_kernel, n_lat_tiles=n_lat_tiles, ctx_tiles=ctx_tiles),
        grid=(n_lat_tiles + 1, n_a_tiles + n_b_tiles),
        in_specs=[pl.BlockSpec((tm, D_MODEL), lambda i, n: (jnp.minimum(ahead(i, n, 1), n_lat_tiles - 1), 0)),
                  pl.BlockSpec(h1c.shape, lambda i, n: (0, 0)),
                  pl.BlockSpec((None, N_MOD, D_MODEL), lambda i, n: (jnp.minimum(i // tiles_per_batch, 2), 0, 0)),
                  pl.BlockSpec((6, D_MODEL), lambda i, n: (0, 0)),
                  pl.BlockSpec((pl.Element(PROJ_TILE), pl.Element(D_MODEL)), weight_row0),
                  pl.BlockSpec((n_gate, D_MODEL), lambda i, n: (gate0 // n_gate, 0)),
                  tab_spec, tab_spec],
        out_specs=[pl.BlockSpec((tm, PROJ_TILE), lambda i, n: (i, n)),
                   pl.BlockSpec((tm, LANES), lambda i, n: (i, 0))],
        out_shape=[jax.ShapeDtypeStruct((n_rows, (n_a_tiles + n_b_tiles) * PROJ_TILE), BF16),
                   jax.ShapeDtypeStruct((n_rows, LANES), F32)],
        scratch_shapes=[pltpu.VMEM((tm, D_MODEL), BF16)],
        compiler_params=_params(("arbitrary", "arbitrary")),
        name="inproj",
    )(h1, h1c, mod, norm_g, w_in_t, w_in_t, cos, sin)


def _log2_sigmoid(z):
    t = z * LOG2E
    return jnp.minimum(t, 0.0) - jnp.log2(1.0 + jnp.exp2(-jnp.abs(t)))


def _gla_kernel(qf_ref, kf_ref, vf_ref, gf_ref, qb_ref, kb_ref, vb_ref, gb_ref, wg_ref, bg_ref,
                of_ref, ob_ref, stf_ref, stb_ref):
    @pl.when(pl.program_id(2) == 0)
    def _():
        stf_ref[...] = jnp.zeros_like(stf_ref)
        stb_ref[...] = jnp.zeros_like(stb_ref)

    c = GLA_BLOCK
    nt = (((1,), (1,)), ((), ()))
    row = lax.broadcasted_iota(jnp.int32, (c, c), 0)
    col = lax.broadcasted_iota(jnp.int32, (c, c), 1)
    mask = {True: col <= row, False: col >= row}
    tri = {fwd: jnp.where(m, 1.0, 0.0).astype(BF16) for fwd, m in mask.items()}
    refs = {True: (qf_ref, kf_ref, vf_ref, gf_ref, stf_ref, of_ref),
            False: (qb_ref, kb_ref, vb_ref, gb_ref, stb_ref, ob_ref)}
    qcols = lambda head: slice(head * GLA_DK, (head + 1) * GLA_DK)
    vcols = lambda head: slice(head * GLA_DV, (head + 1) * GLA_DV)

    state, log_a, cum, factors, att = {}, {}, {}, {}, {}

    def gate_stage(head, fwd):
        d = 0 if fwd else 1
        state[head, fwd] = refs[fwd][4][head]
        z = jnp.dot(refs[fwd][3][...], wg_ref[d, :, qcols(head)], preferred_element_type=F32)
        log_a[head, fwd] = _log2_sigmoid(z + bg_ref[d, :, qcols(head)]) * (1.0 / GLA_GATE_TEMP)

    def cumsum_stage(head, fwd):
        la = log_a[head, fwd]
        hi = la.astype(BF16)
        lo = (la - hi.astype(F32)).astype(BF16)
        cum[head, fwd] = (jnp.dot(tri[fwd], hi, preferred_element_type=F32)
                          + jnp.dot(tri[fwd], lo, preferred_element_type=F32))

    def factor_stage(head, fwd):
        cm = cum[head, fwd]
        cum_end = cm[c - 1:c, :] if fwd else cm[0:1, :]
        cum_mid = cm[c // 2:c // 2 + 1, :]
        q = refs[fwd][0][:, qcols(head)]
        k = refs[fwd][1][:, qcols(head)]
        factors[head, fwd] = (q * jnp.exp2(cm - cum_mid).astype(BF16),
                              k * jnp.exp2(cum_mid - cm).astype(BF16),
                              q * jnp.exp2(cm).astype(BF16),
                              k * jnp.exp2(cum_end - cm).astype(BF16),
                              jnp.exp2(cum_end))

    def att_stage(head, fwd):
        q_att, k_att = factors[head, fwd][:2]
        a = lax.dot_general(q_att, k_att, nt, preferred_element_type=F32)
        att[head, fwd] = jnp.where(mask[fwd], a, 0.0).astype(BF16)

    def out_stage(head, fwd):
        _, _, q_dec, k_end, decay = factors[head, fwd]
        v = refs[fwd][2][:, vcols(head)]
        st = state[head, fwd]
        o = (jnp.dot(att[head, fwd], v, preferred_element_type=F32)
             + lax.dot_general(q_dec, st.astype(BF16), nt, preferred_element_type=F32))
        kv_t = lax.dot_general(v, k_end, (((0,), (0,)), ((), ())), preferred_element_type=F32)
        refs[fwd][5][:, vcols(head)] = o.astype(BF16)
        refs[fwd][4][head] = decay * st + kv_t

    for stage in (gate_stage, cumsum_stage, factor_stage, att_stage, out_stage):
        for head in range(GLA_HEADS_PER_STEP):
            for fwd in (True, False):
                stage(head, fwd)


def _gla(p, gin, wg_pad, bg, batch, seq):
    n_lat = seq // GLA_BLOCK
    ctx_block0 = batch * n_lat
    hps = GLA_HEADS_PER_STEP
    dk, dv = hps * GLA_DK, hps * GLA_DV

    def lat_block(b, s, forward):
        j = jnp.maximum(s - 1, 0)
        return b * n_lat + (j if forward else n_lat - 1 - j)

    def in_block(b, s, forward):
        return jnp.where(s == 0, ctx_block0 + b, lat_block(b, s, forward))

    kq = GLA_HEADS // hps
    kv = 2 * GLA_HEADS * GLA_DK // dv

    def dir_specs(forward):
        return [
            pl.BlockSpec((GLA_BLOCK, dk), lambda b, h, s: (in_block(b, s, forward), h)),
            pl.BlockSpec((GLA_BLOCK, dk), lambda b, h, s: (in_block(b, s, forward), kq + h)),
            pl.BlockSpec((GLA_BLOCK, dv), lambda b, h, s: (in_block(b, s, forward), kv + h)),
            pl.BlockSpec((GLA_BLOCK, LANES), lambda b, h, s: (in_block(b, s, forward), 0)),
        ]

    out_shape = jax.ShapeDtypeStruct((batch * seq, GLA_HEADS * GLA_DV), BF16)
    state = pltpu.VMEM((hps, GLA_DV, GLA_DK), F32)
    return pl.pallas_call(
        _gla_kernel,
        grid=(batch, GLA_HEADS // hps, n_lat + 1),
        in_specs=dir_specs(True) + dir_specs(False) + [
            pl.BlockSpec((2, LANES, dk), lambda b, h, s: (0, 0, h)),
            pl.BlockSpec((2, 1, dk), lambda b, h, s: (0, 0, h))],
        out_specs=[pl.BlockSpec((GLA_BLOCK, dv), lambda b, h, s: (lat_block(b, s, True), h)),
                   pl.BlockSpec((GLA_BLOCK, dv), lambda b, h, s: (lat_block(b, s, False), h))],
        out_shape=[out_shape, out_shape],
        scratch_shapes=[state, state],
        compiler_params=_params(("arbitrary", "arbitrary", "arbitrary")),
        name="gla",
    )(p, p, p, gin, p, p, p, gin, wg_pad, bg)


NA_QTOK = NA_QROWS * GRID_W
NA_SUB_TOK = NA_SUB_ROWS * GRID_W
NA_WIN = NA_WIN_ROWS * GRID_W
NA_SUBS = NA_QROWS // NA_SUB_ROWS
NA_KINDS = ((0, 0), (NA_SUB_ROWS, NA_SUB_ROWS - NA_KR // 2), (GRID_W - NA_SUB_ROWS, GRID_W - NA_WIN_ROWS))


def _na_build_bias(rpb_ref, t_ref, rows):
    n_slots = 2 * NA_KR - 1
    lane = lax.broadcasted_iota(jnp.int32, (GRID_W, LANES), 1)
    q_col = lax.broadcasted_iota(jnp.int32, (GRID_W, LANES), 0)
    k_col = lane & (GRID_W - 1)
    upper = lane >= GRID_W
    c0 = jnp.clip(q_col - NA_KC // 2, 0, GRID_W - NA_KC)
    col_ok = (k_col >= c0) & (k_col < c0 + NA_KC)
    neg = jnp.full((GRID_W, LANES), NEG_INF, F32)

    def toeplitz(hh, slot, lane_off):
        if not 0 <= slot < n_slots:
            return jnp.zeros((GRID_W, LANES), F32)
        base = jnp.broadcast_to(rpb_ref[hh, slot:slot + 1, :], (GRID_W, LANES)) * LOG2E
        return pltpu.roll(base, (LANES - (NA_KC - 1) + lane_off) % LANES, 1, stride=1, stride_axis=0)

    for hh in range(2):
        pairs = {}
        for kind, (r0, w0) in enumerate(NA_KINDS):
            for qr in range(NA_SUB_ROWS):
                band0 = min(max(r0 + qr - NA_KR // 2, 0), rows - NA_KR) - w0
                slot0 = w0 - (r0 + qr) + NA_KR - 1
                for g in range(NA_WIN_ROWS // 2):
                    slot = slot0 + 2 * g
                    lo_ok = band0 <= 2 * g < band0 + NA_KR
                    hi_ok = band0 <= 2 * g + 1 < band0 + NA_KR
                    if lo_ok or hi_ok:
                        if slot not in pairs:
                            pairs[slot] = jnp.where(upper, toeplitz(hh, slot + 1, GRID_W), toeplitz(hh, slot, 0))
                        ok = col_ok if (lo_ok and hi_ok) else (col_ok & upper if hi_ok else col_ok & ~upper)
                        tile = jnp.where(ok, pairs[slot], neg)
                    else:
                        tile = neg
                    t_ref[hh, kind, qr * GRID_W:(qr + 1) * GRID_W, g * LANES:(g + 1) * LANES] = tile


def _na_kernel(q_ref, k_ref, v_ref, kc_ref, vc_ref, rpb_ref, o_ref, t_ref, *, rows):
    b = pl.program_id(1)
    blk = pl.program_id(2)
    n_blk = rows // NA_QROWS

    pl.when((b == 0) & (blk == 0))(lambda: _na_build_bias(rpb_ref, t_ref, rows))

    nt = (((1,), (1,)), ((), ()))
    kc = kc_ref[...]
    vc = vc_ref[...]
    lane = lax.broadcasted_iota(jnp.int32, (NA_SUB_TOK, LANES), 1)
    chains = [(hh, sub) for sub in range(NA_SUBS) for hh in range(2)]
    q_head, kw, vw, kind = {}, [], [], []
    for sub in range(NA_SUBS):
        q = q_ref[sub * NA_SUB_TOK:(sub + 1) * NA_SUB_TOK, :].astype(F32) * (NA_DH ** -0.5 * LOG2E)
        q = q.astype(BF16)
        q_head[0, sub] = jnp.where(lane < NA_DH, q, jnp.zeros_like(q))
        q_head[1, sub] = jnp.where(lane >= NA_DH, q, jnp.zeros_like(q))
        r0 = blk * NA_QROWS + sub * NA_SUB_ROWS
        w0 = jnp.clip(r0 - NA_KR // 2, 0, rows - NA_WIN_ROWS)
        win = pl.ds(pl.multiple_of(w0 * GRID_W, 256), NA_WIN)
        kw.append(k_ref[win, :])
        vw.append(v_ref[win, :])
        kind.append(jnp.where(r0 == 0, 0, jnp.where(r0 == rows - NA_SUB_ROWS, 2, 1)))

    s_lat, s_ctx, p_lat, p_ctx, denom, out = {}, {}, {}, {}, {}, {}
    for hh, sub in chains:
        qs = q_head[hh, sub]
        s_lat[hh, sub] = lax.dot_general(qs, kw[sub], nt, preferred_element_type=F32) + t_ref[hh, kind[sub]]
        s_ctx[hh, sub] = lax.dot_general(qs, kc, nt, preferred_element_type=F32)
    for ch in chains:
        m = jnp.maximum(jnp.max(s_lat[ch], axis=-1, keepdims=True), jnp.max(s_ctx[ch], axis=-1, keepdims=True))
        pl_ = jnp.exp2(s_lat[ch] - m)
        pc_ = jnp.exp2(s_ctx[ch] - m)
        denom[ch] = jnp.sum(pl_, axis=-1, keepdims=True) + jnp.sum(pc_, axis=-1, keepdims=True)
        p_lat[ch] = pl_.astype(BF16)
        p_ctx[ch] = pc_.astype(BF16)
    for hh, sub in chains:
        o = (jnp.dot(p_lat[hh, sub], vw[sub], preferred_element_type=F32)
             + jnp.dot(p_ctx[hh, sub], vc, preferred_element_type=F32))
        out[hh, sub] = o / denom[hh, sub]
    for sub in range(NA_SUBS):
        rows_ = slice(sub * NA_SUB_TOK, (sub + 1) * NA_SUB_TOK)
        o_ref[rows_, :] = jnp.where(lane < NA_DH, out[0, sub], out[1, sub]).astype(o_ref.dtype)


def _na(p, rpb, batch, seq, ctx_len, col0):
    rows = seq // GRID_W
    n_blk = rows // NA_QROWS
    n_pairs = NA_HEADS // 2
    cq = col0 // LANES
    ck = cq + n_pairs
    cv = ck + n_pairs
    ctx_block0 = batch * seq // ctx_len
    return pl.pallas_call(
        functools.partial(_na_kernel, rows=rows),
        grid=(n_pairs, batch, n_blk),
        in_specs=[
            pl.BlockSpec((NA_QTOK, LANES), lambda hp, b, r: (b * n_blk + r, cq + hp)),
            pl.BlockSpec((seq, LANES), lambda hp, b, r: (b, ck + hp)),
            pl.BlockSpec((seq, LANES), lambda hp, b, r: (b, cv + hp)),
            pl.BlockSpec((ctx_len, LANES), lambda hp, b, r: (ctx_block0 + b, ck + hp)),
            pl.BlockSpec((ctx_len, LANES), lambda hp, b, r: (ctx_block0 + b, cv + hp)),
            pl.BlockSpec((2, 2 * NA_KR - 1, LANES), lambda hp, b, r: (hp, 0, 0)),
        ],
        out_specs=pl.BlockSpec((NA_QTOK, LANES), lambda hp, b, r: (b * n_blk + r, hp)),
        out_shape=jax.ShapeDtypeStruct((batch * seq, NA_HEADS * NA_DH), BF16),
        scratch_shapes=[pltpu.VMEM((2, len(NA_KINDS), NA_SUB_TOK, NA_WIN), F32)],
        compiler_params=_params(("arbitrary", "arbitrary", "arbitrary")),
        name="natten",
    )(p, p, p, p, p, rpb)


def _merge_kernel(of_ref, ob_ref, r_ref, gn_ref, b_ref, wa_ref, wb_ref, m1a_ref, m1b_ref, m2a_ref, m2b_ref,
                  o_ref, a_ref):
    for h in range(GLA_HEADS):
        cols = slice(h * GLA_DV, (h + 1) * GLA_DV)
        tot = of_ref[:, cols].astype(F32) + ob_ref[:, cols].astype(F32)
        a_ref[:, cols] = (_rms(tot, gn_ref[...]) * _silu(r_ref[:, cols].astype(F32))).astype(BF16)

    gate_refs = ((m1a_ref, m2a_ref), (m1b_ref, m2b_ref))
    gate_w = m1a_ref.shape[1]
    for c0 in range(0, o_ref.shape[1], MERGE_STRIP):
        cols = slice(c0, c0 + MERGE_STRIP)
        m1_ref, m2_ref = gate_refs[c0 // gate_w]
        gcols = slice(c0 % gate_w, c0 % gate_w + MERGE_STRIP)
        a = _mixed_dot(a_ref[...], wa_ref[:, cols])
        b = _mixed_dot(b_ref[...], wb_ref[:, cols])
        m = _sigmoid(m1_ref[:, gcols].astype(F32)) * a + _sigmoid(m2_ref[:, gcols].astype(F32)) * b
        o_ref[:, cols] = m.astype(o_ref.dtype)


def _merge(o_fwd, o_bwd, gn, o_na, w_gla_o, w_na_o, p, col_r, col_m1):
    n_rows = o_fwd.shape[0]
    v_w = o_fwd.shape[1]
    gw = D_MODEL // 2
    c1 = col_m1 // gw
    once = dict(pipeline_mode=pl.Buffered(1))
    gate_spec = lambda c: pl.BlockSpec((ROW_TILE, gw), lambda i: (i, c))
    return pl.pallas_call(
        _merge_kernel,
        grid=(n_rows // ROW_TILE,),
        in_specs=[pl.BlockSpec((ROW_TILE, v_w), lambda i: (i, 0)),
                  pl.BlockSpec((ROW_TILE, v_w), lambda i: (i, 0)),
                  pl.BlockSpec((ROW_TILE, v_w), lambda i: (i, col_r // v_w)),
                  pl.BlockSpec((1, GLA_DV), lambda i: (0, 0)),
                  pl.BlockSpec((ROW_TILE, o_na.shape[1]), lambda i: (i, 0)),
                  pl.BlockSpec(w_gla_o.shape, lambda i: (0, 0), **once),
                  pl.BlockSpec(w_na_o.shape, lambda i: (0, 0), **once),
                  gate_spec(c1), gate_spec(c1 + 1), gate_spec(c1 + 2), gate_spec(c1 + 3)],
        out_specs=pl.BlockSpec((ROW_TILE, D_MODEL), lambda i: (i, 0)),
        out_shape=jax.ShapeDtypeStruct((n_rows, D_MODEL), BF16),
        scratch_shapes=[pltpu.VMEM((ROW_TILE, v_w), BF16)],
        compiler_params=_params(("arbitrary",)),
        name="merge",
    )(o_fwd, o_bwd, p, gn, o_na, w_gla_o, w_na_o, p, p, p, p)


def _outproj_kernel(m_ref, w_ref, x_ref, mod_ref, g_ref, o_ref):
    gain = mod_ref[5:6, :] * g_ref[3:4, :]
    half = m_ref.shape[0] // 2
    for rows in (slice(0, half), slice(half, 2 * half)):
        y = _mixed_dot(m_ref[rows, :], w_ref[...])
        o_ref[rows, :] = x_ref[rows, :] + _rms(y, gain)


def _outproj(m, w_out, h1, mod, norm_g):
    n_tiles = m.shape[0] // ROW_TILE
    tiles_per_batch = n_tiles // 2
    return pl.pallas_call(
        _outproj_kernel,
        grid=(n_tiles,),
        in_specs=[pl.BlockSpec((ROW_TILE, D_MODEL), lambda i: (i, 0)),
                  pl.BlockSpec((D_MODEL, D_MODEL), lambda i: (0, 0), pipeline_mode=pl.Buffered(1)),
                  pl.BlockSpec((ROW_TILE, D_MODEL), lambda i: (i, 0)),
                  pl.BlockSpec((None, N_MOD, D_MODEL), lambda i: (i // tiles_per_batch, 0, 0)),
                  pl.BlockSpec((6, D_MODEL), lambda i: (0, 0))],
        out_specs=pl.BlockSpec((ROW_TILE, D_MODEL), lambda i: (i, 0)),
        out_shape=jax.ShapeDtypeStruct((m.shape[0], D_MODEL), F32),
        compiler_params=_params(("arbitrary",)),
        name="outproj",
    )(m, w_out, h1, mod, norm_g)


def kernel(x, c, ctx, c_ctx, w_ada, b_ada, norm_g, ffn_wg, ffn_wu, ffn_wd, w_in, gla_wg, gla_bg, gla_norm_g,
           w_gla_o, na_rpb, w_na_o, w_out):
    batch, seq, d = x.shape
    ctx_len = ctx.shape[1]
    depth = w_ada.shape[0]
    assert d == D_MODEL and batch == 2 and batch * ctx_len == ROW_TILE and depth == 1
    assert seq % ROW_TILE == 0 and seq // GRID_W == GRID_W

    qk_w = GLA_HEADS * GLA_DK
    v_w = GLA_HEADS * GLA_DV
    na_w = NA_HEADS * NA_DH
    gate0 = 2 * qk_w + 2 * v_w
    gate1 = gate0 + 2 * GLA_GATE_RANK
    col_nq = gate0
    col_m1 = col_nq + 3 * na_w

    cvec = jnp.zeros((8, d), F32).at[0:batch].set(c).at[batch].set(c_ctx)
    cos, sin = _rope_tables(seq)
    h = x.reshape(batch * seq, d)
    hc = ctx.reshape(batch * ctx_len, d)
    lat_tiles_per_batch = seq // BIG_ROW_TILE
    for l in range(depth):
        mod = _modulation(cvec, w_ada[l], b_ada[l].reshape(1, -1)).reshape(8, N_MOD, d)
        g = norm_g[l]
        h1 = _ffn(h, mod, g, ffn_wg[l], ffn_wu[l], ffn_wd[l], 0, 0, tm=BIG_ROW_TILE, tf=FF_TILE,
                  mod_row=lambda i: i // lat_tiles_per_batch)
        h1c = _ffn(hc, mod, g, ffn_wg[l], ffn_wu[l], ffn_wd[l], 0, 0, tm=ROW_TILE, tf=FF_TILE,
                   mod_row=lambda i: batch)
        ctx_cols = [(qk_w, 2 * qk_w + v_w), (col_nq + na_w, col_nq + 3 * na_w)]
        p, gin = _inproj(h1, h1c, mod, g, jnp.swapaxes(w_in[l], 0, 1), cos, sin, seq, gate0, gate1, ctx_cols)
        wg_pad = jnp.zeros((2, LANES, qk_w), F32)
        wg_pad = wg_pad.at[0, :GLA_GATE_RANK].set(gla_wg[l, 0])
        wg_pad = wg_pad.at[1, GLA_GATE_RANK:2 * GLA_GATE_RANK].set(gla_wg[l, 1])
        bg = gla_bg[l].reshape(2, 1, qk_w)
        gn = gla_norm_g[l].reshape(1, GLA_DV)
        o_fwd, o_bwd = _gla(p, gin, wg_pad, bg, batch, seq)
        rpb = jnp.pad(na_rpb[l], ((0, 0), (0, 0), (0, LANES - (2 * NA_KC - 1))))
        o_na = _na(p, rpb, batch, seq, ctx_len, col_nq)
        m = _merge(o_fwd, o_bwd, gn, o_na, w_gla_o[l], w_na_o[l], p,
                   2 * qk_w + v_w, col_m1)
        h2 = _outproj(m, w_out[l], h1, mod, g)
        h = _ffn(h2, mod, g, ffn_wg[l], ffn_wu[l], ffn_wd[l], 1, 2, tm=BIG_ROW_TILE, tf=FF_TILE,
                 mod_row=lambda i: i // lat_tiles_per_batch)
    return h.reshape(batch, seq, d)
```

```python
import functools

import numpy as np
import jax
import jax.numpy as jnp
from jax import lax
from jax.experimental import pallas as pl
from jax.experimental.pallas import tpu as pltpu

F32 = jnp.float32
BF16 = jnp.bfloat16

D_MODEL = 2048
GRID_W = 64
GLA_HEADS = 4
GLA_DK = 256
GLA_DV = 512
GLA_GATE_RANK = 16
GLA_GATE_TEMP = 16.0
NA_HEADS = 16
NA_DH = 64
NA_KR = 8
NA_KC = 16
ROPE_THETA = 10000.0
EPS = 1e-6
NEG_INF = -1e30
N_MOD = 9
LOG2E = 1.4426950408889634

LANES = 128
ROW_TILE = 512
BIG_ROW_TILE = 1024
FF_TILE = 512
PROJ_TILE = 1024
MERGE_STRIP = 512
ADA_TILE = 1024
GLA_BLOCK = 256
GLA_HEADS_PER_STEP = 4
NA_QROWS = 16
NA_SUB_ROWS = 4
NA_WIN_ROWS = 12
VMEM_LIMIT = 56 * 1024 * 1024
VMEM_LIMIT_FFN = 60 * 1024 * 1024


def _params(sem, vmem_limit=VMEM_LIMIT):
    return pltpu.CompilerParams(dimension_semantics=sem, vmem_limit_bytes=vmem_limit)


def _rms(x, g):
    return x * lax.rsqrt(jnp.mean(x * x, axis=-1, keepdims=True) + EPS) * g


def _sigmoid(x):
    return 0.5 * jnp.tanh(0.5 * x) + 0.5


def _silu(x):
    h = 0.5 * x
    return h * jnp.tanh(h) + h


def _mod_kernel(c_ref, w_ref, b_ref, o_ref):
    o_ref[...] = jnp.dot(_silu(c_ref[...]), w_ref[...], preferred_element_type=F32) + b_ref[...]


def _modulation(cvec, w_ada, b_ada):
    n = w_ada.shape[1]
    tn = ADA_TILE
    return pl.pallas_call(
        _mod_kernel,
        grid=(n // tn,),
        in_specs=[pl.BlockSpec((8, D_MODEL), lambda j: (0, 0)),
                  pl.BlockSpec((D_MODEL, tn), lambda j: (0, j)),
                  pl.BlockSpec((1, tn), lambda j: (0, j))],
        out_specs=pl.BlockSpec((8, tn), lambda j: (0, j)),
        out_shape=jax.ShapeDtypeStruct((8, n), F32),
        compiler_params=_params(("arbitrary",)),
        name="adaln_mod",
    )(cvec, w_ada, b_ada)


def _mixed_dot(a, w):
    return lax.dot_general(a, w, (((1,), (0,)), ((), ())), preferred_element_type=F32)


def _ffn_kernel(x_ref, mod_ref, g_ref, wg_ref, wu_ref, wd_ref, o_hbm, h_ref, acc_ref, sem, *, sub):
    i = pl.program_id(0)
    f = pl.program_id(1)
    n_tiles = pl.num_programs(0)
    last = pl.num_programs(1) - 1
    tm = x_ref.shape[0]
    subtiles = [pl.ds(r0, ROW_TILE) for r0 in range(0, tm, ROW_TILE)]

    def out_copy(k, tile):
        return pltpu.make_async_copy(
            acc_ref.at[subtiles[k], :], o_hbm.at[pl.ds(tile * tm + k * ROW_TILE, ROW_TILE), :], sem.at[k])

    def gate_up(h):
        g = _mixed_dot(h, wg_ref[...])
        u = _mixed_dot(h, wu_ref[...])
        return (_silu(g) * u).astype(BF16)

    @pl.when((i == 0) & (f == 0))
    def _():
        acc_ref[...] = jnp.zeros_like(acc_ref)
        for k in range(len(subtiles)):
            out_copy(k, 0).start()

    @pl.when(f == 0)
    def _():
        gain = g_ref[2 * sub:2 * sub + 1, :] * (1.0 + mod_ref[3 * sub + 1:3 * sub + 2, :])
        shift = mod_ref[3 * sub:3 * sub + 1, :]
        for k, rows in enumerate(subtiles):
            h = (_rms(x_ref[rows, :], gain) + shift).astype(BF16)
            h_ref[rows, :] = h
            a = gate_up(h)
            out_copy(k, jnp.maximum(i - 1, 0)).wait()
            acc_ref[rows, :] = _mixed_dot(a, wd_ref[...])

    @pl.when((f > 0) & (f < last))
    def _():
        for rows in subtiles:
            acc_ref[rows, :] += _mixed_dot(gate_up(h_ref[rows, :]), wd_ref[...])

    @pl.when(f == last)
    def _():
        gain = 0.5 * mod_ref[3 * sub + 2:3 * sub + 3, :] * g_ref[2 * sub + 1:2 * sub + 2, :]
        for k, rows in enumerate(subtiles):
            y = acc_ref[rows, :] + _mixed_dot(gate_up(h_ref[rows, :]), wd_ref[...])
            acc_ref[rows, :] = x_ref[rows, :] + _rms(y, gain)
            out_copy(k, i).start()

    @pl.when((f == last) & (i == n_tiles - 1))
    def _():
        for k in range(len(subtiles)):
            out_copy(k, i).wait()


def _ffn(x2d, mod, norm_g, wg, wu, wd, which, sub, *, tm, tf, mod_row):
    n_tiles = x2d.shape[0] // tm
    d_ff = wg.shape[2]
    return pl.pallas_call(
        functools.partial(_ffn_kernel, sub=sub),
        grid=(n_tiles, d_ff // tf),
        in_specs=[
            pl.BlockSpec((tm, D_MODEL), lambda i, f: (i, 0)),
            pl.BlockSpec((None, N_MOD, D_MODEL), lambda i, f: (mod_row(i), 0, 0)),
            pl.BlockSpec((6, D_MODEL), lambda i, f: (0, 0)),
            pl.BlockSpec((None, D_MODEL, tf), lambda i, f: (which, 0, f)),
            pl.BlockSpec((None, D_MODEL, tf), lambda i, f: (which, 0, f)),
            pl.BlockSpec((None, tf, D_MODEL), lambda i, f: (which, f, 0)),
        ],
        out_specs=pl.BlockSpec(memory_space=pl.ANY),
        out_shape=jax.ShapeDtypeStruct(x2d.shape, F32),
        scratch_shapes=[pltpu.VMEM((tm, D_MODEL), BF16), pltpu.VMEM((tm, D_MODEL), F32),
                        pltpu.SemaphoreType.DMA((tm // ROW_TILE,))],
        compiler_params=_params(("arbitrary", "arbitrary"), VMEM_LIMIT_FFN),
        name="ffn%d_%d" % (sub, tm),
    )(x2d, mod, norm_g, wg, wu, wd)


N_ROPE_TILES = 2 * GLA_HEADS * GLA_DK // PROJ_TILE
N_Q_TILES = N_ROPE_TILES // 2


def _rope_tables(seq):
    half = GLA_DK // 4
    freqs = ROPE_THETA ** (-np.arange(half, dtype=np.float64) / half)
    t = np.arange(seq)
    cos_parts, sin_parts = [], []
    for pos in (t // GRID_W, t % GRID_W):
        ang = pos[:, None].astype(np.float64) * freqs
        cos_parts += [np.cos(ang), np.cos(ang)]
        sin_parts += [-np.sin(ang), np.sin(ang)]
    cos = np.concatenate(cos_parts, axis=1)
    sin = np.concatenate(sin_parts, axis=1)
    cos = np.concatenate([cos, np.ones((BIG_ROW_TILE, GLA_DK))], axis=0)
    sin = np.concatenate([sin, np.zeros((BIG_ROW_TILE, GLA_DK))], axis=0)
    return jnp.asarray(cos, F32), jnp.asarray(sin, F32)


def _dot_nt(a, w_t):
    return lax.dot_general(a, w_t, (((1,), (1,)), ((), ())), preferred_element_type=F32)


def _inproj_kernel(x_ref, xc_ref, mod_ref, g_ref, w_ref, wgate_ref, cos_ref, sin_ref, p_ref, gate_ref, h_ref,
                   *, n_lat_tiles, ctx_tiles):
    i = pl.program_id(0)
    n = pl.program_id(1)

    def project(src_ref, live):
        r = slice(0, src_ref.shape[0])
        alive = (lambda cond: cond) if live is None else (lambda cond: cond & live)

        @pl.when(n == 0)
        def _():
            h = _rms(src_ref[...], g_ref[2:3, :]) * (1.0 + mod_ref[4:5, :]) + mod_ref[3:4, :]
            h_ref[r, :] = h.astype(BF16)
            gate = _dot_nt(h_ref[r, :], wgate_ref[...])
            gate_ref[r, :] = jnp.concatenate(
                [gate, jnp.zeros((gate.shape[0], LANES - gate.shape[1]), F32)], axis=1)

        @pl.when(alive(n < N_ROPE_TILES))
        def _():
            y = _dot_nt(h_ref[r, :], w_ref[...])
            cos = cos_ref[r, :]
            sin = sin_ref[r, :]
            qscale = jnp.where(n < N_Q_TILES, GLA_DK ** -0.5, 1.0).astype(F32)
            for j in range(PROJ_TILE // GLA_DK):
                yj = y[:, j * GLA_DK:(j + 1) * GLA_DK]
                swapped = jnp.concatenate(
                    [pltpu.roll(yj[:, g * LANES:(g + 1) * LANES], LANES // 2, 1) for g in range(GLA_DK // LANES)],
                    axis=1)
                p_ref[r, j * GLA_DK:(j + 1) * GLA_DK] = ((yj * cos + swapped * sin) * qscale).astype(BF16)

        @pl.when(alive(n >= N_ROPE_TILES))
        def _():
            p_ref[r, :] = _dot_nt(h_ref[r, :], w_ref[...]).astype(BF16)

        if live is not None:
            @pl.when(jnp.logical_not(live))
            def _():
                p_ref[r, :] = jnp.zeros((src_ref.shape[0], PROJ_TILE), BF16)

    ctx_live = functools.reduce(jnp.logical_or, [n == t for t in ctx_tiles])
    pl.when(i < n_lat_tiles)(lambda: project(x_ref, None))
    pl.when(i >= n_lat_tiles)(lambda: project(xc_ref, ctx_live))


def _inproj(h1, h1c, mod, norm_g, w_in_t, cos, sin, seq, gate0, gate1, ctx_cols):
    tm = BIG_ROW_TILE
    n_lat_tiles = h1.shape[0] // tm
    n_rows = h1.shape[0] + h1c.shape[0]
    tiles_per_batch = n_lat_tiles // 2
    n_a_tiles = gate0 // PROJ_TILE
    n_b_tiles = (w_in_t.shape[0] - gate1) // PROJ_TILE
    n_gate = gate1 - gate0
    ctx_tiles = [t for t in range(n_a_tiles + n_b_tiles)
                 if any(lo < (t + 1) * PROJ_TILE and t * PROJ_TILE < hi for lo, hi in ctx_cols)]

    def weight_row0(i, n):
        t = jnp.int32(ctx_tiles[0])
        for live in ctx_tiles[1:]:
            t = jnp.where(n >= live, live, t)
        t = jnp.where(i < n_lat_tiles, n, t)
        return pl.multiple_of(t * PROJ_TILE + jnp.where(t < n_a_tiles, 0, n_gate), n_gate), 0

    tab_spec = pl.BlockSpec(
        (tm, GLA_DK), lambda i, n: (jnp.where(i < n_lat_tiles, i % tiles_per_batch, seq // tm), 0))
    return pl.pallas_call(
        functools.partial(_inproj_kernel, n_lat_tiles=n_lat_tiles, ctx_tiles=ctx_tiles),
        grid=(n_lat_tiles + 1, n_a_tiles + n_b_tiles),
        in_specs=[pl.BlockSpec((tm, D_MODEL), lambda i, n: (jnp.minimum(i, n_lat_tiles - 1), 0)),
                  pl.BlockSpec(h1c.shape, lambda i, n: (0, 0)),
                  pl.BlockSpec((None, N_MOD, D_MODEL), lambda i, n: (jnp.minimum(i // tiles_per_batch, 2), 0, 0)),
                  pl.BlockSpec((6, D_MODEL), lambda i, n: (0, 0)),
                  pl.BlockSpec((pl.Element(PROJ_TILE), pl.Element(D_MODEL)), weight_row0),
                  pl.BlockSpec((n_gate, D_MODEL), lambda i, n: (gate0 // n_gate, 0)),
                  tab_spec, tab_spec],
        out_specs=[pl.BlockSpec((tm, PROJ_TILE), lambda i, n: (i, n)),
                   pl.BlockSpec((tm, LANES), lambda i, n: (i, 0))],
        out_shape=[jax.ShapeDtypeStruct((n_rows, (n_a_tiles + n_b_tiles) * PROJ_TILE), BF16),
                   jax.ShapeDtypeStruct((n_rows, LANES), F32)],
        scratch_shapes=[pltpu.VMEM((tm, D_MODEL), BF16)],
        compiler_params=_params(("arbitrary", "arbitrary")),
        name="inproj",
    )(h1, h1c, mod, norm_g, w_in_t, w_in_t, cos, sin)


def _log2_sigmoid(z):
    t = z * LOG2E
    return jnp.minimum(t, 0.0) - jnp.log2(1.0 + jnp.exp2(-jnp.abs(t)))


def _gla_kernel(qf_ref, kf_ref, vf_ref, gf_ref, qb_ref, kb_ref, vb_ref, gb_ref, wg_ref, bg_ref,
                of_ref, ob_ref, stf_ref, stb_ref):
    @pl.when(pl.program_id(2) == 0)
    def _():
        stf_ref[...] = jnp.zeros_like(stf_ref)
        stb_ref[...] = jnp.zeros_like(stb_ref)

    c = GLA_BLOCK
    nt = (((1,), (1,)), ((), ()))
    row = lax.broadcasted_iota(jnp.int32, (c, c), 0)
    col = lax.broadcasted_iota(jnp.int32, (c, c), 1)
    mask = {True: col <= row, False: col >= row}
    tri = {fwd: jnp.where(m, 1.0, 0.0).astype(BF16) for fwd, m in mask.items()}
    refs = {True: (qf_ref, kf_ref, vf_ref, gf_ref, stf_ref, of_ref),
            False: (qb_ref, kb_ref, vb_ref, gb_ref, stb_ref, ob_ref)}
    qcols = lambda head: slice(head * GLA_DK, (head + 1) * GLA_DK)
    vcols = lambda head: slice(head * GLA_DV, (head + 1) * GLA_DV)

    state, log_a, cum, factors, att = {}, {}, {}, {}, {}

    def gate_stage(head, fwd):
        d = 0 if fwd else 1
        state[head, fwd] = refs[fwd][4][head]
        z = jnp.dot(refs[fwd][3][...], wg_ref[d, :, qcols(head)], preferred_element_type=F32)
        log_a[head, fwd] = _log2_sigmoid(z + bg_ref[d, :, qcols(head)]) * (1.0 / GLA_GATE_TEMP)

    def cumsum_stage(head, fwd):
        la = log_a[head, fwd]
        hi = la.astype(BF16)
        lo = (la - hi.astype(F32)).astype(BF16)
        cum[head, fwd] = (jnp.dot(tri[fwd], hi, preferred_element_type=F32)
                          + jnp.dot(tri[fwd], lo, preferred_element_type=F32))

    def factor_stage(head, fwd):
        cm = cum[head, fwd]
        cum_end = cm[c - 1:c, :] if fwd else cm[0:1, :]
        cum_mid = cm[c // 2:c // 2 + 1, :]
        q = refs[fwd][0][:, qcols(head)]
        k = refs[fwd][1][:, qcols(head)]
        factors[head, fwd] = (q * jnp.exp2(cm - cum_mid).astype(BF16),
                              k * jnp.exp2(cum_mid - cm).astype(BF16),
                              q * jnp.exp2(cm).astype(BF16),
                              k * jnp.exp2(cum_end - cm).astype(BF16),
                              jnp.exp2(cum_end))

    def att_stage(head, fwd):
        q_att, k_att = factors[head, fwd][:2]
        a = lax.dot_general(q_att, k_att, nt, preferred_element_type=F32)
        att[head, fwd] = jnp.where(mask[fwd], a, 0.0).astype(BF16)

    def out_stage(head, fwd):
        _, _, q_dec, k_end, decay = factors[head, fwd]
        v = refs[fwd][2][:, vcols(head)]
        st = state[head, fwd]
        o = (jnp.dot(att[head, fwd], v, preferred_element_type=F32)
             + lax.dot_general(q_dec, st.astype(BF16), nt, preferred_element_type=F32))
        kv_t = lax.dot_general(v, k_end, (((0,), (0,)), ((), ())), preferred_element_type=F32)
        refs[fwd][5][:, vcols(head)] = o.astype(BF16)
        refs[fwd][4][head] = decay * st + kv_t

    for stage in (gate_stage, cumsum_stage, factor_stage, att_stage, out_stage):
        for head in range(GLA_HEADS_PER_STEP):
            for fwd in (True, False):
                stage(head, fwd)


def _gla(p, gin, wg_pad, bg, batch, seq):
    n_lat = seq // GLA_BLOCK
    ctx_block0 = batch * n_lat
    hps = GLA_HEADS_PER_STEP
    dk, dv = hps * GLA_DK, hps * GLA_DV

    def lat_block(b, s, forward):
        j = jnp.maximum(s - 1, 0)
        return b * n_lat + (j if forward else n_lat - 1 - j)

    def in_block(b, s, forward):
        return jnp.where(s == 0, ctx_block0 + b, lat_block(b, s, forward))

    kq = GLA_HEADS // hps
    kv = 2 * GLA_HEADS * GLA_DK // dv

    def dir_specs(forward):
        return [
            pl.BlockSpec((GLA_BLOCK, dk), lambda b, h, s: (in_block(b, s, forward), h)),
            pl.BlockSpec((GLA_BLOCK, dk), lambda b, h, s: (in_block(b, s, forward), kq + h)),
            pl.BlockSpec((GLA_BLOCK, dv), lambda b, h, s: (in_block(b, s, forward), kv + h)),
            pl.BlockSpec((GLA_BLOCK, LANES), lambda b, h, s: (in_block(b, s, forward), 0)),
        ]

    out_shape = jax.ShapeDtypeStruct((batch * seq, GLA_HEADS * GLA_DV), BF16)
    state = pltpu.VMEM((hps, GLA_DV, GLA_DK), F32)
    return pl.pallas_call(
        _gla_kernel,
        grid=(batch, GLA_HEADS // hps, n_lat + 1),
        in_specs=dir_specs(True) + dir_specs(False) + [
            pl.BlockSpec((2, LANES, dk), lambda b, h, s: (0, 0, h)),
            pl.BlockSpec((2, 1, dk), lambda b, h, s: (0, 0, h))],
        out_specs=[pl.BlockSpec((GLA_BLOCK, dv), lambda b, h, s: (lat_block(b, s, True), h)),
                   pl.BlockSpec((GLA_BLOCK, dv), lambda b, h, s: (lat_block(b, s, False), h))],
        out_shape=[out_shape, out_shape],
        scratch_shapes=[state, state],
        compiler_params=_params(("arbitrary", "arbitrary", "arbitrary")),
        name="gla",
    )(p, p, p, gin, p, p, p, gin, wg_pad, bg)


NA_QTOK = NA_QROWS * GRID_W
NA_SUB_TOK = NA_SUB_ROWS * GRID_W
NA_WIN = NA_WIN_ROWS * GRID_W
NA_SUBS = NA_QROWS // NA_SUB_ROWS
NA_KINDS = ((0, 0), (NA_SUB_ROWS, NA_SUB_ROWS - NA_KR // 2), (GRID_W - NA_SUB_ROWS, GRID_W - NA_WIN_ROWS))


def _na_build_bias(rpb_ref, t_ref, rows):
    n_slots = 2 * NA_KR - 1
    lane = lax.broadcasted_iota(jnp.int32, (GRID_W, LANES), 1)
    q_col = lax.broadcasted_iota(jnp.int32, (GRID_W, LANES), 0)
    k_col = lane & (GRID_W - 1)
    upper = lane >= GRID_W
    c0 = jnp.clip(q_col - NA_KC // 2, 0, GRID_W - NA_KC)
    col_ok = (k_col >= c0) & (k_col < c0 + NA_KC)
    neg = jnp.full((GRID_W, LANES), NEG_INF, F32)

    def toeplitz(hh, slot, lane_off):
        if not 0 <= slot < n_slots:
            return jnp.zeros((GRID_W, LANES), F32)
        base = jnp.broadcast_to(rpb_ref[hh, slot:slot + 1, :], (GRID_W, LANES)) * LOG2E
        return pltpu.roll(base, (LANES - (NA_KC - 1) + lane_off) % LANES, 1, stride=1, stride_axis=0)

    for hh in range(2):
        pairs = {}
        for kind, (r0, w0) in enumerate(NA_KINDS):
            for qr in range(NA_SUB_ROWS):
                band0 = min(max(r0 + qr - NA_KR // 2, 0), rows - NA_KR) - w0
                slot0 = w0 - (r0 + qr) + NA_KR - 1
                for g in range(NA_WIN_ROWS // 2):
                    slot = slot0 + 2 * g
                    lo_ok = band0 <= 2 * g < band0 + NA_KR
                    hi_ok = band0 <= 2 * g + 1 < band0 + NA_KR
                    if lo_ok or hi_ok:
                        if slot not in pairs:
                            pairs[slot] = jnp.where(upper, toeplitz(hh, slot + 1, GRID_W), toeplitz(hh, slot, 0))
                        ok = col_ok if (lo_ok and hi_ok) else (col_ok & upper if hi_ok else col_ok & ~upper)
                        tile = jnp.where(ok, pairs[slot], neg)
                    else:
                        tile = neg
                    t_ref[hh, kind, qr * GRID_W:(qr + 1) * GRID_W, g * LANES:(g + 1) * LANES] = tile


def _na_kernel(q_ref, k_ref, v_ref, kc_ref, vc_ref, rpb_ref, o_ref, t_ref, *, rows):
    b = pl.program_id(1)
    blk = pl.program_id(2)
    n_blk = rows // NA_QROWS

    pl.when((b == 0) & (blk == 0))(lambda: _na_build_bias(rpb_ref, t_ref, rows))

    nt = (((1,), (1,)), ((), ()))
    kc = kc_ref[...]
    vc = vc_ref[...]
    lane = lax.broadcasted_iota(jnp.int32, (NA_SUB_TOK, LANES), 1)
    chains = [(hh, sub) for sub in range(NA_SUBS) for hh in range(2)]
    q_head, kw, vw, kind = {}, [], [], []
    for sub in range(NA_SUBS):
        q = q_ref[sub * NA_SUB_TOK:(sub + 1) * NA_SUB_TOK, :].astype(F32) * (NA_DH ** -0.5 * LOG2E)
        q = q.astype(BF16)
        q_head[0, sub] = jnp.where(lane < NA_DH, q, jnp.zeros_like(q))
        q_head[1, sub] = jnp.where(lane >= NA_DH, q, jnp.zeros_like(q))
        r0 = blk * NA_QROWS + sub * NA_SUB_ROWS
        w0 = jnp.clip(r0 - NA_KR // 2, 0, rows - NA_WIN_ROWS)
        win = pl.ds(pl.multiple_of(w0 * GRID_W, NA_SUB_TOK), NA_WIN)
        kw.append(k_ref[win, :])
        vw.append(v_ref[win, :])
        kind.append(jnp.where(r0 == 0, 0, jnp.where(r0 == rows - NA_SUB_ROWS, 2, 1)))

    s_lat, s_ctx, p_lat, p_ctx, denom, out = {}, {}, {}, {}, {}, {}
    for hh, sub in chains:
        qs = q_head[hh, sub]
        s_lat[hh, sub] = lax.dot_general(qs, kw[sub], nt, preferred_element_type=F32) + t_ref[hh, kind[sub]]
        s_ctx[hh, sub] = lax.dot_general(qs, kc, nt, preferred_element_type=F32)
    for ch in chains:
        m = jnp.maximum(jnp.max(s_lat[ch], axis=-1, keepdims=True), jnp.max(s_ctx[ch], axis=-1, keepdims=True))
        pl_ = jnp.exp2(s_lat[ch] - m)
        pc_ = jnp.exp2(s_ctx[ch] - m)
        denom[ch] = jnp.sum(pl_, axis=-1, keepdims=True) + jnp.sum(pc_, axis=-1, keepdims=True)
        p_lat[ch] = pl_.astype(BF16)
        p_ctx[ch] = pc_.astype(BF16)
    for hh, sub in chains:
        o = (jnp.dot(p_lat[hh, sub], vw[sub], preferred_element_type=F32)
             + jnp.dot(p_ctx[hh, sub], vc, preferred_element_type=F32))
        out[hh, sub] = o / denom[hh, sub]
    for sub in range(NA_SUBS):
        rows_ = slice(sub * NA_SUB_TOK, (sub + 1) * NA_SUB_TOK)
        o_ref[rows_, :] = jnp.where(lane < NA_DH, out[0, sub], out[1, sub]).astype(o_ref.dtype)


def _na(p, rpb, batch, seq, ctx_len, col0):
    rows = seq // GRID_W
    n_blk = rows // NA_QROWS
    n_pairs = NA_HEADS // 2
    cq = col0 // LANES
    ck = cq + n_pairs
    cv = ck + n_pairs
    ctx_block0 = batch * seq // ctx_len
    return pl.pallas_call(
        functools.partial(_na_kernel, rows=rows),
        grid=(n_pairs, batch, n_blk),
        in_specs=[
            pl.BlockSpec((NA_QTOK, LANES), lambda hp, b, r: (b * n_blk + r, cq + hp)),
            pl.BlockSpec((seq, LANES), lambda hp, b, r: (b, ck + hp)),
            pl.BlockSpec((seq, LANES), lambda hp, b, r: (b, cv + hp)),
            pl.BlockSpec((ctx_len, LANES), lambda hp, b, r: (ctx_block0 + b, ck + hp)),
            pl.BlockSpec((ctx_len, LANES), lambda hp, b, r: (ctx_block0 + b, cv + hp)),
            pl.BlockSpec((2, 2 * NA_KR - 1, LANES), lambda hp, b, r: (hp, 0, 0)),
        ],
        out_specs=pl.BlockSpec((NA_QTOK, LANES), lambda hp, b, r: (b * n_blk + r, hp)),
        out_shape=jax.ShapeDtypeStruct((batch * seq, NA_HEADS * NA_DH), BF16),
        scratch_shapes=[pltpu.VMEM((2, len(NA_KINDS), NA_SUB_TOK, NA_WIN), F32)],
        compiler_params=_params(("arbitrary", "arbitrary", "arbitrary")),
        name="natten",
    )(p, p, p, p, p, rpb)


def _merge_kernel(of_ref, ob_ref, r_ref, gn_ref, b_ref, wa_ref, wb_ref, m1a_ref, m1b_ref, m2a_ref, m2b_ref,
                  o_ref, a_ref):
    for h in range(GLA_HEADS):
        cols = slice(h * GLA_DV, (h + 1) * GLA_DV)
        tot = of_ref[:, cols].astype(F32) + ob_ref[:, cols].astype(F32)
        a_ref[:, cols] = (_rms(tot, gn_ref[...]) * _silu(r_ref[:, cols].astype(F32))).astype(BF16)

    gate_refs = ((m1a_ref, m2a_ref), (m1b_ref, m2b_ref))
    gate_w = m1a_ref.shape[1]
    for c0 in range(0, o_ref.shape[1], MERGE_STRIP):
        cols = slice(c0, c0 + MERGE_STRIP)
        m1_ref, m2_ref = gate_refs[c0 // gate_w]
        gcols = slice(c0 % gate_w, c0 % gate_w + MERGE_STRIP)
        a = _mixed_dot(a_ref[...], wa_ref[:, cols])
        b = _mixed_dot(b_ref[...], wb_ref[:, cols])
        m = _sigmoid(m1_ref[:, gcols].astype(F32)) * a + _sigmoid(m2_ref[:, gcols].astype(F32)) * b
        o_ref[:, cols] = m.astype(o_ref.dtype)


def _merge(o_fwd, o_bwd, gn, o_na, w_gla_o, w_na_o, p, col_r, col_m1):
    n_rows = o_fwd.shape[0]
    v_w = o_fwd.shape[1]
    gw = D_MODEL // 2
    c1 = col_m1 // gw
    once = dict(pipeline_mode=pl.Buffered(1))
    gate_spec = lambda c: pl.BlockSpec((ROW_TILE, gw), lambda i: (i, c))
    return pl.pallas_call(
        _merge_kernel,
        grid=(n_rows // ROW_TILE,),
        in_specs=[pl.BlockSpec((ROW_TILE, v_w), lambda i: (i, 0)),
                  pl.BlockSpec((ROW_TILE, v_w), lambda i: (i, 0)),
                  pl.BlockSpec((ROW_TILE, v_w), lambda i: (i, col_r // v_w)),
                  pl.BlockSpec((1, GLA_DV), lambda i: (0, 0)),
                  pl.BlockSpec((ROW_TILE, o_na.shape[1]), lambda i: (i, 0)),
                  pl.BlockSpec(w_gla_o.shape, lambda i: (0, 0), **once),
                  pl.BlockSpec(w_na_o.shape, lambda i: (0, 0), **once),
                  gate_spec(c1), gate_spec(c1 + 1), gate_spec(c1 + 2), gate_spec(c1 + 3)],
        out_specs=pl.BlockSpec((ROW_TILE, D_MODEL), lambda i: (i, 0)),
        out_shape=jax.ShapeDtypeStruct((n_rows, D_MODEL), BF16),
        scratch_shapes=[pltpu.VMEM((ROW_TILE, v_w), BF16)],
        compiler_params=_params(("arbitrary",)),
        name="merge",
    )(o_fwd, o_bwd, p, gn, o_na, w_gla_o, w_na_o, p, p, p, p)


def _outproj_kernel(m_ref, w_ref, x_ref, mod_ref, g_ref, o_ref):
    y = _mixed_dot(m_ref[...], w_ref[...])
    o_ref[...] = x_ref[...] + mod_ref[5:6, :] * _rms(y, g_ref[3:4, :])


def _outproj(m, w_out, h1, mod, norm_g):
    n_tiles = m.shape[0] // ROW_TILE
    tiles_per_batch = n_tiles // 2
    return pl.pallas_call(
        _outproj_kernel,
        grid=(n_tiles,),
        in_specs=[pl.BlockSpec((ROW_TILE, D_MODEL), lambda i: (i, 0)),
                  pl.BlockSpec((D_MODEL, D_MODEL), lambda i: (0, 0), pipeline_mode=pl.Buffered(1)),
                  pl.BlockSpec((ROW_TILE, D_MODEL), lambda i: (i, 0)),
                  pl.BlockSpec((None, N_MOD, D_MODEL), lambda i: (i // tiles_per_batch, 0, 0)),
                  pl.BlockSpec((6, D_MODEL), lambda i: (0, 0))],
        out_specs=pl.BlockSpec((ROW_TILE, D_MODEL), lambda i: (i, 0)),
        out_shape=jax.ShapeDtypeStruct((m.shape[0], D_MODEL), F32),
        compiler_params=_params(("arbitrary",)),
        name="outproj",
    )(m, w_out, h1, mod, norm_g)


def kernel(x, c, ctx, c_ctx, w_ada, b_ada, norm_g, ffn_wg, ffn_wu, ffn_wd, w_in, gla_wg, gla_bg, gla_norm_g,
           w_gla_o, na_rpb, w_na_o, w_out):
    batch, seq, d = x.shape
    ctx_len = ctx.shape[1]
    depth = w_ada.shape[0]
    assert d == D_MODEL and batch == 2 and batch * ctx_len == ROW_TILE and depth == 1
    assert seq % ROW_TILE == 0 and seq // GRID_W == GRID_W

    qk_w = GLA_HEADS * GLA_DK
    v_w = GLA_HEADS * GLA_DV
    na_w = NA_HEADS * NA_DH
    gate0 = 2 * qk_w + 2 * v_w
    gate1 = gate0 + 2 * GLA_GATE_RANK
    col_nq = gate0
    col_m1 = col_nq + 3 * na_w

    cvec = jnp.zeros((8, d), F32).at[0:batch].set(c).at[batch].set(c_ctx)
    cos, sin = _rope_tables(seq)
    h = x.reshape(batch * seq, d)
    hc = ctx.reshape(batch * ctx_len, d)
    lat_tiles_per_batch = seq // BIG_ROW_TILE
    for l in range(depth):
        mod = _modulation(cvec, w_ada[l], b_ada[l].reshape(1, -1)).reshape(8, N_MOD, d)
        g = norm_g[l]
        h1 = _ffn(h, mod, g, ffn_wg[l], ffn_wu[l], ffn_wd[l], 0, 0, tm=BIG_ROW_TILE, tf=FF_TILE,
                  mod_row=lambda i: i // lat_tiles_per_batch)
        h1c = _ffn(hc, mod, g, ffn_wg[l], ffn_wu[l], ffn_wd[l], 0, 0, tm=ROW_TILE, tf=FF_TILE,
                   mod_row=lambda i: batch)
        ctx_cols = [(qk_w, 2 * qk_w + v_w), (col_nq + na_w, col_nq + 3 * na_w)]
        p, gin = _inproj(h1, h1c, mod, g, jnp.swapaxes(w_in[l], 0, 1), cos, sin, seq, gate0, gate1, ctx_cols)
        wg_pad = jnp.zeros((2, LANES, qk_w), F32)
        wg_pad = wg_pad.at[0, :GLA_GATE_RANK].set(gla_wg[l, 0])
        wg_pad = wg_pad.at[1, GLA_GATE_RANK:2 * GLA_GATE_RANK].set(gla_wg[l, 1])
        bg = gla_bg[l].reshape(2, 1, qk_w)
        gn = gla_norm_g[l].reshape(1, GLA_DV)
        o_fwd, o_bwd = _gla(p, gin, wg_pad, bg, batch, seq)
        rpb = jnp.pad(na_rpb[l], ((0, 0), (0, 0), (0, LANES - (2 * NA_KC - 1))))
        o_na = _na(p, rpb, batch, seq, ctx_len, col_nq)
        m = _merge(o_fwd, o_bwd, gn, o_na, w_gla_o[l], w_na_o[l], p,
                   2 * qk_w + v_w, col_m1)
        h2 = _outproj(m, w_out[l], h1, mod, g)
        h = _ffn(h2, mod, g, ffn_wg[l], ffn_wu[l], ffn_wd[l], 1, 2, tm=BIG_ROW_TILE, tf=FF_TILE,
                 mod_row=lambda i: i // lat_tiles_per_batch)
    return h.reshape(batch, seq, d)
```

```python
import functools

import numpy as np
import jax
import jax.numpy as jnp
from jax import lax
from jax.experimental import pallas as pl
from jax.experimental.pallas import tpu as pltpu

F32 = jnp.float32
BF16 = jnp.bfloat16

D_MODEL = 2048
GRID_W = 64
GLA_HEADS = 4
GLA_DK = 256
GLA_DV = 512
GLA_GATE_RANK = 16
GLA_GATE_TEMP = 16.0
NA_HEADS = 16
NA_DH = 64
NA_KR = 8
NA_KC = 16
ROPE_THETA = 10000.0
EPS = 1e-6
NEG_INF = -1e30
N_MOD = 9
LOG2E = 1.4426950408889634

LANES = 128
ROW_TILE = 512
BIG_ROW_TILE = 1024
FF_TILE = 512
PROJ_TILE = 1024
MERGE_STRIP = 512
ADA_TILE = 1024
GLA_BLOCK = 256
GLA_HEADS_PER_STEP = 4
NA_QROWS = 16
NA_SUB_ROWS = 4
NA_WIN_ROWS = 12
VMEM_LIMIT = 56 * 1024 * 1024
VMEM_LIMIT_FFN = 60 * 1024 * 1024


def _params(sem, vmem_limit=VMEM_LIMIT):
    return pltpu.CompilerParams(dimension_semantics=sem, vmem_limit_bytes=vmem_limit)


def _rms(x, g):
    return x * lax.rsqrt(jnp.mean(x * x, axis=-1, keepdims=True) + EPS) * g


def _sigmoid(x):
    return 0.5 * jnp.tanh(0.5 * x) + 0.5


def _silu(x):
    h = 0.5 * x
    return h * jnp.tanh(h) + h


def _mod_kernel(c_ref, w_ref, b_ref, o_ref):
    o_ref[...] = jnp.dot(_silu(c_ref[...]), w_ref[...], preferred_element_type=F32) + b_ref[...]


def _modulation(cvec, w_ada, b_ada):
    n = w_ada.shape[1]
    tn = ADA_TILE
    return pl.pallas_call(
        _mod_kernel,
        grid=(n // tn,),
        in_specs=[pl.BlockSpec((8, D_MODEL), lambda j: (0, 0)),
                  pl.BlockSpec((D_MODEL, tn), lambda j: (0, j)),
                  pl.BlockSpec((1, tn), lambda j: (0, j))],
        out_specs=pl.BlockSpec((8, tn), lambda j: (0, j)),
        out_shape=jax.ShapeDtypeStruct((8, n), F32),
        compiler_params=_params(("arbitrary",)),
        name="adaln_mod",
    )(cvec, w_ada, b_ada)


def _mixed_dot(a, w):
    return lax.dot_general(a, w, (((1,), (0,)), ((), ())), preferred_element_type=F32)


def _ffn_kernel(x_ref, mod_ref, g_ref, wg_ref, wu_ref, wd_ref, o_hbm, h_ref, acc_ref, sem, *, sub):
    i = pl.program_id(0)
    f = pl.program_id(1)
    n_tiles = pl.num_programs(0)
    last = pl.num_programs(1) - 1
    tm = x_ref.shape[0]
    subtiles = [pl.ds(r0, ROW_TILE) for r0 in range(0, tm, ROW_TILE)]

    def out_copy(k, tile):
        return pltpu.make_async_copy(
            acc_ref.at[subtiles[k], :], o_hbm.at[pl.ds(tile * tm + k * ROW_TILE, ROW_TILE), :], sem.at[k])

    def gate_up(h):
        g = _mixed_dot(h, wg_ref[...])
        u = _mixed_dot(h, wu_ref[...])
        return (_silu(g) * u).astype(BF16)

    @pl.when((i == 0) & (f == 0))
    def _():
        acc_ref[...] = jnp.zeros_like(acc_ref)
        for k in range(len(subtiles)):
            out_copy(k, 0).start()

    @pl.when(f == 0)
    def _():
        gain = g_ref[2 * sub:2 * sub + 1, :] * (1.0 + mod_ref[3 * sub + 1:3 * sub + 2, :])
        shift = mod_ref[3 * sub:3 * sub + 1, :]
        for k, rows in enumerate(subtiles):
            h = (_rms(x_ref[rows, :], gain) + shift).astype(BF16)
            h_ref[rows, :] = h
            a = gate_up(h)
            out_copy(k, jnp.maximum(i - 1, 0)).wait()
            acc_ref[rows, :] = _mixed_dot(a, wd_ref[...])

    @pl.when((f > 0) & (f < last))
    def _():
        for rows in subtiles:
            acc_ref[rows, :] += _mixed_dot(gate_up(h_ref[rows, :]), wd_ref[...])

    @pl.when(f == last)
    def _():
        gain = 0.5 * mod_ref[3 * sub + 2:3 * sub + 3, :] * g_ref[2 * sub + 1:2 * sub + 2, :]
        for k, rows in enumerate(subtiles):
            y = acc_ref[rows, :] + _mixed_dot(gate_up(h_ref[rows, :]), wd_ref[...])
            acc_ref[rows, :] = x_ref[rows, :] + _rms(y, gain)
            out_copy(k, i).start(priority=1)

    @pl.when((f == last) & (i == n_tiles - 1))
    def _():
        for k in range(len(subtiles)):
            out_copy(k, i).wait()


def _ffn(x2d, mod, norm_g, wg, wu, wd, which, sub, *, tm, tf, mod_row):
    n_tiles = x2d.shape[0] // tm
    d_ff = wg.shape[2]
    return pl.pallas_call(
        functools.partial(_ffn_kernel, sub=sub),
        grid=(n_tiles, d_ff // tf),
        in_specs=[
            pl.BlockSpec((tm, D_MODEL), lambda i, f: (i, 0)),
            pl.BlockSpec((None, N_MOD, D_MODEL), lambda i, f: (mod_row(i), 0, 0)),
            pl.BlockSpec((6, D_MODEL), lambda i, f: (0, 0)),
            pl.BlockSpec((None, D_MODEL, tf), lambda i, f: (which, 0, f)),
            pl.BlockSpec((None, D_MODEL, tf), lambda i, f: (which, 0, f)),
            pl.BlockSpec((None, tf, D_MODEL), lambda i, f: (which, f, 0)),
        ],
        out_specs=pl.BlockSpec(memory_space=pl.ANY),
        out_shape=jax.ShapeDtypeStruct(x2d.shape, F32),
        scratch_shapes=[pltpu.VMEM((tm, D_MODEL), BF16), pltpu.VMEM((tm, D_MODEL), F32),
                        pltpu.SemaphoreType.DMA((tm // ROW_TILE,))],
        compiler_params=_params(("arbitrary", "arbitrary"), VMEM_LIMIT_FFN),
        name="ffn%d_%d" % (sub, tm),
    )(x2d, mod, norm_g, wg, wu, wd)


N_ROPE_TILES = 2 * GLA_HEADS * GLA_DK // PROJ_TILE
N_Q_TILES = N_ROPE_TILES // 2


def _rope_tables(seq):
    half = GLA_DK // 4
    freqs = ROPE_THETA ** (-np.arange(half, dtype=np.float64) / half)
    t = np.arange(seq)
    cos_parts, sin_parts = [], []
    for pos in (t // GRID_W, t % GRID_W):
        ang = pos[:, None].astype(np.float64) * freqs
        cos_parts += [np.cos(ang), np.cos(ang)]
        sin_parts += [-np.sin(ang), np.sin(ang)]
    cos = np.concatenate(cos_parts, axis=1)
    sin = np.concatenate(sin_parts, axis=1)
    cos = np.concatenate([cos, np.ones((BIG_ROW_TILE, GLA_DK))], axis=0)
    sin = np.concatenate([sin, np.zeros((BIG_ROW_TILE, GLA_DK))], axis=0)
    return jnp.asarray(cos, F32), jnp.asarray(sin, F32)


def _dot_nt(a, w_t):
    return lax.dot_general(a, w_t, (((1,), (1,)), ((), ())), preferred_element_type=F32)


def _inproj_kernel(x_ref, xc_ref, mod_ref, g_ref, w_ref, wgate_ref, cos_ref, sin_ref, p_ref, gate_ref, h_ref,
                   *, n_lat_tiles, ctx_tiles):
    i = pl.program_id(0)
    n = pl.program_id(1)

    def project(src_ref, live):
        r = slice(0, src_ref.shape[0])
        alive = (lambda cond: cond) if live is None else (lambda cond: cond & live)

        @pl.when(n == 0)
        def _():
            h = _rms(src_ref[...], g_ref[2:3, :]) * (1.0 + mod_ref[4:5, :]) + mod_ref[3:4, :]
            h_ref[r, :] = h.astype(BF16)
            gate = _dot_nt(h_ref[r, :], wgate_ref[...])
            gate_ref[r, :] = jnp.concatenate(
                [gate, jnp.zeros((gate.shape[0], LANES - gate.shape[1]), F32)], axis=1)

        @pl.when(alive(n < N_ROPE_TILES))
        def _():
            y = _dot_nt(h_ref[r, :], w_ref[...])
            cos = cos_ref[r, :]
            sin = sin_ref[r, :]
            qscale = jnp.where(n < N_Q_TILES, GLA_DK ** -0.5, 1.0).astype(F32)
            for j in range(PROJ_TILE // GLA_DK):
                yj = y[:, j * GLA_DK:(j + 1) * GLA_DK]
                swapped = jnp.concatenate(
                    [pltpu.roll(yj[:, g * LANES:(g + 1) * LANES], LANES // 2, 1) for g in range(GLA_DK // LANES)],
                    axis=1)
                p_ref[r, j * GLA_DK:(j + 1) * GLA_DK] = ((yj * cos + swapped * sin) * qscale).astype(BF16)

        @pl.when(alive(n >= N_ROPE_TILES))
        def _():
            p_ref[r, :] = _dot_nt(h_ref[r, :], w_ref[...]).astype(BF16)

        if live is not None:
            @pl.when(jnp.logical_not(live))
            def _():
                p_ref[r, :] = jnp.zeros((src_ref.shape[0], PROJ_TILE), BF16)

    ctx_live = functools.reduce(jnp.logical_or, [n == t for t in ctx_tiles])
    pl.when(i < n_lat_tiles)(lambda: project(x_ref, None))
    pl.when(i >= n_lat_tiles)(lambda: project(xc_ref, ctx_live))


def _inproj(h1, h1c, mod, norm_g, w_in_t, cos, sin, seq, gate0, gate1, ctx_cols):
    tm = BIG_ROW_TILE
    n_lat_tiles = h1.shape[0] // tm
    n_rows = h1.shape[0] + h1c.shape[0]
    tiles_per_batch = n_lat_tiles // 2
    n_a_tiles = gate0 // PROJ_TILE
    n_b_tiles = (w_in_t.shape[0] - gate1) // PROJ_TILE
    n_gate = gate1 - gate0
    ctx_tiles = [t for t in range(n_a_tiles + n_b_tiles)
                 if any(lo < (t + 1) * PROJ_TILE and t * PROJ_TILE < hi for lo, hi in ctx_cols)]

    def weight_row0(i, n):
        t = jnp.int32(ctx_tiles[0])
        for live in ctx_tiles[1:]:
            t = jnp.where(n >= live, live, t)
        t = jnp.where(i < n_lat_tiles, n, t)
        return pl.multiple_of(t * PROJ_TILE + jnp.where(t < n_a_tiles, 0, n_gate), n_gate), 0

    tab_spec = pl.BlockSpec(
        (tm, GLA_DK), lambda i, n: (jnp.where(i < n_lat_tiles, i % tiles_per_batch, seq // tm), 0))
    return pl.pallas_call(
        functools.partial(_inproj_kernel, n_lat_tiles=n_lat_tiles, ctx_tiles=ctx_tiles),
        grid=(n_lat_tiles + 1, n_a_tiles + n_b_tiles),
        in_specs=[pl.BlockSpec((tm, D_MODEL), lambda i, n: (jnp.minimum(i, n_lat_tiles - 1), 0)),
                  pl.BlockSpec(h1c.shape, lambda i, n: (0, 0)),
                  pl.BlockSpec((None, N_MOD, D_MODEL), lambda i, n: (jnp.minimum(i // tiles_per_batch, 2), 0, 0)),
                  pl.BlockSpec((6, D_MODEL), lambda i, n: (0, 0)),
                  pl.BlockSpec((pl.Element(PROJ_TILE), pl.Element(D_MODEL)), weight_row0),
                  pl.BlockSpec((n_gate, D_MODEL), lambda i, n: (gate0 // n_gate, 0)),
                  tab_spec, tab_spec],
        out_specs=[pl.BlockSpec((tm, PROJ_TILE), lambda i, n: (i, n)),
                   pl.BlockSpec((tm, LANES), lambda i, n: (i, 0))],
        out_shape=[jax.ShapeDtypeStruct((n_rows, (n_a_tiles + n_b_tiles) * PROJ_TILE), BF16),
                   jax.ShapeDtypeStruct((n_rows, LANES), F32)],
        scratch_shapes=[pltpu.VMEM((tm, D_MODEL), BF16)],
        compiler_params=_params(("arbitrary", "arbitrary")),
        name="inproj",
    )(h1, h1c, mod, norm_g, w_in_t, w_in_t, cos, sin)


def _log2_sigmoid(z):
    t = z * LOG2E
    return jnp.minimum(t, 0.0) - jnp.log2(1.0 + jnp.exp2(-jnp.abs(t)))


def _gla_kernel(qf_ref, kf_ref, vf_ref, gf_ref, qb_ref, kb_ref, vb_ref, gb_ref, wg_ref, bg_ref,
                of_ref, ob_ref, stf_ref, stb_ref):
    @pl.when(pl.program_id(2) == 0)
    def _():
        stf_ref[...] = jnp.zeros_like(stf_ref)
        stb_ref[...] = jnp.zeros_like(stb_ref)

    c = GLA_BLOCK
    nt = (((1,), (1,)), ((), ()))
    row = lax.broadcasted_iota(jnp.int32, (c, c), 0)
    col = lax.broadcasted_iota(jnp.int32, (c, c), 1)
    mask = {True: col <= row, False: col >= row}
    tri = {fwd: jnp.where(m, 1.0, 0.0).astype(BF16) for fwd, m in mask.items()}
    refs = {True: (qf_ref, kf_ref, vf_ref, gf_ref, stf_ref, of_ref),
            False: (qb_ref, kb_ref, vb_ref, gb_ref, stb_ref, ob_ref)}
    qcols = lambda head: slice(head * GLA_DK, (head + 1) * GLA_DK)
    vcols = lambda head: slice(head * GLA_DV, (head + 1) * GLA_DV)

    state, log_a, cum, factors, att = {}, {}, {}, {}, {}

    def gate_stage(head, fwd):
        d = 0 if fwd else 1
        state[head, fwd] = refs[fwd][4][head]
        z = jnp.dot(refs[fwd][3][...], wg_ref[d, :, qcols(head)], preferred_element_type=F32)
        log_a[head, fwd] = _log2_sigmoid(z + bg_ref[d, :, qcols(head)]) * (1.0 / GLA_GATE_TEMP)

    def cumsum_stage(head, fwd):
        la = log_a[head, fwd]
        hi = la.astype(BF16)
        lo = (la - hi.astype(F32)).astype(BF16)
        cum[head, fwd] = (jnp.dot(tri[fwd], hi, preferred_element_type=F32)
                          + jnp.dot(tri[fwd], lo, preferred_element_type=F32))

    def factor_stage(head, fwd):
        cm = cum[head, fwd]
        cum_end = cm[c - 1:c, :] if fwd else cm[0:1, :]
        cum_mid = cm[c // 2:c // 2 + 1, :]
        q = refs[fwd][0][:, qcols(head)]
        k = refs[fwd][1][:, qcols(head)]
        factors[head, fwd] = (q * jnp.exp2(cm - cum_mid).astype(BF16),
                              k * jnp.exp2(cum_mid - cm).astype(BF16),
                              q * jnp.exp2(cm).astype(BF16),
                              k * jnp.exp2(cum_end - cm).astype(BF16),
                              jnp.exp2(cum_end))

    def att_stage(head, fwd):
        q_att, k_att = factors[head, fwd][:2]
        a = lax.dot_general(q_att, k_att, nt, preferred_element_type=F32)
        att[head, fwd] = jnp.where(mask[fwd], a, 0.0).astype(BF16)

    def out_stage(head, fwd):
        _, _, q_dec, k_end, decay = factors[head, fwd]
        v = refs[fwd][2][:, vcols(head)]
        st = state[head, fwd]
        o = (jnp.dot(att[head, fwd], v, preferred_element_type=F32)
             + lax.dot_general(q_dec, st.astype(BF16), nt, preferred_element_type=F32))
        kv_t = lax.dot_general(v, k_end, (((0,), (0,)), ((), ())), preferred_element_type=F32)
        refs[fwd][5][:, vcols(head)] = o.astype(BF16)
        refs[fwd][4][head] = decay * st + kv_t

    for stage in (gate_stage, cumsum_stage, factor_stage, att_stage, out_stage):
        for head in range(GLA_HEADS_PER_STEP):
            for fwd in (True, False):
                stage(head, fwd)


def _gla(p, gin, wg_pad, bg, batch, seq):
    n_lat = seq // GLA_BLOCK
    ctx_block0 = batch * n_lat
    hps = GLA_HEADS_PER_STEP
    dk, dv = hps * GLA_DK, hps * GLA_DV

    def lat_block(b, s, forward):
        j = jnp.maximum(s - 1, 0)
        return b * n_lat + (j if forward else n_lat - 1 - j)

    def in_block(b, s, forward):
        return jnp.where(s == 0, ctx_block0 + b, lat_block(b, s, forward))

    kq = GLA_HEADS // hps
    kv = 2 * GLA_HEADS * GLA_DK // dv

    def dir_specs(forward):
        return [
            pl.BlockSpec((GLA_BLOCK, dk), lambda b, h, s: (in_block(b, s, forward), h)),
            pl.BlockSpec((GLA_BLOCK, dk), lambda b, h, s: (in_block(b, s, forward), kq + h)),
            pl.BlockSpec((GLA_BLOCK, dv), lambda b, h, s: (in_block(b, s, forward), kv + h)),
            pl.BlockSpec((GLA_BLOCK, LANES), lambda b, h, s: (in_block(b, s, forward), 0)),
        ]

    out_shape = jax.ShapeDtypeStruct((batch * seq, GLA_HEADS * GLA_DV), BF16)
    state = pltpu.VMEM((hps, GLA_DV, GLA_DK), F32)
    return pl.pallas_call(
        _gla_kernel,
        grid=(batch, GLA_HEADS // hps, n_lat + 1),
        in_specs=dir_specs(True) + dir_specs(False) + [
            pl.BlockSpec((2, LANES, dk), lambda b, h, s: (0, 0, h)),
            pl.BlockSpec((2, 1, dk), lambda b, h, s: (0, 0, h))],
        out_specs=[pl.BlockSpec((GLA_BLOCK, dv), lambda b, h, s: (lat_block(b, s, True), h)),
                   pl.BlockSpec((GLA_BLOCK, dv), lambda b, h, s: (lat_block(b, s, False), h))],
        out_shape=[out_shape, out_shape],
        scratch_shapes=[state, state],
        compiler_params=_params(("arbitrary", "arbitrary", "arbitrary")),
        name="gla",
    )(p, p, p, gin, p, p, p, gin, wg_pad, bg)


NA_QTOK = NA_QROWS * GRID_W
NA_SUB_TOK = NA_SUB_ROWS * GRID_W
NA_WIN = NA_WIN_ROWS * GRID_W
NA_SUBS = NA_QROWS // NA_SUB_ROWS
NA_KINDS = ((0, 0), (NA_SUB_ROWS, NA_SUB_ROWS - NA_KR // 2), (GRID_W - NA_SUB_ROWS, GRID_W - NA_WIN_ROWS))


def _na_build_bias(rpb_ref, t_ref, rows):
    n_slots = 2 * NA_KR - 1
    lane = lax.broadcasted_iota(jnp.int32, (GRID_W, LANES), 1)
    q_col = lax.broadcasted_iota(jnp.int32, (GRID_W, LANES), 0)
    k_col = lane & (GRID_W - 1)
    upper = lane >= GRID_W
    c0 = jnp.clip(q_col - NA_KC // 2, 0, GRID_W - NA_KC)
    col_ok = (k_col >= c0) & (k_col < c0 + NA_KC)
    neg = jnp.full((GRID_W, LANES), NEG_INF, F32)

    def toeplitz(hh, slot, lane_off):
        if not 0 <= slot < n_slots:
            return jnp.zeros((GRID_W, LANES), F32)
        base = jnp.broadcast_to(rpb_ref[hh, slot:slot + 1, :], (GRID_W, LANES)) * LOG2E
        return pltpu.roll(base, (LANES - (NA_KC - 1) + lane_off) % LANES, 1, stride=1, stride_axis=0)

    for hh in range(2):
        pairs = {}
        for kind, (r0, w0) in enumerate(NA_KINDS):
            for qr in range(NA_SUB_ROWS):
                band0 = min(max(r0 + qr - NA_KR // 2, 0), rows - NA_KR) - w0
                slot0 = w0 - (r0 + qr) + NA_KR - 1
                for g in range(NA_WIN_ROWS // 2):
                    slot = slot0 + 2 * g
                    lo_ok = band0 <= 2 * g < band0 + NA_KR
                    hi_ok = band0 <= 2 * g + 1 < band0 + NA_KR
                    if lo_ok or hi_ok:
                        if slot not in pairs:
                            pairs[slot] = jnp.where(upper, toeplitz(hh, slot + 1, GRID_W), toeplitz(hh, slot, 0))
                        ok = col_ok if (lo_ok and hi_ok) else (col_ok & upper if hi_ok else col_ok & ~upper)
                        tile = jnp.where(ok, pairs[slot], neg)
                    else:
                        tile = neg
                    t_ref[hh, kind, qr * GRID_W:(qr + 1) * GRID_W, g * LANES:(g + 1) * LANES] = tile


def _na_kernel(q_ref, k_ref, v_ref, kc_ref, vc_ref, rpb_ref, o_ref, t_ref, *, rows):
    b = pl.program_id(1)
    blk = pl.program_id(2)
    n_blk = rows // NA_QROWS

    pl.when((b == 0) & (blk == 0))(lambda: _na_build_bias(rpb_ref, t_ref, rows))

    nt = (((1,), (1,)), ((), ()))
    kc = kc_ref[...]
    vc = vc_ref[...]
    lane = lax.broadcasted_iota(jnp.int32, (NA_SUB_TOK, LANES), 1)
    chains = [(hh, sub) for sub in range(NA_SUBS) for hh in range(2)]
    q_head, kw, vw, kind = {}, [], [], []
    for sub in range(NA_SUBS):
        q = q_ref[sub * NA_SUB_TOK:(sub + 1) * NA_SUB_TOK, :].astype(F32) * (NA_DH ** -0.5 * LOG2E)
        q = q.astype(BF16)
        q_head[0, sub] = jnp.where(lane < NA_DH, q, jnp.zeros_like(q))
        q_head[1, sub] = jnp.where(lane >= NA_DH, q, jnp.zeros_like(q))
        r0 = blk * NA_QROWS + sub * NA_SUB_ROWS
        w0 = jnp.clip(r0 - NA_KR // 2, 0, rows - NA_WIN_ROWS)
        win = pl.ds(pl.multiple_of(w0 * GRID_W, NA_SUB_TOK), NA_WIN)
        kw.append(k_ref[win, :])
        vw.append(v_ref[win, :])
        kind.append(jnp.where(r0 == 0, 0, jnp.where(r0 == rows - NA_SUB_ROWS, 2, 1)))

    s_lat, s_ctx, p_lat, p_ctx, denom, out = {}, {}, {}, {}, {}, {}
    for hh, sub in chains:
        qs = q_head[hh, sub]
        s_lat[hh, sub] = lax.dot_general(qs, kw[sub], nt, preferred_element_type=F32) + t_ref[hh, kind[sub]]
        s_ctx[hh, sub] = lax.dot_general(qs, kc, nt, preferred_element_type=F32)
    for ch in chains:
        m = jnp.maximum(jnp.max(s_lat[ch], axis=-1, keepdims=True), jnp.max(s_ctx[ch], axis=-1, keepdims=True))
        pl_ = jnp.exp2(s_lat[ch] - m)
        pc_ = jnp.exp2(s_ctx[ch] - m)
        denom[ch] = jnp.sum(pl_, axis=-1, keepdims=True) + jnp.sum(pc_, axis=-1, keepdims=True)
        p_lat[ch] = pl_.astype(BF16)
        p_ctx[ch] = pc_.astype(BF16)
    for hh, sub in chains:
        o = (jnp.dot(p_lat[hh, sub], vw[sub], preferred_element_type=F32)
             + jnp.dot(p_ctx[hh, sub], vc, preferred_element_type=F32))
        out[hh, sub] = o / denom[hh, sub]
    for sub in range(NA_SUBS):
        rows_ = slice(sub * NA_SUB_TOK, (sub + 1) * NA_SUB_TOK)
        o_ref[rows_, :] = jnp.where(lane < NA_DH, out[0, sub], out[1, sub]).astype(o_ref.dtype)


def _na(p, rpb, batch, seq, ctx_len, col0):
    rows = seq // GRID_W
    n_blk = rows // NA_QROWS
    n_pairs = NA_HEADS // 2
    cq = col0 // LANES
    ck = cq + n_pairs
    cv = ck + n_pairs
    ctx_block0 = batch * seq // ctx_len
    return pl.pallas_call(
        functools.partial(_na_kernel, rows=rows),
        grid=(n_pairs, batch, n_blk),
        in_specs=[
            pl.BlockSpec((NA_QTOK, LANES), lambda hp, b, r: (b * n_blk + r, cq + hp)),
            pl.BlockSpec((seq, LANES), lambda hp, b, r: (b, ck + hp)),
            pl.BlockSpec((seq, LANES), lambda hp, b, r: (b, cv + hp)),
            pl.BlockSpec((ctx_len, LANES), lambda hp, b, r: (ctx_block0 + b, ck + hp)),
            pl.BlockSpec((ctx_len, LANES), lambda hp, b, r: (ctx_block0 + b, cv + hp)),
            pl.BlockSpec((2, 2 * NA_KR - 1, LANES), lambda hp, b, r: (hp, 0, 0)),
        ],
        out_specs=pl.BlockSpec((NA_QTOK, LANES), lambda hp, b, r: (b * n_blk + r, hp)),
        out_shape=jax.ShapeDtypeStruct((batch * seq, NA_HEADS * NA_DH), BF16),
        scratch_shapes=[pltpu.VMEM((2, len(NA_KINDS), NA_SUB_TOK, NA_WIN), F32)],
        compiler_params=_params(("arbitrary", "arbitrary", "arbitrary")),
        name="natten",
    )(p, p, p, p, p, rpb)


def _merge_kernel(of_ref, ob_ref, r_ref, gn_ref, b_ref, wa_ref, wb_ref, m1a_ref, m1b_ref, m2a_ref, m2b_ref,
                  o_ref, a_ref):
    for h in range(GLA_HEADS):
        cols = slice(h * GLA_DV, (h + 1) * GLA_DV)
        tot = of_ref[:, cols].astype(F32) + ob_ref[:, cols].astype(F32)
        a_ref[:, cols] = (_rms(tot, gn_ref[...]) * _silu(r_ref[:, cols].astype(F32))).astype(BF16)

    gate_refs = ((m1a_ref, m2a_ref), (m1b_ref, m2b_ref))
    gate_w = m1a_ref.shape[1]
    for c0 in range(0, o_ref.shape[1], MERGE_STRIP):
        cols = slice(c0, c0 + MERGE_STRIP)
        m1_ref, m2_ref = gate_refs[c0 // gate_w]
        gcols = slice(c0 % gate_w, c0 % gate_w + MERGE_STRIP)
        a = _mixed_dot(a_ref[...], wa_ref[:, cols])
        b = _mixed_dot(b_ref[...], wb_ref[:, cols])
        m = _sigmoid(m1_ref[:, gcols].astype(F32)) * a + _sigmoid(m2_ref[:, gcols].astype(F32)) * b
        o_ref[:, cols] = m.astype(o_ref.dtype)


def _merge(o_fwd, o_bwd, gn, o_na, w_gla_o, w_na_o, p, col_r, col_m1):
    n_rows = o_fwd.shape[0]
    v_w = o_fwd.shape[1]
    gw = D_MODEL // 2
    c1 = col_m1 // gw
    once = dict(pipeline_mode=pl.Buffered(1))
    gate_spec = lambda c: pl.BlockSpec((ROW_TILE, gw), lambda i: (i, c))
    return pl.pallas_call(
        _merge_kernel,
        grid=(n_rows // ROW_TILE,),
        in_specs=[pl.BlockSpec((ROW_TILE, v_w), lambda i: (i, 0)),
                  pl.BlockSpec((ROW_TILE, v_w), lambda i: (i, 0)),
                  pl.BlockSpec((ROW_TILE, v_w), lambda i: (i, col_r // v_w)),
                  pl.BlockSpec((1, GLA_DV), lambda i: (0, 0)),
                  pl.BlockSpec((ROW_TILE, o_na.shape[1]), lambda i: (i, 0)),
                  pl.BlockSpec(w_gla_o.shape, lambda i: (0, 0), **once),
                  pl.BlockSpec(w_na_o.shape, lambda i: (0, 0), **once),
                  gate_spec(c1), gate_spec(c1 + 1), gate_spec(c1 + 2), gate_spec(c1 + 3)],
        out_specs=pl.BlockSpec((ROW_TILE, D_MODEL), lambda i: (i, 0)),
        out_shape=jax.ShapeDtypeStruct((n_rows, D_MODEL), BF16),
        scratch_shapes=[pltpu.VMEM((ROW_TILE, v_w), BF16)],
        compiler_params=_params(("arbitrary",)),
        name="merge",
    )(o_fwd, o_bwd, p, gn, o_na, w_gla_o, w_na_o, p, p, p, p)


def _outproj_kernel(m_ref, w_ref, x_ref, mod_ref, g_ref, o_ref):
    y = _mixed_dot(m_ref[...], w_ref[...])
    o_ref[...] = x_ref[...] + mod_ref[5:6, :] * _rms(y, g_ref[3:4, :])


def _outproj(m, w_out, h1, mod, norm_g):
    n_tiles = m.shape[0] // ROW_TILE
    tiles_per_batch = n_tiles // 2
    return pl.pallas_call(
        _outproj_kernel,
        grid=(n_tiles,),
        in_specs=[pl.BlockSpec((ROW_TILE, D_MODEL), lambda i: (i, 0)),
                  pl.BlockSpec((D_MODEL, D_MODEL), lambda i: (0, 0), pipeline_mode=pl.Buffered(1)),
                  pl.BlockSpec((ROW_TILE, D_MODEL), lambda i: (i, 0)),
                  pl.BlockSpec((None, N_MOD, D_MODEL), lambda i: (i // tiles_per_batch, 0, 0)),
                  pl.BlockSpec((6, D_MODEL), lambda i: (0, 0))],
        out_specs=pl.BlockSpec((ROW_TILE, D_MODEL), lambda i: (i, 0)),
        out_shape=jax.ShapeDtypeStruct((m.shape[0], D_MODEL), F32),
        compiler_params=_params(("arbitrary",)),
        name="outproj",
    )(m, w_out, h1, mod, norm_g)


def kernel(x, c, ctx, c_ctx, w_ada, b_ada, norm_g, ffn_wg, ffn_wu, ffn_wd, w_in, gla_wg, gla_bg, gla_norm_g,
           w_gla_o, na_rpb, w_na_o, w_out):
    batch, seq, d = x.shape
    ctx_len = ctx.shape[1]
    depth = w_ada.shape[0]
    assert d == D_MODEL and batch == 2 and batch * ctx_len == ROW_TILE and depth == 1
    assert seq % ROW_TILE == 0 and seq // GRID_W == GRID_W

    qk_w = GLA_HEADS * GLA_DK
    v_w = GLA_HEADS * GLA_DV
    na_w = NA_HEADS * NA_DH
    gate0 = 2 * qk_w + 2 * v_w
    gate1 = gate0 + 2 * GLA_GATE_RANK
    col_nq = gate0
    col_m1 = col_nq + 3 * na_w

    cvec = jnp.zeros((8, d), F32).at[0:batch].set(c).at[batch].set(c_ctx)
    cos, sin = _rope_tables(seq)
    h = x.reshape(batch * seq, d)
    hc = ctx.reshape(batch * ctx_len, d)
    lat_tiles_per_batch = seq // BIG_ROW_TILE
    for l in range(depth):
        mod = _modulation(cvec, w_ada[l], b_ada[l].reshape(1, -1)).reshape(8, N_MOD, d)
        g = norm_g[l]
        h1 = _ffn(h, mod, g, ffn_wg[l], ffn_wu[l], ffn_wd[l], 0, 0, tm=BIG_ROW_TILE, tf=FF_TILE,
                  mod_row=lambda i: i // lat_tiles_per_batch)
        h1c = _ffn(hc, mod, g, ffn_wg[l], ffn_wu[l], ffn_wd[l], 0, 0, tm=ROW_TILE, tf=FF_TILE,
                   mod_row=lambda i: batch)
        ctx_cols = [(qk_w, 2 * qk_w + v_w), (col_nq + na_w, col_nq + 3 * na_w)]
        p, gin = _inproj(h1, h1c, mod, g, jnp.swapaxes(w_in[l], 0, 1), cos, sin, seq, gate0, gate1, ctx_cols)
        wg_pad = jnp.zeros((2, LANES, qk_w), F32)
        wg_pad = wg_pad.at[0, :GLA_GATE_RANK].set(gla_wg[l, 0])
        wg_pad = wg_pad.at[1, GLA_GATE_RANK:2 * GLA_GATE_RANK].set(gla_wg[l, 1])
        bg = gla_bg[l].reshape(2, 1, qk_w)
        gn = gla_norm_g[l].reshape(1, GLA_DV)
        o_fwd, o_bwd = _gla(p, gin, wg_pad, bg, batch, seq)
        rpb = jnp.pad(na_rpb[l], ((0, 0), (0, 0), (0, LANES - (2 * NA_KC - 1))))
        o_na = _na(p, rpb, batch, seq, ctx_len, col_nq)
        m = _merge(o_fwd, o_bwd, gn, o_na, w_gla_o[l], w_na_o[l], p,
                   2 * qk_w + v_w, col_m1)
        h2 = _outproj(m, w_out[l], h1, mod, g)
        h = _ffn(h2, mod, g, ffn_wg[l], ffn_wu[l], ffn_wd[l], 1, 2, tm=BIG_ROW_TILE, tf=FF_TILE,
                 mod_row=lambda i: i // lat_tiles_per_batch)
    return h.reshape(batch, seq, d)
```
